```python
import math
import jax, jax.numpy as jnp
from jax import lax
import numpy as np

D_MODEL = 2048
BATCH = 8
SEQ = 2048
DEPTH = 1
DEC_BATCH = 32
DEC_SEQ = 4
PAST_LEN = 8192
PAGE_SIZE = 128

D_FF = 5632
RMS_EPS = 1e-6
RW_HEADS = 16
RW_HEAD_DIM = 64
RW_DIM = RW_HEADS * RW_HEAD_DIM
RW_LORA_W = 64
RW_LORA_A = 64
RW_LORA_G = 128
RW_PROJ = 3 * RW_DIM + RW_LORA_W + RW_LORA_A + RW_LORA_G
RW_LN_EPS = 64e-5
NSA_HEADS = 8
NSA_KV_HEADS = 2
NSA_HPG = NSA_HEADS // NSA_KV_HEADS
NSA_HEAD_DIM = 128
NSA_DIM = NSA_HEADS * NSA_HEAD_DIM
NSA_KV_DIM = NSA_KV_HEADS * NSA_HEAD_DIM
CMP_BLOCK = 32
CMP_HID = 256
SEL_BLOCK = 64
TOP_N = 16
WINDOW = 512
N_BUCKETS = 32
REL_MAX_EXACT = 16
REL_MAX_DIST = 1024
Q_BLOCK = 128
NEG_BIG = -1e30
FORCE_BONUS = 1e4
IN_COLS = RW_PROJ + NSA_DIM + 6 * NSA_KV_DIM + 3 * NSA_HEADS + 2 * D_MODEL

kernel_name = "rwkv7_nsa_parallel_macaron_step"


def rmsnorm(x, g):
    xf = x.astype(jnp.float32)
    y = xf * lax.rsqrt(jnp.mean(xf * xf, axis=-1, keepdims=True) + RMS_EPS)
    return (y * g.astype(jnp.float32)).astype(x.dtype)


def swiglu_half_step(x, pre_g, post_g, w1, w3, w2):
    h = rmsnorm(x, pre_g)
    f = (jax.nn.silu(h @ w1) * (h @ w3)) @ w2
    return x + 0.5 * rmsnorm(f, post_g)


def rel_bucket(dist):
    n = jnp.maximum(dist, 0)
    nf = jnp.maximum(n, 1).astype(jnp.float32)
    large = REL_MAX_EXACT + (jnp.log(nf / REL_MAX_EXACT) / math.log(REL_MAX_DIST / REL_MAX_EXACT)
                             * (N_BUCKETS - REL_MAX_EXACT)).astype(jnp.int32)
    large = jnp.minimum(large, N_BUCKETS - 1)
    return jnp.where(n < REL_MAX_EXACT, n, large)


def masked_softmax(s, mask):
    s = jnp.where(mask, s.astype(jnp.float32), NEG_BIG)
    e = jnp.where(mask, jnp.exp(s - jnp.max(s, axis=-1, keepdims=True)), 0.0)
    return e / jnp.maximum(jnp.sum(e, axis=-1, keepdims=True), 1e-30)


def rwkv7_time_mix(p, prev_row, s0, mu, w0, w2, a0, a2, g2, k_k, k_a, r_k, lnx_w, lnx_b):
    f32 = jnp.float32
    b, t, _ = p.shape
    p_prev = jnp.concatenate([prev_row[:, None].astype(p.dtype), p[:, :-1]], axis=1)
    xs = p + (p_prev - p) * mu
    sp = [RW_DIM, 2 * RW_DIM, 3 * RW_DIM, 3 * RW_DIM + RW_LORA_W, 3 * RW_DIM + RW_LORA_W + RW_LORA_A]
    r, k, v, wd, ad, gd = jnp.split(xs, sp, axis=-1)
    w = -jax.nn.softplus(-(w0 + jnp.tanh(wd) @ w2).astype(f32)) - 0.5
    decay = jnp.exp(-jnp.exp(w))
    a = jax.nn.sigmoid((a0 + ad @ a2).astype(f32))
    g = jax.nn.sigmoid(gd) @ g2
    k_mod = k.astype(f32) * (1.0 + (a - 1.0) * k_a)

    def heads(z):
        return z.astype(f32).reshape(b, t, RW_HEADS, RW_HEAD_DIM)

    kk = heads(k * k_k)
    kk = kk / jnp.maximum(jnp.sqrt(jnp.sum(kk * kk, axis=-1, keepdims=True)), 1e-12)
    r_h, k_h, v_h, w_h, a_h = heads(r), heads(k_mod), heads(v), heads(decay), heads(a)

    def step(S, inp):
        r_t, k_t, v_t, w_t, kk_t, a_t = inp
        sk = jnp.einsum('bhij,bhj->bhi', S, kk_t)
        S = (S * w_t[:, :, None, :] - sk[..., None] * (kk_t * a_t)[:, :, None, :]
             + v_t[..., None] * k_t[:, :, None, :])
        return S, jnp.einsum('bhij,bhj->bhi', S, r_t)

    seq_first = [jnp.moveaxis(z, 1, 0) for z in (r_h, k_h, v_h, w_h, kk, a_h)]
    s_fin, y = lax.scan(step, s0.astype(f32), tuple(seq_first))
    y = jnp.moveaxis(y, 0, 1)
    mean = jnp.mean(y, axis=-1, keepdims=True)
    var = jnp.mean(jnp.square(y - mean), axis=-1, keepdims=True)
    y = ((y - mean) * lax.rsqrt(var + RW_LN_EPS)).reshape(b, t, RW_DIM) * lnx_w + lnx_b
    bonus = jnp.sum(r_h * k_h * r_k, axis=-1, keepdims=True) * v_h
    y = (y + bonus.reshape(b, t, RW_DIM)) * g
    return y.astype(p.dtype), s_fin, p[:, -1]


def nsa_attention(q, gates, kv_all, win_all, q_off, k_off, rel_bias, cmp_pe, cmp_w1, cmp_w2):
    f32 = jnp.float32
    G, HPG, hd = NSA_KV_HEADS, NSA_HPG, NSA_HEAD_DIM
    b, t = q.shape[:2]
    L0 = kv_all.shape[1]
    L = -(-L0 // SEL_BLOCK) * SEL_BLOCK
    kv_all = jnp.pad(kv_all, ((0, 0), (0, L - L0), (0, 0), (0, 0), (0, 0)))
    n_cmp, n_sel = L // CMP_BLOCK, L // SEL_BLOCK
    top_n = min(TOP_N, n_sel)
    blocks = kv_all[:, :, :2].reshape(b, n_cmp, CMP_BLOCK, 2, G, hd) + cmp_pe[:, :, None, :]
    flat = blocks.transpose(0, 1, 3, 4, 2, 5).reshape(b, n_cmp, 2, G, CMP_BLOCK * hd)
    hid = jax.nn.gelu(jnp.einsum('bnkgf,kfc->bnkgc', flat, cmp_w1))
    kc = jnp.einsum('bnkgc,kcd->bnkgd', hid, cmp_w2)
    sel = kv_all[:, :, 2:].reshape(b, n_sel, SEL_BLOCK, 2, G, hd)
    win_pad = jnp.pad(win_all, ((0, 0), (WINDOW, 0), (0, 0), (0, 0), (0, 0)))
    qb = math.gcd(t, Q_BLOCK)
    nqb = t // qb
    q_blocks = q.reshape(b, nqb, qb, G, HPG, hd)
    g_blocks = jax.nn.sigmoid(gates.astype(f32)).reshape(b, nqb, qb, 3, G, HPG)
    scale = NSA_HEAD_DIM ** -0.5
    bias_g = rel_bias.reshape(N_BUCKETS, G, HPG).transpose(1, 0, 2)
    cmp_end = jnp.arange(n_cmp) * CMP_BLOCK + CMP_BLOCK - 1
    sel_ids = jnp.arange(n_sel)
    band = WINDOW + qb
    base = q_off - k_off

    def per_block(kc_b, sel_b, win_b, q_i, g_i, i):
        dt = q_i.dtype
        pos = q_off + i * qb + jnp.arange(qb)
        d_c = pos[:, None] - cmp_end[None, :]
        bias_c = jax.vmap(lambda tb: tb[rel_bucket(d_c)])(bias_g)
        s_c = jnp.einsum('qghd,ngd->ghqn', q_i, kc_b[:, 0]).astype(f32) * scale + bias_c.transpose(0, 3, 1, 2)
        p_c = masked_softmax(s_c, (d_c >= 0)[None, None])
        o_c = jnp.einsum('ghqn,ngd->qghd', p_c.astype(dt), kc_b[:, 1])
        imp = p_c.sum(axis=1).reshape(G, qb, n_sel, SEL_BLOCK // CMP_BLOCK).sum(axis=-1)
        cur = pos // SEL_BLOCK
        forced = (sel_ids[None, :] == 0) | (sel_ids[None, :] == cur[:, None]) | (sel_ids[None, :] == cur[:, None] - 1)
        imp = jnp.where(forced[None], imp + FORCE_BONUS, imp)
        imp = jnp.where((sel_ids[None, :] * SEL_BLOCK <= pos[:, None])[None], imp, NEG_BIG)
        idx = lax.top_k(imp, top_n)[1]
        gath = jax.vmap(lambda s, ix: s[ix])(sel_b.transpose(3, 0, 1, 2, 4), idx)
        gath = gath.reshape(G, qb, top_n * SEL_BLOCK, 2, hd)
        kpos = (idx[..., None] * SEL_BLOCK + jnp.arange(SEL_BLOCK)).reshape(G, qb, top_n * SEL_BLOCK)
        d_s = pos[None, :, None] - kpos
        bias_s = jax.vmap(lambda tb, d: tb[rel_bucket(d)])(bias_g, d_s)
        s_s = jnp.einsum('qghd,gqkd->ghqk', q_i, gath[..., 0, :]).astype(f32) * scale + bias_s.transpose(0, 3, 1, 2)
        p_s = masked_softmax(s_s, (d_s >= 0)[:, None])
        o_s = jnp.einsum('ghqk,gqkd->qghd', p_s.astype(dt), gath[..., 1, :])
        start = base + i * qb
        kw = lax.dynamic_slice_in_dim(win_b, start, band, axis=0)
        kpos_w = k_off - WINDOW + start + jnp.arange(band)
        d_w = pos[:, None] - kpos_w[None, :]
        mask_w = (d_w >= 0) & (d_w < WINDOW) & (kpos_w >= k_off)[None, :]
        bias_w = jax.vmap(lambda tb: tb[rel_bucket(d_w)])(bias_g)
        s_w = jnp.einsum('qghd,kgd->ghqk', q_i, kw[:, 0]).astype(f32) * scale + bias_w.transpose(0, 3, 1, 2)
        p_w = masked_softmax(s_w, mask_w[None, None])
        o_w = jnp.einsum('ghqk,kgd->qghd', p_w.astype(dt), kw[:, 1])
        o = g_i[:, 0, ..., None] * o_c + g_i[:, 1, ..., None] * o_s + g_i[:, 2, ..., None] * o_w
        return o.astype(dt)

    def per_seq(args):
        kc_b, sel_b, win_b, q_b, g_b = args
        return lax.map(lambda a: per_block(kc_b, sel_b, win_b, a[0], a[1], a[2]),
                       (q_b, g_b, jnp.arange(nqb)))

    out = lax.map(per_seq, (kc, sel, win_pad, q_blocks, g_blocks))
    return out.reshape(b, t, NSA_DIM)


def hybrid_layer(x, q_off, kv_past, win_past, s0, shift0, lw, rel_bias):
    (f1_pre, f1_post, f1_w1, f1_w3, f1_w2, mix_pre, mix_post, w_in,
     rw_mu, rw_w0, rw_w2, rw_a0, rw_a2, rw_g2, rw_k_k, rw_k_a, rw_r_k, rw_lnx_w, rw_lnx_b,
     cmp_pe, cmp_w1, cmp_w2, w_br_rw, w_br_nsa, w_out,
     f2_pre, f2_post, f2_w1, f2_w3, f2_w2) = lw
    b, t, _ = x.shape
    x = swiglu_half_step(x, f1_pre, f1_post, f1_w1, f1_w3, f1_w2)
    xn = rmsnorm(x, mix_pre)
    splits = np.cumsum([RW_PROJ, NSA_DIM, 4 * NSA_KV_DIM, 2 * NSA_KV_DIM, 3 * NSA_HEADS, D_MODEL]).tolist()
    p_rw, q, kv_rows, win_rows, nsa_gate, gate_rw, gate_nsa = jnp.split(xn @ w_in, splits, axis=-1)
    y_rw, s_fin, shift_new = rwkv7_time_mix(p_rw, shift0, s0, rw_mu, rw_w0, rw_w2, rw_a0, rw_a2, rw_g2,
                                            rw_k_k, rw_k_a, rw_r_k, rw_lnx_w, rw_lnx_b)
    kv_new = kv_rows.reshape(b, t, 4, NSA_KV_HEADS, NSA_HEAD_DIM)
    win_new = win_rows.reshape(b, t, 2, NSA_KV_HEADS, NSA_HEAD_DIM)
    kv_all = jnp.concatenate([kv_past.astype(x.dtype), kv_new], axis=1)
    win_all = jnp.concatenate([win_past.astype(x.dtype), win_new], axis=1)
    y_nsa = nsa_attention(q.reshape(b, t, NSA_HEADS, NSA_HEAD_DIM), nsa_gate.reshape(b, t, 3, NSA_HEADS),
                          kv_all, win_all, q_off, q_off - win_past.shape[1], rel_bias, cmp_pe, cmp_w1, cmp_w2)
    merged = jax.nn.sigmoid(gate_rw) * (y_rw @ w_br_rw) + jax.nn.sigmoid(gate_nsa) * (y_nsa @ w_br_nsa)
    x = x + rmsnorm(merged @ w_out, mix_post)
    x = swiglu_half_step(x, f2_pre, f2_post, f2_w1, f2_w3, f2_w2)
    keep = min(WINDOW, win_all.shape[1])
    return x, kv_new, win_all[:, win_all.shape[1] - keep:], s_fin, shift_new


def setup_inputs(seed: int = 0) -> dict:
    key = jax.random.key(seed)
    keys = jax.random.split(key, 48)
    cnt = [0]
    f32 = jnp.float32

    def nk():
        cnt[0] += 1
        return keys[cnt[0] - 1]

    def nrm(shape, scale):
        return scale * jax.random.normal(nk(), shape, f32)

    def gain(shape):
        return 1.0 + nrm(shape, 0.05)

    G, hd = NSA_KV_HEADS, NSA_HEAD_DIM
    n_pages = PAST_LEN // PAGE_SIZE
    n_pool = (DEC_BATCH * n_pages * 5) // 4
    win_buf = min(WINDOW, PAST_LEN)
    Dp = (DEPTH,)
    return {
        "x_prompt": nrm((BATCH, SEQ, D_MODEL), 1.0),
        "x_sample": nrm((DEC_BATCH, DEC_SEQ, D_MODEL), 1.0),
        "cache_kv": nrm(Dp + (n_pool, PAGE_SIZE, 4, G, hd), 1.0),
        "cache_win": nrm(Dp + (DEC_BATCH, win_buf, 2, G, hd), 1.0),
        "state_rwkv": nrm(Dp + (DEC_BATCH, RW_HEADS, RW_HEAD_DIM, RW_HEAD_DIM), 0.3),
        "state_shift": nrm(Dp + (DEC_BATCH, RW_PROJ), 1.0),
        "page_table": jax.random.permutation(nk(), n_pool)[:DEC_BATCH * n_pages].reshape(DEC_BATCH, n_pages).astype(jnp.int32),
        "ffn1_pre_g": gain(Dp + (D_MODEL,)),
        "ffn1_post_g": gain(Dp + (D_MODEL,)),
        "ffn1_w1": nrm(Dp + (D_MODEL, D_FF), D_MODEL ** -0.5),
        "ffn1_w3": nrm(Dp + (D_MODEL, D_FF), D_MODEL ** -0.5),
        "ffn1_w2": nrm(Dp + (D_FF, D_MODEL), D_FF ** -0.5),
        "mix_pre_g": gain(Dp + (D_MODEL,)),
        "mix_post_g": gain(Dp + (D_MODEL,)),
        "w_in": nrm(Dp + (D_MODEL, IN_COLS), D_MODEL ** -0.5),
        "rw_mu": jax.random.uniform(nk(), Dp + (RW_PROJ,), f32),
        "rw_w0": -0.6 + nrm(Dp + (RW_DIM,), 0.5),
        "rw_w2": nrm(Dp + (RW_LORA_W, RW_DIM), 0.5 * RW_LORA_W ** -0.5),
        "rw_a0": nrm(Dp + (RW_DIM,), 0.3),
        "rw_a2": nrm(Dp + (RW_LORA_A, RW_DIM), 0.5 * RW_LORA_A ** -0.5),
        "rw_g2": nrm(Dp + (RW_LORA_G, RW_DIM), RW_LORA_G ** -0.5),
        "rw_k_k": 0.85 + nrm(Dp + (RW_DIM,), 0.05),
        "rw_k_a": 1.0 + nrm(Dp + (RW_DIM,), 0.05),
        "rw_r_k": nrm(Dp + (RW_HEADS, RW_HEAD_DIM), 0.1),
        "rw_lnx_w": gain(Dp + (RW_DIM,)),
        "rw_lnx_b": nrm(Dp + (RW_DIM,), 0.02),
        "cmp_pe": nrm(Dp + (CMP_BLOCK, 2, hd), 0.1),
        "cmp_w1": nrm(Dp + (2, CMP_BLOCK * hd, CMP_HID), (CMP_BLOCK * hd) ** -0.5),
        "cmp_w2": nrm(Dp + (2, CMP_HID, hd), CMP_HID ** -0.5),
        "w_br_rw": nrm(Dp + (RW_DIM, D_MODEL), RW_DIM ** -0.5),
        "w_br_nsa": nrm(Dp + (NSA_DIM, D_MODEL), NSA_DIM ** -0.5),
        "w_out": nrm(Dp + (D_MODEL, D_MODEL), D_MODEL ** -0.5),
        "ffn2_pre_g": gain(Dp + (D_MODEL,)),
        "ffn2_post_g": gain(Dp + (D_MODEL,)),
        "ffn2_w1": nrm(Dp + (D_MODEL, D_FF), D_MODEL ** -0.5),
        "ffn2_w3": nrm(Dp + (D_MODEL, D_FF), D_MODEL ** -0.5),
        "ffn2_w2": nrm(Dp + (D_FF, D_MODEL), D_FF ** -0.5),
        "rel_bias": nrm((N_BUCKETS, NSA_HEADS), 0.5),
    }


def reference(x_prompt, x_sample, cache_kv, cache_win, state_rwkv, state_shift, page_table,
              ffn1_pre_g, ffn1_post_g, ffn1_w1, ffn1_w3, ffn1_w2, mix_pre_g, mix_post_g, w_in,
              rw_mu, rw_w0, rw_w2, rw_a0, rw_a2, rw_g2, rw_k_k, rw_k_a, rw_r_k, rw_lnx_w, rw_lnx_b,
              cmp_pe, cmp_w1, cmp_w2, w_br_rw, w_br_nsa, w_out,
              ffn2_pre_g, ffn2_post_g, ffn2_w1, ffn2_w3, ffn2_w2, rel_bias):
    G, hd = NSA_KV_HEADS, NSA_HEAD_DIM
    b_p = x_prompt.shape[0]
    b_s, n_pages = page_table.shape
    past_len = n_pages * cache_kv.shape[2]
    y_prompt, y_sample = x_prompt, x_sample
    kvp, kvs, wp, ws, rp, rs, sp, ss = [], [], [], [], [], [], [], []
    for l in range(DEPTH):
        lw = tuple(w[l] for w in (ffn1_pre_g, ffn1_post_g, ffn1_w1, ffn1_w3, ffn1_w2, mix_pre_g, mix_post_g, w_in,
                                  rw_mu, rw_w0, rw_w2, rw_a0, rw_a2, rw_g2, rw_k_k, rw_k_a, rw_r_k, rw_lnx_w, rw_lnx_b,
                                  cmp_pe, cmp_w1, cmp_w2, w_br_rw, w_br_nsa, w_out,
                                  ffn2_pre_g, ffn2_post_g, ffn2_w1, ffn2_w3, ffn2_w2))
        y_prompt, kv_p, win_p, rw_p, sh_p = hybrid_layer(
            y_prompt, 0,
            jnp.zeros((b_p, 0, 4, G, hd), x_prompt.dtype),
            jnp.zeros((b_p, 0, 2, G, hd), x_prompt.dtype),
            jnp.zeros((b_p, RW_HEADS, RW_HEAD_DIM, RW_HEAD_DIM), jnp.float32),
            jnp.zeros((b_p, RW_PROJ), x_prompt.dtype), lw, rel_bias)
        past_kv = cache_kv[l][page_table].reshape(b_s, past_len, 4, G, hd)
        y_sample, kv_s, win_s, rw_s, sh_s = hybrid_layer(
            y_sample, past_len, past_kv, cache_win[l], state_rwkv[l], state_shift[l], lw, rel_bias)
        kvp.append(kv_p); kvs.append(kv_s); wp.append(win_p); ws.append(win_s)
        rp.append(rw_p); rs.append(rw_s); sp.append(sh_p); ss.append(sh_s)
    return (y_prompt, y_sample, jnp.stack(kvp), jnp.stack(kvs), jnp.stack(wp), jnp.stack(ws),
            jnp.stack(rp), jnp.stack(rs), jnp.stack(sp), jnp.stack(ss))
```

```python
import functools
import math

import jax
import jax.numpy as jnp
from jax import lax
from jax.experimental import pallas as pl
from jax.experimental.pallas import tpu as pltpu

F32 = jnp.float32
BF16 = jnp.bfloat16

RMS_EPS = 1e-6
RW_HEADS = 16
RW_HEAD_DIM = 64
RW_DIM = RW_HEADS * RW_HEAD_DIM
RW_LORA_W = 64
RW_LORA_A = 64
RW_LORA_G = 128
RW_PROJ = 3 * RW_DIM + RW_LORA_W + RW_LORA_A + RW_LORA_G
RW_LN_EPS = 64e-5
NSA_HEADS = 8
NSA_KV_HEADS = 2
NSA_HPG = NSA_HEADS // NSA_KV_HEADS
NSA_HEAD_DIM = 128
NSA_DIM = NSA_HEADS * NSA_HEAD_DIM
NSA_KV_DIM = NSA_KV_HEADS * NSA_HEAD_DIM
CMP_BLOCK = 32
CMP_HID = 256
SEL_BLOCK = 64
TOP_N = 16
WINDOW = 512
N_BUCKETS = 32
REL_MAX_EXACT = 16
REL_MAX_DIST = 1024
Q_BLOCK = 128
NEG_BIG = -1e30
FORCE_BONUS = 1e4
NEVER = -3e38
LANES = 128
VMEM_LIMIT = 56 * 1024 * 1024

PB_Q = 0
PB_KV = NSA_DIM
PB_SEL = PB_KV + 2 * NSA_KV_DIM
PB_WIN = PB_KV + 4 * NSA_KV_DIM
PB_GATE = PB_WIN + 2 * NSA_KV_DIM
PB_COLS = PB_GATE + LANES


def _bucket_thresholds():
    thr = list(range(1, REL_MAX_EXACT + 1))
    n_log = N_BUCKETS - REL_MAX_EXACT
    ratio = REL_MAX_DIST // REL_MAX_EXACT
    n = REL_MAX_EXACT
    for k in range(1, n_log):
        while n ** n_log < REL_MAX_EXACT ** n_log * ratio ** k:
            n += 1
        thr.append(n)
    return thr


BUCKET_THR = _bucket_thresholds()
FAR_DIST = BUCKET_THR[-1]


def _cparams(sem):
    return pltpu.CompilerParams(dimension_semantics=sem, vmem_limit_bytes=VMEM_LIMIT)


def _rms(x, g):
    ms = jnp.mean(x * x, axis=-1, keepdims=True)
    return x * lax.rsqrt(ms + RMS_EPS) * g


def _dot(a, b):
    return jnp.dot(a, b, preferred_element_type=F32)


def _dot_nt(a, b):
    return lax.dot_general(a, b, (((1,), (1,)), ((), ())), preferred_element_type=F32)


def _bias_for_heads(dist, rb_ref, heads):
    n = jnp.maximum(dist, 0)
    reach = [n >= t for t in BUCKET_THR]
    out = []
    for h in heads:
        val = jnp.full(dist.shape, rb_ref[0, h], F32)
        for b, m in enumerate(reach):
            val = jnp.where(m, rb_ref[b + 1, h], val)
        out.append(val)
    return out


def _softmax_update(carry, q, kt, vt, bias, mask, scale):
    m_run, l_run, acc = carry
    s = _dot_nt(q, kt) * scale + bias
    s = jnp.where(mask, s, NEG_BIG)
    m_new = jnp.maximum(m_run, jnp.max(s, axis=-1, keepdims=True))
    alpha = jnp.exp(m_run - m_new)
    e = jnp.where(mask, jnp.exp(s - m_new), 0.0)
    l_new = alpha * l_run + jnp.sum(e, axis=-1, keepdims=True)
    acc = alpha * acc + _dot(e.astype(BF16), vt)
    return m_new, l_new, acc


def _softmax_init(rows):
    return (jnp.full((rows, 1), NEG_BIG, F32), jnp.zeros((rows, 1), F32),
            jnp.zeros((rows, NSA_HEAD_DIM), F32))


def _softmax_finish(carry):
    _, l_run, acc = carry
    return acc / jnp.maximum(l_run, 1e-30)


def _stack_heads(x):
    return jnp.concatenate([x] * NSA_HPG, axis=0)


def _topk_mask(val, k):
    lane = lax.broadcasted_iota(jnp.int32, val.shape, 1)
    width = val.shape[1]
    chosen = jnp.zeros(val.shape, F32)
    for _ in range(k):
        top = jnp.max(val, axis=-1, keepdims=True)
        first = jnp.min(jnp.where(val == top, lane, width), axis=-1, keepdims=True)
        hit = lane == first
        chosen = jnp.where(hit, 1.0, chosen)
        val = jnp.where(hit, NEVER, val)
    return chosen


def _pair_sum(x):
    parts = []
    for c in range(x.shape[1] // LANES):
        blk = x[:, c * LANES:(c + 1) * LANES]
        parts.append(blk + pltpu.roll(blk, LANES - 1, 1))
    return parts[0] if len(parts) == 1 else jnp.concatenate(parts, axis=1)


def _cmp_branch(qg, kc_k, kc_v, bias, visible, scale):
    s = _dot_nt(qg, kc_k) * scale + bias
    s = jnp.where(visible, s, NEG_BIG)
    e = jnp.where(visible, jnp.exp(s - jnp.max(s, axis=-1, keepdims=True)), 0.0)
    p = e / jnp.maximum(jnp.sum(e, axis=-1, keepdims=True), 1e-30)
    return _dot(p.astype(BF16), kc_v), p


def _selection_values(p, qb, pos, n_sel):
    imp = p[0:qb]
    for h in range(1, NSA_HPG):
        imp = imp + p[h * qb:(h + 1) * qb]
    imp = _pair_sum(imp)
    lane = lax.broadcasted_iota(jnp.int32, imp.shape, 1)
    sel_id = lane >> 1
    cur = pos >> 6
    forced = (sel_id == 0) | (sel_id == cur) | (sel_id == cur - 1)
    val = jnp.where(forced, imp + FORCE_BONUS, imp)
    val = jnp.where(sel_id * SEL_BLOCK <= pos, val, NEG_BIG)
    return jnp.where(((lane & 1) == 0) & (sel_id < n_sel), val, NEVER)


def _expand_selection(chosen_bf16, tile):
    width = chosen_bf16.shape[1]
    n_idx = lax.broadcasted_iota(jnp.int32, (width, LANES), 0)
    k_idx = lax.broadcasted_iota(jnp.int32, (width, LANES), 1)
    target = 4 * tile + 2 * (k_idx >> 6)
    expand = jnp.where(n_idx == target, 1.0, 0.0).astype(BF16)
    return _dot(chosen_bf16, expand) > 0.5


def _ffn_body(x_ref, pre_ref, post_ref, w1_ref, w3_ref, w2_ref, o_ref, h_ref, acc_ref):
    j = pl.program_id(1)

    @pl.when(j == 0)
    def _():
        h_ref[...] = _rms(x_ref[...], pre_ref[...]).astype(BF16)
        acc_ref[...] = jnp.zeros_like(acc_ref)

    h = h_ref[...]
    a = _dot(h, w1_ref[...])
    b = _dot(h, w3_ref[...])
    u = (a * jax.nn.sigmoid(a)) * b
    acc_ref[...] += _dot(u.astype(BF16), w2_ref[...])

    @pl.when(j == pl.num_programs(1) - 1)
    def _():
        o_ref[...] = x_ref[...] + 0.5 * _rms(acc_ref[...], post_ref[...])


def _ffn(x, pre_g, post_g, w1, w3, w2, tm, tf):
    m, d = x.shape
    f = w1.shape[1]
    return pl.pallas_call(
        _ffn_body,
        grid=(m // tm, f // tf),
        in_specs=[
            pl.BlockSpec((tm, d), lambda i, j: (i, 0)),
            pl.BlockSpec((1, d), lambda i, j: (0, 0)),
            pl.BlockSpec((1, d), lambda i, j: (0, 0)),
            pl.BlockSpec((d, tf), lambda i, j: (0, j)),
            pl.BlockSpec((d, tf), lambda i, j: (0, j)),
            pl.BlockSpec((tf, d), lambda i, j: (j, 0)),
        ],
        out_specs=pl.BlockSpec((tm, d), lambda i, j: (i, 0)),
        out_shape=jax.ShapeDtypeStruct((m, d), F32),
        scratch_shapes=[pltpu.VMEM((tm, d), BF16), pltpu.VMEM((tm, d), F32)],
        compiler_params=_cparams(("parallel", "arbitrary")),
    )(x, pre_g.reshape(1, d), post_g.reshape(1, d), w1, w3, w2)


def _norm_matmul_body(x_ref, g_ref, w_ref, o_ref, h_ref):
    @pl.when(pl.program_id(1) == 0)
    def _():
        h_ref[...] = _rms(x_ref[...], g_ref[...]).astype(BF16)

    o_ref[...] = _dot(h_ref[...], w_ref[...])


def _norm_matmul(x, g, w, tm, tn):
    m, d = x.shape
    n = w.shape[1]
    return pl.pallas_call(
        _norm_matmul_body,
        grid=(m // tm, n // tn),
        in_specs=[
            pl.BlockSpec((tm, d), lambda i, j: (i, 0)),
            pl.BlockSpec((1, d), lambda i, j: (0, 0)),
            pl.BlockSpec((d, tn), lambda i, j: (0, j)),
        ],
        out_specs=pl.BlockSpec((tm, tn), lambda i, j: (i, j)),
        out_shape=jax.ShapeDtypeStruct((m, n), F32),
        scratch_shapes=[pltpu.VMEM((tm, d), BF16)],
        compiler_params=_cparams(("parallel", "arbitrary")),
    )(x, g.reshape(1, d), w)


def _rwkv_pre_body(p_ref, prev_ref, mu_ref, w0_ref, ww_ref, a0_ref, wa_ref, g2_ref, kk_ref, ka_ref,
                   pre_ref, g_ref, buf_ref):
    tt = p_ref.shape[1]
    p = p_ref[0]
    buf_ref[8:8 + tt, :] = p
    buf_ref[7:8, :] = prev_ref[0, 0]
    p_prev = buf_ref[7:7 + tt, :]
    xs = p + (p_prev - p) * mu_ref[...]
    d = RW_DIM
    r = xs[:, 0:d]
    k = xs[:, d:2 * d]
    v = xs[:, 2 * d:3 * d]
    lora_in = xs[:, 3 * d:3 * d + RW_LORA_W + RW_LORA_A]
    gd = xs[:, 3 * d + RW_LORA_W + RW_LORA_A:]
    z = -(w0_ref[...] + _dot(jnp.tanh(lora_in).astype(BF16), ww_ref[...]))
    softplus = jnp.maximum(z, 0.0) + jnp.log(1.0 + jnp.exp(-jnp.abs(z)))
    decay = jnp.exp(-jnp.exp(-softplus - 0.5))
    a = jax.nn.sigmoid(a0_ref[...] + _dot(lora_in.astype(BF16), wa_ref[...]))
    g_ref[0] = _dot(jax.nn.sigmoid(gd).astype(BF16), g2_ref[...])
    pre_ref[0, :, 0:d] = r
    pre_ref[0, :, d:2 * d] = k * (1.0 + (a - 1.0) * ka_ref[...])
    pre_ref[0, :, 2 * d:3 * d] = v
    pre_ref[0, :, 3 * d:4 * d] = decay
    pre_ref[0, :, 4 * d:5 * d] = k * kk_ref[...]
    pre_ref[0, :, 5 * d:6 * d] = a


def _rwkv_pre(p3, prev, mu, w0, ww, a0, wa, g2, k_k, k_a, tt):
    b, t, c = p3.shape
    d = RW_DIM
    row = lambda n: pl.BlockSpec((1, n), lambda i, j: (0, 0))
    full = lambda s: pl.BlockSpec(s, lambda i, j: (0, 0))
    return pl.pallas_call(
        _rwkv_pre_body,
        grid=(b, t // tt),
        in_specs=[
            pl.BlockSpec((1, tt, c), lambda i, j: (i, j, 0)),
            pl.BlockSpec((1, 1, 1, c), lambda i, j: (i, j, 0, 0)),
            row(c), row(d), full(ww.shape), row(d), full(wa.shape), full(g2.shape), row(d), row(d),
        ],
        out_specs=[
            pl.BlockSpec((1, tt, 6 * d), lambda i, j: (i, j, 0)),
            pl.BlockSpec((1, tt, d), lambda i, j: (i, j, 0)),
        ],
        out_shape=[jax.ShapeDtypeStruct((b, t, 6 * d), F32), jax.ShapeDtypeStruct((b, t, d), F32)],
        scratch_shapes=[pltpu.VMEM((tt + 8, c), F32)],
        compiler_params=_cparams(("parallel", "arbitrary")),
    )(p3, prev, mu.reshape(1, c), w0.reshape(1, d), ww, a0.reshape(1, d), wa, g2,
      k_k.reshape(1, d), k_a.reshape(1, d))


def _rwkv_scan_body(in_ref, s0_ref, lnw_ref, lnb_ref, rk_ref, y_ref, s_ref, kk_ref, b_ref):
    n = RW_HEAD_DIM

    @pl.when(pl.program_id(1) == 0)
    def _():
        s_ref[...] = s0_ref[...]

    def step(t, carry):
        kk_raw = in_ref[t, 4 * n:5 * n, :]
        norm = jnp.sqrt(jnp.sum(kk_raw * kk_raw, axis=0, keepdims=True))
        kk = kk_raw / jnp.maximum(norm, 1e-12)
        kk_ref[...] = kk
        b_ref[...] = kk * in_ref[t, 5 * n:6 * n, :]
        sk = s_ref[0] * kk_ref[0:1, :]
        for j in range(1, n):
            sk = sk + s_ref[j] * kk_ref[j:j + 1, :]
        v = in_ref[t, 2 * n:3 * n, :]
        y = None
        for j in range(n):
            r_j = in_ref[t, pl.ds(j, 1), :]
            k_j = in_ref[t, pl.ds(n + j, 1), :]
            w_j = in_ref[t, pl.ds(3 * n + j, 1), :]
            s_j = s_ref[j] * w_j - sk * b_ref[j:j + 1, :] + v * k_j
            s_ref[j] = s_j
            y = s_j * r_j if y is None else y + s_j * r_j
        mean = jnp.sum(y, axis=0, keepdims=True) * (1.0 / n)
        yc = y - mean
        var = jnp.sum(yc * yc, axis=0, keepdims=True) * (1.0 / n)
        y_norm = yc * lax.rsqrt(var + RW_LN_EPS) * lnw_ref[...] + lnb_ref[...]
        rkr = in_ref[t, 0:n, :] * in_ref[t, n:2 * n, :] * rk_ref[...]
        y_ref[t] = y_norm + jnp.sum(rkr, axis=0, keepdims=True) * v
        return carry

    lax.fori_loop(0, in_ref.shape[0], step, 0)


def _rwkv_scan(scan_in, s0, lnw, lnb, rk, tc):
    t, rows, lanes = scan_in.shape
    n = RW_HEAD_DIM
    lb = min(lanes, LANES)
    tab = pl.BlockSpec((n, lb), lambda l, i: (0, 0))
    return pl.pallas_call(
        _rwkv_scan_body,
        grid=(lanes // lb, t // tc),
        in_specs=[
            pl.BlockSpec((tc, rows, lb), lambda l, i: (i, 0, l)),
            pl.BlockSpec((n, n, lb), lambda l, i: (0, 0, l)),
            tab, tab, tab,
        ],
        out_specs=[
            pl.BlockSpec((tc, n, lb), lambda l, i: (i, 0, l)),
            pl.BlockSpec((n, n, lb), lambda l, i: (0, 0, l)),
        ],
        out_shape=[jax.ShapeDtypeStruct((t, n, lanes), F32), jax.ShapeDtypeStruct((n, n, lanes), F32)],
        scratch_shapes=[pltpu.VMEM((n, lb), F32), pltpu.VMEM((n, lb), F32)],
        compiler_params=_cparams(("parallel", "arbitrary")),
    )(scan_in, s0, lnw, lnb, rk)


def _gather_body(pt_ref, x_ref, cmp_ref, sel_ref):
    half = 2 * NSA_KV_DIM
    cmp_ref[0] = x_ref[0, :, :half]
    sel_ref[0] = x_ref[0, :, half:].astype(BF16)


def _gather_pages(page_table, pool):
    bs, n_pages = page_table.shape
    _, page, cols = pool.shape
    half = cols // 2
    return pl.pallas_call(
        _gather_body,
        grid_spec=pltpu.PrefetchScalarGridSpec(
            num_scalar_prefetch=1,
            grid=(bs, n_pages),
            in_specs=[pl.BlockSpec((1, page, cols), lambda b, p, pt: (pt[b, p], 0, 0))],
            out_specs=[pl.BlockSpec((1, page, half), lambda b, p, pt: (b, p, 0)),
                       pl.BlockSpec((1, page, half), lambda b, p, pt: (b, p, 0))],
        ),
        out_shape=[jax.ShapeDtypeStruct((bs, n_pages * page, half), F32),
                   jax.ShapeDtypeStruct((bs, n_pages * page, half), BF16)],
        compiler_params=_cparams(("parallel", "arbitrary")),
    )(page_table, pool)


def _cmp_mlp_body(x_ref, pe_ref, w1_ref, w2_ref, o_ref, *, row_stride):
    hd = NSA_HEAD_DIM
    for kg in range(2 * NSA_KV_HEADS):
        ch = kg // NSA_KV_HEADS
        acc = None
        for r in range(CMP_BLOCK):
            xr = x_ref[:, r * row_stride + kg * hd:r * row_stride + (kg + 1) * hd]
            xr = (xr + pe_ref[r:r + 1, kg * hd:(kg + 1) * hd]).astype(BF16)
            part = _dot(xr, w1_ref[ch, r * hd:(r + 1) * hd, :])
            acc = part if acc is None else acc + part
        hid = jax.nn.gelu(acc, approximate=True)
        o_ref[:, kg * hd:(kg + 1) * hd] = _dot(hid.astype(BF16), w2_ref[ch])


def _cmp_mlp(blocks, pe_rows, w1, w2, row_stride, nb):
    n_blocks, width = blocks.shape
    out_w = 2 * NSA_KV_DIM
    return pl.pallas_call(
        functools.partial(_cmp_mlp_body, row_stride=row_stride),
        grid=(n_blocks // nb,),
        in_specs=[
            pl.BlockSpec((nb, width), lambda i: (i, 0)),
            pl.BlockSpec(pe_rows.shape, lambda i: (0, 0)),
            pl.BlockSpec(w1.shape, lambda i: (0, 0, 0)),
            pl.BlockSpec(w2.shape, lambda i: (0, 0, 0)),
        ],
        out_specs=pl.BlockSpec((nb, out_w), lambda i: (i, 0)),
        out_shape=jax.ShapeDtypeStruct((n_blocks, out_w), F32),
        compiler_params=_cparams(("parallel",)),
    )(blocks, pe_rows, w1, w2)


N_BIAS_TILES = -(-(FAR_DIST + LANES - 1) // LANES) + 1


def _bias_tiles_body(rb_ref, o_ref):
    h = pl.program_id(0)
    iq = lax.broadcasted_iota(jnp.int32, (Q_BLOCK, LANES), 0)
    ik = lax.broadcasted_iota(jnp.int32, (Q_BLOCK, LANES), 1)
    for m in range(N_BIAS_TILES):
        dist = jnp.maximum(m * LANES + iq - ik, 0)
        val = jnp.full(dist.shape, rb_ref[0, h], F32)
        for b, t in enumerate(BUCKET_THR):
            val = jnp.where(dist >= t, rb_ref[b + 1, h], val)
        o_ref[0, m] = val


def _bias_tiles(rel_bias):
    return pl.pallas_call(
        _bias_tiles_body,
        grid=(NSA_HEADS,),
        in_specs=[pl.BlockSpec(memory_space=pltpu.SMEM)],
        out_specs=pl.BlockSpec((1, N_BIAS_TILES, Q_BLOCK, LANES), lambda h: (h, 0, 0, 0)),
        out_shape=jax.ShapeDtypeStruct((NSA_HEADS, N_BIAS_TILES, Q_BLOCK, LANES), F32),
        compiler_params=_cparams(("arbitrary",)),
    )(rel_bias)


def _nsa_prompt_body(rb_ref, q_ref, gate_ref, sel_ref, win_ref, kc_ref, bt_ref, y_ref, selb_ref, winb_ref,
                     *, n_sel):
    hd = NSA_HEAD_DIM
    qb = Q_BLOCK
    i = pl.program_id(1)
    scale = hd ** -0.5

    @pl.when(i == 0)
    def _():
        selb_ref[...] = sel_ref[0].astype(BF16)
        winb_ref[...] = win_ref[0].astype(BF16)

    q_bf = q_ref[0].astype(BF16)
    gates = jax.nn.sigmoid(gate_ref[0])
    iq = lax.broadcasted_iota(jnp.int32, (qb, LANES), 0)
    ik = lax.broadcasted_iota(jnp.int32, (qb, LANES), 1)
    pos = i * qb + iq
    d_cmp = pos - (ik * CMP_BLOCK + CMP_BLOCK - 1)
    bias_cmp = _bias_for_heads(d_cmp, rb_ref, range(NSA_HEADS))
    vis_cmp = _stack_heads(d_cmp >= 0)
    top_n = min(TOP_N, n_sel)

    for g in range(NSA_KV_HEADS):
        heads = [g * NSA_HPG + h for h in range(NSA_HPG)]
        qg = jnp.concatenate([q_bf[:, hh * hd:(hh + 1) * hd] for hh in heads], axis=0)
        kc_k = kc_ref[0, :, g * hd:(g + 1) * hd].astype(BF16)
        kc_v = kc_ref[0, :, NSA_KV_DIM + g * hd:NSA_KV_DIM + (g + 1) * hd].astype(BF16)
        o_cmp, p_cmp = _cmp_branch(qg, kc_k, kc_v, jnp.concatenate([bias_cmp[hh] for hh in heads], axis=0),
                                   vis_cmp, scale)
        chosen = _topk_mask(_selection_values(p_cmp, qb, pos, n_sel), top_n).astype(BF16)

        def sel_step(j, carry):
            row0 = pl.multiple_of(j * LANES, LANES)
            kt = selb_ref[pl.ds(row0, LANES), g * hd:(g + 1) * hd]
            vt = selb_ref[pl.ds(row0, LANES), NSA_KV_DIM + g * hd:NSA_KV_DIM + (g + 1) * hd]
            causal = (i - j) * LANES + iq - ik >= 0
            mask = _stack_heads(_expand_selection(chosen, j) & causal)
            m = jnp.minimum(i - j, N_BIAS_TILES - 1)
            bias = jnp.concatenate([bt_ref[hh, m] for hh in heads], axis=0)
            return _softmax_update(carry, qg, kt, vt, bias, mask, scale)

        o_sel = _softmax_finish(lax.fori_loop(0, i + 1, sel_step, _softmax_init(NSA_HPG * qb)))

        carry = _softmax_init(NSA_HPG * qb)
        for m in range(WINDOW // LANES + 1):
            tile = i - m
            row0 = pl.multiple_of(jnp.maximum(tile, 0) * LANES, LANES)
            kt = winb_ref[pl.ds(row0, LANES), g * hd:(g + 1) * hd]
            vt = winb_ref[pl.ds(row0, LANES), NSA_KV_DIM + g * hd:NSA_KV_DIM + (g + 1) * hd]
            dist = m * LANES + iq - ik
            reach = jnp.where(tile >= 0, WINDOW, 0)
            mask = _stack_heads((dist >= 0) & (dist < reach))
            bias = jnp.concatenate([bt_ref[hh, m] for hh in heads], axis=0)
            carry = _softmax_update(carry, qg, kt, vt, bias, mask, scale)
        o_win = _softmax_finish(carry)

        for h, hh in enumerate(heads):
            rows = slice(h * qb, (h + 1) * qb)
            out = (gates[:, hh:hh + 1] * o_cmp[rows]
                   + gates[:, NSA_HEADS + hh:NSA_HEADS + hh + 1] * o_sel[rows]
                   + gates[:, 2 * NSA_HEADS + hh:2 * NSA_HEADS + hh + 1] * o_win[rows])
            y_ref[0, :, hh * hd:(hh + 1) * hd] = out.astype(BF16)


def _nsa_prompt(rel_bias, pb3, kc, bias_tiles):
    b, t, _ = pb3.shape
    qb = Q_BLOCK
    half = 2 * NSA_KV_DIM
    return pl.pallas_call(
        functools.partial(_nsa_prompt_body, n_sel=t // SEL_BLOCK),
        grid=(b, t // qb),
        in_specs=[
            pl.BlockSpec(memory_space=pltpu.SMEM),
            pl.BlockSpec((1, qb, NSA_DIM), lambda s, i: (s, i, PB_Q // NSA_DIM)),
            pl.BlockSpec((1, qb, LANES), lambda s, i: (s, i, PB_GATE // LANES)),
            pl.BlockSpec((1, t, half), lambda s, i: (s, 0, PB_SEL // half)),
            pl.BlockSpec((1, t, half), lambda s, i: (s, 0, PB_WIN // half)),
            pl.BlockSpec((1,) + kc.shape[1:], lambda s, i: (s, 0, 0)),
            pl.BlockSpec(bias_tiles.shape, lambda s, i: (0, 0, 0, 0)),
        ],
        out_specs=pl.BlockSpec((1, qb, NSA_DIM), lambda s, i: (s, i, 0)),
        out_shape=jax.ShapeDtypeStruct((b, t, NSA_DIM), BF16),
        scratch_shapes=[pltpu.VMEM((t, half), BF16), pltpu.VMEM((t, half), BF16)],
        compiler_params=_cparams(("parallel", "arbitrary")),
    )(rel_bias, pb3, pb3, pb3, pb3, kc, bias_tiles)


SAMPLE_ROWS = 8


def _nsa_sample_body(rb_ref, q_ref, gate_ref, kc_ref, past_ref, new_sel_ref, cwin_ref, new_win_ref, y_ref,
                     *, past_len):
    hd = NSA_HEAD_DIM
    qb = SAMPLE_ROWS
    scale = hd ** -0.5
    n_cmp = kc_ref.shape[1]
    n_past_tiles = past_len // LANES
    n_win_tiles = cwin_ref.shape[1] // LANES
    k_off = past_len - cwin_ref.shape[1]

    q_bf = q_ref[0].astype(BF16)
    gates = jax.nn.sigmoid(gate_ref[0])
    iq = lax.broadcasted_iota(jnp.int32, (qb, LANES), 0)
    ik = lax.broadcasted_iota(jnp.int32, (qb, LANES), 1)
    pos = past_len + iq
    tq_c = lax.broadcasted_iota(jnp.int32, (qb, n_cmp), 0)
    n_c = lax.broadcasted_iota(jnp.int32, (qb, n_cmp), 1)
    pos_c = past_len + tq_c
    d_cmp = pos_c - (n_c * CMP_BLOCK + CMP_BLOCK - 1)
    bias_cmp = _bias_for_heads(d_cmp, rb_ref, range(NSA_HEADS))
    vis_cmp = _stack_heads(d_cmp >= 0)
    n_sel_past = past_len // SEL_BLOCK
    picks = min(TOP_N, n_sel_past + 1) - 1

    for g in range(NSA_KV_HEADS):
        heads = [g * NSA_HPG + h for h in range(NSA_HPG)]
        qg = jnp.concatenate([q_bf[:, hh * hd:(hh + 1) * hd] for hh in heads], axis=0)
        kc_k = kc_ref[0, :, g * hd:(g + 1) * hd].astype(BF16)
        kc_v = kc_ref[0, :, NSA_KV_DIM + g * hd:NSA_KV_DIM + (g + 1) * hd].astype(BF16)
        o_cmp, p_cmp = _cmp_branch(qg, kc_k, kc_v, jnp.concatenate([bias_cmp[hh] for hh in heads], axis=0),
                                   vis_cmp, scale)
        chosen = _topk_mask(_selection_values(p_cmp, qb, pos_c, n_sel_past), picks).astype(BF16)

        def sel_step(j, carry):
            row0 = pl.multiple_of(j * LANES, LANES)
            kt = past_ref[0, pl.ds(row0, LANES), g * hd:(g + 1) * hd]
            vt = past_ref[0, pl.ds(row0, LANES), NSA_KV_DIM + g * hd:NSA_KV_DIM + (g + 1) * hd]
            mask = _stack_heads(_expand_selection(chosen, j))
            dist = pos - (j * LANES + ik)
            bias = jnp.concatenate(_bias_for_heads(dist, rb_ref, heads), axis=0)
            return _softmax_update(carry, qg, kt, vt, bias, mask, scale)

        carry = lax.fori_loop(0, n_past_tiles, sel_step, _softmax_init(NSA_HPG * qb))
        d_new = iq - ik
        bias_new = jnp.concatenate(_bias_for_heads(d_new, rb_ref, heads), axis=0)
        mask_new = _stack_heads(d_new >= 0)
        carry = _softmax_update(carry, qg, new_sel_ref[0, :, g * hd:(g + 1) * hd],
                                new_sel_ref[0, :, NSA_KV_DIM + g * hd:NSA_KV_DIM + (g + 1) * hd],
                                bias_new, mask_new, scale)
        o_sel = _softmax_finish(carry)

        carry = _softmax_init(NSA_HPG * qb)
        for j in range(n_win_tiles):
            kt = cwin_ref[0, j * LANES:(j + 1) * LANES, g * hd:(g + 1) * hd].astype(BF16)
            vt = cwin_ref[0, j * LANES:(j + 1) * LANES,
                          NSA_KV_DIM + g * hd:NSA_KV_DIM + (g + 1) * hd].astype(BF16)
            dist = pos - (k_off + j * LANES + ik)
            mask = _stack_heads((dist >= 0) & (dist < WINDOW))
            bias = jnp.concatenate(_bias_for_heads(dist, rb_ref, heads), axis=0)
            carry = _softmax_update(carry, qg, kt, vt, bias, mask, scale)
        carry = _softmax_update(carry, qg, new_win_ref[0, :, g * hd:(g + 1) * hd],
                                new_win_ref[0, :, NSA_KV_DIM + g * hd:NSA_KV_DIM + (g + 1) * hd],
                                bias_new, mask_new, scale)
        o_win = _softmax_finish(carry)

        for h, hh in enumerate(heads):
            rows = slice(h * qb, (h + 1) * qb)
            out = (gates[:, hh:hh + 1] * o_cmp[rows]
                   + gates[:, NSA_HEADS + hh:NSA_HEADS + hh + 1] * o_sel[rows]
                   + gates[:, 2 * NSA_HEADS + hh:2 * NSA_HEADS + hh + 1] * o_win[rows])
            y_ref[0, :, hh * hd:(hh + 1) * hd] = out.astype(BF16)


def _nsa_sample(rel_bias, q8, gate8, kc, past_sel, new_sel, cwin, new_win, past_len):
    bs = q8.shape[0]
    blk = lambda a: pl.BlockSpec((1,) + a.shape[1:], lambda s: (s, 0, 0))
    return pl.pallas_call(
        functools.partial(_nsa_sample_body, past_len=past_len),
        grid=(bs,),
        in_specs=[pl.BlockSpec(memory_space=pltpu.SMEM), blk(q8), blk(gate8), blk(kc), blk(past_sel),
                  blk(new_sel), blk(cwin), blk(new_win)],
        out_specs=pl.BlockSpec((1, SAMPLE_ROWS, NSA_DIM), lambda s: (s, 0, 0)),
        out_shape=jax.ShapeDtypeStruct((bs, SAMPLE_ROWS, NSA_DIM), BF16),
        compiler_params=_cparams(("parallel",)),
    )(rel_bias, q8, gate8, kc, past_sel, new_sel, cwin, new_win)


def _merge_body(x_ref, yrw_ref, g_ref, ynsa_ref, gates_ref, wrw_ref, wnsa_ref, wout_ref, post_ref, o_ref):
    d = x_ref.shape[1]
    y_rw = (yrw_ref[...] * g_ref[...]).astype(BF16)
    merged = (jax.nn.sigmoid(gates_ref[:, :d]) * _dot(y_rw, wrw_ref[...])
              + jax.nn.sigmoid(gates_ref[:, d:]) * _dot(ynsa_ref[...], wnsa_ref[...]))
    o_ref[...] = x_ref[...] + _rms(_dot(merged.astype(BF16), wout_ref[...]), post_ref[...])


def _merge(x, y_rw, g_rw, y_nsa, gates, w_rw, w_nsa, w_out, post_g, tm):
    m, d = x.shape
    rows = lambda n: pl.BlockSpec((tm, n), lambda i: (i, 0))
    held = lambda a: pl.BlockSpec(a.shape, lambda i: (0, 0), pipeline_mode=pl.Buffered(1))
    return pl.pallas_call(
        _merge_body,
        grid=(m // tm,),
        in_specs=[rows(d), rows(RW_DIM), rows(RW_DIM), rows(NSA_DIM), rows(2 * d),
                  held(w_rw), held(w_nsa), held(w_out), pl.BlockSpec((1, d), lambda i: (0, 0))],
        out_specs=rows(d),
        out_shape=jax.ShapeDtypeStruct((m, d), F32),
        compiler_params=_cparams(("parallel",)),
    )(x, y_rw, g_rw, y_nsa, gates, w_rw, w_nsa, w_out, post_g.reshape(1, d))


def _row_tile(m, want):
    return want if m % want == 0 else m


def _rwkv_mix(p3, shift0, s0, w):
    b, t, c = p3.shape
    n = RW_HEAD_DIM
    tt = 256 if t % 256 == 0 else t
    prev = jnp.concatenate([shift0[:, None, :], p3[:, tt - 1:t - 1:tt]], axis=1).reshape(b, t // tt, 1, c)
    pre, g_rw = _rwkv_pre(p3, prev, w["mu"], w["w0"], w["ww"], w["a0"], w["wa"], w["g2"], w["k_k"], w["k_a"], tt)
    lanes = b * RW_HEADS
    scan_in = pre.reshape(b, t, 6, RW_HEADS, n).transpose(1, 2, 4, 0, 3).reshape(t, 6 * n, lanes)
    s0_t = s0.transpose(3, 2, 0, 1).reshape(n, n, lanes)
    reps = min(lanes, LANES) // RW_HEADS
    per_head = lambda v: jnp.tile(v.reshape(RW_HEADS, n).T, (1, reps))
    tc = 32 if t % 32 == 0 else t
    y_t, s_t = _rwkv_scan(scan_in, s0_t, per_head(w["lnx_w"]), per_head(w["lnx_b"]), per_head(w["r_k"]), tc)
    y = y_t.reshape(t, n, b, RW_HEADS).transpose(2, 0, 3, 1).reshape(b * t, RW_DIM)
    s_fin = s_t.reshape(n, n, b, RW_HEADS).transpose(2, 3, 1, 0)
    return y, g_rw.reshape(b * t, RW_DIM), s_fin


def _layer(x, w, rel_bias, shift0, s0, past):
    b, t, d = x.shape
    m = b * t
    tm = _row_tile(m, 512)
    x1 = _ffn(x.reshape(m, d), w["f1_pre"], w["f1_post"], w["f1_w1"], w["f1_w3"], w["f1_w2"], tm, 512)
    tp = _row_tile(m, 256)
    p_a = _norm_matmul(x1, w["mix_pre"], w["w_in_a"], tp, w["w_in_a"].shape[1])
    p_b = _norm_matmul(x1, w["mix_pre"], w["w_in_b"], tp, w["w_in_b"].shape[1])
    p_c = _norm_matmul(x1, w["mix_pre"], w["w_in_c"], tp, d)

    p3 = p_a.reshape(b, t, RW_PROJ)
    y_rw, g_rw, s_fin = _rwkv_mix(p3, shift0, s0, w)

    pb3 = p_b.reshape(b, t, PB_COLS)
    kv_new = pb3[:, :, PB_KV:PB_WIN]
    win_new = pb3[:, :, PB_WIN:PB_GATE]
    row_w = 4 * NSA_KV_DIM
    if past is None:
        blocks = kv_new.reshape(m // CMP_BLOCK, CMP_BLOCK * row_w)
        kc = _cmp_mlp(blocks, w["cmp_pe"], w["cmp_w1"], w["cmp_w2"], row_w, min(64, blocks.shape[0]))
        n_cmp = t // CMP_BLOCK
        kc = jnp.pad(kc.reshape(b, n_cmp, 2 * NSA_KV_DIM), ((0, 0), (0, -n_cmp % LANES), (0, 0)))
        y_nsa = _nsa_prompt(rel_bias, pb3, kc, _bias_tiles(rel_bias)).reshape(m, NSA_DIM)
        win_out = win_new[:, t - min(WINDOW, t):]
    else:
        page_table, pool, cache_win = past
        past_len = page_table.shape[1] * pool.shape[1]
        past_cmp, past_sel = _gather_pages(page_table, pool.reshape(pool.shape[0], pool.shape[1], row_w))
        half = 2 * NSA_KV_DIM
        blocks = past_cmp.reshape(b * past_len // CMP_BLOCK, CMP_BLOCK * half)
        kc = _cmp_mlp(blocks, w["cmp_pe"], w["cmp_w1"], w["cmp_w2"], half, min(128, blocks.shape[0]))
        n_cmp = past_len // CMP_BLOCK
        kc = jnp.pad(kc.reshape(b, n_cmp, half), ((0, 0), (0, -n_cmp % LANES), (0, 0)))
        pad_q = lambda a, rows: jnp.pad(a, ((0, 0), (0, rows - t), (0, 0)))
        cwin = cache_win.reshape(b, cache_win.shape[1], half)
        y8 = _nsa_sample(rel_bias, pad_q(pb3[:, :, :NSA_DIM], SAMPLE_ROWS), pad_q(pb3[:, :, PB_GATE:], SAMPLE_ROWS),
                         kc, past_sel, pad_q(pb3[:, :, PB_SEL:PB_WIN].astype(BF16), LANES), cwin,
                         pad_q(win_new.astype(BF16), LANES), past_len)
        y_nsa = y8[:, :t].reshape(m, NSA_DIM)
        win_all = jnp.concatenate([cwin, win_new], axis=1)
        win_out = win_all[:, win_all.shape[1] - min(WINDOW, win_all.shape[1]):]

    x2 = _merge(x1, y_rw, g_rw, y_nsa, p_c, w["w_br_rw"], w["w_br_nsa"], w["w_out"], w["mix_post"],
                _row_tile(m, 256))
    y = _ffn(x2, w["f2_pre"], w["f2_post"], w["f2_w1"], w["f2_w3"], w["f2_w2"], tm, 512)
    g, hd = NSA_KV_HEADS, NSA_HEAD_DIM
    return (y.reshape(b, t, d), kv_new.reshape(b, t, 4, g, hd), win_out.reshape(b, -1, 2, g, hd), s_fin, p3[:, -1])


def _prepare_weights(l, ffn1_pre_g, ffn1_post_g, ffn1_w1, ffn1_w3, ffn1_w2, mix_pre_g, mix_post_g, w_in,
                     rw_mu, rw_w0, rw_w2, rw_a0, rw_a2, rw_g2, rw_k_k, rw_k_a, rw_r_k, rw_lnx_w, rw_lnx_b,
                     cmp_pe, cmp_w1, cmp_w2, w_br_rw, w_br_nsa, w_out,
                     ffn2_pre_g, ffn2_post_g, ffn2_w1, ffn2_w3, ffn2_w2):
    d = w_in.shape[1]
    wi = w_in[l]
    c_q = RW_PROJ
    c_kv = c_q + NSA_DIM
    c_gate = c_kv + 6 * NSA_KV_DIM
    c_grw = c_gate + 3 * NSA_HEADS
    w_in_b = jnp.concatenate([wi[:, c_q:c_gate], jnp.pad(wi[:, c_gate:c_grw], ((0, 0), (0, LANES - 3 * NSA_HEADS)))],
                             axis=1)
    zeros_w = jnp.zeros((RW_LORA_A, RW_DIM), F32)
    zeros_a = jnp.zeros((RW_LORA_W, RW_DIM), F32)
    pe = cmp_pe[l]
    pe_rows = jnp.broadcast_to(pe[:, :, None, :], (CMP_BLOCK, 2, NSA_KV_HEADS, NSA_HEAD_DIM))
    return {
        "f1_pre": ffn1_pre_g[l], "f1_post": ffn1_post_g[l],
        "f1_w1": ffn1_w1[l].astype(BF16), "f1_w3": ffn1_w3[l].astype(BF16), "f1_w2": ffn1_w2[l].astype(BF16),
        "mix_pre": mix_pre_g[l], "mix_post": mix_post_g[l],
        "w_in_a": wi[:, :RW_PROJ].astype(BF16), "w_in_b": w_in_b.astype(BF16), "w_in_c": wi[:, c_grw:].astype(BF16),
        "mu": rw_mu[l], "w0": rw_w0[l], "a0": rw_a0[l],
        "ww": jnp.concatenate([rw_w2[l], zeros_w], axis=0).astype(BF16),
        "wa": jnp.concatenate([zeros_a, rw_a2[l]], axis=0).astype(BF16),
        "g2": rw_g2[l].astype(BF16), "k_k": rw_k_k[l], "k_a": rw_k_a[l],
        "r_k": rw_r_k[l].reshape(-1), "lnx_w": rw_lnx_w[l], "lnx_b": rw_lnx_b[l],
        "cmp_pe": pe_rows.reshape(CMP_BLOCK, 2 * NSA_KV_DIM),
        "cmp_w1": cmp_w1[l].astype(BF16), "cmp_w2": cmp_w2[l].astype(BF16),
        "w_br_rw": w_br_rw[l].astype(BF16), "w_br_nsa": w_br_nsa[l].astype(BF16), "w_out": w_out[l].astype(BF16),
        "f2_pre": ffn2_pre_g[l], "f2_post": ffn2_post_g[l],
        "f2_w1": ffn2_w1[l].astype(BF16), "f2_w3": ffn2_w3[l].astype(BF16), "f2_w2": ffn2_w2[l].astype(BF16),
    }


def kernel(x_prompt, x_sample, cache_kv, cache_win, state_rwkv, state_shift, page_table,
           ffn1_pre_g, ffn1_post_g, ffn1_w1, ffn1_w3, ffn1_w2, mix_pre_g, mix_post_g, w_in,
           rw_mu, rw_w0, rw_w2, rw_a0, rw_a2, rw_g2, rw_k_k, rw_k_a, rw_r_k, rw_lnx_w, rw_lnx_b,
           cmp_pe, cmp_w1, cmp_w2, w_br_rw, w_br_nsa, w_out,
           ffn2_pre_g, ffn2_post_g, ffn2_w1, ffn2_w3, ffn2_w2, rel_bias):
    depth = w_in.shape[0]
    b_p = x_prompt.shape[0]
    y_p, y_s = x_prompt, x_sample
    outs = [[] for _ in range(8)]
    for l in range(depth):
        w = _prepare_weights(l, ffn1_pre_g, ffn1_post_g, ffn1_w1, ffn1_w3, ffn1_w2, mix_pre_g, mix_post_g, w_in,
                             rw_mu, rw_w0, rw_w2, rw_a0, rw_a2, rw_g2, rw_k_k, rw_k_a, rw_r_k, rw_lnx_w, rw_lnx_b,
                             cmp_pe, cmp_w1, cmp_w2, w_br_rw, w_br_nsa, w_out,
                             ffn2_pre_g, ffn2_post_g, ffn2_w1, ffn2_w3, ffn2_w2)
        y_p, kv_p, win_p, rw_p, sh_p = _layer(
            y_p, w, rel_bias, jnp.zeros((b_p, RW_PROJ), F32),
            jnp.zeros((b_p, RW_HEADS, RW_HEAD_DIM, RW_HEAD_DIM), F32), None)
        y_s, kv_s, win_s, rw_s, sh_s = _layer(
            y_s, w, rel_bias, state_shift[l], state_rwkv[l], (page_table, cache_kv[l], cache_win[l]))
        for acc, v in zip(outs, (kv_p, kv_s, win_p, win_s, rw_p, rw_s, sh_p, sh_s)):
            acc.append(v)
    return (y_p, y_s) + tuple(jnp.stack(o) for o in outs)
```

```python
import functools
import math

import jax
import jax.numpy as jnp
from jax import lax
from jax.experimental import pallas as pl
from jax.experimental.pallas import tpu as pltpu

F32 = jnp.float32
BF16 = jnp.bfloat16

RMS_EPS = 1e-6
RW_HEADS = 16
RW_HEAD_DIM = 64
RW_DIM = RW_HEADS * RW_HEAD_DIM
RW_LORA_W = 64
RW_LORA_A = 64
RW_LORA_G = 128
RW_PROJ = 3 * RW_DIM + RW_LORA_W + RW_LORA_A + RW_LORA_G
RW_LN_EPS = 64e-5
NSA_HEADS = 8
NSA_KV_HEADS = 2
NSA_HPG = NSA_HEADS // NSA_KV_HEADS
NSA_HEAD_DIM = 128
NSA_DIM = NSA_HEADS * NSA_HEAD_DIM
NSA_KV_DIM = NSA_KV_HEADS * NSA_HEAD_DIM
CMP_BLOCK = 32
CMP_HID = 256
SEL_BLOCK = 64
TOP_N = 16
WINDOW = 512
N_BUCKETS = 32
REL_MAX_EXACT = 16
REL_MAX_DIST = 1024
Q_BLOCK = 128
NEG_BIG = -1e30
FORCE_BONUS = 1e4
NEVER = -3e38
LANES = 128
VMEM_LIMIT = 56 * 1024 * 1024
SCAN_UNROLL = 8
UPDATE_UNROLL = 16

PB_Q = 0
PB_KV = NSA_DIM
PB_SEL = PB_KV + 2 * NSA_KV_DIM
PB_WIN = PB_KV + 4 * NSA_KV_DIM
PB_GATE = PB_WIN + 2 * NSA_KV_DIM
PB_COLS = PB_GATE + LANES


def _bucket_thresholds():
    thr = list(range(1, REL_MAX_EXACT + 1))
    n_log = N_BUCKETS - REL_MAX_EXACT
    ratio = REL_MAX_DIST // REL_MAX_EXACT
    n = REL_MAX_EXACT
    for k in range(1, n_log):
        while n ** n_log < REL_MAX_EXACT ** n_log * ratio ** k:
            n += 1
        thr.append(n)
    return thr


BUCKET_THR = _bucket_thresholds()
FAR_DIST = BUCKET_THR[-1]


def _cparams(sem):
    return pltpu.CompilerParams(dimension_semantics=sem, vmem_limit_bytes=VMEM_LIMIT)


def _rms(x, g):
    ms = jnp.mean(x * x, axis=-1, keepdims=True)
    return x * lax.rsqrt(ms + RMS_EPS) * g


def _dot(a, b):
    return jnp.dot(a, b, preferred_element_type=F32)


def _dot_nt(a, b):
    return lax.dot_general(a, b, (((1,), (1,)), ((), ())), preferred_element_type=F32)


def _bias_for_heads(dist, rb_ref, heads):
    n = jnp.maximum(dist, 0)
    reach = [n >= t for t in BUCKET_THR]
    out = []
    for h in heads:
        val = jnp.full(dist.shape, rb_ref[0, h], F32)
        for b, m in enumerate(reach):
            val = jnp.where(m, rb_ref[b + 1, h], val)
        out.append(val)
    return out


def _softmax_update(carry, q, kt, vt, bias, mask, scale):
    m_run, l_run, acc = carry
    s = _dot_nt(q, kt) * scale + bias
    s = jnp.where(mask, s, NEG_BIG)
    m_new = jnp.maximum(m_run, jnp.max(s, axis=-1, keepdims=True))
    alpha = jnp.exp(m_run - m_new)
    e = jnp.where(mask, jnp.exp(s - m_new), 0.0)
    l_new = alpha * l_run + jnp.sum(e, axis=-1, keepdims=True)
    acc = alpha * acc + _dot(e.astype(BF16), vt)
    return m_new, l_new, acc


def _softmax_init(rows):
    return (jnp.full((rows, 1), NEG_BIG, F32), jnp.zeros((rows, 1), F32),
            jnp.zeros((rows, NSA_HEAD_DIM), F32))


def _softmax_finish(carry):
    _, l_run, acc = carry
    return acc / jnp.maximum(l_run, 1e-30)


def _stack_heads(x):
    return jnp.concatenate([x] * NSA_HPG, axis=0)


def _topk_mask(val, k):
    lane = lax.broadcasted_iota(jnp.int32, val.shape, 1)
    width = val.shape[1]
    chosen = jnp.zeros(val.shape, F32)
    for _ in range(k):
        top = jnp.max(val, axis=-1, keepdims=True)
        first = jnp.min(jnp.where(val == top, lane, width), axis=-1, keepdims=True)
        hit = lane == first
        chosen = jnp.where(hit, 1.0, chosen)
        val = jnp.where(hit, NEVER, val)
    return chosen


def _pair_sum(x):
    parts = []
    for c in range(x.shape[1] // LANES):
        blk = x[:, c * LANES:(c + 1) * LANES]
        parts.append(blk + pltpu.roll(blk, LANES - 1, 1))
    return parts[0] if len(parts) == 1 else jnp.concatenate(parts, axis=1)


def _cmp_branch(qg, kc_k, kc_v, bias, visible, scale):
    s = _dot_nt(qg, kc_k) * scale + bias
    s = jnp.where(visible, s, NEG_BIG)
    e = jnp.where(visible, jnp.exp(s - jnp.max(s, axis=-1, keepdims=True)), 0.0)
    p = e / jnp.maximum(jnp.sum(e, axis=-1, keepdims=True), 1e-30)
    return _dot(p.astype(BF16), kc_v), p


def _selection_values(p, qb, pos, n_sel):
    imp = p[0:qb]
    for h in range(1, NSA_HPG):
        imp = imp + p[h * qb:(h + 1) * qb]
    imp = _pair_sum(imp)
    lane = lax.broadcasted_iota(jnp.int32, imp.shape, 1)
    sel_id = lane >> 1
    cur = pos >> 6
    forced = (sel_id == 0) | (sel_id == cur) | (sel_id == cur - 1)
    val = jnp.where(forced, imp + FORCE_BONUS, imp)
    val = jnp.where(sel_id * SEL_BLOCK <= pos, val, NEG_BIG)
    return jnp.where(((lane & 1) == 0) & (sel_id < n_sel), val, NEVER)


def _expand_selection(chosen_bf16, tile):
    width = chosen_bf16.shape[1]
    n_idx = lax.broadcasted_iota(jnp.int32, (width, LANES), 0)
    k_idx = lax.broadcasted_iota(jnp.int32, (width, LANES), 1)
    target = 4 * tile + 2 * (k_idx >> 6)
    expand = jnp.where(n_idx == target, 1.0, 0.0).astype(BF16)
    return _dot(chosen_bf16, expand) > 0.5


def _ffn_body(x_ref, pre_ref, post_ref, w1_ref, w3_ref, w2_ref, o_ref, h_ref, acc_ref):
    j = pl.program_id(1)

    @pl.when(j == 0)
    def _():
        h_ref[...] = _rms(x_ref[...], pre_ref[...]).astype(BF16)
        acc_ref[...] = jnp.zeros_like(acc_ref)

    h = h_ref[...]
    a = _dot(h, w1_ref[...])
    b = _dot(h, w3_ref[...])
    u = (a * jax.nn.sigmoid(a)) * b
    acc_ref[...] += _dot(u.astype(BF16), w2_ref[...])

    @pl.when(j == pl.num_programs(1) - 1)
    def _():
        o_ref[...] = x_ref[...] + 0.5 * _rms(acc_ref[...], post_ref[...])


def _ffn(x, pre_g, post_g, w1, w3, w2, tm, tf):
    m, d = x.shape
    f = w1.shape[1]
    return pl.pallas_call(
        _ffn_body,
        grid=(m // tm, f // tf),
        in_specs=[
            pl.BlockSpec((tm, d), lambda i, j: (i, 0)),
            pl.BlockSpec((1, d), lambda i, j: (0, 0)),
            pl.BlockSpec((1, d), lambda i, j: (0, 0)),
            pl.BlockSpec((d, tf), lambda i, j: (0, j)),
            pl.BlockSpec((d, tf), lambda i, j: (0, j)),
            pl.BlockSpec((tf, d), lambda i, j: (j, 0)),
        ],
        out_specs=pl.BlockSpec((tm, d), lambda i, j: (i, 0)),
        out_shape=jax.ShapeDtypeStruct((m, d), F32),
        scratch_shapes=[pltpu.VMEM((tm, d), BF16), pltpu.VMEM((tm, d), F32)],
        compiler_params=_cparams(("parallel", "arbitrary")),
        name="ffn",
    )(x, pre_g.reshape(1, d), post_g.reshape(1, d), w1, w3, w2)


def _norm_matmul_body(x_ref, g_ref, w_ref, o_ref, h_ref):
    @pl.when(pl.program_id(1) == 0)
    def _():
        h_ref[...] = _rms(x_ref[...], g_ref[...]).astype(BF16)

    o_ref[...] = _dot(h_ref[...], w_ref[...])


def _norm_matmul(x, g, w, tm, name):
    m, d = x.shape
    n = w.shape[1]
    return pl.pallas_call(
        _norm_matmul_body,
        grid=(m // tm, 1),
        in_specs=[
            pl.BlockSpec((tm, d), lambda i, j: (i, 0)),
            pl.BlockSpec((1, d), lambda i, j: (0, 0)),
            pl.BlockSpec((d, n), lambda i, j: (0, 0), pipeline_mode=pl.Buffered(1)),
        ],
        out_specs=pl.BlockSpec((tm, n), lambda i, j: (i, 0)),
        out_shape=jax.ShapeDtypeStruct((m, n), F32),
        scratch_shapes=[pltpu.VMEM((tm, d), BF16)],
        compiler_params=_cparams(("parallel", "arbitrary")),
        name=name,
    )(x, g.reshape(1, d), w)


def _rwkv_pre_body(p_ref, prev_ref, mu_ref, w0_ref, ww_ref, a0_ref, wa_ref, g2_ref, kk_ref, ka_ref,
                   pre_ref, g_ref, buf_ref):
    tt = p_ref.shape[1]
    p = p_ref[0]
    buf_ref[8:8 + tt, :] = p
    buf_ref[7:8, :] = prev_ref[0, 0]
    p_prev = buf_ref[7:7 + tt, :]
    xs = p + (p_prev - p) * mu_ref[...]
    d = RW_DIM
    r = xs[:, 0:d]
    k = xs[:, d:2 * d]
    v = xs[:, 2 * d:3 * d]
    lora_in = xs[:, 3 * d:3 * d + RW_LORA_W + RW_LORA_A]
    gd = xs[:, 3 * d + RW_LORA_W + RW_LORA_A:]
    z = -(w0_ref[...] + _dot(jnp.tanh(lora_in).astype(BF16), ww_ref[...]))
    softplus = jnp.maximum(z, 0.0) + jnp.log(1.0 + jnp.exp(-jnp.abs(z)))
    decay = jnp.exp(-jnp.exp(-softplus - 0.5))
    a = jax.nn.sigmoid(a0_ref[...] + _dot(lora_in.astype(BF16), wa_ref[...]))
    g_ref[0] = _dot(jax.nn.sigmoid(gd).astype(BF16), g2_ref[...])
    pre_ref[0, :, 0:d] = r
    pre_ref[0, :, d:2 * d] = k * (1.0 + (a - 1.0) * ka_ref[...])
    pre_ref[0, :, 2 * d:3 * d] = v
    pre_ref[0, :, 3 * d:4 * d] = decay
    pre_ref[0, :, 4 * d:5 * d] = k * kk_ref[...]
    pre_ref[0, :, 5 * d:6 * d] = a


def _rwkv_pre(p3, prev, mu, w0, ww, a0, wa, g2, k_k, k_a, tt):
    b, t, c = p3.shape
    d = RW_DIM
    row = lambda n: pl.BlockSpec((1, n), lambda i, j: (0, 0))
    full = lambda s: pl.BlockSpec(s, lambda i, j: (0, 0))
    return pl.pallas_call(
        _rwkv_pre_body,
        grid=(b, t // tt),
        in_specs=[
            pl.BlockSpec((1, tt, c), lambda i, j: (i, j, 0)),
            pl.BlockSpec((1, 1, 1, c), lambda i, j: (i, j, 0, 0)),
            row(c), row(d), full(ww.shape), row(d), full(wa.shape), full(g2.shape), row(d), row(d),
        ],
        out_specs=[
            pl.BlockSpec((1, tt, 6 * d), lambda i, j: (i, j, 0)),
            pl.BlockSpec((1, tt, d), lambda i, j: (i, j, 0)),
        ],
        out_shape=[jax.ShapeDtypeStruct((b, t, 6 * d), F32), jax.ShapeDtypeStruct((b, t, d), F32)],
        scratch_shapes=[pltpu.VMEM((tt + 8, c), F32)],
        compiler_params=_cparams(("parallel", "arbitrary")),
        name="rwkv_pre",
    )(p3, prev, mu.reshape(1, c), w0.reshape(1, d), ww, a0.reshape(1, d), wa, g2,
      k_k.reshape(1, d), k_a.reshape(1, d))


def _rwkv_scan_body(in_ref, s0_ref, lnw_ref, lnb_ref, rk_ref, y_ref, s_ref, xa_ref, xb_ref, kk_ref, b_ref):
    n = RW_HEAD_DIM
    nh = RW_HEADS
    seqs, steps = in_ref.shape[0], in_ref.shape[1]
    lanes = seqs * nh
    n_fields = in_ref.shape[2] // nh
    fields_per_trip = -(-n_fields // (n // UPDATE_UNROLL))

    def load_field(t, dst_ref, c):
        tile = in_ref[:, t, pl.ds(pl.multiple_of(c * nh, nh), nh), :].reshape(lanes, n)
        dst_ref[pl.ds(pl.multiple_of(c * n, n), n), :] = tile.T

    def normalise_key(x_ref):
        kk_raw = x_ref[4 * n:5 * n, :]
        norm = jnp.sqrt(jnp.sum(kk_raw * kk_raw, axis=0, keepdims=True))
        kk = kk_raw / jnp.maximum(norm, 1e-12)
        kk_ref[...] = kk
        b_ref[...] = kk * x_ref[5 * n:6 * n, :]

    def one_step(t, x_ref, next_ref):
        t_next = jnp.minimum(t + 1, steps - 1)

        def sk_rows(jb, acc):
            for u in range(SCAN_UNROLL):
                j = jb * SCAN_UNROLL + u
                acc = acc + s_ref[j] * kk_ref[pl.ds(j, 1), :]
            return acc

        sk = lax.fori_loop(0, n // SCAN_UNROLL, sk_rows, jnp.zeros((n, lanes), F32))
        v = x_ref[2 * n:3 * n, :]

        def update_rows(jb, acc):
            for f in range(fields_per_trip):
                load_field(t_next, next_ref, jnp.minimum(jb * fields_per_trip + f, n_fields - 1))
            for u in range(UPDATE_UNROLL):
                j = jb * UPDATE_UNROLL + u
                r_j = x_ref[pl.ds(j, 1), :]
                k_j = x_ref[pl.ds(n + j, 1), :]
                w_j = x_ref[pl.ds(3 * n + j, 1), :]
                s_j = s_ref[j] * w_j - sk * b_ref[pl.ds(j, 1), :] + v * k_j
                s_ref[j] = s_j
                acc = acc + s_j * r_j
            return acc

        y = lax.fori_loop(0, n // UPDATE_UNROLL, update_rows, jnp.zeros((n, lanes), F32))
        normalise_key(next_ref)
        mean = jnp.sum(y, axis=0, keepdims=True) * (1.0 / n)
        yc = y - mean
        var = jnp.sum(yc * yc, axis=0, keepdims=True) * (1.0 / n)
        y_norm = yc * lax.rsqrt(var + RW_LN_EPS) * lnw_ref[...] + lnb_ref[...]
        rkr = x_ref[0:n, :] * x_ref[n:2 * n, :] * rk_ref[...]
        out = y_norm + jnp.sum(rkr, axis=0, keepdims=True) * v
        y_ref[:, t, :, :] = out.T.reshape(seqs, nh, n)

    @pl.when(pl.program_id(1) == 0)
    def _():
        s_ref[...] = s0_ref[...]

    for c in range(n_fields):
        load_field(0, xa_ref, c)
    normalise_key(xa_ref)

    def step_pair(tp, carry):
        one_step(2 * tp, xa_ref, xb_ref)
        one_step(2 * tp + 1, xb_ref, xa_ref)
        return carry

    lax.fori_loop(0, steps // 2, step_pair, 0)


def _rwkv_scan(pre4, s0, lnw, lnb, rk, tc):
    b, t, rows, n = pre4.shape
    nh = RW_HEADS
    seqs = min(b, LANES // nh)
    lb = seqs * nh
    tab = pl.BlockSpec((n, lb), lambda l, i: (0, 0))
    return pl.pallas_call(
        _rwkv_scan_body,
        grid=(b // seqs, t // tc),
        in_specs=[
            pl.BlockSpec((seqs, tc, rows, n), lambda l, i: (l, i, 0, 0)),
            pl.BlockSpec((n, n, lb), lambda l, i: (0, 0, l)),
            tab, tab, tab,
        ],
        out_specs=[
            pl.BlockSpec((seqs, tc, nh, n), lambda l, i: (l, i, 0, 0)),
            pl.BlockSpec((n, n, lb), lambda l, i: (0, 0, l)),
        ],
        out_shape=[jax.ShapeDtypeStruct((b, t, nh, n), F32), jax.ShapeDtypeStruct((n, n, b * nh), F32)],
        scratch_shapes=[pltpu.VMEM((rows * n // nh, lb), F32), pltpu.VMEM((rows * n // nh, lb), F32),
                        pltpu.VMEM((n, lb), F32), pltpu.VMEM((n, lb), F32)],
        compiler_params=_cparams(("parallel", "arbitrary")),
        name="rwkv_scan",
    )(pre4, s0, lnw, lnb, rk)


def _gather_body(pt_ref, x_ref, cmp_ref, sel_ref):
    half = 2 * NSA_KV_DIM
    cmp_ref[0] = x_ref[0, :, :half]
    sel_ref[0] = x_ref[0, :, half:].astype(BF16)


def _gather_pages(page_table, pool):
    bs, n_pages = page_table.shape
    _, page, cols = pool.shape
    half = cols // 2
    return pl.pallas_call(
        _gather_body,
        grid_spec=pltpu.PrefetchScalarGridSpec(
            num_scalar_prefetch=1,
            grid=(bs, n_pages),
            in_specs=[pl.BlockSpec((1, page, cols), lambda b, p, pt: (pt[b, p], 0, 0))],
            out_specs=[pl.BlockSpec((1, page, half), lambda b, p, pt: (b, p, 0)),
                       pl.BlockSpec((1, page, half), lambda b, p, pt: (b, p, 0))],
        ),
        out_shape=[jax.ShapeDtypeStruct((bs, n_pages * page, half), F32),
                   jax.ShapeDtypeStruct((bs, n_pages * page, half), BF16)],
        compiler_params=_cparams(("parallel", "arbitrary")),
        name="page_gather",
    )(page_table, pool)


def _cmp_mlp_body(x_ref, pe_ref, w1_ref, w2_ref, o_ref, *, row_stride):
    hd = NSA_HEAD_DIM
    for kg in range(2 * NSA_KV_HEADS):
        ch = kg // NSA_KV_HEADS
        acc = None
        for r in range(CMP_BLOCK):
            xr = x_ref[:, r * row_stride + kg * hd:r * row_stride + (kg + 1) * hd]
            xr = (xr + pe_ref[r:r + 1, kg * hd:(kg + 1) * hd]).astype(BF16)
            part = _dot(xr, w1_ref[ch, r * hd:(r + 1) * hd, :])
            acc = part if acc is None else acc + part
        hid = jax.nn.gelu(acc, approximate=True)
        o_ref[:, kg * hd:(kg + 1) * hd] = _dot(hid.astype(BF16), w2_ref[ch])


def _cmp_mlp(blocks, pe_rows, w1, w2, row_stride, nb):
    n_blocks, width = blocks.shape
    out_w = 2 * NSA_KV_DIM
    return pl.pallas_call(
        functools.partial(_cmp_mlp_body, row_stride=row_stride),
        grid=(n_blocks // nb,),
        in_specs=[
            pl.BlockSpec((nb, width), lambda i: (i, 0)),
            pl.BlockSpec(pe_rows.shape, lambda i: (0, 0)),
            pl.BlockSpec(w1.shape, lambda i: (0, 0, 0)),
            pl.BlockSpec(w2.shape, lambda i: (0, 0, 0)),
        ],
        out_specs=pl.BlockSpec((nb, out_w), lambda i: (i, 0)),
        out_shape=jax.ShapeDtypeStruct((n_blocks, out_w), F32),
        compiler_params=_cparams(("parallel",)),
        name="cmp_mlp",
    )(blocks, pe_rows, w1, w2)


N_BIAS_TILES = -(-(FAR_DIST + LANES - 1) // LANES) + 1


def _bias_tiles_body(rb_ref, o_ref):
    h = pl.program_id(0)
    iq = lax.broadcasted_iota(jnp.int32, (Q_BLOCK, LANES), 0)
    ik = lax.broadcasted_iota(jnp.int32, (Q_BLOCK, LANES), 1)
    for m in range(N_BIAS_TILES):
        dist = jnp.maximum(m * LANES + iq - ik, 0)
        val = jnp.full(dist.shape, rb_ref[0, h], F32)
        for b, t in enumerate(BUCKET_THR):
            val = jnp.where(dist >= t, rb_ref[b + 1, h], val)
        o_ref[0, m] = val


def _bias_tiles(rel_bias):
    return pl.pallas_call(
        _bias_tiles_body,
        grid=(NSA_HEADS,),
        in_specs=[pl.BlockSpec(memory_space=pltpu.SMEM)],
        out_specs=pl.BlockSpec((1, N_BIAS_TILES, Q_BLOCK, LANES), lambda h: (h, 0, 0, 0)),
        out_shape=jax.ShapeDtypeStruct((NSA_HEADS, N_BIAS_TILES, Q_BLOCK, LANES), F32),
        compiler_params=_cparams(("arbitrary",)),
        name="bias_tiles",
    )(rel_bias)


def _nsa_prompt_body(rb_ref, q_ref, gate_ref, sel_ref, win_ref, kc_ref, bt_ref, y_ref, selb_ref, winb_ref,
                     *, n_sel):
    hd = NSA_HEAD_DIM
    qb = Q_BLOCK
    i = pl.program_id(1)
    scale = hd ** -0.5

    @pl.when(i == 0)
    def _():
        selb_ref[...] = sel_ref[0].astype(BF16)
        winb_ref[...] = win_ref[0].astype(BF16)

    q_bf = q_ref[0].astype(BF16)
    gates = jax.nn.sigmoid(gate_ref[0])
    iq = lax.broadcasted_iota(jnp.int32, (qb, LANES), 0)
    ik = lax.broadcasted_iota(jnp.int32, (qb, LANES), 1)
    pos = i * qb + iq
    d_cmp = pos - (ik * CMP_BLOCK + CMP_BLOCK - 1)
    bias_cmp = _bias_for_heads(d_cmp, rb_ref, range(NSA_HEADS))
    vis_cmp = _stack_heads(d_cmp >= 0)
    top_n = min(TOP_N, n_sel)

    for g in range(NSA_KV_HEADS):
        heads = [g * NSA_HPG + h for h in range(NSA_HPG)]
        qg = jnp.concatenate([q_bf[:, hh * hd:(hh + 1) * hd] for hh in heads], axis=0)
        kc_k = kc_ref[0, :, g * hd:(g + 1) * hd].astype(BF16)
        kc_v = kc_ref[0, :, NSA_KV_DIM + g * hd:NSA_KV_DIM + (g + 1) * hd].astype(BF16)
        o_cmp, p_cmp = _cmp_branch(qg, kc_k, kc_v, jnp.concatenate([bias_cmp[hh] for hh in heads], axis=0),
                                   vis_cmp, scale)
        chosen = _topk_mask(_selection_values(p_cmp, qb, pos, n_sel), top_n).astype(BF16)

        def sel_step(j, carry):
            row0 = pl.multiple_of(j * LANES, LANES)
            kt = selb_ref[pl.ds(row0, LANES), g * hd:(g + 1) * hd]
            vt = selb_ref[pl.ds(row0, LANES), NSA_KV_DIM + g * hd:NSA_KV_DIM + (g + 1) * hd]
            causal = (i - j) * LANES + iq - ik >= 0
            mask = _stack_heads(_expand_selection(chosen, j) & causal)
            m = jnp.minimum(i - j, N_BIAS_TILES - 1)
            bias = jnp.concatenate([bt_ref[hh, m] for hh in heads], axis=0)
            return _softmax_update(carry, qg, kt, vt, bias, mask, scale)

        o_sel = _softmax_finish(lax.fori_loop(0, i + 1, sel_step, _softmax_init(NSA_HPG * qb)))

        carry = _softmax_init(NSA_HPG * qb)
        for m in range(WINDOW // LANES + 1):
            tile = i - m
            row0 = pl.multiple_of(jnp.maximum(tile, 0) * LANES, LANES)
            kt = winb_ref[pl.ds(row0, LANES), g * hd:(g + 1) * hd]
            vt = winb_ref[pl.ds(row0, LANES), NSA_KV_DIM + g * hd:NSA_KV_DIM + (g + 1) * hd]
            dist = m * LANES + iq - ik
            reach = jnp.where(tile >= 0, WINDOW, 0)
            mask = _stack_heads((dist >= 0) & (dist < reach))
            bias = jnp.concatenate([bt_ref[hh, m] for hh in heads], axis=0)
            carry = _softmax_update(carry, qg, kt, vt, bias, mask, scale)
        o_win = _softmax_finish(carry)

        for h, hh in enumerate(heads):
            rows = slice(h * qb, (h + 1) * qb)
            out = (gates[:, hh:hh + 1] * o_cmp[rows]
                   + gates[:, NSA_HEADS + hh:NSA_HEADS + hh + 1] * o_sel[rows]
                   + gates[:, 2 * NSA_HEADS + hh:2 * NSA_HEADS + hh + 1] * o_win[rows])
            y_ref[0, :, hh * hd:(hh + 1) * hd] = out.astype(BF16)


def _nsa_prompt(rel_bias, pb3, kc, bias_tiles):
    b, t, _ = pb3.shape
    qb = Q_BLOCK
    half = 2 * NSA_KV_DIM
    return pl.pallas_call(
        functools.partial(_nsa_prompt_body, n_sel=t // SEL_BLOCK),
        grid=(b, t // qb),
        in_specs=[
            pl.BlockSpec(memory_space=pltpu.SMEM),
            pl.BlockSpec((1, qb, NSA_DIM), lambda s, i: (s, i, PB_Q // NSA_DIM)),
            pl.BlockSpec((1, qb, LANES), lambda s, i: (s, i, PB_GATE // LANES)),
            pl.BlockSpec((1, t, half), lambda s, i: (s, 0, PB_SEL // half)),
            pl.BlockSpec((1, t, half), lambda s, i: (s, 0, PB_WIN // half)),
            pl.BlockSpec((1,) + kc.shape[1:], lambda s, i: (s, 0, 0)),
            pl.BlockSpec(bias_tiles.shape, lambda s, i: (0, 0, 0, 0)),
        ],
        out_specs=pl.BlockSpec((1, qb, NSA_DIM), lambda s, i: (s, i, 0)),
        out_shape=jax.ShapeDtypeStruct((b, t, NSA_DIM), BF16),
        scratch_shapes=[pltpu.VMEM((t, half), BF16), pltpu.VMEM((t, half), BF16)],
        compiler_params=_cparams(("parallel", "arbitrary")),
        name="nsa_prompt",
    )(rel_bias, pb3, pb3, pb3, pb3, kc, bias_tiles)


SAMPLE_ROWS = 8


def _nsa_sample_body(rb_ref, q_ref, gate_ref, kc_ref, past_ref, new_sel_ref, cwin_ref, new_win_ref, y_ref,
                     *, past_len):
    hd = NSA_HEAD_DIM
    qb = SAMPLE_ROWS
    scale = hd ** -0.5
    n_cmp = kc_ref.shape[1]
    n_past_tiles = past_len // LANES
    n_win_tiles = cwin_ref.shape[1] // LANES
    k_off = past_len - cwin_ref.shape[1]

    q_bf = q_ref[0].astype(BF16)
    gates = jax.nn.sigmoid(gate_ref[0])
    iq = lax.broadcasted_iota(jnp.int32, (qb, LANES), 0)
    ik = lax.broadcasted_iota(jnp.int32, (qb, LANES), 1)
    pos = past_len + iq
    tq_c = lax.broadcasted_iota(jnp.int32, (qb, n_cmp), 0)
    n_c = lax.broadcasted_iota(jnp.int32, (qb, n_cmp), 1)
    pos_c = past_len + tq_c
    d_cmp = pos_c - (n_c * CMP_BLOCK + CMP_BLOCK - 1)
    bias_cmp = _bias_for_heads(d_cmp, rb_ref, range(NSA_HEADS))
    vis_cmp = _stack_heads(d_cmp >= 0)
    n_sel_past = past_len // SEL_BLOCK
    picks = min(TOP_N, n_sel_past + 1) - 1

    for g in range(NSA_KV_HEADS):
        heads = [g * NSA_HPG + h for h in range(NSA_HPG)]
        qg = jnp.concatenate([q_bf[:, hh * hd:(hh + 1) * hd] for hh in heads], axis=0)
        kc_k = kc_ref[0, :, g * hd:(g + 1) * hd].astype(BF16)
        kc_v = kc_ref[0, :, NSA_KV_DIM + g * hd:NSA_KV_DIM + (g + 1) * hd].astype(BF16)
        o_cmp, p_cmp = _cmp_branch(qg, kc_k, kc_v, jnp.concatenate([bias_cmp[hh] for hh in heads], axis=0),
                                   vis_cmp, scale)
        chosen = _topk_mask(_selection_values(p_cmp, qb, pos_c, n_sel_past), picks).astype(BF16)

        def sel_step(j, carry):
            row0 = pl.multiple_of(j * LANES, LANES)
            kt = past_ref[0, pl.ds(row0, LANES), g * hd:(g + 1) * hd]
            vt = past_ref[0, pl.ds(row0, LANES), NSA_KV_DIM + g * hd:NSA_KV_DIM + (g + 1) * hd]
            mask = _stack_heads(_expand_selection(chosen, j))
            dist = pos - (j * LANES + ik)
            bias = jnp.concatenate(_bias_for_heads(dist, rb_ref, heads), axis=0)
            return _softmax_update(carry, qg, kt, vt, bias, mask, scale)

        carry = lax.fori_loop(0, n_past_tiles, sel_step, _softmax_init(NSA_HPG * qb))
        d_new = iq - ik
        bias_new = jnp.concatenate(_bias_for_heads(d_new, rb_ref, heads), axis=0)
        mask_new = _stack_heads(d_new >= 0)
        carry = _softmax_update(carry, qg, new_sel_ref[0, :, g * hd:(g + 1) * hd],
                                new_sel_ref[0, :, NSA_KV_DIM + g * hd:NSA_KV_DIM + (g + 1) * hd],
                                bias_new, mask_new, scale)
        o_sel = _softmax_finish(carry)

        carry = _softmax_init(NSA_HPG * qb)
        for j in range(n_win_tiles):
            kt = cwin_ref[0, j * LANES:(j + 1) * LANES, g * hd:(g + 1) * hd].astype(BF16)
            vt = cwin_ref[0, j * LANES:(j + 1) * LANES,
                          NSA_KV_DIM + g * hd:NSA_KV_DIM + (g + 1) * hd].astype(BF16)
            dist = pos - (k_off + j * LANES + ik)
            mask = _stack_heads((dist >= 0) & (dist < WINDOW))
            bias = jnp.concatenate(_bias_for_heads(dist, rb_ref, heads), axis=0)
            carry = _softmax_update(carry, qg, kt, vt, bias, mask, scale)
        carry = _softmax_update(carry, qg, new_win_ref[0, :, g * hd:(g + 1) * hd],
                                new_win_ref[0, :, NSA_KV_DIM + g * hd:NSA_KV_DIM + (g + 1) * hd],
                                bias_new, mask_new, scale)
        o_win = _softmax_finish(carry)

        for h, hh in enumerate(heads):
            rows = slice(h * qb, (h + 1) * qb)
            out = (gates[:, hh:hh + 1] * o_cmp[rows]
                   + gates[:, NSA_HEADS + hh:NSA_HEADS + hh + 1] * o_sel[rows]
                   + gates[:, 2 * NSA_HEADS + hh:2 * NSA_HEADS + hh + 1] * o_win[rows])
            y_ref[0, :, hh * hd:(hh + 1) * hd] = out.astype(BF16)


def _nsa_sample(rel_bias, q8, gate8, kc, past_sel, new_sel, cwin, new_win, past_len):
    bs = q8.shape[0]
    blk = lambda a: pl.BlockSpec((1,) + a.shape[1:], lambda s: (s, 0, 0))
    return pl.pallas_call(
        functools.partial(_nsa_sample_body, past_len=past_len),
        grid=(bs,),
        in_specs=[pl.BlockSpec(memory_space=pltpu.SMEM), blk(q8), blk(gate8), blk(kc), blk(past_sel),
                  blk(new_sel), blk(cwin), blk(new_win)],
        out_specs=pl.BlockSpec((1, SAMPLE_ROWS, NSA_DIM), lambda s: (s, 0, 0)),
        out_shape=jax.ShapeDtypeStruct((bs, SAMPLE_ROWS, NSA_DIM), BF16),
        compiler_params=_cparams(("parallel",)),
        name="nsa_sample",
    )(rel_bias, q8, gate8, kc, past_sel, new_sel, cwin, new_win)


def _merge_body(x_ref, yrw_ref, g_ref, ynsa_ref, gates_ref, wrw_ref, wnsa_ref, wout_ref, post_ref, o_ref):
    d = x_ref.shape[1]
    y_rw = (yrw_ref[...] * g_ref[...]).astype(BF16)
    merged = (jax.nn.sigmoid(gates_ref[:, :d]) * _dot(y_rw, wrw_ref[...])
              + jax.nn.sigmoid(gates_ref[:, d:]) * _dot(ynsa_ref[...], wnsa_ref[...]))
    o_ref[...] = x_ref[...] + _rms(_dot(merged.astype(BF16), wout_ref[...]), post_ref[...])


def _merge(x, y_rw, g_rw, y_nsa, gates, w_rw, w_nsa, w_out, post_g, tm):
    m, d = x.shape
    rows = lambda n: pl.BlockSpec((tm, n), lambda i: (i, 0))
    held = lambda a: pl.BlockSpec(a.shape, lambda i: (0, 0), pipeline_mode=pl.Buffered(1))
    return pl.pallas_call(
        _merge_body,
        grid=(m // tm,),
        in_specs=[rows(d), rows(RW_DIM), rows(RW_DIM), rows(NSA_DIM), rows(2 * d),
                  held(w_rw), held(w_nsa), held(w_out), pl.BlockSpec((1, d), lambda i: (0, 0))],
        out_specs=rows(d),
        out_shape=jax.ShapeDtypeStruct((m, d), F32),
        compiler_params=_cparams(("parallel",)),
        name="merge",
    )(x, y_rw, g_rw, y_nsa, gates, w_rw, w_nsa, w_out, post_g.reshape(1, d))


def _row_tile(m, want):
    return want if m % want == 0 else m


def _rwkv_mix(p3, shift0, s0, w):
    b, t, c = p3.shape
    n = RW_HEAD_DIM
    tt = 256 if t % 256 == 0 else t
    prev = jnp.concatenate([shift0[:, None, :], p3[:, tt - 1:t - 1:tt]], axis=1).reshape(b, t // tt, 1, c)
    pre, g_rw = _rwkv_pre(p3, prev, w["mu"], w["w0"], w["ww"], w["a0"], w["wa"], w["g2"], w["k_k"], w["k_a"], tt)
    lanes = b * RW_HEADS
    s0_t = s0.transpose(3, 2, 0, 1).reshape(n, n, lanes)
    reps = min(lanes, LANES) // RW_HEADS
    per_head = lambda v: jnp.tile(v.reshape(RW_HEADS, n).T, (1, reps))
    tc = 32 if t % 32 == 0 else t
    y4, s_t = _rwkv_scan(pre.reshape(b, t, 6 * RW_HEADS, n), s0_t, per_head(w["lnx_w"]), per_head(w["lnx_b"]),
                         per_head(w["r_k"]), tc)
    y = y4.reshape(b * t, RW_DIM)
    s_fin = s_t.reshape(n, n, b, RW_HEADS).transpose(2, 3, 1, 0)
    return y, g_rw.reshape(b * t, RW_DIM), s_fin


def _layer(x, w, rel_bias, shift0, s0, past):
    b, t, d = x.shape
    m = b * t
    tm = _row_tile(m, 512)
    x1 = _ffn(x.reshape(m, d), w["f1_pre"], w["f1_post"], w["f1_w1"], w["f1_w3"], w["f1_w2"], tm, 512)
    p_a = _norm_matmul(x1, w["mix_pre"], w["w_in_a"], tm, "proj_rwkv")
    p_b = _norm_matmul(x1, w["mix_pre"], w["w_in_b"], tm, "proj_nsa")
    p_c = _norm_matmul(x1, w["mix_pre"], w["w_in_c"], tm, "proj_gates")

    p3 = p_a.reshape(b, t, RW_PROJ)
    y_rw, g_rw, s_fin = _rwkv_mix(p3, shift0, s0, w)

    pb3 = p_b.reshape(b, t, PB_COLS)
    kv_new = pb3[:, :, PB_KV:PB_WIN]
    win_new = pb3[:, :, PB_WIN:PB_GATE]
    row_w = 4 * NSA_KV_DIM
    if past is None:
        blocks = kv_new.reshape(m // CMP_BLOCK, CMP_BLOCK * row_w)
        kc = _cmp_mlp(blocks, w["cmp_pe"], w["cmp_w1"], w["cmp_w2"], row_w, min(64, blocks.shape[0]))
        n_cmp = t // CMP_BLOCK
        kc = jnp.pad(kc.reshape(b, n_cmp, 2 * NSA_KV_DIM), ((0, 0), (0, -n_cmp % LANES), (0, 0)))
        y_nsa = _nsa_prompt(rel_bias, pb3, kc, _bias_tiles(rel_bias)).reshape(m, NSA_DIM)
        win_out = win_new[:, t - min(WINDOW, t):]
    else:
        page_table, pool, cache_win = past
        past_len = page_table.shape[1] * pool.shape[1]
        past_cmp, past_sel = _gather_pages(page_table, pool.reshape(pool.shape[0], pool.shape[1], row_w))
        half = 2 * NSA_KV_DIM
        blocks = past_cmp.reshape(b * past_len // CMP_BLOCK, CMP_BLOCK * half)
        kc = _cmp_mlp(blocks, w["cmp_pe"], w["cmp_w1"], w["cmp_w2"], half, min(128, blocks.shape[0]))
        n_cmp = past_len // CMP_BLOCK
        kc = jnp.pad(kc.reshape(b, n_cmp, half), ((0, 0), (0, -n_cmp % LANES), (0, 0)))
        pad_q = lambda a, rows: jnp.pad(a, ((0, 0), (0, rows - t), (0, 0)))
        cwin = cache_win.reshape(b, cache_win.shape[1], half)
        y8 = _nsa_sample(rel_bias, pad_q(pb3[:, :, :NSA_DIM], SAMPLE_ROWS), pad_q(pb3[:, :, PB_GATE:], SAMPLE_ROWS),
                         kc, past_sel, pad_q(pb3[:, :, PB_SEL:PB_WIN].astype(BF16), LANES), cwin,
                         pad_q(win_new.astype(BF16), LANES), past_len)
        y_nsa = y8[:, :t].reshape(m, NSA_DIM)
        win_all = jnp.concatenate([cwin, win_new], axis=1)
        win_out = win_all[:, win_all.shape[1] - min(WINDOW, win_all.shape[1]):]

    x2 = _merge(x1, y_rw, g_rw, y_nsa, p_c, w["w_br_rw"], w["w_br_nsa"], w["w_out"], w["mix_post"],
                _row_tile(m, 256))
    y = _ffn(x2, w["f2_pre"], w["f2_post"], w["f2_w1"], w["f2_w3"], w["f2_w2"], tm, 512)
    g, hd = NSA_KV_HEADS, NSA_HEAD_DIM
    return (y.reshape(b, t, d), kv_new.reshape(b, t, 4, g, hd), win_out.reshape(b, -1, 2, g, hd), s_fin, p3[:, -1])


def _prepare_weights(l, ffn1_pre_g, ffn1_post_g, ffn1_w1, ffn1_w3, ffn1_w2, mix_pre_g, mix_post_g, w_in,
                     rw_mu, rw_w0, rw_w2, rw_a0, rw_a2, rw_g2, rw_k_k, rw_k_a, rw_r_k, rw_lnx_w, rw_lnx_b,
                     cmp_pe, cmp_w1, cmp_w2, w_br_rw, w_br_nsa, w_out,
                     ffn2_pre_g, ffn2_post_g, ffn2_w1, ffn2_w3, ffn2_w2):
    d = w_in.shape[1]
    wi = w_in[l]
    c_q = RW_PROJ
    c_kv = c_q + NSA_DIM
    c_gate = c_kv + 6 * NSA_KV_DIM
    c_grw = c_gate + 3 * NSA_HEADS
    w_in_b = jnp.concatenate([wi[:, c_q:c_gate], jnp.pad(wi[:, c_gate:c_grw], ((0, 0), (0, LANES - 3 * NSA_HEADS)))],
                             axis=1)
    zeros_w = jnp.zeros((RW_LORA_A, RW_DIM), F32)
    zeros_a = jnp.zeros((RW_LORA_W, RW_DIM), F32)
    pe = cmp_pe[l]
    pe_rows = jnp.broadcast_to(pe[:, :, None, :], (CMP_BLOCK, 2, NSA_KV_HEADS, NSA_HEAD_DIM))
    return {
        "f1_pre": ffn1_pre_g[l], "f1_post": ffn1_post_g[l],
        "f1_w1": ffn1_w1[l].astype(BF16), "f1_w3": ffn1_w3[l].astype(BF16), "f1_w2": ffn1_w2[l].astype(BF16),
        "mix_pre": mix_pre_g[l], "mix_post": mix_post_g[l],
        "w_in_a": wi[:, :RW_PROJ].astype(BF16), "w_in_b": w_in_b.astype(BF16), "w_in_c": wi[:, c_grw:].astype(BF16),
        "mu": rw_mu[l], "w0": rw_w0[l], "a0": rw_a0[l],
        "ww": jnp.concatenate([rw_w2[l], zeros_w], axis=0).astype(BF16),
        "wa": jnp.concatenate([zeros_a, rw_a2[l]], axis=0).astype(BF16),
        "g2": rw_g2[l].astype(BF16), "k_k": rw_k_k[l], "k_a": rw_k_a[l],
        "r_k": rw_r_k[l].reshape(-1), "lnx_w": rw_lnx_w[l], "lnx_b": rw_lnx_b[l],
        "cmp_pe": pe_rows.reshape(CMP_BLOCK, 2 * NSA_KV_DIM),
        "cmp_w1": cmp_w1[l].astype(BF16), "cmp_w2": cmp_w2[l].astype(BF16),
        "w_br_rw": w_br_rw[l].astype(BF16), "w_br_nsa": w_br_nsa[l].astype(BF16), "w_out": w_out[l].astype(BF16),
        "f2_pre": ffn2_pre_g[l], "f2_post": ffn2_post_g[l],
        "f2_w1": ffn2_w1[l].astype(BF16), "f2_w3": ffn2_w3[l].astype(BF16), "f2_w2": ffn2_w2[l].astype(BF16),
    }


def kernel(x_prompt, x_sample, cache_kv, cache_win, state_rwkv, state_shift, page_table,
           ffn1_pre_g, ffn1_post_g, ffn1_w1, ffn1_w3, ffn1_w2, mix_pre_g, mix_post_g, w_in,
           rw_mu, rw_w0, rw_w2, rw_a0, rw_a2, rw_g2, rw_k_k, rw_k_a, rw_r_k, rw_lnx_w, rw_lnx_b,
           cmp_pe, cmp_w1, cmp_w2, w_br_rw, w_br_nsa, w_out,
           ffn2_pre_g, ffn2_post_g, ffn2_w1, ffn2_w3, ffn2_w2, rel_bias):
    depth = w_in.shape[0]
    b_p = x_prompt.shape[0]
    y_p, y_s = x_prompt, x_sample
    outs = [[] for _ in range(8)]
    for l in range(depth):
        w = _prepare_weights(l, ffn1_pre_g, ffn1_post_g, ffn1_w1, ffn1_w3, ffn1_w2, mix_pre_g, mix_post_g, w_in,
                             rw_mu, rw_w0, rw_w2, rw_a0, rw_a2, rw_g2, rw_k_k, rw_k_a, rw_r_k, rw_lnx_w, rw_lnx_b,
                             cmp_pe, cmp_w1, cmp_w2, w_br_rw, w_br_nsa, w_out,
                             ffn2_pre_g, ffn2_post_g, ffn2_w1, ffn2_w3, ffn2_w2)
        y_p, kv_p, win_p, rw_p, sh_p = _layer(
            y_p, w, rel_bias, jnp.zeros((b_p, RW_PROJ), F32),
            jnp.zeros((b_p, RW_HEADS, RW_HEAD_DIM, RW_HEAD_DIM), F32), None)
        y_s, kv_s, win_s, rw_s, sh_s = _layer(
            y_s, w, rel_bias, state_shift[l], state_rwkv[l], (page_table, cache_kv[l], cache_win[l]))
        for acc, v in zip(outs, (kv_p, kv_s, win_p, win_s, rw_p, rw_s, sh_p, sh_s)):
            acc.append(v)
    return (y_p, y_s) + tuple(jnp.stack(o) for o in outs)
```

```python
import functools
import math

import jax
import jax.numpy as jnp
from jax import lax
from jax.experimental import pallas as pl
from jax.experimental.pallas import tpu as pltpu

F32 = jnp.float32
BF16 = jnp.bfloat16

RMS_EPS = 1e-6
RW_HEADS = 16
RW_HEAD_DIM = 64
RW_DIM = RW_HEADS * RW_HEAD_DIM
RW_LORA_W = 64
RW_LORA_A = 64
RW_LORA_G = 128
RW_PROJ = 3 * RW_DIM + RW_LORA_W + RW_LORA_A + RW_LORA_G
RW_LN_EPS = 64e-5
NSA_HEADS = 8
NSA_KV_HEADS = 2
NSA_HPG = NSA_HEADS // NSA_KV_HEADS
NSA_HEAD_DIM = 128
NSA_DIM = NSA_HEADS * NSA_HEAD_DIM
NSA_KV_DIM = NSA_KV_HEADS * NSA_HEAD_DIM
CMP_BLOCK = 32
CMP_HID = 256
SEL_BLOCK = 64
SEL_SHIFT = SEL_BLOCK.bit_length() - 1
TOP_N = 16
WINDOW = 512
N_BUCKETS = 32
REL_MAX_EXACT = 16
REL_MAX_DIST = 1024
Q_BLOCK = 128
NEG_BIG = -1e30
FORCE_BONUS = 1e4
NEVER = -3e38
LANES = 128
SUBLANES = 8
SCAN_FIELDS = 6
VMEM_LIMIT = 56 * 1024 * 1024
SCAN_UNROLL = 8
UPDATE_UNROLL = 16

PB_Q = 0
PB_KV = NSA_DIM
PB_SEL = PB_KV + 2 * NSA_KV_DIM
PB_WIN = PB_KV + 4 * NSA_KV_DIM
PB_GATE = PB_WIN + 2 * NSA_KV_DIM
PB_COLS = PB_GATE + LANES


def _bucket_thresholds():
    thr = list(range(1, REL_MAX_EXACT + 1))
    n_log = N_BUCKETS - REL_MAX_EXACT
    ratio = REL_MAX_DIST // REL_MAX_EXACT
    n = REL_MAX_EXACT
    for k in range(1, n_log):
        while n ** n_log < REL_MAX_EXACT ** n_log * ratio ** k:
            n += 1
        thr.append(n)
    return thr


BUCKET_THR = _bucket_thresholds()
FAR_DIST = BUCKET_THR[-1]


def _cparams(sem):
    return pltpu.CompilerParams(dimension_semantics=sem, vmem_limit_bytes=VMEM_LIMIT)


def _rms(x, g):
    ms = jnp.mean(x * x, axis=-1, keepdims=True)
    return x * lax.rsqrt(ms + RMS_EPS) * g


def _dot(a, b):
    return jnp.dot(a, b, preferred_element_type=F32)


def _dot_nt(a, b):
    return lax.dot_general(a, b, (((1,), (1,)), ((), ())), preferred_element_type=F32)


def _bias_for_heads(dist, rb_ref, heads):
    n = jnp.maximum(dist, 0)
    reach = [n >= t for t in BUCKET_THR]
    out = []
    for h in heads:
        val = jnp.full(dist.shape, rb_ref[0, h], F32)
        for b, m in enumerate(reach):
            val = jnp.where(m, rb_ref[b + 1, h], val)
        out.append(val)
    return out


def _softmax_update(carry, q, kt, vt, bias, mask, scale):
    m_run, l_run, acc = carry
    s = _dot_nt(q, kt) * scale + bias
    s = jnp.where(mask, s, NEG_BIG)
    m_new = jnp.maximum(m_run, jnp.max(s, axis=-1, keepdims=True))
    alpha = jnp.exp(m_run - m_new)
    e = jnp.where(mask, jnp.exp(s - m_new), 0.0)
    l_new = alpha * l_run + jnp.sum(e, axis=-1, keepdims=True)
    acc = alpha * acc + _dot(e.astype(BF16), vt)
    return m_new, l_new, acc


def _softmax_init(rows):
    return (jnp.full((rows, 1), NEG_BIG, F32), jnp.zeros((rows, 1), F32),
            jnp.zeros((rows, NSA_HEAD_DIM), F32))


def _softmax_finish(carry):
    _, l_run, acc = carry
    return acc / jnp.maximum(l_run, 1e-30)


def _stack_heads(x):
    return jnp.concatenate([x] * NSA_HPG, axis=0)


def _topk_mask(val, k):
    lane = lax.broadcasted_iota(jnp.int32, val.shape, 1)
    width = val.shape[1]
    chosen = jnp.zeros(val.shape, F32)
    for _ in range(k):
        top = jnp.max(val, axis=-1, keepdims=True)
        first = jnp.min(jnp.where(val == top, lane, width), axis=-1, keepdims=True)
        hit = lane == first
        chosen = jnp.where(hit, 1.0, chosen)
        val = jnp.where(hit, NEVER, val)
    return chosen


def _pair_sum(x):
    parts = []
    for c in range(x.shape[1] // LANES):
        blk = x[:, c * LANES:(c + 1) * LANES]
        parts.append(blk + pltpu.roll(blk, LANES - 1, 1))
    return parts[0] if len(parts) == 1 else jnp.concatenate(parts, axis=1)


def _cmp_branch(qg, kc_k, kc_v, bias, visible, scale):
    s = _dot_nt(qg, kc_k) * scale + bias
    s = jnp.where(visible, s, NEG_BIG)
    e = jnp.where(visible, jnp.exp(s - jnp.max(s, axis=-1, keepdims=True)), 0.0)
    p = e / jnp.maximum(jnp.sum(e, axis=-1, keepdims=True), 1e-30)
    return _dot(p.astype(BF16), kc_v), p


def _selection_values(p, qb, pos, n_sel):
    imp = p[0:qb]
    for h in range(1, NSA_HPG):
        imp = imp + p[h * qb:(h + 1) * qb]
    imp = _pair_sum(imp)
    lane = lax.broadcasted_iota(jnp.int32, imp.shape, 1)
    sel_id = lane >> 1
    cur = pos >> SEL_SHIFT
    forced = (sel_id == 0) | (sel_id == cur) | (sel_id == cur - 1)
    val = jnp.where(forced, imp + FORCE_BONUS, imp)
    val = jnp.where(sel_id * SEL_BLOCK <= pos, val, NEG_BIG)
    return jnp.where(((lane & 1) == 0) & (sel_id < n_sel), val, NEVER)


def _expand_selection(chosen_bf16, tile, tile_keys=LANES):
    width = chosen_bf16.shape[1]
    n_idx = lax.broadcasted_iota(jnp.int32, (width, tile_keys), 0)
    k_idx = lax.broadcasted_iota(jnp.int32, (width, tile_keys), 1)
    target = 2 * ((tile * tile_keys + k_idx) >> SEL_SHIFT)
    expand = jnp.where(n_idx == target, 1.0, 0.0).astype(BF16)
    return _dot(chosen_bf16, expand) > 0.5


def _ffn_body(x_ref, pre_ref, post_ref, w1_ref, w3_ref, w2_ref, o_ref, h_ref, acc_ref):
    j = pl.program_id(1)

    @pl.when(j == 0)
    def _():
        h_ref[...] = _rms(x_ref[...], pre_ref[...]).astype(BF16)
        acc_ref[...] = jnp.zeros_like(acc_ref)

    h = h_ref[...]
    a = _dot(h, w1_ref[...])
    b = _dot(h, w3_ref[...])
    u = (a * jax.nn.sigmoid(a)) * b
    acc_ref[...] += _dot(u.astype(BF16), w2_ref[...])

    @pl.when(j == pl.num_programs(1) - 1)
    def _():
        o_ref[...] = x_ref[...] + 0.5 * _rms(acc_ref[...], post_ref[...])


def _ffn(x, pre_g, post_g, w1, w3, w2, tm, tf):
    m, d = x.shape
    f = w1.shape[1]
    return pl.pallas_call(
        _ffn_body,
        grid=(m // tm, f // tf),
        in_specs=[
            pl.BlockSpec((tm, d), lambda i, j: (i, 0)),
            pl.BlockSpec((1, d), lambda i, j: (0, 0)),
            pl.BlockSpec((1, d), lambda i, j: (0, 0)),
            pl.BlockSpec((d, tf), lambda i, j: (0, j)),
            pl.BlockSpec((d, tf), lambda i, j: (0, j)),
            pl.BlockSpec((tf, d), lambda i, j: (j, 0)),
        ],
        out_specs=pl.BlockSpec((tm, d), lambda i, j: (i, 0)),
        out_shape=jax.ShapeDtypeStruct((m, d), F32),
        scratch_shapes=[pltpu.VMEM((tm, d), BF16), pltpu.VMEM((tm, d), F32)],
        compiler_params=_cparams(("parallel", "arbitrary")),
        name="ffn",
    )(x, pre_g.reshape(1, d), post_g.reshape(1, d), w1, w3, w2)


def _norm_matmul_body(x_ref, g_ref, w_ref, o_ref, h_ref):
    @pl.when(pl.program_id(1) == 0)
    def _():
        h_ref[...] = _rms(x_ref[...], g_ref[...]).astype(BF16)

    o_ref[...] = _dot(h_ref[...], w_ref[...])


def _norm_matmul(x, g, w, tm, name):
    m, d = x.shape
    n = w.shape[1]
    return pl.pallas_call(
        _norm_matmul_body,
        grid=(m // tm, 1),
        in_specs=[
            pl.BlockSpec((tm, d), lambda i, j: (i, 0)),
            pl.BlockSpec((1, d), lambda i, j: (0, 0)),
            pl.BlockSpec((d, n), lambda i, j: (0, 0), pipeline_mode=pl.Buffered(1)),
        ],
        out_specs=pl.BlockSpec((tm, n), lambda i, j: (i, 0)),
        out_shape=jax.ShapeDtypeStruct((m, n), F32),
        scratch_shapes=[pltpu.VMEM((tm, d), BF16)],
        compiler_params=_cparams(("parallel", "arbitrary")),
        name=name,
    )(x, g.reshape(1, d), w)


def _rwkv_pre_body(p_ref, prev_ref, mu_ref, w0_ref, ww_ref, a0_ref, wa_ref, g2_ref, kk_ref, ka_ref,
                   pre_ref, g_ref, buf_ref):
    tt = p_ref.shape[1]
    p = p_ref[0]
    buf_ref[8:8 + tt, :] = p
    buf_ref[7:8, :] = prev_ref[0, 0]
    p_prev = buf_ref[7:7 + tt, :]
    xs = p + (p_prev - p) * mu_ref[...]
    d = RW_DIM
    r = xs[:, 0:d]
    k = xs[:, d:2 * d]
    v = xs[:, 2 * d:3 * d]
    lora_in = xs[:, 3 * d:3 * d + RW_LORA_W + RW_LORA_A]
    gd = xs[:, 3 * d + RW_LORA_W + RW_LORA_A:]
    z = -(w0_ref[...] + _dot(jnp.tanh(lora_in).astype(BF16), ww_ref[...]))
    softplus = jnp.maximum(z, 0.0) + jnp.log(1.0 + jnp.exp(-jnp.abs(z)))
    decay = jnp.exp(-jnp.exp(-softplus - 0.5))
    a = jax.nn.sigmoid(a0_ref[...] + _dot(lora_in.astype(BF16), wa_ref[...]))
    g_ref[0] = _dot(jax.nn.sigmoid(gd).astype(BF16), g2_ref[...])

    n = RW_HEAD_DIM
    pairs = d // LANES
    low_half = lax.broadcasted_iota(jnp.int32, (tt, LANES), 1) < n

    def put(fp, xa, xb):
        for pair in range(pairs):
            col_a = xa[:, pair * LANES:(pair + 1) * LANES]
            col_b = xb[:, pair * LANES:(pair + 1) * LANES]
            pieces = (jnp.where(low_half, col_a, pltpu.roll(col_b, n, 1)),
                      jnp.where(low_half, pltpu.roll(col_a, n, 1), col_b))
            for parity, piece in enumerate(pieces):
                pre_ref[:, fp, parity, pair, :] = piece

    put(0, r, k * (1.0 + (a - 1.0) * ka_ref[...]))
    put(1, v, decay)
    put(2, k * kk_ref[...], a)


def _rwkv_pre(p3, prev, mu, w0, ww, a0, wa, g2, k_k, k_a, tt):
    b, t, c = p3.shape
    d = RW_DIM
    pairs = RW_HEADS // 2
    row = lambda n: pl.BlockSpec((1, n), lambda i, j: (0, 0))
    full = lambda s: pl.BlockSpec(s, lambda i, j: (0, 0))
    return pl.pallas_call(
        _rwkv_pre_body,
        grid=(b, t // tt),
        in_specs=[
            pl.BlockSpec((1, tt, c), lambda i, j: (i, j, 0)),
            pl.BlockSpec((1, 1, 1, c), lambda i, j: (i, j, 0, 0)),
            row(c), row(d), full(ww.shape), row(d), full(wa.shape), full(g2.shape), row(d), row(d),
        ],
        out_specs=[
            pl.BlockSpec((tt, SCAN_FIELDS // 2, 2, pairs, LANES), lambda i, j: (j, 0, 0, i, 0)),
            pl.BlockSpec((1, tt, d), lambda i, j: (i, j, 0)),
        ],
        out_shape=[jax.ShapeDtypeStruct((t, SCAN_FIELDS // 2, 2, b * pairs, LANES), F32),
                   jax.ShapeDtypeStruct((b, t, d), F32)],
        scratch_shapes=[pltpu.VMEM((tt + 8, c), F32)],
        compiler_params=_cparams(("parallel", "arbitrary")),
        name="rwkv_pre",
    )(p3, prev, mu.reshape(1, c), w0.reshape(1, d), ww, a0.reshape(1, d), wa, g2,
      k_k.reshape(1, d), k_a.reshape(1, d))


def _rwkv_scan_body(in_ref, s0_ref, lnw_ref, lnb_ref, rk_ref, y_ref, s_ref, xa_ref, xb_ref, kk_ref, b_ref):
    n = RW_HEAD_DIM
    steps = in_ref.shape[0]
    n_fields = in_ref.shape[1]
    half = in_ref.shape[3]
    lanes = 2 * half
    fields_per_trip = -(-n_fields // (n // UPDATE_UNROLL))

    def load_field(t, dst_ref, fp):
        tile = in_ref[t, fp].reshape(lanes, 2 * n)
        dst_ref[pl.ds(pl.multiple_of(fp * 2 * n, 2 * n), 2 * n), :] = tile.T

    def normalise_key(x_ref):
        kk_raw = x_ref[4 * n:5 * n, :]
        norm = jnp.sqrt(jnp.sum(kk_raw * kk_raw, axis=0, keepdims=True))
        kk = kk_raw / jnp.maximum(norm, 1e-12)
        kk_ref[...] = kk
        b_ref[...] = kk * x_ref[5 * n:6 * n, :]

    def one_step(t, x_ref, next_ref):
        t_next = jnp.minimum(t + 1, steps - 1)

        def sk_rows(jb, acc):
            for u in range(SCAN_UNROLL):
                j = jb * SCAN_UNROLL + u
                acc = acc + s_ref[j] * kk_ref[pl.ds(j, 1), :]
            return acc

        sk = lax.fori_loop(0, n // SCAN_UNROLL, sk_rows, jnp.zeros((n, lanes), F32))
        v = x_ref[2 * n:3 * n, :]

        def update_rows(jb, acc):
            for f in range(fields_per_trip):
                load_field(t_next, next_ref, jnp.minimum(jb * fields_per_trip + f, n_fields - 1))
            for u in range(UPDATE_UNROLL):
                j = jb * UPDATE_UNROLL + u
                r_j = x_ref[pl.ds(j, 1), :]
                k_j = x_ref[pl.ds(n + j, 1), :]
                w_j = x_ref[pl.ds(3 * n + j, 1), :]
                s_j = s_ref[j] * w_j - sk * b_ref[pl.ds(j, 1), :] + v * k_j
                s_ref[j] = s_j
                acc = acc + s_j * r_j
            return acc

        y = lax.fori_loop(0, n // UPDATE_UNROLL, update_rows, jnp.zeros((n, lanes), F32))
        normalise_key(next_ref)
        mean = jnp.sum(y, axis=0, keepdims=True) * (1.0 / n)
        yc = y - mean
        var = jnp.sum(yc * yc, axis=0, keepdims=True) * (1.0 / n)
        y_norm = yc * lax.rsqrt(var + RW_LN_EPS) * lnw_ref[...] + lnb_ref[...]
        rkr = x_ref[0:n, :] * x_ref[n:2 * n, :] * rk_ref[...]
        out = y_norm + jnp.sum(rkr, axis=0, keepdims=True) * v
        y_ref[t] = jnp.concatenate([out[:, :half], out[:, half:]], axis=0).T

    @pl.when(pl.program_id(1) == 0)
    def _():
        s_ref[...] = s0_ref[...]

    for c in range(n_fields):
        load_field(0, xa_ref, c)
    normalise_key(xa_ref)

    def step_pair(tp, carry):
        one_step(2 * tp, xa_ref, xb_ref)
        one_step(2 * tp + 1, xb_ref, xa_ref)
        return carry

    lax.fori_loop(0, steps // 2, step_pair, 0)


def _rwkv_scan(pre5, s0, lnw, lnb, rk, tc, seqs):
    t, field_pairs, _, rows, _ = pre5.shape
    n = RW_HEAD_DIM
    pairs = RW_HEADS // 2
    x_rows = field_pairs * 2 * n
    lb = seqs * RW_HEADS
    assert tc % 2 == 0 and t % tc == 0
    tab = pl.BlockSpec((n, lb), lambda l, i: (0, 0))
    return pl.pallas_call(
        _rwkv_scan_body,
        grid=(rows // (seqs * pairs), t // tc),
        in_specs=[
            pl.BlockSpec((tc, field_pairs, 2, seqs * pairs, LANES), lambda l, i: (i, 0, 0, l, 0)),
            pl.BlockSpec((n, n, lb), lambda l, i: (0, 0, l)),
            tab, tab, tab,
        ],
        out_specs=[
            pl.BlockSpec((tc, seqs * pairs, LANES), lambda l, i: (i, l, 0)),
            pl.BlockSpec((n, n, lb), lambda l, i: (0, 0, l)),
        ],
        out_shape=[jax.ShapeDtypeStruct((t, rows, LANES), F32),
                   jax.ShapeDtypeStruct((n, n, rows * 2), F32)],
        scratch_shapes=[pltpu.VMEM((x_rows, lb), F32), pltpu.VMEM((x_rows, lb), F32),
                        pltpu.VMEM((n, lb), F32), pltpu.VMEM((n, lb), F32)],
        compiler_params=_cparams(("parallel", "arbitrary")),
        name="rwkv_scan",
    )(pre5, s0, lnw, lnb, rk)


GATHER_PAGES = 8


def _gather_body(pt_ref, *refs):
    page_refs, (cmp_ref, sel_ref) = refs[:-2], refs[-2:]
    half = 2 * NSA_KV_DIM
    for k, x_ref in enumerate(page_refs):
        rows = x_ref.shape[1]
        cmp_ref[0, k * rows:(k + 1) * rows, :] = x_ref[0, :, :half]
        sel_ref[0, k * rows:(k + 1) * rows, :] = x_ref[0, :, half:].astype(BF16)


def _gather_pages(page_table, pool):
    bs, n_pages = page_table.shape
    _, page, cols = pool.shape
    half = cols // 2
    per_step = math.gcd(GATHER_PAGES, n_pages)

    def page_spec(k):
        return pl.BlockSpec((1, page, cols), lambda b, p, pt: (pt[b, p * per_step + k], 0, 0))

    return pl.pallas_call(
        _gather_body,
        grid_spec=pltpu.PrefetchScalarGridSpec(
            num_scalar_prefetch=1,
            grid=(bs, n_pages // per_step),
            in_specs=[page_spec(k) for k in range(per_step)],
            out_specs=[pl.BlockSpec((1, per_step * page, half), lambda b, p, pt: (b, p, 0)),
                       pl.BlockSpec((1, per_step * page, half), lambda b, p, pt: (b, p, 0))],
        ),
        out_shape=[jax.ShapeDtypeStruct((bs, n_pages * page, half), F32),
                   jax.ShapeDtypeStruct((bs, n_pages * page, half), BF16)],
        compiler_params=_cparams(("parallel", "arbitrary")),
        name="page_gather",
    )(page_table, *([pool] * per_step))


def _cmp_mlp_body(x_ref, pe_ref, w1_ref, w2_ref, o_ref, *, row_stride):
    hd = NSA_HEAD_DIM
    for kg in range(2 * NSA_KV_HEADS):
        ch = kg // NSA_KV_HEADS
        acc = None
        for r in range(CMP_BLOCK):
            xr = x_ref[:, r * row_stride + kg * hd:r * row_stride + (kg + 1) * hd]
            xr = (xr + pe_ref[r:r + 1, kg * hd:(kg + 1) * hd]).astype(BF16)
            part = _dot(xr, w1_ref[ch, r * hd:(r + 1) * hd, :])
            acc = part if acc is None else acc + part
        hid = jax.nn.gelu(acc, approximate=True)
        o_ref[:, kg * hd:(kg + 1) * hd] = _dot(hid.astype(BF16), w2_ref[ch])


def _cmp_mlp(blocks, pe_rows, w1, w2, row_stride, nb):
    n_blocks, width = blocks.shape
    out_w = 2 * NSA_KV_DIM
    return pl.pallas_call(
        functools.partial(_cmp_mlp_body, row_stride=row_stride),
        grid=(n_blocks // nb,),
        in_specs=[
            pl.BlockSpec((nb, width), lambda i: (i, 0)),
            pl.BlockSpec(pe_rows.shape, lambda i: (0, 0)),
            pl.BlockSpec(w1.shape, lambda i: (0, 0, 0)),
            pl.BlockSpec(w2.shape, lambda i: (0, 0, 0)),
        ],
        out_specs=pl.BlockSpec((nb, out_w), lambda i: (i, 0)),
        out_shape=jax.ShapeDtypeStruct((n_blocks, out_w), F32),
        compiler_params=_cparams(("parallel",)),
        name="cmp_mlp",
    )(blocks, pe_rows, w1, w2)


N_BIAS_TILES = -(-(FAR_DIST + LANES - 1) // LANES) + 1


def _bias_tiles_body(rb_ref, o_ref):
    h = pl.program_id(0)
    iq = lax.broadcasted_iota(jnp.int32, (Q_BLOCK, LANES), 0)
    ik = lax.broadcasted_iota(jnp.int32, (Q_BLOCK, LANES), 1)
    for m in range(N_BIAS_TILES):
        dist = jnp.maximum(m * LANES + iq - ik, 0)
        val = jnp.full(dist.shape, rb_ref[0, h], F32)
        for b, t in enumerate(BUCKET_THR):
            val = jnp.where(dist >= t, rb_ref[b + 1, h], val)
        o_ref[0, m] = val


def _bias_tiles(rel_bias):
    return pl.pallas_call(
        _bias_tiles_body,
        grid=(NSA_HEADS,),
        in_specs=[pl.BlockSpec(memory_space=pltpu.SMEM)],
        out_specs=pl.BlockSpec((1, N_BIAS_TILES, Q_BLOCK, LANES), lambda h: (h, 0, 0, 0)),
        out_shape=jax.ShapeDtypeStruct((NSA_HEADS, N_BIAS_TILES, Q_BLOCK, LANES), F32),
        compiler_params=_cparams(("arbitrary",)),
        name="bias_tiles",
    )(rel_bias)


def _nsa_prompt_body(rb_ref, q_ref, gate_ref, sel_ref, win_ref, kc_ref, bt_ref, y_ref, selb_ref, winb_ref,
                     *, n_sel):
    hd = NSA_HEAD_DIM
    qb = Q_BLOCK
    i = pl.program_id(1)
    scale = hd ** -0.5

    @pl.when(i == 0)
    def _():
        selb_ref[...] = sel_ref[0].astype(BF16)
        winb_ref[...] = win_ref[0].astype(BF16)

    q_bf = q_ref[0].astype(BF16)
    gates = jax.nn.sigmoid(gate_ref[0])
    iq = lax.broadcasted_iota(jnp.int32, (qb, LANES), 0)
    ik = lax.broadcasted_iota(jnp.int32, (qb, LANES), 1)
    pos = i * qb + iq
    d_cmp = pos - (ik * CMP_BLOCK + CMP_BLOCK - 1)
    bias_cmp = _bias_for_heads(d_cmp, rb_ref, range(NSA_HEADS))
    vis_cmp = _stack_heads(d_cmp >= 0)
    top_n = min(TOP_N, n_sel)
    sel_tiles = SEL_TILES if selb_ref.shape[0] % (SEL_TILES * LANES) == 0 else 1
    sel_keys = sel_tiles * LANES
    win_tiles = min(WINDOW // LANES + 1, winb_ref.shape[0] // LANES)
    win_keys = win_tiles * LANES
    iq_s = lax.broadcasted_iota(jnp.int32, (qb, sel_keys), 0)
    ik_s = lax.broadcasted_iota(jnp.int32, (qb, sel_keys), 1)
    iq_w = lax.broadcasted_iota(jnp.int32, (qb, win_keys), 0)
    ik_w = lax.broadcasted_iota(jnp.int32, (qb, win_keys), 1)
    groups = [[g * NSA_HPG + h for h in range(NSA_HPG)] for g in range(NSA_KV_HEADS)]

    def key_cols(g):
        return slice(g * hd, (g + 1) * hd), slice(NSA_KV_DIM + g * hd, NSA_KV_DIM + (g + 1) * hd)

    def tile_bias(heads, first_tile, n_tiles):
        cols = []
        for k in range(n_tiles):
            m = jnp.clip(i - (first_tile + k), 0, N_BIAS_TILES - 1)
            cols.append(jnp.concatenate([bt_ref[hh, m] for hh in heads], axis=0))
        return cols[0] if n_tiles == 1 else jnp.concatenate(cols, axis=1)

    qgs, o_cmps, values = [], [], []
    for g, heads in enumerate(groups):
        k_cols, v_cols = key_cols(g)
        qg = jnp.concatenate([q_bf[:, hh * hd:(hh + 1) * hd] for hh in heads], axis=0)
        o_cmp, p_cmp = _cmp_branch(qg, kc_ref[0, :, k_cols].astype(BF16), kc_ref[0, :, v_cols].astype(BF16),
                                   jnp.concatenate([bias_cmp[hh] for hh in heads], axis=0), vis_cmp, scale)
        qgs.append(qg)
        o_cmps.append(o_cmp)
        values.append(_selection_values(p_cmp, qb, pos, n_sel))
    chosen = _topk_mask(jnp.concatenate(values, axis=0), top_n).astype(BF16)
    chosens = [chosen[g * qb:(g + 1) * qb] for g in range(NSA_KV_HEADS)]

    def sel_step(j, carries):
        row0 = pl.multiple_of(j * sel_keys, sel_keys)
        causal = i * qb + iq_s - (j * sel_keys + ik_s) >= 0
        out = []
        for g, heads in enumerate(groups):
            k_cols, v_cols = key_cols(g)
            mask = _stack_heads(_expand_selection(chosens[g], j, sel_keys) & causal)
            out.append(_softmax_update(carries[g], qgs[g], selb_ref[pl.ds(row0, sel_keys), k_cols],
                                       selb_ref[pl.ds(row0, sel_keys), v_cols],
                                       tile_bias(heads, j * sel_tiles, sel_tiles), mask, scale))
        return tuple(out)

    n_steps = (i + sel_tiles) // sel_tiles
    sel_carries = lax.fori_loop(0, n_steps, sel_step, tuple(_softmax_init(NSA_HPG * qb) for _ in groups))

    first = jnp.maximum(i - (win_tiles - 1), 0)
    row0 = pl.multiple_of(first * LANES, LANES)
    d_win = i * qb + iq_w - (first * LANES + ik_w)
    mask_win = _stack_heads((d_win >= 0) & (d_win < WINDOW))
    for g, heads in enumerate(groups):
        k_cols, v_cols = key_cols(g)
        o_sel = _softmax_finish(sel_carries[g])
        o_win = _softmax_finish(_softmax_update(
            _softmax_init(NSA_HPG * qb), qgs[g], winb_ref[pl.ds(row0, win_keys), k_cols],
            winb_ref[pl.ds(row0, win_keys), v_cols], tile_bias(heads, first, win_tiles), mask_win, scale))
        for h, hh in enumerate(heads):
            rows = slice(h * qb, (h + 1) * qb)
            out = (gates[:, hh:hh + 1] * o_cmps[g][rows]
                   + gates[:, NSA_HEADS + hh:NSA_HEADS + hh + 1] * o_sel[rows]
                   + gates[:, 2 * NSA_HEADS + hh:2 * NSA_HEADS + hh + 1] * o_win[rows])
            y_ref[0, :, hh * hd:(hh + 1) * hd] = out.astype(BF16)


def _nsa_prompt(rel_bias, pb3, kc, bias_tiles):
    b, t, _ = pb3.shape
    qb = Q_BLOCK
    half = 2 * NSA_KV_DIM
    return pl.pallas_call(
        functools.partial(_nsa_prompt_body, n_sel=t // SEL_BLOCK),
        grid=(b, t // qb),
        in_specs=[
            pl.BlockSpec(memory_space=pltpu.SMEM),
            pl.BlockSpec((1, qb, NSA_DIM), lambda s, i: (s, i, PB_Q // NSA_DIM)),
            pl.BlockSpec((1, qb, LANES), lambda s, i: (s, i, PB_GATE // LANES)),
            pl.BlockSpec((1, t, half), lambda s, i: (s, 0, PB_SEL // half)),
            pl.BlockSpec((1, t, half), lambda s, i: (s, 0, PB_WIN // half)),
            pl.BlockSpec((1,) + kc.shape[1:], lambda s, i: (s, 0, 0)),
            pl.BlockSpec(bias_tiles.shape, lambda s, i: (0, 0, 0, 0)),
        ],
        out_specs=pl.BlockSpec((1, qb, NSA_DIM), lambda s, i: (s, i, 0)),
        out_shape=jax.ShapeDtypeStruct((b, t, NSA_DIM), BF16),
        scratch_shapes=[pltpu.VMEM((t, half), BF16), pltpu.VMEM((t, half), BF16)],
        compiler_params=_cparams(("parallel", "arbitrary")),
        name="nsa_prompt",
    )(rel_bias, pb3, pb3, pb3, pb3, kc, bias_tiles)


SEL_TILES = 2
SAMPLE_ROWS = 8
SAMPLE_TILE_KEYS = 1024


def _nsa_sample_body(rb_ref, q_ref, gate_ref, kc_ref, past_ref, new_sel_ref, cwin_ref, new_win_ref, y_ref,
                     *, past_len):
    hd = NSA_HEAD_DIM
    qb = SAMPLE_ROWS
    scale = hd ** -0.5
    n_cmp = kc_ref.shape[1]
    win_rows = cwin_ref.shape[1]
    k_off = past_len - win_rows
    tile_keys = min(SAMPLE_TILE_KEYS, past_len)
    assert past_len % tile_keys == 0

    q_bf = q_ref[0].astype(BF16)
    gates = jax.nn.sigmoid(gate_ref[0])
    iq = lax.broadcasted_iota(jnp.int32, (qb, LANES), 0)
    ik = lax.broadcasted_iota(jnp.int32, (qb, LANES), 1)
    pos_t = past_len + lax.broadcasted_iota(jnp.int32, (qb, tile_keys), 0)
    ik_t = lax.broadcasted_iota(jnp.int32, (qb, tile_keys), 1)
    d_win = (past_len + lax.broadcasted_iota(jnp.int32, (qb, win_rows), 0)
             - (k_off + lax.broadcasted_iota(jnp.int32, (qb, win_rows), 1)))
    tq_c = lax.broadcasted_iota(jnp.int32, (qb, n_cmp), 0)
    n_c = lax.broadcasted_iota(jnp.int32, (qb, n_cmp), 1)
    pos_c = past_len + tq_c
    d_cmp = pos_c - (n_c * CMP_BLOCK + CMP_BLOCK - 1)
    bias_cmp = _bias_for_heads(d_cmp, rb_ref, range(NSA_HEADS))
    vis_cmp = _stack_heads(d_cmp >= 0)
    n_sel_past = past_len // SEL_BLOCK
    picks = min(TOP_N, n_sel_past + 1) - 1

    groups = [[g * NSA_HPG + h for h in range(NSA_HPG)] for g in range(NSA_KV_HEADS)]

    def key_cols(g):
        return slice(g * hd, (g + 1) * hd), slice(NSA_KV_DIM + g * hd, NSA_KV_DIM + (g + 1) * hd)

    qgs, o_cmps, values = [], [], []
    for g, heads in enumerate(groups):
        k_cols, v_cols = key_cols(g)
        qg = jnp.concatenate([q_bf[:, hh * hd:(hh + 1) * hd] for hh in heads], axis=0)
        o_cmp, p_cmp = _cmp_branch(qg, kc_ref[0, :, k_cols].astype(BF16), kc_ref[0, :, v_cols].astype(BF16),
                                   jnp.concatenate([bias_cmp[hh] for hh in heads], axis=0), vis_cmp, scale)
        qgs.append(qg)
        o_cmps.append(o_cmp)
        values.append(_selection_values(p_cmp, qb, pos_c, n_sel_past))
    chosen = _topk_mask(jnp.concatenate(values, axis=0), picks).astype(BF16)
    chosens = [chosen[g * qb:(g + 1) * qb] for g in range(NSA_KV_HEADS)]

    def sel_step(j, carries):
        row0 = pl.multiple_of(j * tile_keys, tile_keys)
        dist = pos_t - (j * tile_keys + ik_t)
        out = []
        for g, heads in enumerate(groups):
            k_cols, v_cols = key_cols(g)
            mask = _stack_heads(_expand_selection(chosens[g], j, tile_keys))
            bias = jnp.concatenate(_bias_for_heads(dist, rb_ref, heads), axis=0)
            out.append(_softmax_update(carries[g], qgs[g], past_ref[0, pl.ds(row0, tile_keys), k_cols],
                                       past_ref[0, pl.ds(row0, tile_keys), v_cols], bias, mask, scale))
        return tuple(out)

    sel_carries = lax.fori_loop(0, past_len // tile_keys, sel_step,
                                tuple(_softmax_init(NSA_HPG * qb) for _ in groups))
    d_new = iq - ik
    mask_new = _stack_heads(d_new >= 0)
    mask_win = _stack_heads((d_win >= 0) & (d_win < WINDOW))
    for g, heads in enumerate(groups):
        k_cols, v_cols = key_cols(g)
        qg, o_cmp = qgs[g], o_cmps[g]
        bias_new = jnp.concatenate(_bias_for_heads(d_new, rb_ref, heads), axis=0)
        o_sel = _softmax_finish(_softmax_update(sel_carries[g], qg, new_sel_ref[0, :, k_cols],
                                                new_sel_ref[0, :, v_cols], bias_new, mask_new, scale))
        bias_win = jnp.concatenate(_bias_for_heads(d_win, rb_ref, heads), axis=0)
        carry = _softmax_update(_softmax_init(NSA_HPG * qb), qg, cwin_ref[0, :, k_cols].astype(BF16),
                                cwin_ref[0, :, v_cols].astype(BF16), bias_win, mask_win, scale)
        o_win = _softmax_finish(_softmax_update(carry, qg, new_win_ref[0, :, k_cols], new_win_ref[0, :, v_cols],
                                                bias_new, mask_new, scale))

        for h, hh in enumerate(heads):
            rows = slice(h * qb, (h + 1) * qb)
            out = (gates[:, hh:hh + 1] * o_cmp[rows]
                   + gates[:, NSA_HEADS + hh:NSA_HEADS + hh + 1] * o_sel[rows]
                   + gates[:, 2 * NSA_HEADS + hh:2 * NSA_HEADS + hh + 1] * o_win[rows])
            y_ref[0, :, hh * hd:(hh + 1) * hd] = out.astype(BF16)


def _nsa_sample(rel_bias, q8, gate8, kc, past_sel, new_sel, cwin, new_win, past_len):
    bs = q8.shape[0]
    blk = lambda a: pl.BlockSpec((1,) + a.shape[1:], lambda s: (s, 0, 0))
    return pl.pallas_call(
        functools.partial(_nsa_sample_body, past_len=past_len),
        grid=(bs,),
        in_specs=[pl.BlockSpec(memory_space=pltpu.SMEM), blk(q8), blk(gate8), blk(kc), blk(past_sel),
                  blk(new_sel), blk(cwin), blk(new_win)],
        out_specs=pl.BlockSpec((1, SAMPLE_ROWS, NSA_DIM), lambda s: (s, 0, 0)),
        out_shape=jax.ShapeDtypeStruct((bs, SAMPLE_ROWS, NSA_DIM), BF16),
        compiler_params=_cparams(("parallel",)),
        name="nsa_sample",
    )(rel_bias, q8, gate8, kc, past_sel, new_sel, cwin, new_win)


def _merge_body(x_ref, yrw_ref, g_ref, ynsa_ref, gates_ref, wrw_ref, wnsa_ref, wout_ref, post_ref, o_ref):
    tm, d = x_ref.shape
    y_rw = jnp.concatenate([yrw_ref[:, pair, :] for pair in range(RW_DIM // LANES)], axis=1)
    y_rw = (y_rw * g_ref[...]).astype(BF16)
    merged = (jax.nn.sigmoid(gates_ref[:, :d]) * _dot(y_rw, wrw_ref[...])
              + jax.nn.sigmoid(gates_ref[:, d:]) * _dot(ynsa_ref[...], wnsa_ref[...]))
    o_ref[...] = x_ref[...] + _rms(_dot(merged.astype(BF16), wout_ref[...]), post_ref[...])


def _merge(x, y_rw, g_rw, y_nsa, gates, w_rw, w_nsa, w_out, post_g, tm):
    m, d = x.shape
    rows = lambda n: pl.BlockSpec((tm, n), lambda i: (i, 0))
    held = lambda a: pl.BlockSpec(a.shape, lambda i: (0, 0), pipeline_mode=pl.Buffered(1))
    return pl.pallas_call(
        _merge_body,
        grid=(m // tm,),
        in_specs=[rows(d), pl.BlockSpec((tm,) + y_rw.shape[1:], lambda i: (i, 0, 0)),
                  rows(RW_DIM), rows(NSA_DIM), rows(2 * d),
                  held(w_rw), held(w_nsa), held(w_out), pl.BlockSpec((1, d), lambda i: (0, 0))],
        out_specs=rows(d),
        out_shape=jax.ShapeDtypeStruct((m, d), F32),
        compiler_params=_cparams(("parallel",)),
        name="merge",
    )(x, y_rw, g_rw, y_nsa, gates, w_rw, w_nsa, w_out, post_g.reshape(1, d))


def _row_tile(m, want):
    return want if m % want == 0 else m


def _rwkv_mix(p3, shift0, s0, w):
    b, t, c = p3.shape
    n = RW_HEAD_DIM
    tt = 256 if t % 256 == 0 else t
    prev = jnp.concatenate([shift0[:, None, :], p3[:, tt - 1:t - 1:tt]], axis=1).reshape(b, t // tt, 1, c)
    pre, g_rw = _rwkv_pre(p3, prev, w["mu"], w["w0"], w["ww"], w["a0"], w["wa"], w["g2"], w["k_k"], w["k_a"], tt)
    seqs = min(b, LANES // RW_HEADS)
    groups = b // seqs
    pairs = RW_HEADS // 2
    s0_t = s0.reshape(groups, seqs, pairs, 2, n, n).transpose(5, 4, 0, 3, 1, 2).reshape(n, n, b * RW_HEADS)
    per_lane = lambda v: jnp.broadcast_to(v.reshape(pairs, 2, n).transpose(2, 1, 0)[:, :, None, :],
                                          (n, 2, seqs, pairs)).reshape(n, seqs * RW_HEADS)
    tc = 32 if t % 32 == 0 else t
    y_t, s_t = _rwkv_scan(pre, s0_t, per_lane(w["lnx_w"]), per_lane(w["lnx_b"]), per_lane(w["r_k"]), tc, seqs)
    s_fin = s_t.reshape(n, n, groups, 2, seqs, pairs).transpose(2, 4, 5, 3, 1, 0).reshape(b, RW_HEADS, n, n)
    y_tok = y_t.reshape(t, b, pairs, LANES).transpose(1, 0, 2, 3).reshape(b * t, pairs, LANES)
    return y_tok, g_rw.reshape(b * t, RW_DIM), s_fin


def _layer(x, w, rel_bias, shift0, s0, past):
    b, t, d = x.shape
    m = b * t
    tm = _row_tile(m, 512)
    x1 = _ffn(x.reshape(m, d), w["f1_pre"], w["f1_post"], w["f1_w1"], w["f1_w3"], w["f1_w2"], tm, 512)
    p_a = _norm_matmul(x1, w["mix_pre"], w["w_in_a"], tm, "proj_rwkv")
    p_b = _norm_matmul(x1, w["mix_pre"], w["w_in_b"], tm, "proj_nsa")
    p_c = _norm_matmul(x1, w["mix_pre"], w["w_in_c"], tm, "proj_gates")

    p3 = p_a.reshape(b, t, RW_PROJ)
    y_rw, g_rw, s_fin = _rwkv_mix(p3, shift0, s0, w)

    pb3 = p_b.reshape(b, t, PB_COLS)
    kv_new = pb3[:, :, PB_KV:PB_WIN]
    win_new = pb3[:, :, PB_WIN:PB_GATE]
    row_w = 4 * NSA_KV_DIM
    if past is None:
        blocks = kv_new.reshape(m // CMP_BLOCK, CMP_BLOCK * row_w)
        kc = _cmp_mlp(blocks, w["cmp_pe"], w["cmp_w1"], w["cmp_w2"], row_w, min(64, blocks.shape[0]))
        n_cmp = t // CMP_BLOCK
        kc = jnp.pad(kc.reshape(b, n_cmp, 2 * NSA_KV_DIM), ((0, 0), (0, -n_cmp % LANES), (0, 0)))
        y_nsa = _nsa_prompt(rel_bias, pb3, kc, _bias_tiles(rel_bias)).reshape(m, NSA_DIM)
        win_out = win_new[:, t - min(WINDOW, t):]
    else:
        page_table, pool, cache_win = past
        past_len = page_table.shape[1] * pool.shape[1]
        past_cmp, past_sel = _gather_pages(page_table, pool.reshape(pool.shape[0], pool.shape[1], row_w))
        half = 2 * NSA_KV_DIM
        blocks = past_cmp.reshape(b * past_len // CMP_BLOCK, CMP_BLOCK * half)
        kc = _cmp_mlp(blocks, w["cmp_pe"], w["cmp_w1"], w["cmp_w2"], half, min(128, blocks.shape[0]))
        n_cmp = past_len // CMP_BLOCK
        kc = jnp.pad(kc.reshape(b, n_cmp, half), ((0, 0), (0, -n_cmp % LANES), (0, 0)))
        pad_q = lambda a, rows: jnp.pad(a, ((0, 0), (0, rows - t), (0, 0)))
        cwin = cache_win.reshape(b, cache_win.shape[1], half)
        y8 = _nsa_sample(rel_bias, pad_q(pb3[:, :, :NSA_DIM], SAMPLE_ROWS), pad_q(pb3[:, :, PB_GATE:], SAMPLE_ROWS),
                         kc, past_sel, pad_q(pb3[:, :, PB_SEL:PB_WIN].astype(BF16), LANES), cwin,
                         pad_q(win_new.astype(BF16), LANES), past_len)
        y_nsa = y8[:, :t].reshape(m, NSA_DIM)
        win_all = jnp.concatenate([cwin, win_new], axis=1)
        win_out = win_all[:, win_all.shape[1] - min(WINDOW, win_all.shape[1]):]

    x2 = _merge(x1, y_rw, g_rw, y_nsa, p_c, w["w_br_rw"], w["w_br_nsa"], w["w_out"], w["mix_post"],
                _row_tile(m, 256))
    y = _ffn(x2, w["f2_pre"], w["f2_post"], w["f2_w1"], w["f2_w3"], w["f2_w2"], tm, 512)
    g, hd = NSA_KV_HEADS, NSA_HEAD_DIM
    return (y.reshape(b, t, d), kv_new.reshape(b, t, 4, g, hd), win_out.reshape(b, -1, 2, g, hd), s_fin, p3[:, -1])


def _prepare_weights(l, ffn1_pre_g, ffn1_post_g, ffn1_w1, ffn1_w3, ffn1_w2, mix_pre_g, mix_post_g, w_in,
                     rw_mu, rw_w0, rw_w2, rw_a0, rw_a2, rw_g2, rw_k_k, rw_k_a, rw_r_k, rw_lnx_w, rw_lnx_b,
                     cmp_pe, cmp_w1, cmp_w2, w_br_rw, w_br_nsa, w_out,
                     ffn2_pre_g, ffn2_post_g, ffn2_w1, ffn2_w3, ffn2_w2):
    d = w_in.shape[1]
    wi = w_in[l]
    c_q = RW_PROJ
    c_kv = c_q + NSA_DIM
    c_gate = c_kv + 6 * NSA_KV_DIM
    c_grw = c_gate + 3 * NSA_HEADS
    w_in_b = jnp.concatenate([wi[:, c_q:c_gate], jnp.pad(wi[:, c_gate:c_grw], ((0, 0), (0, LANES - 3 * NSA_HEADS)))],
                             axis=1)
    zeros_w = jnp.zeros((RW_LORA_A, RW_DIM), F32)
    zeros_a = jnp.zeros((RW_LORA_W, RW_DIM), F32)
    pe = cmp_pe[l]
    pe_rows = jnp.broadcast_to(pe[:, :, None, :], (CMP_BLOCK, 2, NSA_KV_HEADS, NSA_HEAD_DIM))
    return {
        "f1_pre": ffn1_pre_g[l], "f1_post": ffn1_post_g[l],
        "f1_w1": ffn1_w1[l].astype(BF16), "f1_w3": ffn1_w3[l].astype(BF16), "f1_w2": ffn1_w2[l].astype(BF16),
        "mix_pre": mix_pre_g[l], "mix_post": mix_post_g[l],
        "w_in_a": wi[:, :RW_PROJ].astype(BF16), "w_in_b": w_in_b.astype(BF16), "w_in_c": wi[:, c_grw:].astype(BF16),
        "mu": rw_mu[l], "w0": rw_w0[l], "a0": rw_a0[l],
        "ww": jnp.concatenate([rw_w2[l], zeros_w], axis=0).astype(BF16),
        "wa": jnp.concatenate([zeros_a, rw_a2[l]], axis=0).astype(BF16),
        "g2": rw_g2[l].astype(BF16), "k_k": rw_k_k[l], "k_a": rw_k_a[l],
        "r_k": rw_r_k[l].reshape(-1), "lnx_w": rw_lnx_w[l], "lnx_b": rw_lnx_b[l],
        "cmp_pe": pe_rows.reshape(CMP_BLOCK, 2 * NSA_KV_DIM),
        "cmp_w1": cmp_w1[l].astype(BF16), "cmp_w2": cmp_w2[l].astype(BF16),
        "w_br_rw": w_br_rw[l].astype(BF16), "w_br_nsa": w_br_nsa[l].astype(BF16), "w_out": w_out[l].astype(BF16),
        "f2_pre": ffn2_pre_g[l], "f2_post": ffn2_post_g[l],
        "f2_w1": ffn2_w1[l].astype(BF16), "f2_w3": ffn2_w3[l].astype(BF16), "f2_w2": ffn2_w2[l].astype(BF16),
    }


def kernel(x_prompt, x_sample, cache_kv, cache_win, state_rwkv, state_shift, page_table,
           ffn1_pre_g, ffn1_post_g, ffn1_w1, ffn1_w3, ffn1_w2, mix_pre_g, mix_post_g, w_in,
           rw_mu, rw_w0, rw_w2, rw_a0, rw_a2, rw_g2, rw_k_k, rw_k_a, rw_r_k, rw_lnx_w, rw_lnx_b,
           cmp_pe, cmp_w1, cmp_w2, w_br_rw, w_br_nsa, w_out,
           ffn2_pre_g, ffn2_post_g, ffn2_w1, ffn2_w3, ffn2_w2, rel_bias):
    depth = w_in.shape[0]
    b_p = x_prompt.shape[0]
    y_p, y_s = x_prompt, x_sample
    outs = [[] for _ in range(8)]
    for l in range(depth):
        w = _prepare_weights(l, ffn1_pre_g, ffn1_post_g, ffn1_w1, ffn1_w3, ffn1_w2, mix_pre_g, mix_post_g, w_in,
                             rw_mu, rw_w0, rw_w2, rw_a0, rw_a2, rw_g2, rw_k_k, rw_k_a, rw_r_k, rw_lnx_w, rw_lnx_b,
                             cmp_pe, cmp_w1, cmp_w2, w_br_rw, w_br_nsa, w_out,
                             ffn2_pre_g, ffn2_post_g, ffn2_w1, ffn2_w3, ffn2_w2)
        y_p, kv_p, win_p, rw_p, sh_p = _layer(
            y_p, w, rel_bias, jnp.zeros((b_p, RW_PROJ), F32),
            jnp.zeros((b_p, RW_HEADS, RW_HEAD_DIM, RW_HEAD_DIM), F32), None)
        y_s, kv_s, win_s, rw_s, sh_s = _layer(
            y_s, w, rel_bias, state_shift[l], state_rwkv[l], (page_table, cache_kv[l], cache_win[l]))
        for acc, v in zip(outs, (kv_p, kv_s, win_p, win_s, rw_p, rw_s, sh_p, sh_s)):
            acc.append(v)
    return (y_p, y_s) + tuple(jnp.stack(o) for o in outs)
```

```python
import functools
import math

import jax
import jax.numpy as jnp
from jax import lax
from jax.experimental import pallas as pl
from jax.experimental.pallas import tpu as pltpu

F32 = jnp.float32
BF16 = jnp.bfloat16

RMS_EPS = 1e-6
RW_HEADS = 16
RW_HEAD_DIM = 64
RW_DIM = RW_HEADS * RW_HEAD_DIM
RW_LORA_W = 64
RW_LORA_A = 64
RW_LORA_G = 128
RW_PROJ = 3 * RW_DIM + RW_LORA_W + RW_LORA_A + RW_LORA_G
RW_LN_EPS = 64e-5
NSA_HEADS = 8
NSA_KV_HEADS = 2
NSA_HPG = NSA_HEADS // NSA_KV_HEADS
NSA_HEAD_DIM = 128
NSA_DIM = NSA_HEADS * NSA_HEAD_DIM
NSA_KV_DIM = NSA_KV_HEADS * NSA_HEAD_DIM
CMP_BLOCK = 32
CMP_HID = 256
SEL_BLOCK = 64
SEL_SHIFT = SEL_BLOCK.bit_length() - 1
TOP_N = 16
WINDOW = 512
N_BUCKETS = 32
REL_MAX_EXACT = 16
REL_MAX_DIST = 1024
Q_BLOCK = 128
NEG_BIG = -1e30
FORCE_BONUS = 1e4
NEVER = -3e38
LANES = 128
SUBLANES = 8
SCAN_FIELDS = 6
VMEM_LIMIT = 56 * 1024 * 1024
SCAN_UNROLL = 8
UPDATE_UNROLL = 16

PB_Q = 0
PB_KV = NSA_DIM
PB_SEL = PB_KV + 2 * NSA_KV_DIM
PB_WIN = PB_KV + 4 * NSA_KV_DIM
PB_GATE = PB_WIN + 2 * NSA_KV_DIM
PB_COLS = PB_GATE + LANES


def _bucket_thresholds():
    thr = list(range(1, REL_MAX_EXACT + 1))
    n_log = N_BUCKETS - REL_MAX_EXACT
    ratio = REL_MAX_DIST // REL_MAX_EXACT
    n = REL_MAX_EXACT
    for k in range(1, n_log):
        while n ** n_log < REL_MAX_EXACT ** n_log * ratio ** k:
            n += 1
        thr.append(n)
    return thr


BUCKET_THR = _bucket_thresholds()
FAR_DIST = BUCKET_THR[-1]


def _cparams(sem):
    return pltpu.CompilerParams(dimension_semantics=sem, vmem_limit_bytes=VMEM_LIMIT)


def _rms(x, g):
    ms = jnp.mean(x * x, axis=-1, keepdims=True)
    return x * lax.rsqrt(ms + RMS_EPS) * g


def _dot(a, b):
    return jnp.dot(a, b, preferred_element_type=F32)


def _dot_nt(a, b):
    return lax.dot_general(a, b, (((1,), (1,)), ((), ())), preferred_element_type=F32)


def _bias_for_heads(dist, rb_ref, heads):
    n = jnp.maximum(dist, 0)
    reach = [n >= t for t in BUCKET_THR]
    out = []
    for h in heads:
        val = jnp.full(dist.shape, rb_ref[0, h], F32)
        for b, m in enumerate(reach):
            val = jnp.where(m, rb_ref[b + 1, h], val)
        out.append(val)
    return out


def _softmax_update(carry, q, kt, vt, bias, mask, scale):
    m_run, l_run, acc = carry
    s = _dot_nt(q, kt) * scale + bias
    s = jnp.where(mask, s, NEG_BIG)
    m_new = jnp.maximum(m_run, jnp.max(s, axis=-1, keepdims=True))
    alpha = jnp.exp(m_run - m_new)
    e = jnp.where(mask, jnp.exp(s - m_new), 0.0)
    l_new = alpha * l_run + jnp.sum(e, axis=-1, keepdims=True)
    acc = alpha * acc + _dot(e.astype(BF16), vt)
    return m_new, l_new, acc


def _softmax_init(rows):
    return (jnp.full((rows, 1), NEG_BIG, F32), jnp.zeros((rows, 1), F32),
            jnp.zeros((rows, NSA_HEAD_DIM), F32))


def _softmax_finish(carry):
    _, l_run, acc = carry
    return acc / jnp.maximum(l_run, 1e-30)


def _stack_heads(x):
    return jnp.concatenate([x] * NSA_HPG, axis=0)


def _topk_mask(val, k):
    lane = lax.broadcasted_iota(jnp.int32, val.shape, 1)
    width = val.shape[1]
    chosen = jnp.zeros(val.shape, F32)
    for _ in range(k):
        top = jnp.max(val, axis=-1, keepdims=True)
        first = jnp.min(jnp.where(val == top, lane, width), axis=-1, keepdims=True)
        hit = lane == first
        chosen = jnp.where(hit, 1.0, chosen)
        val = jnp.where(hit, NEVER, val)
    return chosen


def _pair_sum(x):
    parts = []
    for c in range(x.shape[1] // LANES):
        blk = x[:, c * LANES:(c + 1) * LANES]
        parts.append(blk + pltpu.roll(blk, LANES - 1, 1))
    return parts[0] if len(parts) == 1 else jnp.concatenate(parts, axis=1)


def _cmp_branch(qg, kc_k, kc_v, bias, visible, scale):
    s = _dot_nt(qg, kc_k) * scale + bias
    s = jnp.where(visible, s, NEG_BIG)
    e = jnp.where(visible, jnp.exp(s - jnp.max(s, axis=-1, keepdims=True)), 0.0)
    p = e / jnp.maximum(jnp.sum(e, axis=-1, keepdims=True), 1e-30)
    return _dot(p.astype(BF16), kc_v), p


def _selection_values(p, qb, pos, n_sel):
    imp = p[0:qb]
    for h in range(1, NSA_HPG):
        imp = imp + p[h * qb:(h + 1) * qb]
    imp = _pair_sum(imp)
    lane = lax.broadcasted_iota(jnp.int32, imp.shape, 1)
    sel_id = lane >> 1
    cur = pos >> SEL_SHIFT
    forced = (sel_id == 0) | (sel_id == cur) | (sel_id == cur - 1)
    val = jnp.where(forced, imp + FORCE_BONUS, imp)
    val = jnp.where(sel_id * SEL_BLOCK <= pos, val, NEG_BIG)
    return jnp.where(((lane & 1) == 0) & (sel_id < n_sel), val, NEVER)


def _expand_selection(chosen_bf16, tile, tile_keys=LANES):
    width = chosen_bf16.shape[1]
    n_idx = lax.broadcasted_iota(jnp.int32, (width, tile_keys), 0)
    k_idx = lax.broadcasted_iota(jnp.int32, (width, tile_keys), 1)
    target = 2 * ((tile * tile_keys + k_idx) >> SEL_SHIFT)
    expand = jnp.where(n_idx == target, 1.0, 0.0).astype(BF16)
    return _dot(chosen_bf16, expand) > 0.5


def _ffn_body(x_ref, pre_ref, post_ref, w1_ref, w3_ref, w2_ref, o_ref, h_ref, acc_ref):
    j = pl.program_id(1)

    @pl.when(j == 0)
    def _():
        h_ref[...] = _rms(x_ref[...], pre_ref[...]).astype(BF16)
        acc_ref[...] = jnp.zeros_like(acc_ref)

    h = h_ref[...]
    a = _dot(h, w1_ref[...])
    b = _dot(h, w3_ref[...])
    u = (a * jax.nn.sigmoid(a)) * b
    acc_ref[...] += _dot(u.astype(BF16), w2_ref[...])

    @pl.when(j == pl.num_programs(1) - 1)
    def _():
        o_ref[...] = x_ref[...] + 0.5 * _rms(acc_ref[...], post_ref[...])


def _ffn(x, pre_g, post_g, w1, w3, w2, tm, tf):
    m, d = x.shape
    f = w1.shape[1]
    return pl.pallas_call(
        _ffn_body,
        grid=(m // tm, f // tf),
        in_specs=[
            pl.BlockSpec((tm, d), lambda i, j: (i, 0)),
            pl.BlockSpec((1, d), lambda i, j: (0, 0)),
            pl.BlockSpec((1, d), lambda i, j: (0, 0)),
            pl.BlockSpec((d, tf), lambda i, j: (0, j)),
            pl.BlockSpec((d, tf), lambda i, j: (0, j)),
            pl.BlockSpec((tf, d), lambda i, j: (j, 0)),
        ],
        out_specs=pl.BlockSpec((tm, d), lambda i, j: (i, 0)),
        out_shape=jax.ShapeDtypeStruct((m, d), F32),
        scratch_shapes=[pltpu.VMEM((tm, d), BF16), pltpu.VMEM((tm, d), F32)],
        compiler_params=_cparams(("parallel", "arbitrary")),
        name="ffn",
    )(x, pre_g.reshape(1, d), post_g.reshape(1, d), w1, w3, w2)


def _norm_matmul_body(x_ref, g_ref, w_ref, o_ref, h_ref):
    @pl.when(pl.program_id(1) == 0)
    def _():
        h_ref[...] = _rms(x_ref[...], g_ref[...]).astype(BF16)

    o_ref[...] = _dot(h_ref[...], w_ref[...])


def _norm_matmul(x, g, w, tm, name):
    m, d = x.shape
    n = w.shape[1]
    return pl.pallas_call(
        _norm_matmul_body,
        grid=(m // tm, 1),
        in_specs=[
            pl.BlockSpec((tm, d), lambda i, j: (i, 0)),
            pl.BlockSpec((1, d), lambda i, j: (0, 0)),
            pl.BlockSpec((d, n), lambda i, j: (0, 0), pipeline_mode=pl.Buffered(1)),
        ],
        out_specs=pl.BlockSpec((tm, n), lambda i, j: (i, 0)),
        out_shape=jax.ShapeDtypeStruct((m, n), F32),
        scratch_shapes=[pltpu.VMEM((tm, d), BF16)],
        compiler_params=_cparams(("parallel", "arbitrary")),
        name=name,
    )(x, g.reshape(1, d), w)


def _rwkv_pre_body(p_ref, prev_ref, mu_ref, w0_ref, ww_ref, a0_ref, wa_ref, g2_ref, kk_ref, ka_ref,
                   pre_ref, g_ref, buf_ref):
    tt = p_ref.shape[1]
    p = p_ref[0]
    buf_ref[8:8 + tt, :] = p
    buf_ref[7:8, :] = prev_ref[0, 0]
    p_prev = buf_ref[7:7 + tt, :]
    xs = p + (p_prev - p) * mu_ref[...]
    d = RW_DIM
    r = xs[:, 0:d]
    k = xs[:, d:2 * d]
    v = xs[:, 2 * d:3 * d]
    lora_in = xs[:, 3 * d:3 * d + RW_LORA_W + RW_LORA_A]
    gd = xs[:, 3 * d + RW_LORA_W + RW_LORA_A:]
    z = -(w0_ref[...] + _dot(jnp.tanh(lora_in).astype(BF16), ww_ref[...]))
    softplus = jnp.maximum(z, 0.0) + jnp.log(1.0 + jnp.exp(-jnp.abs(z)))
    decay = jnp.exp(-jnp.exp(-softplus - 0.5))
    a = jax.nn.sigmoid(a0_ref[...] + _dot(lora_in.astype(BF16), wa_ref[...]))
    g_ref[0] = _dot(jax.nn.sigmoid(gd).astype(BF16), g2_ref[...])

    n = RW_HEAD_DIM
    pairs = d // LANES
    low_half = lax.broadcasted_iota(jnp.int32, (tt, LANES), 1) < n

    def put(fp, xa, xb):
        for pair in range(pairs):
            col_a = xa[:, pair * LANES:(pair + 1) * LANES]
            col_b = xb[:, pair * LANES:(pair + 1) * LANES]
            pieces = (jnp.where(low_half, col_a, pltpu.roll(col_b, n, 1)),
                      jnp.where(low_half, pltpu.roll(col_a, n, 1), col_b))
            for parity, piece in enumerate(pieces):
                pre_ref[:, fp, parity, pair, :] = piece

    put(0, r, k * (1.0 + (a - 1.0) * ka_ref[...]))
    put(1, v, decay)
    put(2, k * kk_ref[...], a)


def _rwkv_pre(p3, prev, mu, w0, ww, a0, wa, g2, k_k, k_a, tt):
    b, t, c = p3.shape
    d = RW_DIM
    pairs = RW_HEADS // 2
    row = lambda n: pl.BlockSpec((1, n), lambda i, j: (0, 0))
    full = lambda s: pl.BlockSpec(s, lambda i, j: (0, 0))
    return pl.pallas_call(
        _rwkv_pre_body,
        grid=(b, t // tt),
        in_specs=[
            pl.BlockSpec((1, tt, c), lambda i, j: (i, j, 0)),
            pl.BlockSpec((1, 1, 1, c), lambda i, j: (i, j, 0, 0)),
            row(c), row(d), full(ww.shape), row(d), full(wa.shape), full(g2.shape), row(d), row(d),
        ],
        out_specs=[
            pl.BlockSpec((tt, SCAN_FIELDS // 2, 2, pairs, LANES), lambda i, j: (j, 0, 0, i, 0)),
            pl.BlockSpec((1, tt, d), lambda i, j: (i, j, 0)),
        ],
        out_shape=[jax.ShapeDtypeStruct((t, SCAN_FIELDS // 2, 2, b * pairs, LANES), F32),
                   jax.ShapeDtypeStruct((b, t, d), F32)],
        scratch_shapes=[pltpu.VMEM((tt + 8, c), F32)],
        compiler_params=_cparams(("parallel", "arbitrary")),
        name="rwkv_pre",
    )(p3, prev, mu.reshape(1, c), w0.reshape(1, d), ww, a0.reshape(1, d), wa, g2,
      k_k.reshape(1, d), k_a.reshape(1, d))


def _rwkv_scan_body(in_ref, s0_ref, lnw_ref, lnb_ref, rk_ref, y_ref, s_ref, xa_ref, xb_ref, kk_ref, b_ref,
                    out_ref):
    n = RW_HEAD_DIM
    steps = in_ref.shape[0]
    n_fields = in_ref.shape[1]
    half = in_ref.shape[3]
    lanes = 2 * half
    fields_per_trip = -(-n_fields // (n // UPDATE_UNROLL))

    def load_field(t, dst_ref, fp):
        tile = in_ref[t, fp].reshape(lanes, 2 * n)
        dst_ref[pl.ds(pl.multiple_of(fp * 2 * n, 2 * n), 2 * n), :] = tile.T

    def normalise_key(x_ref):
        kk_raw = x_ref[4 * n:5 * n, :]
        norm = jnp.sqrt(jnp.sum(kk_raw * kk_raw, axis=0, keepdims=True))
        kk = kk_raw / jnp.maximum(norm, 1e-12)
        kk_ref[...] = kk
        b_ref[...] = kk * x_ref[5 * n:6 * n, :]

    def one_step(t, x_ref, next_ref):
        t_next = jnp.minimum(t + 1, steps - 1)

        def sk_rows(jb, acc):
            for u in range(SCAN_UNROLL):
                j = jb * SCAN_UNROLL + u
                acc = acc + s_ref[j] * kk_ref[pl.ds(j, 1), :]
            return acc

        sk = lax.fori_loop(0, n // SCAN_UNROLL, sk_rows, jnp.zeros((n, lanes), F32))
        v = x_ref[2 * n:3 * n, :]

        def update_rows(jb, acc):
            for f in range(fields_per_trip):
                load_field(t_next, next_ref, jnp.minimum(jb * fields_per_trip + f, n_fields - 1))
            emit_output(jnp.maximum(t - 1, 0))
            for u in range(UPDATE_UNROLL):
                j = jb * UPDATE_UNROLL + u
                r_j = x_ref[pl.ds(j, 1), :]
                k_j = x_ref[pl.ds(n + j, 1), :]
                w_j = x_ref[pl.ds(3 * n + j, 1), :]
                s_j = s_ref[j] * w_j - sk * b_ref[pl.ds(j, 1), :] + v * k_j
                s_ref[j] = s_j
                acc = acc + s_j * r_j
            return acc

        y = lax.fori_loop(0, n // UPDATE_UNROLL, update_rows, jnp.zeros((n, lanes), F32))
        normalise_key(next_ref)
        mean = jnp.sum(y, axis=0, keepdims=True) * (1.0 / n)
        yc = y - mean
        var = jnp.sum(yc * yc, axis=0, keepdims=True) * (1.0 / n)
        y_norm = yc * lax.rsqrt(var + RW_LN_EPS) * lnw_ref[...] + lnb_ref[...]
        rkr = x_ref[0:n, :] * x_ref[n:2 * n, :] * rk_ref[...]
        out_ref[...] = y_norm + jnp.sum(rkr, axis=0, keepdims=True) * v

    def emit_output(t):
        out = out_ref[...]
        y_ref[t] = jnp.concatenate([out[:, :half], out[:, half:]], axis=0).T

    @pl.when(pl.program_id(1) == 0)
    def _():
        s_ref[...] = s0_ref[...]

    out_ref[...] = jnp.zeros_like(out_ref)
    for c in range(n_fields):
        load_field(0, xa_ref, c)
    normalise_key(xa_ref)

    def step_pair(tp, carry):
        one_step(2 * tp, xa_ref, xb_ref)
        one_step(2 * tp + 1, xb_ref, xa_ref)
        return carry

    lax.fori_loop(0, steps // 2, step_pair, 0)
    emit_output(steps - 1)


def _rwkv_scan(pre5, s0, lnw, lnb, rk, tc, seqs):
    t, field_pairs, _, rows, _ = pre5.shape
    n = RW_HEAD_DIM
    pairs = RW_HEADS // 2
    x_rows = field_pairs * 2 * n
    lb = seqs * RW_HEADS
    assert tc % 2 == 0 and t % tc == 0
    tab = pl.BlockSpec((n, lb), lambda l, i: (0, 0))
    return pl.pallas_call(
        _rwkv_scan_body,
        grid=(rows // (seqs * pairs), t // tc),
        in_specs=[
            pl.BlockSpec((tc, field_pairs, 2, seqs * pairs, LANES), lambda l, i: (i, 0, 0, l, 0)),
            pl.BlockSpec((n, n, lb), lambda l, i: (0, 0, l)),
            tab, tab, tab,
        ],
        out_specs=[
            pl.BlockSpec((tc, seqs * pairs, LANES), lambda l, i: (i, l, 0)),
            pl.BlockSpec((n, n, lb), lambda l, i: (0, 0, l)),
        ],
        out_shape=[jax.ShapeDtypeStruct((t, rows, LANES), F32),
                   jax.ShapeDtypeStruct((n, n, rows * 2), F32)],
        scratch_shapes=[pltpu.VMEM((x_rows, lb), F32), pltpu.VMEM((x_rows, lb), F32),
                        pltpu.VMEM((n, lb), F32), pltpu.VMEM((n, lb), F32), pltpu.VMEM((n, lb), F32)],
        compiler_params=_cparams(("parallel", "arbitrary")),
        name="rwkv_scan",
    )(pre5, s0, lnw, lnb, rk)


GATHER_PAGES = 8


def _gather_body(pt_ref, *refs):
    page_refs, (cmp_ref, sel_ref) = refs[:-2], refs[-2:]
    for k, x_ref in enumerate(page_refs):
        _, rows, slabs, hd = x_ref.shape
        for s in range(slabs):
            piece = x_ref[0, :, s, :]
            if s < slabs // 2:
                cmp_ref[0, k * rows:(k + 1) * rows, s * hd:(s + 1) * hd] = piece
            else:
                s2 = s - slabs // 2
                sel_ref[0, k * rows:(k + 1) * rows, s2 * hd:(s2 + 1) * hd] = piece.astype(BF16)


def _gather_pages(page_table, pool):
    bs, n_pages = page_table.shape
    _, page, slabs, hd = pool.shape
    half = slabs * hd // 2
    per_step = math.gcd(GATHER_PAGES, n_pages)

    def page_spec(k):
        return pl.BlockSpec((1, page, slabs, hd), lambda b, p, pt: (pt[b, p * per_step + k], 0, 0, 0))

    return pl.pallas_call(
        _gather_body,
        grid_spec=pltpu.PrefetchScalarGridSpec(
            num_scalar_prefetch=1,
            grid=(bs, n_pages // per_step),
            in_specs=[page_spec(k) for k in range(per_step)],
            out_specs=[pl.BlockSpec((1, per_step * page, half), lambda b, p, pt: (b, p, 0)),
                       pl.BlockSpec((1, per_step * page, half), lambda b, p, pt: (b, p, 0))],
        ),
        out_shape=[jax.ShapeDtypeStruct((bs, n_pages * page, half), F32),
                   jax.ShapeDtypeStruct((bs, n_pages * page, half), BF16)],
        compiler_params=_cparams(("parallel", "arbitrary")),
        name="page_gather",
    )(page_table, *([pool] * per_step))


def _cmp_mlp_body(x_ref, pe_ref, w1_ref, w2_ref, o_ref):
    hd = NSA_HEAD_DIM
    for kg in range(2 * NSA_KV_HEADS):
        ch = kg // NSA_KV_HEADS
        acc = None
        for r in range(CMP_BLOCK):
            xr = x_ref[:, r, kg * hd:(kg + 1) * hd]
            xr = (xr + pe_ref[r:r + 1, kg * hd:(kg + 1) * hd]).astype(BF16)
            part = _dot(xr, w1_ref[ch, r * hd:(r + 1) * hd, :])
            acc = part if acc is None else acc + part
        hid = jax.nn.gelu(acc, approximate=True)
        o_ref[:, kg * hd:(kg + 1) * hd] = _dot(hid.astype(BF16), w2_ref[ch])


def _cmp_mlp(blocks, pe_rows, w1, w2, nb, col_block):
    n_blocks, rows, _ = blocks.shape
    out_w = 2 * NSA_KV_DIM
    return pl.pallas_call(
        _cmp_mlp_body,
        grid=(n_blocks // nb,),
        in_specs=[
            pl.BlockSpec((nb, rows, out_w), lambda i: (i, 0, col_block)),
            pl.BlockSpec(pe_rows.shape, lambda i: (0, 0)),
            pl.BlockSpec(w1.shape, lambda i: (0, 0, 0)),
            pl.BlockSpec(w2.shape, lambda i: (0, 0, 0)),
        ],
        out_specs=pl.BlockSpec((nb, out_w), lambda i: (i, 0)),
        out_shape=jax.ShapeDtypeStruct((n_blocks, out_w), F32),
        compiler_params=_cparams(("parallel",)),
        name="cmp_mlp",
    )(blocks, pe_rows, w1, w2)


N_BIAS_TILES = -(-(FAR_DIST + LANES - 1) // LANES) + 1


def _bias_tiles_body(rb_ref, o_ref):
    h = pl.program_id(0)
    iq = lax.broadcasted_iota(jnp.int32, (Q_BLOCK, LANES), 0)
    ik = lax.broadcasted_iota(jnp.int32, (Q_BLOCK, LANES), 1)
    for m in range(N_BIAS_TILES):
        dist = jnp.maximum(m * LANES + iq - ik, 0)
        val = jnp.full(dist.shape, rb_ref[0, h], F32)
        for b, t in enumerate(BUCKET_THR):
            val = jnp.where(dist >= t, rb_ref[b + 1, h], val)
        o_ref[0, m] = val


def _bias_tiles(rel_bias):
    return pl.pallas_call(
        _bias_tiles_body,
        grid=(NSA_HEADS,),
        in_specs=[pl.BlockSpec(memory_space=pltpu.SMEM)],
        out_specs=pl.BlockSpec((1, N_BIAS_TILES, Q_BLOCK, LANES), lambda h: (h, 0, 0, 0)),
        out_shape=jax.ShapeDtypeStruct((NSA_HEADS, N_BIAS_TILES, Q_BLOCK, LANES), F32),
        compiler_params=_cparams(("arbitrary",)),
        name="bias_tiles",
    )(rel_bias)


def _nsa_prompt_body(rb_ref, q_ref, gate_ref, sel_ref, win_ref, kc_ref, bt_ref, y_ref, selb_ref, winb_ref,
                     *, n_sel):
    hd = NSA_HEAD_DIM
    qb = Q_BLOCK
    i = pl.program_id(1)
    scale = hd ** -0.5

    @pl.when(i == 0)
    def _():
        selb_ref[...] = sel_ref[0].astype(BF16)
        winb_ref[...] = win_ref[0].astype(BF16)

    q_bf = q_ref[0].astype(BF16)
    gates = jax.nn.sigmoid(gate_ref[0])
    iq = lax.broadcasted_iota(jnp.int32, (qb, LANES), 0)
    ik = lax.broadcasted_iota(jnp.int32, (qb, LANES), 1)
    pos = i * qb + iq
    d_cmp = pos - (ik * CMP_BLOCK + CMP_BLOCK - 1)
    bias_cmp = _bias_for_heads(d_cmp, rb_ref, range(NSA_HEADS))
    vis_cmp = _stack_heads(d_cmp >= 0)
    top_n = min(TOP_N, n_sel)
    sel_tiles = SEL_TILES if selb_ref.shape[0] % (SEL_TILES * LANES) == 0 else 1
    sel_keys = sel_tiles * LANES
    win_tiles = min(WINDOW // LANES + 1, winb_ref.shape[0] // LANES)
    win_keys = win_tiles * LANES
    iq_s = lax.broadcasted_iota(jnp.int32, (qb, sel_keys), 0)
    ik_s = lax.broadcasted_iota(jnp.int32, (qb, sel_keys), 1)
    iq_w = lax.broadcasted_iota(jnp.int32, (qb, win_keys), 0)
    ik_w = lax.broadcasted_iota(jnp.int32, (qb, win_keys), 1)
    groups = [[g * NSA_HPG + h for h in range(NSA_HPG)] for g in range(NSA_KV_HEADS)]

    def key_cols(g):
        return slice(g * hd, (g + 1) * hd), slice(NSA_KV_DIM + g * hd, NSA_KV_DIM + (g + 1) * hd)

    def tile_bias(heads, first_tile, n_tiles):
        cols = []
        for k in range(n_tiles):
            m = jnp.clip(i - (first_tile + k), 0, N_BIAS_TILES - 1)
            cols.append(jnp.concatenate([bt_ref[hh, m] for hh in heads], axis=0))
        return cols[0] if n_tiles == 1 else jnp.concatenate(cols, axis=1)

    qgs, o_cmps, values = [], [], []
    for g, heads in enumerate(groups):
        k_cols, v_cols = key_cols(g)
        qg = jnp.concatenate([q_bf[:, hh * hd:(hh + 1) * hd] for hh in heads], axis=0)
        o_cmp, p_cmp = _cmp_branch(qg, kc_ref[0, :, k_cols].astype(BF16), kc_ref[0, :, v_cols].astype(BF16),
                                   jnp.concatenate([bias_cmp[hh] for hh in heads], axis=0), vis_cmp, scale)
        qgs.append(qg)
        o_cmps.append(o_cmp)
        values.append(_selection_values(p_cmp, qb, pos, n_sel))
    chosen = _topk_mask(jnp.concatenate(values, axis=0), top_n).astype(BF16)
    chosens = [chosen[g * qb:(g + 1) * qb] for g in range(NSA_KV_HEADS)]

    def sel_step(j, carries):
        row0 = pl.multiple_of(j * sel_keys, sel_keys)
        causal = i * qb + iq_s - (j * sel_keys + ik_s) >= 0
        out = []
        for g, heads in enumerate(groups):
            k_cols, v_cols = key_cols(g)
            mask = _stack_heads(_expand_selection(chosens[g], j, sel_keys) & causal)
            out.append(_softmax_update(carries[g], qgs[g], selb_ref[pl.ds(row0, sel_keys), k_cols],
                                       selb_ref[pl.ds(row0, sel_keys), v_cols],
                                       tile_bias(heads, j * sel_tiles, sel_tiles), mask, scale))
        return tuple(out)

    n_steps = (i + sel_tiles) // sel_tiles
    sel_carries = lax.fori_loop(0, n_steps, sel_step, tuple(_softmax_init(NSA_HPG * qb) for _ in groups))

    first = jnp.maximum(i - (win_tiles - 1), 0)
    row0 = pl.multiple_of(first * LANES, LANES)
    d_win = i * qb + iq_w - (first * LANES + ik_w)
    mask_win = _stack_heads((d_win >= 0) & (d_win < WINDOW))
    for g, heads in enumerate(groups):
        k_cols, v_cols = key_cols(g)
        o_sel = _softmax_finish(sel_carries[g])
        o_win = _softmax_finish(_softmax_update(
            _softmax_init(NSA_HPG * qb), qgs[g], winb_ref[pl.ds(row0, win_keys), k_cols],
            winb_ref[pl.ds(row0, win_keys), v_cols], tile_bias(heads, first, win_tiles), mask_win, scale))
        for h, hh in enumerate(heads):
            rows = slice(h * qb, (h + 1) * qb)
            out = (gates[:, hh:hh + 1] * o_cmps[g][rows]
                   + gates[:, NSA_HEADS + hh:NSA_HEADS + hh + 1] * o_sel[rows]
                   + gates[:, 2 * NSA_HEADS + hh:2 * NSA_HEADS + hh + 1] * o_win[rows])
            y_ref[0, :, hh * hd:(hh + 1) * hd] = out.astype(BF16)


def _nsa_prompt(rel_bias, pb3, kc, bias_tiles):
    b, t, _ = pb3.shape
    qb = Q_BLOCK
    half = 2 * NSA_KV_DIM
    return pl.pallas_call(
        functools.partial(_nsa_prompt_body, n_sel=t // SEL_BLOCK),
        grid=(b, t // qb),
        in_specs=[
            pl.BlockSpec(memory_space=pltpu.SMEM),
            pl.BlockSpec((1, qb, NSA_DIM), lambda s, i: (s, i, PB_Q // NSA_DIM)),
            pl.BlockSpec((1, qb, LANES), lambda s, i: (s, i, PB_GATE // LANES)),
            pl.BlockSpec((1, t, half), lambda s, i: (s, 0, PB_SEL // half)),
            pl.BlockSpec((1, t, half), lambda s, i: (s, 0, PB_WIN // half)),
            pl.BlockSpec((1,) + kc.shape[1:], lambda s, i: (s, 0, 0)),
            pl.BlockSpec(bias_tiles.shape, lambda s, i: (0, 0, 0, 0)),
        ],
        out_specs=pl.BlockSpec((1, qb, NSA_DIM), lambda s, i: (s, i, 0)),
        out_shape=jax.ShapeDtypeStruct((b, t, NSA_DIM), BF16),
        scratch_shapes=[pltpu.VMEM((t, half), BF16), pltpu.VMEM((t, half), BF16)],
        compiler_params=_cparams(("parallel", "arbitrary")),
        name="nsa_prompt",
    )(rel_bias, pb3, pb3, pb3, pb3, kc, bias_tiles)


SEL_TILES = 2
SAMPLE_ROWS = 8
SAMPLE_TILE_KEYS = 1024


def _nsa_sample_body(rb_ref, q_ref, gate_ref, kc_ref, past_ref, new_sel_ref, cwin_ref, new_win_ref, y_ref,
                     *, past_len):
    hd = NSA_HEAD_DIM
    qb = SAMPLE_ROWS
    scale = hd ** -0.5
    n_cmp = kc_ref.shape[1]
    win_rows = cwin_ref.shape[1]
    k_off = past_len - win_rows
    tile_keys = min(SAMPLE_TILE_KEYS, past_len)
    assert past_len % tile_keys == 0

    q_bf = q_ref[0].astype(BF16)
    gates = jax.nn.sigmoid(gate_ref[0])
    iq = lax.broadcasted_iota(jnp.int32, (qb, LANES), 0)
    ik = lax.broadcasted_iota(jnp.int32, (qb, LANES), 1)
    pos_t = past_len + lax.broadcasted_iota(jnp.int32, (qb, tile_keys), 0)
    ik_t = lax.broadcasted_iota(jnp.int32, (qb, tile_keys), 1)
    d_win = (past_len + lax.broadcasted_iota(jnp.int32, (qb, win_rows), 0)
             - (k_off + lax.broadcasted_iota(jnp.int32, (qb, win_rows), 1)))
    tq_c = lax.broadcasted_iota(jnp.int32, (qb, n_cmp), 0)
    n_c = lax.broadcasted_iota(jnp.int32, (qb, n_cmp), 1)
    pos_c = past_len + tq_c
    d_cmp = pos_c - (n_c * CMP_BLOCK + CMP_BLOCK - 1)
    bias_cmp = _bias_for_heads(d_cmp, rb_ref, range(NSA_HEADS))
    vis_cmp = _stack_heads(d_cmp >= 0)
    n_sel_past = past_len // SEL_BLOCK
    picks = min(TOP_N, n_sel_past + 1) - 1

    groups = [[g * NSA_HPG + h for h in range(NSA_HPG)] for g in range(NSA_KV_HEADS)]

    def key_cols(g):
        return slice(g * hd, (g + 1) * hd), slice(NSA_KV_DIM + g * hd, NSA_KV_DIM + (g + 1) * hd)

    qgs, o_cmps, values = [], [], []
    for g, heads in enumerate(groups):
        k_cols, v_cols = key_cols(g)
        qg = jnp.concatenate([q_bf[:, hh * hd:(hh + 1) * hd] for hh in heads], axis=0)
        o_cmp, p_cmp = _cmp_branch(qg, kc_ref[0, :, k_cols].astype(BF16), kc_ref[0, :, v_cols].astype(BF16),
                                   jnp.concatenate([bias_cmp[hh] for hh in heads], axis=0), vis_cmp, scale)
        qgs.append(qg)
        o_cmps.append(o_cmp)
        values.append(_selection_values(p_cmp, qb, pos_c, n_sel_past))
    chosen = _topk_mask(jnp.concatenate(values, axis=0), picks).astype(BF16)
    chosens = [chosen[g * qb:(g + 1) * qb] for g in range(NSA_KV_HEADS)]

    def sel_step(j, carries):
        row0 = pl.multiple_of(j * tile_keys, tile_keys)
        dist = pos_t - (j * tile_keys + ik_t)
        out = []
        for g, heads in enumerate(groups):
            k_cols, v_cols = key_cols(g)
            mask = _stack_heads(_expand_selection(chosens[g], j, tile_keys))
            bias = jnp.concatenate(_bias_for_heads(dist, rb_ref, heads), axis=0)
            out.append(_softmax_update(carries[g], qgs[g], past_ref[0, pl.ds(row0, tile_keys), k_cols],
                                       past_ref[0, pl.ds(row0, tile_keys), v_cols], bias, mask, scale))
        return tuple(out)

    sel_carries = lax.fori_loop(0, past_len // tile_keys, sel_step,
                                tuple(_softmax_init(NSA_HPG * qb) for _ in groups))
    d_new = iq - ik
    mask_new = _stack_heads(d_new >= 0)
    mask_win = _stack_heads((d_win >= 0) & (d_win < WINDOW))
    for g, heads in enumerate(groups):
        k_cols, v_cols = key_cols(g)
        qg, o_cmp = qgs[g], o_cmps[g]
        bias_new = jnp.concatenate(_bias_for_heads(d_new, rb_ref, heads), axis=0)
        o_sel = _softmax_finish(_softmax_update(sel_carries[g], qg, new_sel_ref[0, :, k_cols],
                                                new_sel_ref[0, :, v_cols], bias_new, mask_new, scale))
        bias_win = jnp.concatenate(_bias_for_heads(d_win, rb_ref, heads), axis=0)
        carry = _softmax_update(_softmax_init(NSA_HPG * qb), qg, cwin_ref[0, :, k_cols].astype(BF16),
                                cwin_ref[0, :, v_cols].astype(BF16), bias_win, mask_win, scale)
        o_win = _softmax_finish(_softmax_update(carry, qg, new_win_ref[0, :, k_cols], new_win_ref[0, :, v_cols],
                                                bias_new, mask_new, scale))

        for h, hh in enumerate(heads):
            rows = slice(h * qb, (h + 1) * qb)
            out = (gates[:, hh:hh + 1] * o_cmp[rows]
                   + gates[:, NSA_HEADS + hh:NSA_HEADS + hh + 1] * o_sel[rows]
                   + gates[:, 2 * NSA_HEADS + hh:2 * NSA_HEADS + hh + 1] * o_win[rows])
            y_ref[0, :, hh * hd:(hh + 1) * hd] = out.astype(BF16)


def _nsa_sample(rel_bias, q8, gate8, kc, past_sel, new_sel, cwin, new_win, past_len):
    bs = q8.shape[0]
    blk = lambda a: pl.BlockSpec((1,) + a.shape[1:], lambda s: (s, 0, 0))
    return pl.pallas_call(
        functools.partial(_nsa_sample_body, past_len=past_len),
        grid=(bs,),
        in_specs=[pl.BlockSpec(memory_space=pltpu.SMEM), blk(q8), blk(gate8), blk(kc), blk(past_sel),
                  blk(new_sel), blk(cwin), blk(new_win)],
        out_specs=pl.BlockSpec((1, SAMPLE_ROWS, NSA_DIM), lambda s: (s, 0, 0)),
        out_shape=jax.ShapeDtypeStruct((bs, SAMPLE_ROWS, NSA_DIM), BF16),
        compiler_params=_cparams(("parallel",)),
        name="nsa_sample",
    )(rel_bias, q8, gate8, kc, past_sel, new_sel, cwin, new_win)


def _merge_body(x_ref, yrw_ref, g_ref, ynsa_ref, gates_ref, wrw_ref, wnsa_ref, wout_ref, post_ref, o_ref):
    tm, d = x_ref.shape
    y_rw = jnp.concatenate([yrw_ref[:, pair, :] for pair in range(RW_DIM // LANES)], axis=1)
    y_rw = (y_rw * g_ref[...]).astype(BF16)
    merged = (jax.nn.sigmoid(gates_ref[:, :d]) * _dot(y_rw, wrw_ref[...])
              + jax.nn.sigmoid(gates_ref[:, d:]) * _dot(ynsa_ref[...], wnsa_ref[...]))
    o_ref[...] = x_ref[...] + _rms(_dot(merged.astype(BF16), wout_ref[...]), post_ref[...])


def _merge(x, y_rw, g_rw, y_nsa, gates, w_rw, w_nsa, w_out, post_g, tm):
    m, d = x.shape
    rows = lambda n: pl.BlockSpec((tm, n), lambda i: (i, 0))
    held = lambda a: pl.BlockSpec(a.shape, lambda i: (0, 0), pipeline_mode=pl.Buffered(1))
    return pl.pallas_call(
        _merge_body,
        grid=(m // tm,),
        in_specs=[rows(d), pl.BlockSpec((tm,) + y_rw.shape[1:], lambda i: (i, 0, 0)),
                  rows(RW_DIM), rows(NSA_DIM), rows(2 * d),
                  held(w_rw), held(w_nsa), held(w_out), pl.BlockSpec((1, d), lambda i: (0, 0))],
        out_specs=rows(d),
        out_shape=jax.ShapeDtypeStruct((m, d), F32),
        compiler_params=_cparams(("parallel",)),
        name="merge",
    )(x, y_rw, g_rw, y_nsa, gates, w_rw, w_nsa, w_out, post_g.reshape(1, d))


def _row_tile(m, want):
    return want if m % want == 0 else m


def _rwkv_mix(p3, shift0, s0, w):
    b, t, c = p3.shape
    n = RW_HEAD_DIM
    tt = 256 if t % 256 == 0 else t
    prev = jnp.concatenate([shift0[:, None, :], p3[:, tt - 1:t - 1:tt]], axis=1).reshape(b, t // tt, 1, c)
    pre, g_rw = _rwkv_pre(p3, prev, w["mu"], w["w0"], w["ww"], w["a0"], w["wa"], w["g2"], w["k_k"], w["k_a"], tt)
    seqs = min(b, LANES // RW_HEADS)
    groups = b // seqs
    pairs = RW_HEADS // 2
    s0_t = s0.reshape(groups, seqs, pairs, 2, n, n).transpose(5, 4, 0, 3, 1, 2).reshape(n, n, b * RW_HEADS)
    per_lane = lambda v: jnp.broadcast_to(v.reshape(pairs, 2, n).transpose(2, 1, 0)[:, :, None, :],
                                          (n, 2, seqs, pairs)).reshape(n, seqs * RW_HEADS)
    tc = 32 if t % 32 == 0 else t
    y_t, s_t = _rwkv_scan(pre, s0_t, per_lane(w["lnx_w"]), per_lane(w["lnx_b"]), per_lane(w["r_k"]), tc, seqs)
    s_fin = s_t.reshape(n, n, groups, 2, seqs, pairs).transpose(2, 4, 5, 3, 1, 0).reshape(b, RW_HEADS, n, n)
    y_tok = y_t.reshape(t, b, pairs, LANES).transpose(1, 0, 2, 3).reshape(b * t, pairs, LANES)
    return y_tok, g_rw.reshape(b * t, RW_DIM), s_fin


def _layer(x, w, rel_bias, shift0, s0, past):
    b, t, d = x.shape
    m = b * t
    tm = _row_tile(m, 512)
    x1 = _ffn(x.reshape(m, d), w["f1_pre"], w["f1_post"], w["f1_w1"], w["f1_w3"], w["f1_w2"], tm, 512)
    p_a = _norm_matmul(x1, w["mix_pre"], w["w_in_a"], tm, "proj_rwkv")
    p_b = _norm_matmul(x1, w["mix_pre"], w["w_in_b"], tm, "proj_nsa")
    p_c = _norm_matmul(x1, w["mix_pre"], w["w_in_c"], tm, "proj_gates")

    p3 = p_a.reshape(b, t, RW_PROJ)
    y_rw, g_rw, s_fin = _rwkv_mix(p3, shift0, s0, w)

    pb3 = p_b.reshape(b, t, PB_COLS)
    kv_new = pb3[:, :, PB_KV:PB_WIN]
    win_new = pb3[:, :, PB_WIN:PB_GATE]
    row_w = 4 * NSA_KV_DIM
    half = 2 * NSA_KV_DIM
    if past is None:
        blocks = p_b.reshape(m // CMP_BLOCK, CMP_BLOCK, PB_COLS)
        kc = _cmp_mlp(blocks, w["cmp_pe"], w["cmp_w1"], w["cmp_w2"], min(64, blocks.shape[0]), PB_KV // half)
        n_cmp = t // CMP_BLOCK
        kc = jnp.pad(kc.reshape(b, n_cmp, 2 * NSA_KV_DIM), ((0, 0), (0, -n_cmp % LANES), (0, 0)))
        y_nsa = _nsa_prompt(rel_bias, pb3, kc, _bias_tiles(rel_bias)).reshape(m, NSA_DIM)
        win_out = win_new[:, t - min(WINDOW, t):]
    else:
        page_table, pool, cache_win = past
        past_len = page_table.shape[1] * pool.shape[1]
        slabs = row_w // NSA_HEAD_DIM
        past_cmp, past_sel = _gather_pages(
            page_table, pool.reshape(pool.shape[0], pool.shape[1], slabs, NSA_HEAD_DIM))
        blocks = past_cmp.reshape(b * past_len // CMP_BLOCK, CMP_BLOCK, half)
        kc = _cmp_mlp(blocks, w["cmp_pe"], w["cmp_w1"], w["cmp_w2"], min(128, blocks.shape[0]), 0)
        n_cmp = past_len // CMP_BLOCK
        kc = jnp.pad(kc.reshape(b, n_cmp, half), ((0, 0), (0, -n_cmp % LANES), (0, 0)))
        pad_q = lambda a, rows: jnp.pad(a, ((0, 0), (0, rows - t), (0, 0)))
        cwin = cache_win.reshape(b, cache_win.shape[1], half)
        y8 = _nsa_sample(rel_bias, pad_q(pb3[:, :, :NSA_DIM], SAMPLE_ROWS), pad_q(pb3[:, :, PB_GATE:], SAMPLE_ROWS),
                         kc, past_sel, pad_q(pb3[:, :, PB_SEL:PB_WIN].astype(BF16), LANES), cwin,
                         pad_q(win_new.astype(BF16), LANES), past_len)
        y_nsa = y8[:, :t].reshape(m, NSA_DIM)
        win_all = jnp.concatenate([cwin, win_new], axis=1)
        win_out = win_all[:, win_all.shape[1] - min(WINDOW, win_all.shape[1]):]

    x2 = _merge(x1, y_rw, g_rw, y_nsa, p_c, w["w_br_rw"], w["w_br_nsa"], w["w_out"], w["mix_post"],
                _row_tile(m, 256))
    y = _ffn(x2, w["f2_pre"], w["f2_post"], w["f2_w1"], w["f2_w3"], w["f2_w2"], tm, 512)
    g, hd = NSA_KV_HEADS, NSA_HEAD_DIM
    return (y.reshape(b, t, d), kv_new.reshape(b, t, 4, g, hd), win_out.reshape(b, -1, 2, g, hd), s_fin, p3[:, -1])


def _prepare_weights(l, ffn1_pre_g, ffn1_post_g, ffn1_w1, ffn1_w3, ffn1_w2, mix_pre_g, mix_post_g, w_in,
                     rw_mu, rw_w0, rw_w2, rw_a0, rw_a2, rw_g2, rw_k_k, rw_k_a, rw_r_k, rw_lnx_w, rw_lnx_b,
                     cmp_pe, cmp_w1, cmp_w2, w_br_rw, w_br_nsa, w_out,
                     ffn2_pre_g, ffn2_post_g, ffn2_w1, ffn2_w3, ffn2_w2):
    d = w_in.shape[1]
    wi = w_in[l]
    c_q = RW_PROJ
    c_kv = c_q + NSA_DIM
    c_gate = c_kv + 6 * NSA_KV_DIM
    c_grw = c_gate + 3 * NSA_HEADS
    w_in_b = jnp.concatenate([wi[:, c_q:c_gate], jnp.pad(wi[:, c_gate:c_grw], ((0, 0), (0, LANES - 3 * NSA_HEADS)))],
                             axis=1)
    zeros_w = jnp.zeros((RW_LORA_A, RW_DIM), F32)
    zeros_a = jnp.zeros((RW_LORA_W, RW_DIM), F32)
    pe = cmp_pe[l]
    pe_rows = jnp.broadcast_to(pe[:, :, None, :], (CMP_BLOCK, 2, NSA_KV_HEADS, NSA_HEAD_DIM))
    return {
        "f1_pre": ffn1_pre_g[l], "f1_post": ffn1_post_g[l],
        "f1_w1": ffn1_w1[l].astype(BF16), "f1_w3": ffn1_w3[l].astype(BF16), "f1_w2": ffn1_w2[l].astype(BF16),
        "mix_pre": mix_pre_g[l], "mix_post": mix_post_g[l],
        "w_in_a": wi[:, :RW_PROJ].astype(BF16), "w_in_b": w_in_b.astype(BF16), "w_in_c": wi[:, c_grw:].astype(BF16),
        "mu": rw_mu[l], "w0": rw_w0[l], "a0": rw_a0[l],
        "ww": jnp.concatenate([rw_w2[l], zeros_w], axis=0).astype(BF16),
        "wa": jnp.concatenate([zeros_a, rw_a2[l]], axis=0).astype(BF16),
        "g2": rw_g2[l].astype(BF16), "k_k": rw_k_k[l], "k_a": rw_k_a[l],
        "r_k": rw_r_k[l].reshape(-1), "lnx_w": rw_lnx_w[l], "lnx_b": rw_lnx_b[l],
        "cmp_pe": pe_rows.reshape(CMP_BLOCK, 2 * NSA_KV_DIM),
        "cmp_w1": cmp_w1[l].astype(BF16), "cmp_w2": cmp_w2[l].astype(BF16),
        "w_br_rw": w_br_rw[l].astype(BF16), "w_br_nsa": w_br_nsa[l].astype(BF16), "w_out": w_out[l].astype(BF16),
        "f2_pre": ffn2_pre_g[l], "f2_post": ffn2_post_g[l],
        "f2_w1": ffn2_w1[l].astype(BF16), "f2_w3": ffn2_w3[l].astype(BF16), "f2_w2": ffn2_w2[l].astype(BF16),
    }


def kernel(x_prompt, x_sample, cache_kv, cache_win, state_rwkv, state_shift, page_table,
           ffn1_pre_g, ffn1_post_g, ffn1_w1, ffn1_w3, ffn1_w2, mix_pre_g, mix_post_g, w_in,
           rw_mu, rw_w0, rw_w2, rw_a0, rw_a2, rw_g2, rw_k_k, rw_k_a, rw_r_k, rw_lnx_w, rw_lnx_b,
           cmp_pe, cmp_w1, cmp_w2, w_br_rw, w_br_nsa, w_out,
           ffn2_pre_g, ffn2_post_g, ffn2_w1, ffn2_w3, ffn2_w2, rel_bias):
    depth = w_in.shape[0]
    b_p = x_prompt.shape[0]
    y_p, y_s = x_prompt, x_sample
    outs = [[] for _ in range(8)]
    for l in range(depth):
        w = _prepare_weights(l, ffn1_pre_g, ffn1_post_g, ffn1_w1, ffn1_w3, ffn1_w2, mix_pre_g, mix_post_g, w_in,
                             rw_mu, rw_w0, rw_w2, rw_a0, rw_a2, rw_g2, rw_k_k, rw_k_a, rw_r_k, rw_lnx_w, rw_lnx_b,
                             cmp_pe, cmp_w1, cmp_w2, w_br_rw, w_br_nsa, w_out,
                             ffn2_pre_g, ffn2_post_g, ffn2_w1, ffn2_w3, ffn2_w2)
        y_p, kv_p, win_p, rw_p, sh_p = _layer(
            y_p, w, rel_bias, jnp.zeros((b_p, RW_PROJ), F32),
            jnp.zeros((b_p, RW_HEADS, RW_HEAD_DIM, RW_HEAD_DIM), F32), None)
        y_s, kv_s, win_s, rw_s, sh_s = _layer(
            y_s, w, rel_bias, state_shift[l], state_rwkv[l], (page_table, cache_kv[l], cache_win[l]))
        for acc, v in zip(outs, (kv_p, kv_s, win_p, win_s, rw_p, rw_s, sh_p, sh_s)):
            acc.append(v)
    return (y_p, y_s) + tuple(jnp.stack(o) for o in outs)
```

```python
import functools
import math

import jax
import jax.numpy as jnp
from jax import lax
from jax.experimental import pallas as pl
from jax.experimental.pallas import tpu as pltpu
from jax.experimental.pallas import tpu_sc as plsc

F32 = jnp.float32
BF16 = jnp.bfloat16

RMS_EPS = 1e-6
RW_HEADS = 16
RW_HEAD_DIM = 64
RW_DIM = RW_HEADS * RW_HEAD_DIM
RW_LORA_W = 64
RW_LORA_A = 64
RW_LORA_G = 128
RW_PROJ = 3 * RW_DIM + RW_LORA_W + RW_LORA_A + RW_LORA_G
RW_LN_EPS = 64e-5
NSA_HEADS = 8
NSA_KV_HEADS = 2
NSA_HPG = NSA_HEADS // NSA_KV_HEADS
NSA_HEAD_DIM = 128
NSA_DIM = NSA_HEADS * NSA_HEAD_DIM
NSA_KV_DIM = NSA_KV_HEADS * NSA_HEAD_DIM
CMP_BLOCK = 32
CMP_HID = 256
SEL_BLOCK = 64
SEL_SHIFT = SEL_BLOCK.bit_length() - 1
TOP_N = 16
WINDOW = 512
N_BUCKETS = 32
REL_MAX_EXACT = 16
REL_MAX_DIST = 1024
Q_BLOCK = 128
NEG_BIG = -1e30
FORCE_BONUS = 1e4
NEVER = -3e38
LANES = 128
SUBLANES = 8
SCAN_FIELDS = 6
VMEM_LIMIT = 56 * 1024 * 1024
SCAN_UNROLL = 8
UPDATE_UNROLL = 16

PB_Q = 0
PB_KV = NSA_DIM
PB_SEL = PB_KV + 2 * NSA_KV_DIM
PB_WIN = PB_KV + 4 * NSA_KV_DIM
PB_GATE = PB_WIN + 2 * NSA_KV_DIM
PB_COLS = PB_GATE + LANES


def _bucket_thresholds():
    thr = list(range(1, REL_MAX_EXACT + 1))
    n_log = N_BUCKETS - REL_MAX_EXACT
    ratio = REL_MAX_DIST // REL_MAX_EXACT
    n = REL_MAX_EXACT
    for k in range(1, n_log):
        while n ** n_log < REL_MAX_EXACT ** n_log * ratio ** k:
            n += 1
        thr.append(n)
    return thr


BUCKET_THR = _bucket_thresholds()
FAR_DIST = BUCKET_THR[-1]


def _cparams(sem):
    return pltpu.CompilerParams(dimension_semantics=sem, vmem_limit_bytes=VMEM_LIMIT)


def _rms(x, g):
    ms = jnp.mean(x * x, axis=-1, keepdims=True)
    return x * lax.rsqrt(ms + RMS_EPS) * g


def _dot(a, b):
    return jnp.dot(a, b, preferred_element_type=F32)


def _dot_nt(a, b):
    return lax.dot_general(a, b, (((1,), (1,)), ((), ())), preferred_element_type=F32)


def _bias_for_heads(dist, rb_ref, heads):
    n = jnp.maximum(dist, 0)
    reach = [n >= t for t in BUCKET_THR]
    out = []
    for h in heads:
        val = jnp.full(dist.shape, rb_ref[0, h], F32)
        for b, m in enumerate(reach):
            val = jnp.where(m, rb_ref[b + 1, h], val)
        out.append(val)
    return out


def _softmax_update(carry, q, kt, vt, bias, mask, scale):
    m_run, l_run, acc = carry
    s = _dot_nt(q, kt) * scale + bias
    s = jnp.where(mask, s, NEG_BIG)
    m_new = jnp.maximum(m_run, jnp.max(s, axis=-1, keepdims=True))
    alpha = jnp.exp(m_run - m_new)
    e = jnp.where(mask, jnp.exp(s - m_new), 0.0)
    l_new = alpha * l_run + jnp.sum(e, axis=-1, keepdims=True)
    acc = alpha * acc + _dot(e.astype(BF16), vt)
    return m_new, l_new, acc


def _softmax_init(rows):
    return (jnp.full((rows, 1), NEG_BIG, F32), jnp.zeros((rows, 1), F32),
            jnp.zeros((rows, NSA_HEAD_DIM), F32))


def _softmax_finish(carry):
    _, l_run, acc = carry
    return acc / jnp.maximum(l_run, 1e-30)


def _stack_heads(x):
    return jnp.concatenate([x] * NSA_HPG, axis=0)


def _topk_mask(val, k):
    lane = lax.broadcasted_iota(jnp.int32, val.shape, 1)
    width = val.shape[1]
    chosen = jnp.zeros(val.shape, F32)
    for _ in range(k):
        top = jnp.max(val, axis=-1, keepdims=True)
        first = jnp.min(jnp.where(val == top, lane, width), axis=-1, keepdims=True)
        hit = lane == first
        chosen = jnp.where(hit, 1.0, chosen)
        val = jnp.where(hit, NEVER, val)
    return chosen


def _pair_sum(x):
    parts = []
    for c in range(x.shape[1] // LANES):
        blk = x[:, c * LANES:(c + 1) * LANES]
        parts.append(blk + pltpu.roll(blk, LANES - 1, 1))
    return parts[0] if len(parts) == 1 else jnp.concatenate(parts, axis=1)


def _cmp_branch(qg, kc_k, kc_v, bias, visible, scale):
    s = _dot_nt(qg, kc_k) * scale + bias
    s = jnp.where(visible, s, NEG_BIG)
    e = jnp.where(visible, jnp.exp(s - jnp.max(s, axis=-1, keepdims=True)), 0.0)
    p = e / jnp.maximum(jnp.sum(e, axis=-1, keepdims=True), 1e-30)
    return _dot(p.astype(BF16), kc_v), p


def _selection_values(p, qb, pos, n_sel):
    imp = p[0:qb]
    for h in range(1, NSA_HPG):
        imp = imp + p[h * qb:(h + 1) * qb]
    imp = _pair_sum(imp)
    lane = lax.broadcasted_iota(jnp.int32, imp.shape, 1)
    sel_id = lane >> 1
    cur = pos >> SEL_SHIFT
    forced = (sel_id == 0) | (sel_id == cur) | (sel_id == cur - 1)
    val = jnp.where(forced, imp + FORCE_BONUS, imp)
    val = jnp.where(sel_id * SEL_BLOCK <= pos, val, NEG_BIG)
    return jnp.where(((lane & 1) == 0) & (sel_id < n_sel), val, NEVER)


def _expand_selection(chosen_bf16, tile, tile_keys=LANES):
    width = chosen_bf16.shape[1]
    n_idx = lax.broadcasted_iota(jnp.int32, (width, tile_keys), 0)
    k_idx = lax.broadcasted_iota(jnp.int32, (width, tile_keys), 1)
    target = 2 * ((tile * tile_keys + k_idx) >> SEL_SHIFT)
    expand = jnp.where(n_idx == target, 1.0, 0.0).astype(BF16)
    return _dot(chosen_bf16, expand) > 0.5


def _ffn_body(x_ref, pre_ref, post_ref, w1_ref, w3_ref, w2_ref, o_ref, h_ref, acc_ref):
    j = pl.program_id(1)

    @pl.when(j == 0)
    def _():
        h_ref[...] = _rms(x_ref[...], pre_ref[...]).astype(BF16)
        acc_ref[...] = jnp.zeros_like(acc_ref)

    h = h_ref[...]
    a = _dot(h, w1_ref[...])
    b = _dot(h, w3_ref[...])
    u = (a * jax.nn.sigmoid(a)) * b
    acc_ref[...] += _dot(u.astype(BF16), w2_ref[...])

    @pl.when(j == pl.num_programs(1) - 1)
    def _():
        o_ref[...] = x_ref[...] + 0.5 * _rms(acc_ref[...], post_ref[...])


def _ffn(x, pre_g, post_g, w1, w3, w2, tm, tf):
    m, d = x.shape
    f = w1.shape[1]
    return pl.pallas_call(
        _ffn_body,
        grid=(m // tm, f // tf),
        in_specs=[
            pl.BlockSpec((tm, d), lambda i, j: (i, 0)),
            pl.BlockSpec((1, d), lambda i, j: (0, 0)),
            pl.BlockSpec((1, d), lambda i, j: (0, 0)),
            pl.BlockSpec((d, tf), lambda i, j: (0, j)),
            pl.BlockSpec((d, tf), lambda i, j: (0, j)),
            pl.BlockSpec((tf, d), lambda i, j: (j, 0)),
        ],
        out_specs=pl.BlockSpec((tm, d), lambda i, j: (i, 0)),
        out_shape=jax.ShapeDtypeStruct((m, d), F32),
        scratch_shapes=[pltpu.VMEM((tm, d), BF16), pltpu.VMEM((tm, d), F32)],
        compiler_params=_cparams(("parallel", "arbitrary")),
        name="ffn",
    )(x, pre_g.reshape(1, d), post_g.reshape(1, d), w1, w3, w2)


def _norm_matmul_body(x_ref, g_ref, w_ref, o_ref, h_ref):
    @pl.when(pl.program_id(1) == 0)
    def _():
        h_ref[...] = _rms(x_ref[...], g_ref[...]).astype(BF16)

    o_ref[...] = _dot(h_ref[...], w_ref[...])


def _norm_matmul(x, g, w, tm, name):
    m, d = x.shape
    n = w.shape[1]
    return pl.pallas_call(
        _norm_matmul_body,
        grid=(m // tm, 1),
        in_specs=[
            pl.BlockSpec((tm, d), lambda i, j: (i, 0)),
            pl.BlockSpec((1, d), lambda i, j: (0, 0)),
            pl.BlockSpec((d, n), lambda i, j: (0, 0), pipeline_mode=pl.Buffered(1)),
        ],
        out_specs=pl.BlockSpec((tm, n), lambda i, j: (i, 0)),
        out_shape=jax.ShapeDtypeStruct((m, n), F32),
        scratch_shapes=[pltpu.VMEM((tm, d), BF16)],
        compiler_params=_cparams(("parallel", "arbitrary")),
        name=name,
    )(x, g.reshape(1, d), w)


def _rwkv_pre_body(p_ref, prev_ref, mu_ref, w0_ref, ww_ref, a0_ref, wa_ref, g2_ref, kk_ref, ka_ref,
                   pre_ref, g_ref, buf_ref):
    tt = p_ref.shape[1]
    p = p_ref[0]
    buf_ref[8:8 + tt, :] = p
    buf_ref[7:8, :] = prev_ref[0, 0]
    p_prev = buf_ref[7:7 + tt, :]
    xs = p + (p_prev - p) * mu_ref[...]
    d = RW_DIM
    r = xs[:, 0:d]
    k = xs[:, d:2 * d]
    v = xs[:, 2 * d:3 * d]
    lora_in = xs[:, 3 * d:3 * d + RW_LORA_W + RW_LORA_A]
    gd = xs[:, 3 * d + RW_LORA_W + RW_LORA_A:]
    z = -(w0_ref[...] + _dot(jnp.tanh(lora_in).astype(BF16), ww_ref[...]))
    softplus = jnp.maximum(z, 0.0) + jnp.log(1.0 + jnp.exp(-jnp.abs(z)))
    decay = jnp.exp(-jnp.exp(-softplus - 0.5))
    a = jax.nn.sigmoid(a0_ref[...] + _dot(lora_in.astype(BF16), wa_ref[...]))
    g_ref[0] = _dot(jax.nn.sigmoid(gd).astype(BF16), g2_ref[...])

    n = RW_HEAD_DIM
    pairs = d // LANES
    low_half = lax.broadcasted_iota(jnp.int32, (tt, LANES), 1) < n

    def put(fp, xa, xb):
        for pair in range(pairs):
            col_a = xa[:, pair * LANES:(pair + 1) * LANES]
            col_b = xb[:, pair * LANES:(pair + 1) * LANES]
            pieces = (jnp.where(low_half, col_a, pltpu.roll(col_b, n, 1)),
                      jnp.where(low_half, pltpu.roll(col_a, n, 1), col_b))
            for parity, piece in enumerate(pieces):
                pre_ref[:, fp, parity, pair, :] = piece

    put(0, r, k * (1.0 + (a - 1.0) * ka_ref[...]))
    put(1, v, decay)
    put(2, k * kk_ref[...], a)


def _rwkv_pre(p3, prev, mu, w0, ww, a0, wa, g2, k_k, k_a, tt):
    b, t, c = p3.shape
    d = RW_DIM
    pairs = RW_HEADS // 2
    row = lambda n: pl.BlockSpec((1, n), lambda i, j: (0, 0))
    full = lambda s: pl.BlockSpec(s, lambda i, j: (0, 0))
    return pl.pallas_call(
        _rwkv_pre_body,
        grid=(b, t // tt),
        in_specs=[
            pl.BlockSpec((1, tt, c), lambda i, j: (i, j, 0)),
            pl.BlockSpec((1, 1, 1, c), lambda i, j: (i, j, 0, 0)),
            row(c), row(d), full(ww.shape), row(d), full(wa.shape), full(g2.shape), row(d), row(d),
        ],
        out_specs=[
            pl.BlockSpec((tt, SCAN_FIELDS // 2, 2, pairs, LANES), lambda i, j: (j, 0, 0, i, 0)),
            pl.BlockSpec((1, tt, d), lambda i, j: (i, j, 0)),
        ],
        out_shape=[jax.ShapeDtypeStruct((t, SCAN_FIELDS // 2, 2, b * pairs, LANES), F32),
                   jax.ShapeDtypeStruct((b, t, d), F32)],
        scratch_shapes=[pltpu.VMEM((tt + 8, c), F32)],
        compiler_params=_cparams(("parallel", "arbitrary")),
        name="rwkv_pre",
    )(p3, prev, mu.reshape(1, c), w0.reshape(1, d), ww, a0.reshape(1, d), wa, g2,
      k_k.reshape(1, d), k_a.reshape(1, d))


def _rwkv_scan_body(in_ref, s0_ref, lnw_ref, lnb_ref, rk_ref, y_ref, s_ref, xa_ref, xb_ref, kk_ref, b_ref,
                    out_ref):
    n = RW_HEAD_DIM
    steps = in_ref.shape[0]
    n_fields = in_ref.shape[1]
    half = in_ref.shape[3]
    lanes = 2 * half
    fields_per_trip = -(-n_fields // (n // UPDATE_UNROLL))

    def load_field(t, dst_ref, fp):
        tile = in_ref[t, fp].reshape(lanes, 2 * n)
        dst_ref[pl.ds(pl.multiple_of(fp * 2 * n, 2 * n), 2 * n), :] = tile.T

    def normalise_key(x_ref):
        kk_raw = x_ref[4 * n:5 * n, :]
        norm = jnp.sqrt(jnp.sum(kk_raw * kk_raw, axis=0, keepdims=True))
        kk = kk_raw / jnp.maximum(norm, 1e-12)
        kk_ref[...] = kk
        b_ref[...] = kk * x_ref[5 * n:6 * n, :]

    def one_step(t, x_ref, next_ref):
        t_next = jnp.minimum(t + 1, steps - 1)

        def sk_rows(jb, acc):
            for u in range(SCAN_UNROLL):
                j = jb * SCAN_UNROLL + u
                acc = acc + s_ref[j] * kk_ref[pl.ds(j, 1), :]
            return acc

        sk = lax.fori_loop(0, n // SCAN_UNROLL, sk_rows, jnp.zeros((n, lanes), F32))
        v = x_ref[2 * n:3 * n, :]

        def update_rows(jb, acc):
            for f in range(fields_per_trip):
                load_field(t_next, next_ref, jnp.minimum(jb * fields_per_trip + f, n_fields - 1))
            emit_output(jnp.maximum(t - 1, 0))
            for u in range(UPDATE_UNROLL):
                j = jb * UPDATE_UNROLL + u
                r_j = x_ref[pl.ds(j, 1), :]
                k_j = x_ref[pl.ds(n + j, 1), :]
                w_j = x_ref[pl.ds(3 * n + j, 1), :]
                s_j = s_ref[j] * w_j - sk * b_ref[pl.ds(j, 1), :] + v * k_j
                s_ref[j] = s_j
                acc = acc + s_j * r_j
            return acc

        y = lax.fori_loop(0, n // UPDATE_UNROLL, update_rows, jnp.zeros((n, lanes), F32))
        normalise_key(next_ref)
        mean = jnp.sum(y, axis=0, keepdims=True) * (1.0 / n)
        yc = y - mean
        var = jnp.sum(yc * yc, axis=0, keepdims=True) * (1.0 / n)
        y_norm = yc * lax.rsqrt(var + RW_LN_EPS) * lnw_ref[...] + lnb_ref[...]
        rkr = x_ref[0:n, :] * x_ref[n:2 * n, :] * rk_ref[...]
        out_ref[...] = y_norm + jnp.sum(rkr, axis=0, keepdims=True) * v

    def emit_output(t):
        out = out_ref[...]
        y_ref[t] = jnp.concatenate([out[:, :half], out[:, half:]], axis=0).T

    @pl.when(pl.program_id(1) == 0)
    def _():
        s_ref[...] = s0_ref[...]

    out_ref[...] = jnp.zeros_like(out_ref)
    for c in range(n_fields):
        load_field(0, xa_ref, c)
    normalise_key(xa_ref)

    def step_pair(tp, carry):
        one_step(2 * tp, xa_ref, xb_ref)
        one_step(2 * tp + 1, xb_ref, xa_ref)
        return carry

    lax.fori_loop(0, steps // 2, step_pair, 0)
    emit_output(steps - 1)


def _rwkv_scan(pre5, s0, lnw, lnb, rk, tc, seqs):
    t, field_pairs, _, rows, _ = pre5.shape
    n = RW_HEAD_DIM
    pairs = RW_HEADS // 2
    x_rows = field_pairs * 2 * n
    lb = seqs * RW_HEADS
    assert tc % 2 == 0 and t % tc == 0
    tab = pl.BlockSpec((n, lb), lambda l, i: (0, 0))
    return pl.pallas_call(
        _rwkv_scan_body,
        grid=(rows // (seqs * pairs), t // tc),
        in_specs=[
            pl.BlockSpec((tc, field_pairs, 2, seqs * pairs, LANES), lambda l, i: (i, 0, 0, l, 0)),
            pl.BlockSpec((n, n, lb), lambda l, i: (0, 0, l)),
            tab, tab, tab,
        ],
        out_specs=[
            pl.BlockSpec((tc, seqs * pairs, LANES), lambda l, i: (i, l, 0)),
            pl.BlockSpec((n, n, lb), lambda l, i: (0, 0, l)),
        ],
        out_shape=[jax.ShapeDtypeStruct((t, rows, LANES), F32),
                   jax.ShapeDtypeStruct((n, n, rows * 2), F32)],
        scratch_shapes=[pltpu.VMEM((x_rows, lb), F32), pltpu.VMEM((x_rows, lb), F32),
                        pltpu.VMEM((n, lb), F32), pltpu.VMEM((n, lb), F32), pltpu.VMEM((n, lb), F32)],
        compiler_params=_cparams(("parallel", "arbitrary")),
        name="rwkv_scan",
    )(pre5, s0, lnw, lnb, rk)


GATHER_PAGES = 8


GATHER_WINDOW = 128


def _gather_rows(rows, idx):
    n = idx.shape[1]
    width = rows.shape[1]
    mesh = plsc.VectorSubcoreMesh(core_axis_name="core", subcore_axis_name="subcore")

    @pl.kernel(out_type=jax.ShapeDtypeStruct((n, width), rows.dtype), mesh=mesh)
    def gather_kernel(x_hbm, i_hbm, o_hbm):
        def body(i_vmem, o_vmem):
            pltpu.sync_copy(x_hbm.at[i_vmem.at[0]], o_vmem)

        pltpu.emit_pipeline(
            body,
            grid=(n // GATHER_WINDOW,),
            in_specs=[pl.BlockSpec((1, GATHER_WINDOW), index_map=lambda i: (0, i))],
            out_specs=[pl.BlockSpec((GATHER_WINDOW, width), index_map=lambda i: (i, 0))],
            core_axis_name=("core", "subcore"),
            dimension_semantics=(pltpu.PARALLEL,),
        )(i_hbm, o_hbm)

    return gather_kernel(rows, idx)


def _gather_selected(page_table, pool):
    bs, n_pages = page_table.shape
    n_pool, page, slabs, hd = pool.shape
    past_len = n_pages * page
    tok = jnp.arange(past_len, dtype=jnp.int32)
    row0 = (page_table[:, tok // page] * page + tok % page) * slabs
    idx = row0[:, None, :] + (slabs // 2 + jnp.arange(slabs // 2, dtype=jnp.int32))[None, :, None]
    out = _gather_rows(pool.reshape(n_pool * page * slabs, hd), idx.reshape(1, -1))
    return out.reshape(bs, slabs // 2, past_len, hd)


def _paged_kv_body(pt_ref, *refs):
    n_pages = len(refs) - 5
    page_refs = refs[:n_pages]
    pe_ref, w1_ref, w2_ref, kc_ref, out_ref = refs[n_pages:]
    _, rows, slabs, hd = page_refs[0].shape
    per_page = rows // CMP_BLOCK
    cols = []
    for r in range(CMP_BLOCK):
        tiles = [x_ref[0, nl * CMP_BLOCK + r] + pe_ref[r] for x_ref in page_refs for nl in range(per_page)]
        cols.append(jnp.concatenate(tiles, axis=0).astype(BF16))
    hid = jax.nn.gelu(_dot(jnp.concatenate(cols, axis=1), w1_ref[...]), approximate=True)
    out_ref[0] = _dot(hid[:, :CMP_HID].astype(BF16), w2_ref[0])
    out_ref[1] = _dot(hid[:, CMP_HID:].astype(BF16), w2_ref[1])
    n_blocks = n_pages * per_page
    for kg in range(slabs // 2):
        kc_ref[0, :, kg * hd:(kg + 1) * hd] = out_ref[kg // NSA_KV_HEADS, pl.ds(kg, n_blocks, stride=slabs), :]


def _paged_kv(page_table, pool, pe_tiles, w1cat, w2):
    bs, n_pages = page_table.shape
    _, page, slabs, hd = pool.shape
    half = slabs * hd // 2
    per_step = math.gcd(GATHER_PAGES, n_pages)
    blocks_step = per_step * page // CMP_BLOCK

    def page_spec(k):
        return pl.BlockSpec((1, page, slabs, hd), lambda b, p, pt: (pt[b, p * per_step + k], 0, 0, 0))

    held = lambda a: pl.BlockSpec(a.shape, lambda b, p, pt: (0,) * a.ndim)
    return pl.pallas_call(
        _paged_kv_body,
        grid_spec=pltpu.PrefetchScalarGridSpec(
            num_scalar_prefetch=1,
            grid=(bs, n_pages // per_step),
            in_specs=[page_spec(k) for k in range(per_step)] + [held(pe_tiles), held(w1cat), held(w2)],
            out_specs=pl.BlockSpec((1, blocks_step, half), lambda b, p, pt: (b, p, 0)),
            scratch_shapes=[pltpu.VMEM((2, blocks_step * slabs, hd), F32)],
        ),
        out_shape=jax.ShapeDtypeStruct((bs, n_pages * page // CMP_BLOCK, half), F32),
        compiler_params=_cparams(("parallel", "arbitrary")),
        name="paged_kv",
    )(page_table, *([pool] * per_step), pe_tiles, w1cat, w2)


def _cmp_mlp_body(x_ref, pe_ref, w1_ref, w2_ref, o_ref):
    hd = NSA_HEAD_DIM
    for kg in range(2 * NSA_KV_HEADS):
        ch = kg // NSA_KV_HEADS
        acc = None
        for r in range(CMP_BLOCK):
            xr = x_ref[:, r, kg * hd:(kg + 1) * hd]
            xr = (xr + pe_ref[r:r + 1, kg * hd:(kg + 1) * hd]).astype(BF16)
            part = _dot(xr, w1_ref[ch, r * hd:(r + 1) * hd, :])
            acc = part if acc is None else acc + part
        hid = jax.nn.gelu(acc, approximate=True)
        o_ref[:, kg * hd:(kg + 1) * hd] = _dot(hid.astype(BF16), w2_ref[ch])


def _cmp_mlp(blocks, pe_rows, w1, w2, nb, col_block):
    n_blocks, rows, _ = blocks.shape
    out_w = 2 * NSA_KV_DIM
    return pl.pallas_call(
        _cmp_mlp_body,
        grid=(n_blocks // nb,),
        in_specs=[
            pl.BlockSpec((nb, rows, out_w), lambda i: (i, 0, col_block)),
            pl.BlockSpec(pe_rows.shape, lambda i: (0, 0)),
            pl.BlockSpec(w1.shape, lambda i: (0, 0, 0)),
            pl.BlockSpec(w2.shape, lambda i: (0, 0, 0)),
        ],
        out_specs=pl.BlockSpec((nb, out_w), lambda i: (i, 0)),
        out_shape=jax.ShapeDtypeStruct((n_blocks, out_w), F32),
        compiler_params=_cparams(("parallel",)),
        name="cmp_mlp",
    )(blocks, pe_rows, w1, w2)


N_BIAS_TILES = -(-(FAR_DIST + LANES - 1) // LANES) + 1


def _bias_tiles_body(rb_ref, o_ref):
    h = pl.program_id(0)
    iq = lax.broadcasted_iota(jnp.int32, (Q_BLOCK, LANES), 0)
    ik = lax.broadcasted_iota(jnp.int32, (Q_BLOCK, LANES), 1)
    for m in range(N_BIAS_TILES):
        dist = jnp.maximum(m * LANES + iq - ik, 0)
        val = jnp.full(dist.shape, rb_ref[0, h], F32)
        for b, t in enumerate(BUCKET_THR):
            val = jnp.where(dist >= t, rb_ref[b + 1, h], val)
        o_ref[0, m] = val


def _bias_tiles(rel_bias):
    return pl.pallas_call(
        _bias_tiles_body,
        grid=(NSA_HEADS,),
        in_specs=[pl.BlockSpec(memory_space=pltpu.SMEM)],
        out_specs=pl.BlockSpec((1, N_BIAS_TILES, Q_BLOCK, LANES), lambda h: (h, 0, 0, 0)),
        out_shape=jax.ShapeDtypeStruct((NSA_HEADS, N_BIAS_TILES, Q_BLOCK, LANES), F32),
        compiler_params=_cparams(("arbitrary",)),
        name="bias_tiles",
    )(rel_bias)


def _nsa_prompt_body(rb_ref, q_ref, gate_ref, sel_ref, win_ref, kc_ref, bt_ref, y_ref, selb_ref, winb_ref,
                     *, n_sel):
    hd = NSA_HEAD_DIM
    qb = Q_BLOCK
    i = pl.program_id(1)
    scale = hd ** -0.5

    @pl.when(i == 0)
    def _():
        selb_ref[...] = sel_ref[0].astype(BF16)
        winb_ref[...] = win_ref[0].astype(BF16)

    q_bf = q_ref[0].astype(BF16)
    gates = jax.nn.sigmoid(gate_ref[0])
    iq = lax.broadcasted_iota(jnp.int32, (qb, LANES), 0)
    ik = lax.broadcasted_iota(jnp.int32, (qb, LANES), 1)
    pos = i * qb + iq
    d_cmp = pos - (ik * CMP_BLOCK + CMP_BLOCK - 1)
    bias_cmp = _bias_for_heads(d_cmp, rb_ref, range(NSA_HEADS))
    vis_cmp = _stack_heads(d_cmp >= 0)
    top_n = min(TOP_N, n_sel)
    sel_tiles = SEL_TILES if selb_ref.shape[0] % (SEL_TILES * LANES) == 0 else 1
    sel_keys = sel_tiles * LANES
    win_tiles = min(WINDOW // LANES + 1, winb_ref.shape[0] // LANES)
    win_keys = win_tiles * LANES
    iq_s = lax.broadcasted_iota(jnp.int32, (qb, sel_keys), 0)
    ik_s = lax.broadcasted_iota(jnp.int32, (qb, sel_keys), 1)
    iq_w = lax.broadcasted_iota(jnp.int32, (qb, win_keys), 0)
    ik_w = lax.broadcasted_iota(jnp.int32, (qb, win_keys), 1)
    groups = [[g * NSA_HPG + h for h in range(NSA_HPG)] for g in range(NSA_KV_HEADS)]

    def key_cols(g):
        return slice(g * hd, (g + 1) * hd), slice(NSA_KV_DIM + g * hd, NSA_KV_DIM + (g + 1) * hd)

    def tile_bias(heads, first_tile, n_tiles):
        cols = []
        for k in range(n_tiles):
            m = jnp.clip(i - (first_tile + k), 0, N_BIAS_TILES - 1)
            cols.append(jnp.concatenate([bt_ref[hh, m] for hh in heads], axis=0))
        return cols[0] if n_tiles == 1 else jnp.concatenate(cols, axis=1)

    qgs, o_cmps, values = [], [], []
    for g, heads in enumerate(groups):
        k_cols, v_cols = key_cols(g)
        qg = jnp.concatenate([q_bf[:, hh * hd:(hh + 1) * hd] for hh in heads], axis=0)
        o_cmp, p_cmp = _cmp_branch(qg, kc_ref[0, :, k_cols].astype(BF16), kc_ref[0, :, v_cols].astype(BF16),
                                   jnp.concatenate([bias_cmp[hh] for hh in heads], axis=0), vis_cmp, scale)
        qgs.append(qg)
        o_cmps.append(o_cmp)
        values.append(_selection_values(p_cmp, qb, pos, n_sel))
    chosen = _topk_mask(jnp.concatenate(values, axis=0), top_n).astype(BF16)
    chosens = [chosen[g * qb:(g + 1) * qb] for g in range(NSA_KV_HEADS)]

    def sel_step(j, carries):
        row0 = pl.multiple_of(j * sel_keys, sel_keys)
        causal = i * qb + iq_s - (j * sel_keys + ik_s) >= 0
        out = []
        for g, heads in enumerate(groups):
            k_cols, v_cols = key_cols(g)
            mask = _stack_heads(_expand_selection(chosens[g], j, sel_keys) & causal)
            out.append(_softmax_update(carries[g], qgs[g], selb_ref[pl.ds(row0, sel_keys), k_cols],
                                       selb_ref[pl.ds(row0, sel_keys), v_cols],
                                       tile_bias(heads, j * sel_tiles, sel_tiles), mask, scale))
        return tuple(out)

    n_steps = (i + sel_tiles) // sel_tiles
    sel_carries = lax.fori_loop(0, n_steps, sel_step, tuple(_softmax_init(NSA_HPG * qb) for _ in groups))

    first = jnp.maximum(i - (win_tiles - 1), 0)
    row0 = pl.multiple_of(first * LANES, LANES)
    d_win = i * qb + iq_w - (first * LANES + ik_w)
    mask_win = _stack_heads((d_win >= 0) & (d_win < WINDOW))
    for g, heads in enumerate(groups):
        k_cols, v_cols = key_cols(g)
        o_sel = _softmax_finish(sel_carries[g])
        o_win = _softmax_finish(_softmax_update(
            _softmax_init(NSA_HPG * qb), qgs[g], winb_ref[pl.ds(row0, win_keys), k_cols],
            winb_ref[pl.ds(row0, win_keys), v_cols], tile_bias(heads, first, win_tiles), mask_win, scale))
        for h, hh in enumerate(heads):
            rows = slice(h * qb, (h + 1) * qb)
            out = (gates[:, hh:hh + 1] * o_cmps[g][rows]
                   + gates[:, NSA_HEADS + hh:NSA_HEADS + hh + 1] * o_sel[rows]
                   + gates[:, 2 * NSA_HEADS + hh:2 * NSA_HEADS + hh + 1] * o_win[rows])
            y_ref[0, :, hh * hd:(hh + 1) * hd] = out.astype(BF16)


def _nsa_prompt(rel_bias, pb3, kc, bias_tiles):
    b, t, _ = pb3.shape
    qb = Q_BLOCK
    half = 2 * NSA_KV_DIM
    return pl.pallas_call(
        functools.partial(_nsa_prompt_body, n_sel=t // SEL_BLOCK),
        grid=(b, t // qb),
        in_specs=[
            pl.BlockSpec(memory_space=pltpu.SMEM),
            pl.BlockSpec((1, qb, NSA_DIM), lambda s, i: (s, i, PB_Q // NSA_DIM)),
            pl.BlockSpec((1, qb, LANES), lambda s, i: (s, i, PB_GATE // LANES)),
            pl.BlockSpec((1, t, half), lambda s, i: (s, 0, PB_SEL // half)),
            pl.BlockSpec((1, t, half), lambda s, i: (s, 0, PB_WIN // half)),
            pl.BlockSpec((1,) + kc.shape[1:], lambda s, i: (s, 0, 0)),
            pl.BlockSpec(bias_tiles.shape, lambda s, i: (0, 0, 0, 0)),
        ],
        out_specs=pl.BlockSpec((1, qb, NSA_DIM), lambda s, i: (s, i, 0)),
        out_shape=jax.ShapeDtypeStruct((b, t, NSA_DIM), BF16),
        scratch_shapes=[pltpu.VMEM((t, half), BF16), pltpu.VMEM((t, half), BF16)],
        compiler_params=_cparams(("parallel", "arbitrary")),
        name="nsa_prompt",
    )(rel_bias, pb3, pb3, pb3, pb3, kc, bias_tiles)


SEL_TILES = 2
SAMPLE_ROWS = 8
SAMPLE_TILE_KEYS = 1024


def _nsa_sample_body(rb_ref, q_ref, gate_ref, kc_ref, past_ref, new_sel_ref, cwin_ref, new_win_ref, y_ref,
                     *, past_len):
    hd = NSA_HEAD_DIM
    qb = SAMPLE_ROWS
    scale = hd ** -0.5
    n_cmp = kc_ref.shape[1]
    win_rows = cwin_ref.shape[1]
    k_off = past_len - win_rows
    tile_keys = min(SAMPLE_TILE_KEYS, past_len)
    assert past_len % tile_keys == 0

    q_bf = q_ref[0].astype(BF16)
    gates = jax.nn.sigmoid(gate_ref[0])
    iq = lax.broadcasted_iota(jnp.int32, (qb, LANES), 0)
    ik = lax.broadcasted_iota(jnp.int32, (qb, LANES), 1)
    pos_t = past_len + lax.broadcasted_iota(jnp.int32, (qb, tile_keys), 0)
    ik_t = lax.broadcasted_iota(jnp.int32, (qb, tile_keys), 1)
    d_win = (past_len + lax.broadcasted_iota(jnp.int32, (qb, win_rows), 0)
             - (k_off + lax.broadcasted_iota(jnp.int32, (qb, win_rows), 1)))
    tq_c = lax.broadcasted_iota(jnp.int32, (qb, n_cmp), 0)
    n_c = lax.broadcasted_iota(jnp.int32, (qb, n_cmp), 1)
    pos_c = past_len + tq_c
    d_cmp = pos_c - (n_c * CMP_BLOCK + CMP_BLOCK - 1)
    bias_cmp = _bias_for_heads(d_cmp, rb_ref, range(NSA_HEADS))
    vis_cmp = _stack_heads(d_cmp >= 0)
    n_sel_past = past_len // SEL_BLOCK
    picks = min(TOP_N, n_sel_past + 1) - 1

    groups = [[g * NSA_HPG + h for h in range(NSA_HPG)] for g in range(NSA_KV_HEADS)]

    def key_cols(g):
        return slice(g * hd, (g + 1) * hd), slice(NSA_KV_DIM + g * hd, NSA_KV_DIM + (g + 1) * hd)

    qgs, o_cmps, values = [], [], []
    for g, heads in enumerate(groups):
        k_cols, v_cols = key_cols(g)
        qg = jnp.concatenate([q_bf[:, hh * hd:(hh + 1) * hd] for hh in heads], axis=0)
        o_cmp, p_cmp = _cmp_branch(qg, kc_ref[0, :, k_cols].astype(BF16), kc_ref[0, :, v_cols].astype(BF16),
                                   jnp.concatenate([bias_cmp[hh] for hh in heads], axis=0), vis_cmp, scale)
        qgs.append(qg)
        o_cmps.append(o_cmp)
        values.append(_selection_values(p_cmp, qb, pos_c, n_sel_past))
    chosen = _topk_mask(jnp.concatenate(values, axis=0), picks).astype(BF16)
    chosens = [chosen[g * qb:(g + 1) * qb] for g in range(NSA_KV_HEADS)]

    def sel_step(j, carries):
        row0 = pl.multiple_of(j * tile_keys, tile_keys)
        dist = pos_t - (j * tile_keys + ik_t)
        out = []
        for g, heads in enumerate(groups):
            k_cols, v_cols = key_cols(g)
            mask = _stack_heads(_expand_selection(chosens[g], j, tile_keys))
            bias = jnp.concatenate(_bias_for_heads(dist, rb_ref, heads), axis=0)
            kt = past_ref[0, g, pl.ds(row0, tile_keys), :].astype(BF16)
            vt = past_ref[0, NSA_KV_HEADS + g, pl.ds(row0, tile_keys), :].astype(BF16)
            out.append(_softmax_update(carries[g], qgs[g], kt, vt, bias, mask, scale))
        return tuple(out)

    sel_carries = lax.fori_loop(0, past_len // tile_keys, sel_step,
                                tuple(_softmax_init(NSA_HPG * qb) for _ in groups))
    d_new = iq - ik
    mask_new = _stack_heads(d_new >= 0)
    mask_win = _stack_heads((d_win >= 0) & (d_win < WINDOW))
    for g, heads in enumerate(groups):
        k_cols, v_cols = key_cols(g)
        qg, o_cmp = qgs[g], o_cmps[g]
        bias_new = jnp.concatenate(_bias_for_heads(d_new, rb_ref, heads), axis=0)
        o_sel = _softmax_finish(_softmax_update(sel_carries[g], qg, new_sel_ref[0, :, k_cols],
                                                new_sel_ref[0, :, v_cols], bias_new, mask_new, scale))
        bias_win = jnp.concatenate(_bias_for_heads(d_win, rb_ref, heads), axis=0)
        carry = _softmax_update(_softmax_init(NSA_HPG * qb), qg, cwin_ref[0, :, k_cols].astype(BF16),
                                cwin_ref[0, :, v_cols].astype(BF16), bias_win, mask_win, scale)
        o_win = _softmax_finish(_softmax_update(carry, qg, new_win_ref[0, :, k_cols], new_win_ref[0, :, v_cols],
                                                bias_new, mask_new, scale))

        for h, hh in enumerate(heads):
            rows = slice(h * qb, (h + 1) * qb)
            out = (gates[:, hh:hh + 1] * o_cmp[rows]
                   + gates[:, NSA_HEADS + hh:NSA_HEADS + hh + 1] * o_sel[rows]
                   + gates[:, 2 * NSA_HEADS + hh:2 * NSA_HEADS + hh + 1] * o_win[rows])
            y_ref[0, :, hh * hd:(hh + 1) * hd] = out.astype(BF16)


def _nsa_sample(rel_bias, q8, gate8, kc, past_sel, new_sel, cwin, new_win, past_len):
    bs = q8.shape[0]
    blk = lambda a: pl.BlockSpec((1,) + a.shape[1:], lambda s: (s,) + (0,) * (a.ndim - 1))
    return pl.pallas_call(
        functools.partial(_nsa_sample_body, past_len=past_len),
        grid=(bs,),
        in_specs=[pl.BlockSpec(memory_space=pltpu.SMEM), blk(q8), blk(gate8), blk(kc), blk(past_sel),
                  blk(new_sel), blk(cwin), blk(new_win)],
        out_specs=pl.BlockSpec((1, SAMPLE_ROWS, NSA_DIM), lambda s: (s, 0, 0)),
        out_shape=jax.ShapeDtypeStruct((bs, SAMPLE_ROWS, NSA_DIM), BF16),
        compiler_params=_cparams(("parallel",)),
        name="nsa_sample",
    )(rel_bias, q8, gate8, kc, past_sel, new_sel, cwin, new_win)


def _merge_body(x_ref, yrw_ref, g_ref, ynsa_ref, gates_ref, wrw_ref, wnsa_ref, wout_ref, post_ref, o_ref):
    tm, d = x_ref.shape
    y_rw = jnp.concatenate([yrw_ref[:, pair, :] for pair in range(RW_DIM // LANES)], axis=1)
    y_rw = (y_rw * g_ref[...]).astype(BF16)
    merged = (jax.nn.sigmoid(gates_ref[:, :d]) * _dot(y_rw, wrw_ref[...])
              + jax.nn.sigmoid(gates_ref[:, d:]) * _dot(ynsa_ref[...], wnsa_ref[...]))
    o_ref[...] = x_ref[...] + _rms(_dot(merged.astype(BF16), wout_ref[...]), post_ref[...])


def _merge(x, y_rw, g_rw, y_nsa, gates, w_rw, w_nsa, w_out, post_g, tm):
    m, d = x.shape
    rows = lambda n: pl.BlockSpec((tm, n), lambda i: (i, 0))
    held = lambda a: pl.BlockSpec(a.shape, lambda i: (0, 0), pipeline_mode=pl.Buffered(1))
    return pl.pallas_call(
        _merge_body,
        grid=(m // tm,),
        in_specs=[rows(d), pl.BlockSpec((tm,) + y_rw.shape[1:], lambda i: (i, 0, 0)),
                  rows(RW_DIM), rows(NSA_DIM), rows(2 * d),
                  held(w_rw), held(w_nsa), held(w_out), pl.BlockSpec((1, d), lambda i: (0, 0))],
        out_specs=rows(d),
        out_shape=jax.ShapeDtypeStruct((m, d), F32),
        compiler_params=_cparams(("parallel",)),
        name="merge",
    )(x, y_rw, g_rw, y_nsa, gates, w_rw, w_nsa, w_out, post_g.reshape(1, d))


def _row_tile(m, want):
    return want if m % want == 0 else m


def _rwkv_mix(p3, shift0, s0, w):
    b, t, c = p3.shape
    n = RW_HEAD_DIM
    tt = 256 if t % 256 == 0 else t
    prev = jnp.concatenate([shift0[:, None, :], p3[:, tt - 1:t - 1:tt]], axis=1).reshape(b, t // tt, 1, c)
    pre, g_rw = _rwkv_pre(p3, prev, w["mu"], w["w0"], w["ww"], w["a0"], w["wa"], w["g2"], w["k_k"], w["k_a"], tt)
    seqs = min(b, LANES // RW_HEADS)
    groups = b // seqs
    pairs = RW_HEADS // 2
    s0_t = s0.reshape(groups, seqs, pairs, 2, n, n).transpose(5, 4, 0, 3, 1, 2).reshape(n, n, b * RW_HEADS)
    per_lane = lambda v: jnp.broadcast_to(v.reshape(pairs, 2, n).transpose(2, 1, 0)[:, :, None, :],
                                          (n, 2, seqs, pairs)).reshape(n, seqs * RW_HEADS)
    tc = 32 if t % 32 == 0 else t
    y_t, s_t = _rwkv_scan(pre, s0_t, per_lane(w["lnx_w"]), per_lane(w["lnx_b"]), per_lane(w["r_k"]), tc, seqs)
    s_fin = s_t.reshape(n, n, groups, 2, seqs, pairs).transpose(2, 4, 5, 3, 1, 0).reshape(b, RW_HEADS, n, n)
    y_tok = y_t.reshape(t, b, pairs, LANES).transpose(1, 0, 2, 3).reshape(b * t, pairs, LANES)
    return y_tok, g_rw.reshape(b * t, RW_DIM), s_fin


def _layer(x, w, rel_bias, shift0, s0, past):
    b, t, d = x.shape
    m = b * t
    tm = _row_tile(m, 512)
    x1 = _ffn(x.reshape(m, d), w["f1_pre"], w["f1_post"], w["f1_w1"], w["f1_w3"], w["f1_w2"], tm, 512)
    p_a = _norm_matmul(x1, w["mix_pre"], w["w_in_a"], tm, "proj_rwkv")
    p_b = _norm_matmul(x1, w["mix_pre"], w["w_in_b"], tm, "proj_nsa")
    p_c = _norm_matmul(x1, w["mix_pre"], w["w_in_c"], tm, "proj_gates")

    p3 = p_a.reshape(b, t, RW_PROJ)
    y_rw, g_rw, s_fin = _rwkv_mix(p3, shift0, s0, w)

    pb3 = p_b.reshape(b, t, PB_COLS)
    kv_new = pb3[:, :, PB_KV:PB_WIN]
    win_new = pb3[:, :, PB_WIN:PB_GATE]
    row_w = 4 * NSA_KV_DIM
    half = 2 * NSA_KV_DIM
    if past is None:
        blocks = p_b.reshape(m // CMP_BLOCK, CMP_BLOCK, PB_COLS)
        kc = _cmp_mlp(blocks, w["cmp_pe"], w["cmp_w1"], w["cmp_w2"], min(64, blocks.shape[0]), PB_KV // half)
        n_cmp = t // CMP_BLOCK
        kc = jnp.pad(kc.reshape(b, n_cmp, 2 * NSA_KV_DIM), ((0, 0), (0, -n_cmp % LANES), (0, 0)))
        y_nsa = _nsa_prompt(rel_bias, pb3, kc, _bias_tiles(rel_bias)).reshape(m, NSA_DIM)
        win_out = win_new[:, t - min(WINDOW, t):]
    else:
        page_table, pool, cache_win = past
        past_len = page_table.shape[1] * pool.shape[1]
        slabs = row_w // NSA_HEAD_DIM
        pool4 = pool.reshape(pool.shape[0], pool.shape[1], slabs, NSA_HEAD_DIM)
        kc = _paged_kv(page_table, pool4, w["cmp_pe_tiles"], w["cmp_w1cat"], w["cmp_w2"])
        past_sel = _gather_selected(page_table, pool4)
        n_cmp = past_len // CMP_BLOCK
        kc = jnp.pad(kc, ((0, 0), (0, -n_cmp % LANES), (0, 0)))
        pad_q = lambda a, rows: jnp.pad(a, ((0, 0), (0, rows - t), (0, 0)))
        cwin = cache_win.reshape(b, cache_win.shape[1], half)
        y8 = _nsa_sample(rel_bias, pad_q(pb3[:, :, :NSA_DIM], SAMPLE_ROWS), pad_q(pb3[:, :, PB_GATE:], SAMPLE_ROWS),
                         kc, past_sel, pad_q(pb3[:, :, PB_SEL:PB_WIN].astype(BF16), LANES), cwin,
                         pad_q(win_new.astype(BF16), LANES), past_len)
        y_nsa = y8[:, :t].reshape(m, NSA_DIM)
        win_all = jnp.concatenate([cwin, win_new], axis=1)
        win_out = win_all[:, win_all.shape[1] - min(WINDOW, win_all.shape[1]):]

    x2 = _merge(x1, y_rw, g_rw, y_nsa, p_c, w["w_br_rw"], w["w_br_nsa"], w["w_out"], w["mix_post"],
                _row_tile(m, 256))
    y = _ffn(x2, w["f2_pre"], w["f2_post"], w["f2_w1"], w["f2_w3"], w["f2_w2"], tm, 512)
    g, hd = NSA_KV_HEADS, NSA_HEAD_DIM
    return (y.reshape(b, t, d), kv_new.reshape(b, t, 4, g, hd), win_out.reshape(b, -1, 2, g, hd), s_fin, p3[:, -1])


def _prepare_weights(l, ffn1_pre_g, ffn1_post_g, ffn1_w1, ffn1_w3, ffn1_w2, mix_pre_g, mix_post_g, w_in,
                     rw_mu, rw_w0, rw_w2, rw_a0, rw_a2, rw_g2, rw_k_k, rw_k_a, rw_r_k, rw_lnx_w, rw_lnx_b,
                     cmp_pe, cmp_w1, cmp_w2, w_br_rw, w_br_nsa, w_out,
                     ffn2_pre_g, ffn2_post_g, ffn2_w1, ffn2_w3, ffn2_w2):
    d = w_in.shape[1]
    wi = w_in[l]
    c_q = RW_PROJ
    c_kv = c_q + NSA_DIM
    c_gate = c_kv + 6 * NSA_KV_DIM
    c_grw = c_gate + 3 * NSA_HEADS
    w_in_b = jnp.concatenate([wi[:, c_q:c_gate], jnp.pad(wi[:, c_gate:c_grw], ((0, 0), (0, LANES - 3 * NSA_HEADS)))],
                             axis=1)
    zeros_w = jnp.zeros((RW_LORA_A, RW_DIM), F32)
    zeros_a = jnp.zeros((RW_LORA_W, RW_DIM), F32)
    pe = cmp_pe[l]
    pe_rows = jnp.broadcast_to(pe[:, :, None, :], (CMP_BLOCK, 2, NSA_KV_HEADS, NSA_HEAD_DIM))
    return {
        "f1_pre": ffn1_pre_g[l], "f1_post": ffn1_post_g[l],
        "f1_w1": ffn1_w1[l].astype(BF16), "f1_w3": ffn1_w3[l].astype(BF16), "f1_w2": ffn1_w2[l].astype(BF16),
        "mix_pre": mix_pre_g[l], "mix_post": mix_post_g[l],
        "w_in_a": wi[:, :RW_PROJ].astype(BF16), "w_in_b": w_in_b.astype(BF16), "w_in_c": wi[:, c_grw:].astype(BF16),
        "mu": rw_mu[l], "w0": rw_w0[l], "a0": rw_a0[l],
        "ww": jnp.concatenate([rw_w2[l], zeros_w], axis=0).astype(BF16),
        "wa": jnp.concatenate([zeros_a, rw_a2[l]], axis=0).astype(BF16),
        "g2": rw_g2[l].astype(BF16), "k_k": rw_k_k[l], "k_a": rw_k_a[l],
        "r_k": rw_r_k[l].reshape(-1), "lnx_w": rw_lnx_w[l], "lnx_b": rw_lnx_b[l],
        "cmp_pe": pe_rows.reshape(CMP_BLOCK, 2 * NSA_KV_DIM),
        "cmp_pe_tiles": jnp.pad(pe_rows.reshape(CMP_BLOCK, 2 * NSA_KV_HEADS, NSA_HEAD_DIM),
                                ((0, 0), (0, 2 * NSA_KV_HEADS), (0, 0))),
        "cmp_w1": cmp_w1[l].astype(BF16), "cmp_w2": cmp_w2[l].astype(BF16),
        "cmp_w1cat": jnp.concatenate([cmp_w1[l, 0], cmp_w1[l, 1]], axis=1).astype(BF16),
        "w_br_rw": w_br_rw[l].astype(BF16), "w_br_nsa": w_br_nsa[l].astype(BF16), "w_out": w_out[l].astype(BF16),
        "f2_pre": ffn2_pre_g[l], "f2_post": ffn2_post_g[l],
        "f2_w1": ffn2_w1[l].astype(BF16), "f2_w3": ffn2_w3[l].astype(BF16), "f2_w2": ffn2_w2[l].astype(BF16),
    }


def kernel(x_prompt, x_sample, cache_kv, cache_win, state_rwkv, state_shift, page_table,
           ffn1_pre_g, ffn1_post_g, ffn1_w1, ffn1_w3, ffn1_w2, mix_pre_g, mix_post_g, w_in,
           rw_mu, rw_w0, rw_w2, rw_a0, rw_a2, rw_g2, rw_k_k, rw_k_a, rw_r_k, rw_lnx_w, rw_lnx_b,
           cmp_pe, cmp_w1, cmp_w2, w_br_rw, w_br_nsa, w_out,
           ffn2_pre_g, ffn2_post_g, ffn2_w1, ffn2_w3, ffn2_w2, rel_bias):
    depth = w_in.shape[0]
    b_p = x_prompt.shape[0]
    y_p, y_s = x_prompt, x_sample
    outs = [[] for _ in range(8)]
    for l in range(depth):
        w = _prepare_weights(l, ffn1_pre_g, ffn1_post_g, ffn1_w1, ffn1_w3, ffn1_w2, mix_pre_g, mix_post_g, w_in,
                             rw_mu, rw_w0, rw_w2, rw_a0, rw_a2, rw_g2, rw_k_k, rw_k_a, rw_r_k, rw_lnx_w, rw_lnx_b,
                             cmp_pe, cmp_w1, cmp_w2, w_br_rw, w_br_nsa, w_out,
                             ffn2_pre_g, ffn2_post_g, ffn2_w1, ffn2_w3, ffn2_w2)
        y_p, kv_p, win_p, rw_p, sh_p = _layer(
            y_p, w, rel_bias, jnp.zeros((b_p, RW_PROJ), F32),
            jnp.zeros((b_p, RW_HEADS, RW_HEAD_DIM, RW_HEAD_DIM), F32), None)
        y_s, kv_s, win_s, rw_s, sh_s = _layer(
            y_s, w, rel_bias, state_shift[l], state_rwkv[l], (page_table, cache_kv[l], cache_win[l]))
        for acc, v in zip(outs, (kv_p, kv_s, win_p, win_s, rw_p, rw_s, sh_p, sh_s)):
            acc.append(v)
    return (y_p, y_s) + tuple(jnp.stack(o) for o in outs)
```

```python
import functools
import math

import jax
import jax.numpy as jnp
from jax import lax
from jax.experimental import pallas as pl
from jax.experimental.pallas import tpu as pltpu
from jax.experimental.pallas import tpu_sc as plsc

F32 = jnp.float32
BF16 = jnp.bfloat16

RMS_EPS = 1e-6
RW_HEADS = 16
RW_HEAD_DIM = 64
RW_DIM = RW_HEADS * RW_HEAD_DIM
RW_LORA_W = 64
RW_LORA_A = 64
RW_LORA_G = 128
RW_PROJ = 3 * RW_DIM + RW_LORA_W + RW_LORA_A + RW_LORA_G
RW_LN_EPS = 64e-5
NSA_HEADS = 8
NSA_KV_HEADS = 2
NSA_HPG = NSA_HEADS // NSA_KV_HEADS
NSA_HEAD_DIM = 128
NSA_DIM = NSA_HEADS * NSA_HEAD_DIM
NSA_KV_DIM = NSA_KV_HEADS * NSA_HEAD_DIM
CMP_BLOCK = 32
CMP_HID = 256
SEL_BLOCK = 64
SEL_SHIFT = SEL_BLOCK.bit_length() - 1
TOP_N = 16
WINDOW = 512
N_BUCKETS = 32
REL_MAX_EXACT = 16
REL_MAX_DIST = 1024
Q_BLOCK = 128
NEG_BIG = -1e30
FORCE_BONUS = 1e4
NEVER = -3e38
LANES = 128
SUBLANES = 8
SCAN_FIELDS = 6
VMEM_LIMIT = 56 * 1024 * 1024
SCAN_UNROLL = 8
UPDATE_UNROLL = 16

PB_Q = 0
PB_KV = NSA_DIM
PB_SEL = PB_KV + 2 * NSA_KV_DIM
PB_WIN = PB_KV + 4 * NSA_KV_DIM
PB_GATE = PB_WIN + 2 * NSA_KV_DIM
PB_COLS = PB_GATE + LANES


def _bucket_thresholds():
    thr = list(range(1, REL_MAX_EXACT + 1))
    n_log = N_BUCKETS - REL_MAX_EXACT
    ratio = REL_MAX_DIST // REL_MAX_EXACT
    n = REL_MAX_EXACT
    for k in range(1, n_log):
        while n ** n_log < REL_MAX_EXACT ** n_log * ratio ** k:
            n += 1
        thr.append(n)
    return thr


BUCKET_THR = _bucket_thresholds()
FAR_DIST = BUCKET_THR[-1]


def _cparams(sem):
    return pltpu.CompilerParams(dimension_semantics=sem, vmem_limit_bytes=VMEM_LIMIT)


def _rms(x, g):
    ms = jnp.mean(x * x, axis=-1, keepdims=True)
    return x * lax.rsqrt(ms + RMS_EPS) * g


def _dot(a, b):
    return jnp.dot(a, b, preferred_element_type=F32)


def _dot_nt(a, b):
    return lax.dot_general(a, b, (((1,), (1,)), ((), ())), preferred_element_type=F32)


def _bias_for_heads(dist, rb_ref, heads):
    n = jnp.maximum(dist, 0)
    reach = [n >= t for t in BUCKET_THR]
    out = []
    for h in heads:
        val = jnp.full(dist.shape, rb_ref[0, h], F32)
        for b, m in enumerate(reach):
            val = jnp.where(m, rb_ref[b + 1, h], val)
        out.append(val)
    return out


def _softmax_update(carry, q, kt, vt, bias, mask, scale):
    m_run, l_run, acc = carry
    s = _dot_nt(q, kt) * scale + bias
    s = jnp.where(mask, s, NEG_BIG)
    m_new = jnp.maximum(m_run, jnp.max(s, axis=-1, keepdims=True))
    alpha = jnp.exp(m_run - m_new)
    e = jnp.where(mask, jnp.exp(s - m_new), 0.0)
    l_new = alpha * l_run + jnp.sum(e, axis=-1, keepdims=True)
    acc = alpha * acc + _dot(e.astype(BF16), vt)
    return m_new, l_new, acc


def _softmax_init(rows):
    return (jnp.full((rows, 1), NEG_BIG, F32), jnp.zeros((rows, 1), F32),
            jnp.zeros((rows, NSA_HEAD_DIM), F32))


def _softmax_finish(carry):
    _, l_run, acc = carry
    return acc / jnp.maximum(l_run, 1e-30)


def _stack_heads(x):
    return jnp.concatenate([x] * NSA_HPG, axis=0)


def _topk_mask(val, k):
    lane = lax.broadcasted_iota(jnp.int32, val.shape, 1)
    chosen = jnp.zeros(val.shape, F32)
    for _ in range(k):
        hit = lane == jnp.argmax(val, axis=-1, keepdims=True)
        chosen = jnp.where(hit, 1.0, chosen)
        val = jnp.where(hit, NEVER, val)
    return chosen


def _pair_sum(x):
    parts = []
    for c in range(x.shape[1] // LANES):
        blk = x[:, c * LANES:(c + 1) * LANES]
        parts.append(blk + pltpu.roll(blk, LANES - 1, 1))
    return parts[0] if len(parts) == 1 else jnp.concatenate(parts, axis=1)


def _cmp_branch(qg, kc_k, kc_v, bias, visible, scale):
    s = _dot_nt(qg, kc_k) * scale + bias
    s = jnp.where(visible, s, NEG_BIG)
    e = jnp.where(visible, jnp.exp(s - jnp.max(s, axis=-1, keepdims=True)), 0.0)
    p = e / jnp.maximum(jnp.sum(e, axis=-1, keepdims=True), 1e-30)
    return _dot(p.astype(BF16), kc_v), p


def _selection_values(p, qb, pos, n_sel):
    imp = p[0:qb]
    for h in range(1, NSA_HPG):
        imp = imp + p[h * qb:(h + 1) * qb]
    imp = _pair_sum(imp)
    lane = lax.broadcasted_iota(jnp.int32, imp.shape, 1)
    sel_id = lane >> 1
    cur = pos >> SEL_SHIFT
    forced = (sel_id == 0) | (sel_id == cur) | (sel_id == cur - 1)
    val = jnp.where(forced, imp + FORCE_BONUS, imp)
    val = jnp.where(sel_id * SEL_BLOCK <= pos, val, NEG_BIG)
    return jnp.where(((lane & 1) == 0) & (sel_id < n_sel), val, NEVER)


def _expand_selection(chosen_bf16, tile, tile_keys=LANES):
    width = chosen_bf16.shape[1]
    n_idx = lax.broadcasted_iota(jnp.int32, (width, tile_keys), 0)
    k_idx = lax.broadcasted_iota(jnp.int32, (width, tile_keys), 1)
    target = 2 * ((tile * tile_keys + k_idx) >> SEL_SHIFT)
    expand = jnp.where(n_idx == target, 1.0, 0.0).astype(BF16)
    return _dot(chosen_bf16, expand) > 0.5


def _ffn_body(x_ref, pre_ref, post_ref, w1_ref, w3_ref, w2_ref, o_ref, h_ref, acc_ref):
    j = pl.program_id(1)

    @pl.when(j == 0)
    def _():
        h_ref[...] = _rms(x_ref[...], pre_ref[...]).astype(BF16)
        acc_ref[...] = jnp.zeros_like(acc_ref)

    h = h_ref[...]
    a = _dot(h, w1_ref[...])
    b = _dot(h, w3_ref[...])
    u = (a * jax.nn.sigmoid(a)) * b
    acc_ref[...] += _dot(u.astype(BF16), w2_ref[...])

    @pl.when(j == pl.num_programs(1) - 1)
    def _():
        o_ref[...] = x_ref[...] + 0.5 * _rms(acc_ref[...], post_ref[...])


def _ffn(x, pre_g, post_g, w1, w3, w2, tm, tf):
    m, d = x.shape
    f = w1.shape[1]
    return pl.pallas_call(
        _ffn_body,
        grid=(m // tm, f // tf),
        in_specs=[
            pl.BlockSpec((tm, d), lambda i, j: (i, 0)),
            pl.BlockSpec((1, d), lambda i, j: (0, 0)),
            pl.BlockSpec((1, d), lambda i, j: (0, 0)),
            pl.BlockSpec((d, tf), lambda i, j: (0, j)),
            pl.BlockSpec((d, tf), lambda i, j: (0, j)),
            pl.BlockSpec((tf, d), lambda i, j: (j, 0)),
        ],
        out_specs=pl.BlockSpec((tm, d), lambda i, j: (i, 0)),
        out_shape=jax.ShapeDtypeStruct((m, d), F32),
        scratch_shapes=[pltpu.VMEM((tm, d), BF16), pltpu.VMEM((tm, d), F32)],
        compiler_params=_cparams(("parallel", "arbitrary")),
        name="ffn",
    )(x, pre_g.reshape(1, d), post_g.reshape(1, d), w1, w3, w2)


def _norm_matmul_body(x_ref, g_ref, w_ref, o_ref, h_ref):
    @pl.when(pl.program_id(1) == 0)
    def _():
        h_ref[...] = _rms(x_ref[...], g_ref[...]).astype(BF16)

    o_ref[...] = _dot(h_ref[...], w_ref[...])


def _norm_matmul(x, g, w, tm, name):
    m, d = x.shape
    n = w.shape[1]
    return pl.pallas_call(
        _norm_matmul_body,
        grid=(m // tm, 1),
        in_specs=[
            pl.BlockSpec((tm, d), lambda i, j: (i, 0)),
            pl.BlockSpec((1, d), lambda i, j: (0, 0)),
            pl.BlockSpec((d, n), lambda i, j: (0, 0), pipeline_mode=pl.Buffered(1)),
        ],
        out_specs=pl.BlockSpec((tm, n), lambda i, j: (i, 0)),
        out_shape=jax.ShapeDtypeStruct((m, n), F32),
        scratch_shapes=[pltpu.VMEM((tm, d), BF16)],
        compiler_params=_cparams(("parallel", "arbitrary")),
        name=name,
    )(x, g.reshape(1, d), w)


def _rwkv_pre_body(p_ref, prev_ref, mu_ref, w0_ref, ww_ref, a0_ref, wa_ref, g2_ref, kk_ref, ka_ref,
                   pre_ref, g_ref, buf_ref):
    tt = p_ref.shape[1]
    p = p_ref[0]
    buf_ref[8:8 + tt, :] = p
    buf_ref[7:8, :] = prev_ref[0, 0]
    p_prev = buf_ref[7:7 + tt, :]
    xs = p + (p_prev - p) * mu_ref[...]
    d = RW_DIM
    r = xs[:, 0:d]
    k = xs[:, d:2 * d]
    v = xs[:, 2 * d:3 * d]
    lora_in = xs[:, 3 * d:3 * d + RW_LORA_W + RW_LORA_A]
    gd = xs[:, 3 * d + RW_LORA_W + RW_LORA_A:]
    z = -(w0_ref[...] + _dot(jnp.tanh(lora_in).astype(BF16), ww_ref[...]))
    softplus = jnp.maximum(z, 0.0) + jnp.log(1.0 + jnp.exp(-jnp.abs(z)))
    decay = jnp.exp(-jnp.exp(-softplus - 0.5))
    a = jax.nn.sigmoid(a0_ref[...] + _dot(lora_in.astype(BF16), wa_ref[...]))
    g_ref[0] = _dot(jax.nn.sigmoid(gd).astype(BF16), g2_ref[...])

    n = RW_HEAD_DIM
    pairs = d // LANES
    low_half = lax.broadcasted_iota(jnp.int32, (tt, LANES), 1) < n

    def put(fp, xa, xb):
        for pair in range(pairs):
            col_a = xa[:, pair * LANES:(pair + 1) * LANES]
            col_b = xb[:, pair * LANES:(pair + 1) * LANES]
            pieces = (jnp.where(low_half, col_a, pltpu.roll(col_b, n, 1)),
                      jnp.where(low_half, pltpu.roll(col_a, n, 1), col_b))
            for parity, piece in enumerate(pieces):
                pre_ref[:, fp, parity, pair, :] = piece

    put(0, r, k * (1.0 + (a - 1.0) * ka_ref[...]))
    put(1, v, decay)
    put(2, k * kk_ref[...], a)


def _rwkv_pre(p3, prev, mu, w0, ww, a0, wa, g2, k_k, k_a, tt):
    b, t, c = p3.shape
    d = RW_DIM
    pairs = RW_HEADS // 2
    row = lambda n: pl.BlockSpec((1, n), lambda i, j: (0, 0))
    full = lambda s: pl.BlockSpec(s, lambda i, j: (0, 0))
    return pl.pallas_call(
        _rwkv_pre_body,
        grid=(b, t // tt),
        in_specs=[
            pl.BlockSpec((1, tt, c), lambda i, j: (i, j, 0)),
            pl.BlockSpec((1, 1, 1, c), lambda i, j: (i, j, 0, 0)),
            row(c), row(d), full(ww.shape), row(d), full(wa.shape), full(g2.shape), row(d), row(d),
        ],
        out_specs=[
            pl.BlockSpec((tt, SCAN_FIELDS // 2, 2, pairs, LANES), lambda i, j: (j, 0, 0, i, 0)),
            pl.BlockSpec((1, tt, d), lambda i, j: (i, j, 0)),
        ],
        out_shape=[jax.ShapeDtypeStruct((t, SCAN_FIELDS // 2, 2, b * pairs, LANES), F32),
                   jax.ShapeDtypeStruct((b, t, d), F32)],
        scratch_shapes=[pltpu.VMEM((tt + 8, c), F32)],
        compiler_params=_cparams(("parallel", "arbitrary")),
        name="rwkv_pre",
    )(p3, prev, mu.reshape(1, c), w0.reshape(1, d), ww, a0.reshape(1, d), wa, g2,
      k_k.reshape(1, d), k_a.reshape(1, d))


def _rwkv_scan_body(in_ref, s0_ref, lnw_ref, lnb_ref, rk_ref, y_ref, s_ref, xa_ref, xb_ref, kk_ref, b_ref,
                    out_ref):
    n = RW_HEAD_DIM
    steps = in_ref.shape[0]
    n_fields = in_ref.shape[1]
    half = in_ref.shape[3]
    lanes = 2 * half
    fields_per_trip = -(-n_fields // (n // UPDATE_UNROLL))

    def load_field(t, dst_ref, fp):
        tile = in_ref[t, fp].reshape(lanes, 2 * n)
        dst_ref[pl.ds(pl.multiple_of(fp * 2 * n, 2 * n), 2 * n), :] = tile.T

    def normalise_key(x_ref):
        kk_raw = x_ref[4 * n:5 * n, :]
        norm = jnp.sqrt(jnp.sum(kk_raw * kk_raw, axis=0, keepdims=True))
        kk = kk_raw / jnp.maximum(norm, 1e-12)
        kk_ref[...] = kk
        b_ref[...] = kk * x_ref[5 * n:6 * n, :]

    def one_step(t, x_ref, next_ref):
        t_next = jnp.minimum(t + 1, steps - 1)

        def sk_rows(jb, acc):
            for u in range(SCAN_UNROLL):
                j = jb * SCAN_UNROLL + u
                acc = acc + s_ref[j] * kk_ref[pl.ds(j, 1), :]
            return acc

        sk = lax.fori_loop(0, n // SCAN_UNROLL, sk_rows, jnp.zeros((n, lanes), F32))
        v = x_ref[2 * n:3 * n, :]

        def update_rows(jb, acc):
            for f in range(fields_per_trip):
                load_field(t_next, next_ref, jnp.minimum(jb * fields_per_trip + f, n_fields - 1))
            emit_output(jnp.maximum(t - 1, 0))
            for u in range(UPDATE_UNROLL):
                j = jb * UPDATE_UNROLL + u
                r_j = x_ref[pl.ds(j, 1), :]
                k_j = x_ref[pl.ds(n + j, 1), :]
                w_j = x_ref[pl.ds(3 * n + j, 1), :]
                s_j = s_ref[j] * w_j - sk * b_ref[pl.ds(j, 1), :] + v * k_j
                s_ref[j] = s_j
                acc = acc + s_j * r_j
            return acc

        y = lax.fori_loop(0, n // UPDATE_UNROLL, update_rows, jnp.zeros((n, lanes), F32))
        normalise_key(next_ref)
        mean = jnp.sum(y, axis=0, keepdims=True) * (1.0 / n)
        yc = y - mean
        var = jnp.sum(yc * yc, axis=0, keepdims=True) * (1.0 / n)
        y_norm = yc * lax.rsqrt(var + RW_LN_EPS) * lnw_ref[...] + lnb_ref[...]
        rkr = x_ref[0:n, :] * x_ref[n:2 * n, :] * rk_ref[...]
        out_ref[...] = y_norm + jnp.sum(rkr, axis=0, keepdims=True) * v

    def emit_output(t):
        out = out_ref[...]
        y_ref[t] = jnp.concatenate([out[:, :half], out[:, half:]], axis=0).T

    @pl.when(pl.program_id(1) == 0)
    def _():
        s_ref[...] = s0_ref[...]

    out_ref[...] = jnp.zeros_like(out_ref)
    for c in range(n_fields):
        load_field(0, xa_ref, c)
    normalise_key(xa_ref)

    def step_pair(tp, carry):
        one_step(2 * tp, xa_ref, xb_ref)
        one_step(2 * tp + 1, xb_ref, xa_ref)
        return carry

    lax.fori_loop(0, steps // 2, step_pair, 0)
    emit_output(steps - 1)


def _rwkv_scan(pre5, s0, lnw, lnb, rk, tc, seqs):
    t, field_pairs, _, rows, _ = pre5.shape
    n = RW_HEAD_DIM
    pairs = RW_HEADS // 2
    x_rows = field_pairs * 2 * n
    lb = seqs * RW_HEADS
    assert tc % 2 == 0 and t % tc == 0
    tab = pl.BlockSpec((n, lb), lambda l, i: (0, 0))
    return pl.pallas_call(
        _rwkv_scan_body,
        grid=(rows // (seqs * pairs), t // tc),
        in_specs=[
            pl.BlockSpec((tc, field_pairs, 2, seqs * pairs, LANES), lambda l, i: (i, 0, 0, l, 0)),
            pl.BlockSpec((n, n, lb), lambda l, i: (0, 0, l)),
            tab, tab, tab,
        ],
        out_specs=[
            pl.BlockSpec((tc, seqs * pairs, LANES), lambda l, i: (i, l, 0)),
            pl.BlockSpec((n, n, lb), lambda l, i: (0, 0, l)),
        ],
        out_shape=[jax.ShapeDtypeStruct((t, rows, LANES), F32),
                   jax.ShapeDtypeStruct((n, n, rows * 2), F32)],
        scratch_shapes=[pltpu.VMEM((x_rows, lb), F32), pltpu.VMEM((x_rows, lb), F32),
                        pltpu.VMEM((n, lb), F32), pltpu.VMEM((n, lb), F32), pltpu.VMEM((n, lb), F32)],
        compiler_params=_cparams(("parallel", "arbitrary")),
        name="rwkv_scan",
    )(pre5, s0, lnw, lnb, rk)


GATHER_PAGES = 8


GATHER_WINDOW = 128


def _gather_rows(rows, idx):
    n = idx.shape[1]
    width = rows.shape[1]
    mesh = plsc.VectorSubcoreMesh(core_axis_name="core", subcore_axis_name="subcore")

    @pl.kernel(out_type=jax.ShapeDtypeStruct((n, width), rows.dtype), mesh=mesh)
    def gather_kernel(x_hbm, i_hbm, o_hbm):
        def body(i_vmem, o_vmem):
            pltpu.sync_copy(x_hbm.at[i_vmem.at[0]], o_vmem)

        pltpu.emit_pipeline(
            body,
            grid=(n // GATHER_WINDOW,),
            in_specs=[pl.BlockSpec((1, GATHER_WINDOW), index_map=lambda i: (0, i))],
            out_specs=[pl.BlockSpec((GATHER_WINDOW, width), index_map=lambda i: (i, 0))],
            core_axis_name=("core", "subcore"),
            dimension_semantics=(pltpu.PARALLEL,),
        )(i_hbm, o_hbm)

    return gather_kernel(rows, idx)


def _gather_selected(page_table, pool):
    bs, n_pages = page_table.shape
    n_pool, page, slabs, hd = pool.shape
    past_len = n_pages * page
    tok = jnp.arange(past_len, dtype=jnp.int32)
    row0 = (page_table[:, tok // page] * page + tok % page) * slabs
    idx = row0[:, None, :] + (slabs // 2 + jnp.arange(slabs // 2, dtype=jnp.int32))[None, :, None]
    out = _gather_rows(pool.reshape(n_pool * page * slabs, hd), idx.reshape(1, -1))
    return out.reshape(bs, slabs // 2, past_len, hd)


def _paged_kv_body(pt_ref, *refs):
    n_pages = len(refs) - 5
    page_refs = refs[:n_pages]
    pe_ref, w1_ref, w2_ref, kc_ref, out_ref = refs[n_pages:]
    _, rows, slabs, hd = page_refs[0].shape
    per_page = rows // CMP_BLOCK
    cols = []
    for r in range(CMP_BLOCK):
        tiles = [x_ref[0, nl * CMP_BLOCK + r] + pe_ref[r] for x_ref in page_refs for nl in range(per_page)]
        cols.append(jnp.concatenate(tiles, axis=0).astype(BF16))
    hid = jax.nn.gelu(_dot(jnp.concatenate(cols, axis=1), w1_ref[...]), approximate=True)
    out_ref[0] = _dot(hid[:, :CMP_HID].astype(BF16), w2_ref[0])
    out_ref[1] = _dot(hid[:, CMP_HID:].astype(BF16), w2_ref[1])
    n_blocks = n_pages * per_page
    for kg in range(slabs // 2):
        kc_ref[0, :, kg * hd:(kg + 1) * hd] = out_ref[kg // NSA_KV_HEADS, pl.ds(kg, n_blocks, stride=slabs), :]


def _paged_kv(page_table, pool, pe_tiles, w1cat, w2):
    bs, n_pages = page_table.shape
    _, page, slabs, hd = pool.shape
    half = slabs * hd // 2
    per_step = math.gcd(GATHER_PAGES, n_pages)
    blocks_step = per_step * page // CMP_BLOCK

    def page_spec(k):
        return pl.BlockSpec((1, page, slabs, hd), lambda b, p, pt: (pt[b, p * per_step + k], 0, 0, 0))

    held = lambda a: pl.BlockSpec(a.shape, lambda b, p, pt: (0,) * a.ndim)
    return pl.pallas_call(
        _paged_kv_body,
        grid_spec=pltpu.PrefetchScalarGridSpec(
            num_scalar_prefetch=1,
            grid=(bs, n_pages // per_step),
            in_specs=[page_spec(k) for k in range(per_step)] + [held(pe_tiles), held(w1cat), held(w2)],
            out_specs=pl.BlockSpec((1, blocks_step, half), lambda b, p, pt: (b, p, 0)),
            scratch_shapes=[pltpu.VMEM((2, blocks_step * slabs, hd), F32)],
        ),
        out_shape=jax.ShapeDtypeStruct((bs, n_pages * page // CMP_BLOCK, half), F32),
        compiler_params=_cparams(("parallel", "arbitrary")),
        name="paged_kv",
    )(page_table, *([pool] * per_step), pe_tiles, w1cat, w2)


def _cmp_mlp_body(x_ref, pe_ref, w1_ref, w2_ref, o_ref):
    hd = NSA_HEAD_DIM
    for kg in range(2 * NSA_KV_HEADS):
        ch = kg // NSA_KV_HEADS
        acc = None
        for r in range(CMP_BLOCK):
            xr = x_ref[:, r, kg * hd:(kg + 1) * hd]
            xr = (xr + pe_ref[r:r + 1, kg * hd:(kg + 1) * hd]).astype(BF16)
            part = _dot(xr, w1_ref[ch, r * hd:(r + 1) * hd, :])
            acc = part if acc is None else acc + part
        hid = jax.nn.gelu(acc, approximate=True)
        o_ref[:, kg * hd:(kg + 1) * hd] = _dot(hid.astype(BF16), w2_ref[ch])


def _cmp_mlp(blocks, pe_rows, w1, w2, nb, col_block):
    n_blocks, rows, _ = blocks.shape
    out_w = 2 * NSA_KV_DIM
    return pl.pallas_call(
        _cmp_mlp_body,
        grid=(n_blocks // nb,),
        in_specs=[
            pl.BlockSpec((nb, rows, out_w), lambda i: (i, 0, col_block)),
            pl.BlockSpec(pe_rows.shape, lambda i: (0, 0)),
            pl.BlockSpec(w1.shape, lambda i: (0, 0, 0)),
            pl.BlockSpec(w2.shape, lambda i: (0, 0, 0)),
        ],
        out_specs=pl.BlockSpec((nb, out_w), lambda i: (i, 0)),
        out_shape=jax.ShapeDtypeStruct((n_blocks, out_w), F32),
        compiler_params=_cparams(("parallel",)),
        name="cmp_mlp",
    )(blocks, pe_rows, w1, w2)


N_BIAS_TILES = -(-(FAR_DIST + LANES - 1) // LANES) + 1


def _bias_tiles_body(rb_ref, o_ref, cmp_ref):
    h = pl.program_id(0)
    iq = lax.broadcasted_iota(jnp.int32, (Q_BLOCK, LANES), 0)
    ik = lax.broadcasted_iota(jnp.int32, (Q_BLOCK, LANES), 1)

    def bias_of(dist):
        return _bias_for_heads(dist, rb_ref, [h])[0]

    for m in range(N_BIAS_TILES):
        o_ref[0, m] = bias_of(m * LANES + iq - ik)
    for i in range(cmp_ref.shape[1]):
        cmp_ref[0, i] = bias_of(i * Q_BLOCK + iq - (ik * CMP_BLOCK + CMP_BLOCK - 1))


def _bias_tiles(rel_bias, n_q_blocks):
    tile = (Q_BLOCK, LANES)
    return pl.pallas_call(
        _bias_tiles_body,
        grid=(NSA_HEADS,),
        in_specs=[pl.BlockSpec(memory_space=pltpu.SMEM)],
        out_specs=[pl.BlockSpec((1, N_BIAS_TILES) + tile, lambda h: (h, 0, 0, 0)),
                   pl.BlockSpec((1, n_q_blocks) + tile, lambda h: (h, 0, 0, 0))],
        out_shape=[jax.ShapeDtypeStruct((NSA_HEADS, N_BIAS_TILES) + tile, F32),
                   jax.ShapeDtypeStruct((NSA_HEADS, n_q_blocks) + tile, F32)],
        compiler_params=_cparams(("arbitrary",)),
        name="bias_tiles",
    )(rel_bias)


def _nsa_prompt_body(q_ref, gate_ref, sel_ref, win_ref, kc_ref, bt_ref, cb_ref, y_ref, selb_ref, winb_ref,
                     *, n_sel):
    hd = NSA_HEAD_DIM
    qb = Q_BLOCK
    i = pl.program_id(1)
    scale = hd ** -0.5

    @pl.when(i == 0)
    def _():
        selb_ref[...] = sel_ref[0].astype(BF16)
        winb_ref[...] = win_ref[0].astype(BF16)

    q_bf = q_ref[0].astype(BF16)
    gates = jax.nn.sigmoid(gate_ref[0])
    iq = lax.broadcasted_iota(jnp.int32, (qb, LANES), 0)
    ik = lax.broadcasted_iota(jnp.int32, (qb, LANES), 1)
    pos = i * qb + iq
    d_cmp = pos - (ik * CMP_BLOCK + CMP_BLOCK - 1)
    bias_cmp = [cb_ref[hh, 0] for hh in range(NSA_HEADS)]
    vis_cmp = _stack_heads(d_cmp >= 0)
    top_n = min(TOP_N, n_sel)
    sel_tiles = SEL_TILES if selb_ref.shape[0] % (SEL_TILES * LANES) == 0 else 1
    sel_keys = sel_tiles * LANES
    win_tiles = min(WINDOW // LANES + 1, winb_ref.shape[0] // LANES)
    win_keys = win_tiles * LANES
    iq_s = lax.broadcasted_iota(jnp.int32, (qb, sel_keys), 0)
    ik_s = lax.broadcasted_iota(jnp.int32, (qb, sel_keys), 1)
    iq_w = lax.broadcasted_iota(jnp.int32, (qb, win_keys), 0)
    ik_w = lax.broadcasted_iota(jnp.int32, (qb, win_keys), 1)
    groups = [[g * NSA_HPG + h for h in range(NSA_HPG)] for g in range(NSA_KV_HEADS)]

    def key_cols(g):
        return slice(g * hd, (g + 1) * hd), slice(NSA_KV_DIM + g * hd, NSA_KV_DIM + (g + 1) * hd)

    def tile_bias(heads, first_tile, n_tiles):
        cols = []
        for k in range(n_tiles):
            m = jnp.clip(i - (first_tile + k), 0, N_BIAS_TILES - 1)
            cols.append(jnp.concatenate([bt_ref[hh, m] for hh in heads], axis=0))
        return cols[0] if n_tiles == 1 else jnp.concatenate(cols, axis=1)

    qgs, o_cmps, values = [], [], []
    for g, heads in enumerate(groups):
        k_cols, v_cols = key_cols(g)
        qg = jnp.concatenate([q_bf[:, hh * hd:(hh + 1) * hd] for hh in heads], axis=0)
        o_cmp, p_cmp = _cmp_branch(qg, kc_ref[0, :, k_cols].astype(BF16), kc_ref[0, :, v_cols].astype(BF16),
                                   jnp.concatenate([bias_cmp[hh] for hh in heads], axis=0), vis_cmp, scale)
        qgs.append(qg)
        o_cmps.append(o_cmp)
        values.append(_selection_values(p_cmp, qb, pos, n_sel))
    chosen = _topk_mask(jnp.concatenate(values, axis=0), top_n).astype(BF16)
    chosens = [chosen[g * qb:(g + 1) * qb] for g in range(NSA_KV_HEADS)]

    def sel_step(j, carries):
        row0 = pl.multiple_of(j * sel_keys, sel_keys)
        causal = i * qb + iq_s - (j * sel_keys + ik_s) >= 0
        out = []
        for g, heads in enumerate(groups):
            k_cols, v_cols = key_cols(g)
            mask = _stack_heads(_expand_selection(chosens[g], j, sel_keys) & causal)
            out.append(_softmax_update(carries[g], qgs[g], selb_ref[pl.ds(row0, sel_keys), k_cols],
                                       selb_ref[pl.ds(row0, sel_keys), v_cols],
                                       tile_bias(heads, j * sel_tiles, sel_tiles), mask, scale))
        return tuple(out)

    n_steps = (i + sel_tiles) // sel_tiles
    sel_carries = lax.fori_loop(0, n_steps, sel_step, tuple(_softmax_init(NSA_HPG * qb) for _ in groups))

    first = jnp.maximum(i - (win_tiles - 1), 0)
    row0 = pl.multiple_of(first * LANES, LANES)
    d_win = i * qb + iq_w - (first * LANES + ik_w)
    mask_win = _stack_heads((d_win >= 0) & (d_win < WINDOW))
    for g, heads in enumerate(groups):
        k_cols, v_cols = key_cols(g)
        o_sel = _softmax_finish(sel_carries[g])
        o_win = _softmax_finish(_softmax_update(
            _softmax_init(NSA_HPG * qb), qgs[g], winb_ref[pl.ds(row0, win_keys), k_cols],
            winb_ref[pl.ds(row0, win_keys), v_cols], tile_bias(heads, first, win_tiles), mask_win, scale))
        for h, hh in enumerate(heads):
            rows = slice(h * qb, (h + 1) * qb)
            out = (gates[:, hh:hh + 1] * o_cmps[g][rows]
                   + gates[:, NSA_HEADS + hh:NSA_HEADS + hh + 1] * o_sel[rows]
                   + gates[:, 2 * NSA_HEADS + hh:2 * NSA_HEADS + hh + 1] * o_win[rows])
            y_ref[0, :, hh * hd:(hh + 1) * hd] = out.astype(BF16)


def _nsa_prompt(pb3, kc, bias_tiles, cmp_bias):
    b, t, _ = pb3.shape
    qb = Q_BLOCK
    half = 2 * NSA_KV_DIM
    return pl.pallas_call(
        functools.partial(_nsa_prompt_body, n_sel=t // SEL_BLOCK),
        grid=(b, t // qb),
        in_specs=[
            pl.BlockSpec((1, qb, NSA_DIM), lambda s, i: (s, i, PB_Q // NSA_DIM)),
            pl.BlockSpec((1, qb, LANES), lambda s, i: (s, i, PB_GATE // LANES)),
            pl.BlockSpec((1, t, half), lambda s, i: (s, 0, PB_SEL // half)),
            pl.BlockSpec((1, t, half), lambda s, i: (s, 0, PB_WIN // half)),
            pl.BlockSpec((1,) + kc.shape[1:], lambda s, i: (s, 0, 0)),
            pl.BlockSpec(bias_tiles.shape, lambda s, i: (0, 0, 0, 0)),
            pl.BlockSpec((NSA_HEADS, 1, qb, LANES), lambda s, i: (0, i, 0, 0)),
        ],
        out_specs=pl.BlockSpec((1, qb, NSA_DIM), lambda s, i: (s, i, 0)),
        out_shape=jax.ShapeDtypeStruct((b, t, NSA_DIM), BF16),
        scratch_shapes=[pltpu.VMEM((t, half), BF16), pltpu.VMEM((t, half), BF16)],
        compiler_params=_cparams(("parallel", "arbitrary")),
        name="nsa_prompt",
    )(pb3, pb3, pb3, pb3, kc, bias_tiles, cmp_bias)


SEL_TILES = 2
SAMPLE_ROWS = 8
SAMPLE_TILE_KEYS = 1024


def _nsa_sample_body(rb_ref, q_ref, gate_ref, kc_ref, past_ref, new_sel_ref, cwin_ref, new_win_ref, y_ref,
                     *, past_len):
    hd = NSA_HEAD_DIM
    qb = SAMPLE_ROWS
    scale = hd ** -0.5
    n_cmp = kc_ref.shape[1]
    win_rows = cwin_ref.shape[1]
    k_off = past_len - win_rows
    tile_keys = min(SAMPLE_TILE_KEYS, past_len)
    assert past_len % tile_keys == 0

    q_bf = q_ref[0].astype(BF16)
    gates = jax.nn.sigmoid(gate_ref[0])
    iq = lax.broadcasted_iota(jnp.int32, (qb, LANES), 0)
    ik = lax.broadcasted_iota(jnp.int32, (qb, LANES), 1)
    pos_t = past_len + lax.broadcasted_iota(jnp.int32, (qb, tile_keys), 0)
    ik_t = lax.broadcasted_iota(jnp.int32, (qb, tile_keys), 1)
    d_win = (past_len + lax.broadcasted_iota(jnp.int32, (qb, win_rows), 0)
             - (k_off + lax.broadcasted_iota(jnp.int32, (qb, win_rows), 1)))
    tq_c = lax.broadcasted_iota(jnp.int32, (qb, n_cmp), 0)
    n_c = lax.broadcasted_iota(jnp.int32, (qb, n_cmp), 1)
    pos_c = past_len + tq_c
    d_cmp = pos_c - (n_c * CMP_BLOCK + CMP_BLOCK - 1)
    bias_cmp = _bias_for_heads(d_cmp, rb_ref, range(NSA_HEADS))
    vis_cmp = _stack_heads(d_cmp >= 0)
    n_sel_past = past_len // SEL_BLOCK
    picks = min(TOP_N, n_sel_past + 1) - 1

    groups = [[g * NSA_HPG + h for h in range(NSA_HPG)] for g in range(NSA_KV_HEADS)]

    def key_cols(g):
        return slice(g * hd, (g + 1) * hd), slice(NSA_KV_DIM + g * hd, NSA_KV_DIM + (g + 1) * hd)

    qgs, o_cmps, values = [], [], []
    for g, heads in enumerate(groups):
        k_cols, v_cols = key_cols(g)
        qg = jnp.concatenate([q_bf[:, hh * hd:(hh + 1) * hd] for hh in heads], axis=0)
        o_cmp, p_cmp = _cmp_branch(qg, kc_ref[0, :, k_cols].astype(BF16), kc_ref[0, :, v_cols].astype(BF16),
                                   jnp.concatenate([bias_cmp[hh] for hh in heads], axis=0), vis_cmp, scale)
        qgs.append(qg)
        o_cmps.append(o_cmp)
        values.append(_selection_values(p_cmp, qb, pos_c, n_sel_past))
    chosen = _topk_mask(jnp.concatenate(values, axis=0), picks).astype(BF16)
    chosens = [chosen[g * qb:(g + 1) * qb] for g in range(NSA_KV_HEADS)]

    def sel_step(j, carries):
        row0 = pl.multiple_of(j * tile_keys, tile_keys)
        dist = pos_t - (j * tile_keys + ik_t)
        out = []
        for g, heads in enumerate(groups):
            k_cols, v_cols = key_cols(g)
            mask = _stack_heads(_expand_selection(chosens[g], j, tile_keys))
            bias = jnp.concatenate(_bias_for_heads(dist, rb_ref, heads), axis=0)
            kt = past_ref[0, g, pl.ds(row0, tile_keys), :].astype(BF16)
            vt = past_ref[0, NSA_KV_HEADS + g, pl.ds(row0, tile_keys), :].astype(BF16)
            out.append(_softmax_update(carries[g], qgs[g], kt, vt, bias, mask, scale))
        return tuple(out)

    sel_carries = lax.fori_loop(0, past_len // tile_keys, sel_step,
                                tuple(_softmax_init(NSA_HPG * qb) for _ in groups))
    d_new = iq - ik
    mask_new = _stack_heads(d_new >= 0)
    mask_win = _stack_heads((d_win >= 0) & (d_win < WINDOW))
    for g, heads in enumerate(groups):
        k_cols, v_cols = key_cols(g)
        qg, o_cmp = qgs[g], o_cmps[g]
        bias_new = jnp.concatenate(_bias_for_heads(d_new, rb_ref, heads), axis=0)
        o_sel = _softmax_finish(_softmax_update(sel_carries[g], qg, new_sel_ref[0, :, k_cols],
                                                new_sel_ref[0, :, v_cols], bias_new, mask_new, scale))
        bias_win = jnp.concatenate(_bias_for_heads(d_win, rb_ref, heads), axis=0)
        carry = _softmax_update(_softmax_init(NSA_HPG * qb), qg, cwin_ref[0, :, k_cols].astype(BF16),
                                cwin_ref[0, :, v_cols].astype(BF16), bias_win, mask_win, scale)
        o_win = _softmax_finish(_softmax_update(carry, qg, new_win_ref[0, :, k_cols], new_win_ref[0, :, v_cols],
                                                bias_new, mask_new, scale))

        for h, hh in enumerate(heads):
            rows = slice(h * qb, (h + 1) * qb)
            out = (gates[:, hh:hh + 1] * o_cmp[rows]
                   + gates[:, NSA_HEADS + hh:NSA_HEADS + hh + 1] * o_sel[rows]
                   + gates[:, 2 * NSA_HEADS + hh:2 * NSA_HEADS + hh + 1] * o_win[rows])
            y_ref[0, :, hh * hd:(hh + 1) * hd] = out.astype(BF16)


def _nsa_sample(rel_bias, q8, gate8, kc, past_sel, new_sel, cwin, new_win, past_len):
    bs = q8.shape[0]
    blk = lambda a: pl.BlockSpec((1,) + a.shape[1:], lambda s: (s,) + (0,) * (a.ndim - 1))
    return pl.pallas_call(
        functools.partial(_nsa_sample_body, past_len=past_len),
        grid=(bs,),
        in_specs=[pl.BlockSpec(memory_space=pltpu.SMEM), blk(q8), blk(gate8), blk(kc), blk(past_sel),
                  blk(new_sel), blk(cwin), blk(new_win)],
        out_specs=pl.BlockSpec((1, SAMPLE_ROWS, NSA_DIM), lambda s: (s, 0, 0)),
        out_shape=jax.ShapeDtypeStruct((bs, SAMPLE_ROWS, NSA_DIM), BF16),
        compiler_params=_cparams(("parallel",)),
        name="nsa_sample",
    )(rel_bias, q8, gate8, kc, past_sel, new_sel, cwin, new_win)


def _merge_body(x_ref, yrw_ref, g_ref, ynsa_ref, gates_ref, wrw_ref, wnsa_ref, wout_ref, post_ref, o_ref):
    tm, d = x_ref.shape
    y_rw = jnp.concatenate([yrw_ref[:, pair, :] for pair in range(RW_DIM // LANES)], axis=1)
    y_rw = (y_rw * g_ref[...]).astype(BF16)
    merged = (jax.nn.sigmoid(gates_ref[:, :d]) * _dot(y_rw, wrw_ref[...])
              + jax.nn.sigmoid(gates_ref[:, d:]) * _dot(ynsa_ref[...], wnsa_ref[...]))
    o_ref[...] = x_ref[...] + _rms(_dot(merged.astype(BF16), wout_ref[...]), post_ref[...])


def _merge(x, y_rw, g_rw, y_nsa, gates, w_rw, w_nsa, w_out, post_g, tm):
    m, d = x.shape
    rows = lambda n: pl.BlockSpec((tm, n), lambda i: (i, 0))
    held = lambda a: pl.BlockSpec(a.shape, lambda i: (0, 0), pipeline_mode=pl.Buffered(1))
    return pl.pallas_call(
        _merge_body,
        grid=(m // tm,),
        in_specs=[rows(d), pl.BlockSpec((tm,) + y_rw.shape[1:], lambda i: (i, 0, 0)),
                  rows(RW_DIM), rows(NSA_DIM), rows(2 * d),
                  held(w_rw), held(w_nsa), held(w_out), pl.BlockSpec((1, d), lambda i: (0, 0))],
        out_specs=rows(d),
        out_shape=jax.ShapeDtypeStruct((m, d), F32),
        compiler_params=_cparams(("parallel",)),
        name="merge",
    )(x, y_rw, g_rw, y_nsa, gates, w_rw, w_nsa, w_out, post_g.reshape(1, d))


def _row_tile(m, want):
    return want if m % want == 0 else m


def _rwkv_mix(p3, shift0, s0, w):
    b, t, c = p3.shape
    n = RW_HEAD_DIM
    tt = 256 if t % 256 == 0 else t
    prev = jnp.concatenate([shift0[:, None, :], p3[:, tt - 1:t - 1:tt]], axis=1).reshape(b, t // tt, 1, c)
    pre, g_rw = _rwkv_pre(p3, prev, w["mu"], w["w0"], w["ww"], w["a0"], w["wa"], w["g2"], w["k_k"], w["k_a"], tt)
    seqs = min(b, LANES // RW_HEADS)
    groups = b // seqs
    pairs = RW_HEADS // 2
    s0_t = s0.reshape(groups, seqs, pairs, 2, n, n).transpose(5, 4, 0, 3, 1, 2).reshape(n, n, b * RW_HEADS)
    per_lane = lambda v: jnp.broadcast_to(v.reshape(pairs, 2, n).transpose(2, 1, 0)[:, :, None, :],
                                          (n, 2, seqs, pairs)).reshape(n, seqs * RW_HEADS)
    tc = 32 if t % 32 == 0 else t
    y_t, s_t = _rwkv_scan(pre, s0_t, per_lane(w["lnx_w"]), per_lane(w["lnx_b"]), per_lane(w["r_k"]), tc, seqs)
    s_fin = s_t.reshape(n, n, groups, 2, seqs, pairs).transpose(2, 4, 5, 3, 1, 0).reshape(b, RW_HEADS, n, n)
    y_tok = y_t.reshape(t, b, pairs, LANES).transpose(1, 0, 2, 3).reshape(b * t, pairs, LANES)
    return y_tok, g_rw.reshape(b * t, RW_DIM), s_fin


def _layer(x, w, rel_bias, shift0, s0, past):
    b, t, d = x.shape
    m = b * t
    tm = _row_tile(m, 512)
    x1 = _ffn(x.reshape(m, d), w["f1_pre"], w["f1_post"], w["f1_w1"], w["f1_w3"], w["f1_w2"], tm, 512)
    p_a = _norm_matmul(x1, w["mix_pre"], w["w_in_a"], tm, "proj_rwkv")
    p_b = _norm_matmul(x1, w["mix_pre"], w["w_in_b"], tm, "proj_nsa")
    p_c = _norm_matmul(x1, w["mix_pre"], w["w_in_c"], tm, "proj_gates")

    p3 = p_a.reshape(b, t, RW_PROJ)
    y_rw, g_rw, s_fin = _rwkv_mix(p3, shift0, s0, w)

    pb3 = p_b.reshape(b, t, PB_COLS)
    kv_new = pb3[:, :, PB_KV:PB_WIN]
    win_new = pb3[:, :, PB_WIN:PB_GATE]
    row_w = 4 * NSA_KV_DIM
    half = 2 * NSA_KV_DIM
    if past is None:
        blocks = p_b.reshape(m // CMP_BLOCK, CMP_BLOCK, PB_COLS)
        kc = _cmp_mlp(blocks, w["cmp_pe"], w["cmp_w1"], w["cmp_w2"], min(64, blocks.shape[0]), PB_KV // half)
        n_cmp = t // CMP_BLOCK
        kc = jnp.pad(kc.reshape(b, n_cmp, 2 * NSA_KV_DIM), ((0, 0), (0, -n_cmp % LANES), (0, 0)))
        y_nsa = _nsa_prompt(pb3, kc, *_bias_tiles(rel_bias, t // Q_BLOCK)).reshape(m, NSA_DIM)
        win_out = win_new[:, t - min(WINDOW, t):]
    else:
        page_table, pool, cache_win = past
        past_len = page_table.shape[1] * pool.shape[1]
        slabs = row_w // NSA_HEAD_DIM
        pool4 = pool.reshape(pool.shape[0], pool.shape[1], slabs, NSA_HEAD_DIM)
        kc = _paged_kv(page_table, pool4, w["cmp_pe_tiles"], w["cmp_w1cat"], w["cmp_w2"])
        past_sel = _gather_selected(page_table, pool4)
        n_cmp = past_len // CMP_BLOCK
        kc = jnp.pad(kc, ((0, 0), (0, -n_cmp % LANES), (0, 0)))
        pad_q = lambda a, rows: jnp.pad(a, ((0, 0), (0, rows - t), (0, 0)))
        cwin = cache_win.reshape(b, cache_win.shape[1], half)
        y8 = _nsa_sample(rel_bias, pad_q(pb3[:, :, :NSA_DIM], SAMPLE_ROWS), pad_q(pb3[:, :, PB_GATE:], SAMPLE_ROWS),
                         kc, past_sel, pad_q(pb3[:, :, PB_SEL:PB_WIN].astype(BF16), LANES), cwin,
                         pad_q(win_new.astype(BF16), LANES), past_len)
        y_nsa = y8[:, :t].reshape(m, NSA_DIM)
        win_all = jnp.concatenate([cwin, win_new], axis=1)
        win_out = win_all[:, win_all.shape[1] - min(WINDOW, win_all.shape[1]):]

    x2 = _merge(x1, y_rw, g_rw, y_nsa, p_c, w["w_br_rw"], w["w_br_nsa"], w["w_out"], w["mix_post"],
                _row_tile(m, 256))
    y = _ffn(x2, w["f2_pre"], w["f2_post"], w["f2_w1"], w["f2_w3"], w["f2_w2"], tm, 512)
    g, hd = NSA_KV_HEADS, NSA_HEAD_DIM
    return (y.reshape(b, t, d), kv_new.reshape(b, t, 4, g, hd), win_out.reshape(b, -1, 2, g, hd), s_fin, p3[:, -1])


def _prepare_weights(l, ffn1_pre_g, ffn1_post_g, ffn1_w1, ffn1_w3, ffn1_w2, mix_pre_g, mix_post_g, w_in,
                     rw_mu, rw_w0, rw_w2, rw_a0, rw_a2, rw_g2, rw_k_k, rw_k_a, rw_r_k, rw_lnx_w, rw_lnx_b,
                     cmp_pe, cmp_w1, cmp_w2, w_br_rw, w_br_nsa, w_out,
                     ffn2_pre_g, ffn2_post_g, ffn2_w1, ffn2_w3, ffn2_w2):
    d = w_in.shape[1]
    wi = w_in[l]
    c_q = RW_PROJ
    c_kv = c_q + NSA_DIM
    c_gate = c_kv + 6 * NSA_KV_DIM
    c_grw = c_gate + 3 * NSA_HEADS
    w_in_b = jnp.concatenate([wi[:, c_q:c_gate], jnp.pad(wi[:, c_gate:c_grw], ((0, 0), (0, LANES - 3 * NSA_HEADS)))],
                             axis=1)
    zeros_w = jnp.zeros((RW_LORA_A, RW_DIM), F32)
    zeros_a = jnp.zeros((RW_LORA_W, RW_DIM), F32)
    pe = cmp_pe[l]
    pe_rows = jnp.broadcast_to(pe[:, :, None, :], (CMP_BLOCK, 2, NSA_KV_HEADS, NSA_HEAD_DIM))
    return {
        "f1_pre": ffn1_pre_g[l], "f1_post": ffn1_post_g[l],
        "f1_w1": ffn1_w1[l].astype(BF16), "f1_w3": ffn1_w3[l].astype(BF16), "f1_w2": ffn1_w2[l].astype(BF16),
        "mix_pre": mix_pre_g[l], "mix_post": mix_post_g[l],
        "w_in_a": wi[:, :RW_PROJ].astype(BF16), "w_in_b": w_in_b.astype(BF16), "w_in_c": wi[:, c_grw:].astype(BF16),
        "mu": rw_mu[l], "w0": rw_w0[l], "a0": rw_a0[l],
        "ww": jnp.concatenate([rw_w2[l], zeros_w], axis=0).astype(BF16),
        "wa": jnp.concatenate([zeros_a, rw_a2[l]], axis=0).astype(BF16),
        "g2": rw_g2[l].astype(BF16), "k_k": rw_k_k[l], "k_a": rw_k_a[l],
        "r_k": rw_r_k[l].reshape(-1), "lnx_w": rw_lnx_w[l], "lnx_b": rw_lnx_b[l],
        "cmp_pe": pe_rows.reshape(CMP_BLOCK, 2 * NSA_KV_DIM),
        "cmp_pe_tiles": jnp.pad(pe_rows.reshape(CMP_BLOCK, 2 * NSA_KV_HEADS, NSA_HEAD_DIM),
                                ((0, 0), (0, 2 * NSA_KV_HEADS), (0, 0))),
        "cmp_w1": cmp_w1[l].astype(BF16), "cmp_w2": cmp_w2[l].astype(BF16),
        "cmp_w1cat": jnp.concatenate([cmp_w1[l, 0], cmp_w1[l, 1]], axis=1).astype(BF16),
        "w_br_rw": w_br_rw[l].astype(BF16), "w_br_nsa": w_br_nsa[l].astype(BF16), "w_out": w_out[l].astype(BF16),
        "f2_pre": ffn2_pre_g[l], "f2_post": ffn2_post_g[l],
        "f2_w1": ffn2_w1[l].astype(BF16), "f2_w3": ffn2_w3[l].astype(BF16), "f2_w2": ffn2_w2[l].astype(BF16),
    }


def kernel(x_prompt, x_sample, cache_kv, cache_win, state_rwkv, state_shift, page_table,
           ffn1_pre_g, ffn1_post_g, ffn1_w1, ffn1_w3, ffn1_w2, mix_pre_g, mix_post_g, w_in,
           rw_mu, rw_w0, rw_w2, rw_a0, rw_a2, rw_g2, rw_k_k, rw_k_a, rw_r_k, rw_lnx_w, rw_lnx_b,
           cmp_pe, cmp_w1, cmp_w2, w_br_rw, w_br_nsa, w_out,
           ffn2_pre_g, ffn2_post_g, ffn2_w1, ffn2_w3, ffn2_w2, rel_bias):
    depth = w_in.shape[0]
    b_p = x_prompt.shape[0]
    y_p, y_s = x_prompt, x_sample
    outs = [[] for _ in range(8)]
    for l in range(depth):
        w = _prepare_weights(l, ffn1_pre_g, ffn1_post_g, ffn1_w1, ffn1_w3, ffn1_w2, mix_pre_g, mix_post_g, w_in,
                             rw_mu, rw_w0, rw_w2, rw_a0, rw_a2, rw_g2, rw_k_k, rw_k_a, rw_r_k, rw_lnx_w, rw_lnx_b,
                             cmp_pe, cmp_w1, cmp_w2, w_br_rw, w_br_nsa, w_out,
                             ffn2_pre_g, ffn2_post_g, ffn2_w1, ffn2_w3, ffn2_w2)
        y_p, kv_p, win_p, rw_p, sh_p = _layer(
            y_p, w, rel_bias, jnp.zeros((b_p, RW_PROJ), F32),
            jnp.zeros((b_p, RW_HEADS, RW_HEAD_DIM, RW_HEAD_DIM), F32), None)
        y_s, kv_s, win_s, rw_s, sh_s = _layer(
            y_s, w, rel_bias, state_shift[l], state_rwkv[l], (page_table, cache_kv[l], cache_win[l]))
        for acc, v in zip(outs, (kv_p, kv_s, win_p, win_s, rw_p, rw_s, sh_p, sh_s)):
            acc.append(v)
    return (y_p, y_s) + tuple(jnp.stack(o) for o in outs)
```

```python
import functools
import math

import jax
import jax.numpy as jnp
from jax import lax
from jax.experimental import pallas as pl
from jax.experimental.pallas import tpu as pltpu
from jax.experimental.pallas import tpu_sc as plsc

F32 = jnp.float32
BF16 = jnp.bfloat16

RMS_EPS = 1e-6
RW_HEADS = 16
RW_HEAD_DIM = 64
RW_DIM = RW_HEADS * RW_HEAD_DIM
RW_LORA_W = 64
RW_LORA_A = 64
RW_LORA_G = 128
RW_PROJ = 3 * RW_DIM + RW_LORA_W + RW_LORA_A + RW_LORA_G
RW_LN_EPS = 64e-5
NSA_HEADS = 8
NSA_KV_HEADS = 2
NSA_HPG = NSA_HEADS // NSA_KV_HEADS
NSA_HEAD_DIM = 128
NSA_DIM = NSA_HEADS * NSA_HEAD_DIM
NSA_KV_DIM = NSA_KV_HEADS * NSA_HEAD_DIM
CMP_BLOCK = 32
CMP_HID = 256
SEL_BLOCK = 64
SEL_SHIFT = SEL_BLOCK.bit_length() - 1
TOP_N = 16
WINDOW = 512
N_BUCKETS = 32
REL_MAX_EXACT = 16
REL_MAX_DIST = 1024
Q_BLOCK = 128
NEG_BIG = -1e30
FORCE_BONUS = 1e4
NEVER = -3e38
LOG2_E = math.log2(math.e)
LANES = 128
SUBLANES = 8
SCAN_FIELDS = 6
VMEM_LIMIT = 56 * 1024 * 1024
SCAN_UNROLL = 8
UPDATE_UNROLL = 16

PB_Q = 0
PB_KV = NSA_DIM
PB_SEL = PB_KV + 2 * NSA_KV_DIM
PB_WIN = PB_KV + 4 * NSA_KV_DIM
PB_GATE = PB_WIN + 2 * NSA_KV_DIM
PB_COLS = PB_GATE + LANES


def _bucket_thresholds():
    thr = list(range(1, REL_MAX_EXACT + 1))
    n_log = N_BUCKETS - REL_MAX_EXACT
    ratio = REL_MAX_DIST // REL_MAX_EXACT
    n = REL_MAX_EXACT
    for k in range(1, n_log):
        while n ** n_log < REL_MAX_EXACT ** n_log * ratio ** k:
            n += 1
        thr.append(n)
    return thr


BUCKET_THR = _bucket_thresholds()
FAR_DIST = BUCKET_THR[-1]


def _cparams(sem):
    return pltpu.CompilerParams(dimension_semantics=sem, vmem_limit_bytes=VMEM_LIMIT)


def _rms(x, g):
    ms = jnp.mean(x * x, axis=-1, keepdims=True)
    return x * lax.rsqrt(ms + RMS_EPS) * g


def _dot(a, b):
    return jnp.dot(a, b, preferred_element_type=F32)


def _dot_nt(a, b):
    return lax.dot_general(a, b, (((1,), (1,)), ((), ())), preferred_element_type=F32)


def _bias_for_heads(dist, rb_ref, heads):
    n = jnp.maximum(dist, 0)
    reach = [n >= t for t in BUCKET_THR]
    out = []
    for h in heads:
        val = jnp.full(dist.shape, rb_ref[0, h], F32)
        for b, m in enumerate(reach):
            val = jnp.where(m, rb_ref[b + 1, h], val)
        out.append(val)
    return out


def _softmax_update(carry, q, kt, vt, bias, mask, scale):
    m_run, l_run, acc = carry
    s = _dot_nt(q, kt) * scale + bias
    s = jnp.where(mask, s, NEG_BIG)
    m_new = jnp.maximum(m_run, jnp.max(s, axis=-1, keepdims=True))
    alpha = jnp.exp(m_run - m_new)
    e = jnp.where(mask, jnp.exp(s - m_new), 0.0)
    l_new = alpha * l_run + jnp.sum(e, axis=-1, keepdims=True)
    acc = alpha * acc + _dot(e.astype(BF16), vt)
    return m_new, l_new, acc


def _softmax_init(rows):
    return (jnp.full((rows, 1), NEG_BIG, F32), jnp.zeros((rows, 1), F32),
            jnp.zeros((rows, NSA_HEAD_DIM), F32))


def _softmax_finish(carry):
    _, l_run, acc = carry
    return acc / jnp.maximum(l_run, 1e-30)


def _softmax_update_t(carry, q, kt, v_t, bias_t, mask_t):
    m_run, l_run, acc = carry
    s = _dot_nt(kt, q) + bias_t
    s = jnp.where(mask_t, s, NEG_BIG)
    m_new = jnp.maximum(m_run, jnp.max(s, axis=0, keepdims=True))
    alpha = jnp.exp2(m_run - m_new)
    e = jnp.where(mask_t, jnp.exp2(s - m_new), 0.0)
    l_new = alpha * l_run + jnp.sum(e, axis=0, keepdims=True)
    acc = alpha * acc + _dot(v_t, e.astype(BF16))
    return m_new, l_new, acc


def _softmax_init_t(rows):
    return (jnp.full((1, rows), NEG_BIG, F32), jnp.zeros((1, rows), F32),
            jnp.zeros((NSA_HEAD_DIM, rows), F32))


def _stack_heads(x):
    return jnp.concatenate([x] * NSA_HPG, axis=0)


def _stack_heads_t(x):
    return jnp.concatenate([x] * NSA_HPG, axis=1)


def _topk_mask(val, k):
    lane = lax.broadcasted_iota(jnp.int32, val.shape, 1)
    chosen = jnp.zeros(val.shape, F32)
    for _ in range(k):
        hit = lane == jnp.argmax(val, axis=-1, keepdims=True)
        chosen = jnp.where(hit, 1.0, chosen)
        val = jnp.where(hit, NEVER, val)
    return chosen


def _pair_sum(x):
    parts = []
    for c in range(x.shape[1] // LANES):
        blk = x[:, c * LANES:(c + 1) * LANES]
        parts.append(blk + pltpu.roll(blk, LANES - 1, 1))
    return parts[0] if len(parts) == 1 else jnp.concatenate(parts, axis=1)


def _cmp_branch(qg, kc_k, kc_v, bias, visible, scale):
    s = _dot_nt(qg, kc_k) * scale + bias
    s = jnp.where(visible, s, NEG_BIG)
    e = jnp.where(visible, jnp.exp(s - jnp.max(s, axis=-1, keepdims=True)), 0.0)
    p = e / jnp.maximum(jnp.sum(e, axis=-1, keepdims=True), 1e-30)
    return _dot(p.astype(BF16), kc_v), p


def _selection_values(p, qb, pos, n_sel):
    imp = p[0:qb]
    for h in range(1, NSA_HPG):
        imp = imp + p[h * qb:(h + 1) * qb]
    return _selection_values_of(imp, pos, n_sel)


def _selection_values_of(imp, pos, n_sel):
    imp = _pair_sum(imp)
    lane = lax.broadcasted_iota(jnp.int32, imp.shape, 1)
    sel_id = lane >> 1
    cur = pos >> SEL_SHIFT
    forced = (sel_id == 0) | (sel_id == cur) | (sel_id == cur - 1)
    val = jnp.where(forced, imp + FORCE_BONUS, imp)
    val = jnp.where(sel_id * SEL_BLOCK <= pos, val, NEG_BIG)
    return jnp.where(((lane & 1) == 0) & (sel_id < n_sel), val, NEVER)


def _expand_selection(chosen_bf16, tile, tile_keys=LANES):
    width = chosen_bf16.shape[1]
    n_idx = lax.broadcasted_iota(jnp.int32, (width, tile_keys), 0)
    k_idx = lax.broadcasted_iota(jnp.int32, (width, tile_keys), 1)
    target = 2 * ((tile * tile_keys + k_idx) >> SEL_SHIFT)
    expand = jnp.where(n_idx == target, 1.0, 0.0).astype(BF16)
    return _dot(chosen_bf16, expand) > 0.5


def _ffn_body(x_ref, pre_ref, post_ref, w1_ref, w3_ref, w2_ref, o_ref, h_ref, acc_ref):
    j = pl.program_id(1)

    @pl.when(j == 0)
    def _():
        h_ref[...] = _rms(x_ref[...], pre_ref[...]).astype(BF16)
        acc_ref[...] = jnp.zeros_like(acc_ref)

    h = h_ref[...]
    a = _dot(h, w1_ref[...])
    b = _dot(h, w3_ref[...])
    u = (a * jax.nn.sigmoid(a)) * b
    acc_ref[...] += _dot(u.astype(BF16), w2_ref[...])

    @pl.when(j == pl.num_programs(1) - 1)
    def _():
        o_ref[...] = x_ref[...] + 0.5 * _rms(acc_ref[...], post_ref[...])


def _ffn(x, pre_g, post_g, w1, w3, w2, tm, tf):
    m, d = x.shape
    f = w1.shape[1]
    return pl.pallas_call(
        _ffn_body,
        grid=(m // tm, f // tf),
        in_specs=[
            pl.BlockSpec((tm, d), lambda i, j: (i, 0)),
            pl.BlockSpec((1, d), lambda i, j: (0, 0)),
            pl.BlockSpec((1, d), lambda i, j: (0, 0)),
            pl.BlockSpec((d, tf), lambda i, j: (0, j)),
            pl.BlockSpec((d, tf), lambda i, j: (0, j)),
            pl.BlockSpec((tf, d), lambda i, j: (j, 0)),
        ],
        out_specs=pl.BlockSpec((tm, d), lambda i, j: (i, 0)),
        out_shape=jax.ShapeDtypeStruct((m, d), F32),
        scratch_shapes=[pltpu.VMEM((tm, d), BF16), pltpu.VMEM((tm, d), F32)],
        compiler_params=_cparams(("parallel", "arbitrary")),
        name="ffn",
    )(x, pre_g.reshape(1, d), post_g.reshape(1, d), w1, w3, w2)


def _norm_matmul_body(x_ref, g_ref, w_ref, o_ref, h_ref):
    @pl.when(pl.program_id(1) == 0)
    def _():
        h_ref[...] = _rms(x_ref[...], g_ref[...]).astype(BF16)

    o_ref[...] = _dot(h_ref[...], w_ref[...])


def _norm_matmul(x, g, w, tm, name):
    m, d = x.shape
    n = w.shape[1]
    return pl.pallas_call(
        _norm_matmul_body,
        grid=(m // tm, 1),
        in_specs=[
            pl.BlockSpec((tm, d), lambda i, j: (i, 0)),
            pl.BlockSpec((1, d), lambda i, j: (0, 0)),
            pl.BlockSpec((d, n), lambda i, j: (0, 0), pipeline_mode=pl.Buffered(1)),
        ],
        out_specs=pl.BlockSpec((tm, n), lambda i, j: (i, 0)),
        out_shape=jax.ShapeDtypeStruct((m, n), F32),
        scratch_shapes=[pltpu.VMEM((tm, d), BF16)],
        compiler_params=_cparams(("parallel", "arbitrary")),
        name=name,
    )(x, g.reshape(1, d), w)


def _rwkv_pre_body(p_ref, prev_ref, mu_ref, w0_ref, ww_ref, a0_ref, wa_ref, g2_ref, kk_ref, ka_ref,
                   pre_ref, g_ref, buf_ref):
    tt = p_ref.shape[1]
    p = p_ref[0]
    buf_ref[8:8 + tt, :] = p
    buf_ref[7:8, :] = prev_ref[0, 0]
    p_prev = buf_ref[7:7 + tt, :]
    xs = p + (p_prev - p) * mu_ref[...]
    d = RW_DIM
    r = xs[:, 0:d]
    k = xs[:, d:2 * d]
    v = xs[:, 2 * d:3 * d]
    lora_in = xs[:, 3 * d:3 * d + RW_LORA_W + RW_LORA_A]
    gd = xs[:, 3 * d + RW_LORA_W + RW_LORA_A:]
    z = -(w0_ref[...] + _dot(jnp.tanh(lora_in).astype(BF16), ww_ref[...]))
    softplus = jnp.maximum(z, 0.0) + jnp.log(1.0 + jnp.exp(-jnp.abs(z)))
    decay = jnp.exp(-jnp.exp(-softplus - 0.5))
    a = jax.nn.sigmoid(a0_ref[...] + _dot(lora_in.astype(BF16), wa_ref[...]))
    g_ref[0] = _dot(jax.nn.sigmoid(gd).astype(BF16), g2_ref[...])

    n = RW_HEAD_DIM
    pairs = d // LANES
    low_half = lax.broadcasted_iota(jnp.int32, (tt, LANES), 1) < n

    def put(fp, xa, xb):
        for pair in range(pairs):
            col_a = xa[:, pair * LANES:(pair + 1) * LANES]
            col_b = xb[:, pair * LANES:(pair + 1) * LANES]
            pieces = (jnp.where(low_half, col_a, pltpu.roll(col_b, n, 1)),
                      jnp.where(low_half, pltpu.roll(col_a, n, 1), col_b))
            for parity, piece in enumerate(pieces):
                pre_ref[:, fp, parity, pair, :] = piece

    put(0, r, k * (1.0 + (a - 1.0) * ka_ref[...]))
    put(1, v, decay)
    put(2, k * kk_ref[...], a)


def _rwkv_pre(p3, prev, mu, w0, ww, a0, wa, g2, k_k, k_a, tt):
    b, t, c = p3.shape
    d = RW_DIM
    pairs = RW_HEADS // 2
    row = lambda n: pl.BlockSpec((1, n), lambda i, j: (0, 0))
    full = lambda s: pl.BlockSpec(s, lambda i, j: (0, 0))
    return pl.pallas_call(
        _rwkv_pre_body,
        grid=(b, t // tt),
        in_specs=[
            pl.BlockSpec((1, tt, c), lambda i, j: (i, j, 0)),
            pl.BlockSpec((1, 1, 1, c), lambda i, j: (i, j, 0, 0)),
            row(c), row(d), full(ww.shape), row(d), full(wa.shape), full(g2.shape), row(d), row(d),
        ],
        out_specs=[
            pl.BlockSpec((tt, SCAN_FIELDS // 2, 2, pairs, LANES), lambda i, j: (j, 0, 0, i, 0)),
            pl.BlockSpec((1, tt, d), lambda i, j: (i, j, 0)),
        ],
        out_shape=[jax.ShapeDtypeStruct((t, SCAN_FIELDS // 2, 2, b * pairs, LANES), F32),
                   jax.ShapeDtypeStruct((b, t, d), F32)],
        scratch_shapes=[pltpu.VMEM((tt + 8, c), F32)],
        compiler_params=_cparams(("parallel", "arbitrary")),
        name="rwkv_pre",
    )(p3, prev, mu.reshape(1, c), w0.reshape(1, d), ww, a0.reshape(1, d), wa, g2,
      k_k.reshape(1, d), k_a.reshape(1, d))


def _rwkv_scan_body(in_ref, s0_ref, lnw_ref, lnb_ref, rk_ref, y_ref, s_ref, xa_ref, xb_ref, kk_ref, b_ref,
                    out_ref):
    n = RW_HEAD_DIM
    steps = in_ref.shape[0]
    n_fields = in_ref.shape[1]
    half = in_ref.shape[3]
    lanes = 2 * half
    fields_per_trip = -(-n_fields // (n // UPDATE_UNROLL))

    def load_field(t, dst_ref, fp):
        tile = in_ref[t, fp].reshape(lanes, 2 * n)
        dst_ref[pl.ds(pl.multiple_of(fp * 2 * n, 2 * n), 2 * n), :] = tile.T

    def normalise_key(x_ref):
        kk_raw = x_ref[4 * n:5 * n, :]
        norm = jnp.sqrt(jnp.sum(kk_raw * kk_raw, axis=0, keepdims=True))
        kk = kk_raw / jnp.maximum(norm, 1e-12)
        kk_ref[...] = kk
        b_ref[...] = kk * x_ref[5 * n:6 * n, :]

    def one_step(t, x_ref, next_ref):
        t_next = jnp.minimum(t + 1, steps - 1)

        def sk_rows(jb, acc):
            for u in range(SCAN_UNROLL):
                j = jb * SCAN_UNROLL + u
                acc = acc + s_ref[j] * kk_ref[pl.ds(j, 1), :]
            return acc

        sk = lax.fori_loop(0, n // SCAN_UNROLL, sk_rows, jnp.zeros((n, lanes), F32))
        v = x_ref[2 * n:3 * n, :]

        def update_rows(jb, acc):
            for f in range(fields_per_trip):
                load_field(t_next, next_ref, jnp.minimum(jb * fields_per_trip + f, n_fields - 1))
            emit_output(jnp.maximum(t - 1, 0))
            for u in range(UPDATE_UNROLL):
                j = jb * UPDATE_UNROLL + u
                r_j = x_ref[pl.ds(j, 1), :]
                k_j = x_ref[pl.ds(n + j, 1), :]
                w_j = x_ref[pl.ds(3 * n + j, 1), :]
                s_j = s_ref[j] * w_j - sk * b_ref[pl.ds(j, 1), :] + v * k_j
                s_ref[j] = s_j
                acc = acc + s_j * r_j
            return acc

        y = lax.fori_loop(0, n // UPDATE_UNROLL, update_rows, jnp.zeros((n, lanes), F32))
        normalise_key(next_ref)
        mean = jnp.sum(y, axis=0, keepdims=True) * (1.0 / n)
        yc = y - mean
        var = jnp.sum(yc * yc, axis=0, keepdims=True) * (1.0 / n)
        y_norm = yc * lax.rsqrt(var + RW_LN_EPS) * lnw_ref[...] + lnb_ref[...]
        rkr = x_ref[0:n, :] * x_ref[n:2 * n, :] * rk_ref[...]
        out_ref[...] = y_norm + jnp.sum(rkr, axis=0, keepdims=True) * v

    def emit_output(t):
        out = out_ref[...]
        y_ref[t] = jnp.concatenate([out[:, :half], out[:, half:]], axis=0).T

    @pl.when(pl.program_id(1) == 0)
    def _():
        s_ref[...] = s0_ref[...]

    out_ref[...] = jnp.zeros_like(out_ref)
    for c in range(n_fields):
        load_field(0, xa_ref, c)
    normalise_key(xa_ref)

    def step_pair(tp, carry):
        one_step(2 * tp, xa_ref, xb_ref)
        one_step(2 * tp + 1, xb_ref, xa_ref)
        return carry

    lax.fori_loop(0, steps // 2, step_pair, 0)
    emit_output(steps - 1)


def _rwkv_scan(pre5, s0, lnw, lnb, rk, tc, seqs):
    t, field_pairs, _, rows, _ = pre5.shape
    n = RW_HEAD_DIM
    pairs = RW_HEADS // 2
    x_rows = field_pairs * 2 * n
    lb = seqs * RW_HEADS
    assert tc % 2 == 0 and t % tc == 0
    tab = pl.BlockSpec((n, lb), lambda l, i: (0, 0))
    return pl.pallas_call(
        _rwkv_scan_body,
        grid=(rows // (seqs * pairs), t // tc),
        in_specs=[
            pl.BlockSpec((tc, field_pairs, 2, seqs * pairs, LANES), lambda l, i: (i, 0, 0, l, 0)),
            pl.BlockSpec((n, n, lb), lambda l, i: (0, 0, l)),
            tab, tab, tab,
        ],
        out_specs=[
            pl.BlockSpec((tc, seqs * pairs, LANES), lambda l, i: (i, l, 0)),
            pl.BlockSpec((n, n, lb), lambda l, i: (0, 0, l)),
        ],
        out_shape=[jax.ShapeDtypeStruct((t, rows, LANES), F32),
                   jax.ShapeDtypeStruct((n, n, rows * 2), F32)],
        scratch_shapes=[pltpu.VMEM((x_rows, lb), F32), pltpu.VMEM((x_rows, lb), F32),
                        pltpu.VMEM((n, lb), F32), pltpu.VMEM((n, lb), F32), pltpu.VMEM((n, lb), F32)],
        compiler_params=_cparams(("parallel", "arbitrary")),
        name="rwkv_scan",
    )(pre5, s0, lnw, lnb, rk)


GATHER_PAGES = 8


GATHER_WINDOW = 128


def _gather_rows(rows, idx):
    n = idx.shape[1]
    width = rows.shape[1]
    mesh = plsc.VectorSubcoreMesh(core_axis_name="core", subcore_axis_name="subcore")

    @pl.kernel(out_type=jax.ShapeDtypeStruct((n, width), rows.dtype), mesh=mesh)
    def gather_kernel(x_hbm, i_hbm, o_hbm):
        def body(i_vmem, o_vmem):
            pltpu.sync_copy(x_hbm.at[i_vmem.at[0]], o_vmem)

        pltpu.emit_pipeline(
            body,
            grid=(n // GATHER_WINDOW,),
            in_specs=[pl.BlockSpec((1, GATHER_WINDOW), index_map=lambda i: (0, i))],
            out_specs=[pl.BlockSpec((GATHER_WINDOW, width), index_map=lambda i: (i, 0))],
            core_axis_name=("core", "subcore"),
            dimension_semantics=(pltpu.PARALLEL,),
        )(i_hbm, o_hbm)

    return gather_kernel(rows, idx)


def _gather_selected(page_table, pool):
    bs, n_pages = page_table.shape
    n_pool, page, slabs, hd = pool.shape
    past_len = n_pages * page
    tok = jnp.arange(past_len, dtype=jnp.int32)
    row0 = (page_table[:, tok // page] * page + tok % page) * slabs
    idx = row0[:, None, :] + (slabs // 2 + jnp.arange(slabs // 2, dtype=jnp.int32))[None, :, None]
    out = _gather_rows(pool.reshape(n_pool * page * slabs, hd), idx.reshape(1, -1))
    return out.reshape(bs, slabs // 2, past_len, hd)


def _paged_kv_body(pt_ref, *refs):
    n_pages = len(refs) - 5
    page_refs = refs[:n_pages]
    pe_ref, w1_ref, w2_ref, kc_ref, out_ref = refs[n_pages:]
    _, rows, slabs, hd = page_refs[0].shape
    per_page = rows // CMP_BLOCK
    cols = []
    for r in range(CMP_BLOCK):
        tiles = [x_ref[0, nl * CMP_BLOCK + r] + pe_ref[r] for x_ref in page_refs for nl in range(per_page)]
        cols.append(jnp.concatenate(tiles, axis=0).astype(BF16))
    hid = jax.nn.gelu(_dot(jnp.concatenate(cols, axis=1), w1_ref[...]), approximate=True)
    out_ref[0] = _dot(hid[:, :CMP_HID].astype(BF16), w2_ref[0])
    out_ref[1] = _dot(hid[:, CMP_HID:].astype(BF16), w2_ref[1])
    n_blocks = n_pages * per_page
    for kg in range(slabs // 2):
        kc_ref[0, :, kg * hd:(kg + 1) * hd] = out_ref[kg // NSA_KV_HEADS, pl.ds(kg, n_blocks, stride=slabs), :]


def _paged_kv(page_table, pool, pe_tiles, w1cat, w2):
    bs, n_pages = page_table.shape
    _, page, slabs, hd = pool.shape
    half = slabs * hd // 2
    per_step = math.gcd(GATHER_PAGES, n_pages)
    blocks_step = per_step * page // CMP_BLOCK

    def page_spec(k):
        return pl.BlockSpec((1, page, slabs, hd), lambda b, p, pt: (pt[b, p * per_step + k], 0, 0, 0))

    held = lambda a: pl.BlockSpec(a.shape, lambda b, p, pt: (0,) * a.ndim)
    return pl.pallas_call(
        _paged_kv_body,
        grid_spec=pltpu.PrefetchScalarGridSpec(
            num_scalar_prefetch=1,
            grid=(bs, n_pages // per_step),
            in_specs=[page_spec(k) for k in range(per_step)] + [held(pe_tiles), held(w1cat), held(w2)],
            out_specs=pl.BlockSpec((1, blocks_step, half), lambda b, p, pt: (b, p, 0)),
            scratch_shapes=[pltpu.VMEM((2, blocks_step * slabs, hd), F32)],
        ),
        out_shape=jax.ShapeDtypeStruct((bs, n_pages * page // CMP_BLOCK, half), F32),
        compiler_params=_cparams(("parallel", "arbitrary")),
        name="paged_kv",
    )(page_table, *([pool] * per_step), pe_tiles, w1cat, w2)


def _cmp_mlp_body(x_ref, pe_ref, w1_ref, w2_ref, o_ref):
    hd = NSA_HEAD_DIM
    for kg in range(2 * NSA_KV_HEADS):
        ch = kg // NSA_KV_HEADS
        acc = None
        for r in range(CMP_BLOCK):
            xr = x_ref[:, r, kg * hd:(kg + 1) * hd]
            xr = (xr + pe_ref[r:r + 1, kg * hd:(kg + 1) * hd]).astype(BF16)
            part = _dot(xr, w1_ref[ch, r * hd:(r + 1) * hd, :])
            acc = part if acc is None else acc + part
        hid = jax.nn.gelu(acc, approximate=True)
        o_ref[:, kg * hd:(kg + 1) * hd] = _dot(hid.astype(BF16), w2_ref[ch])


def _cmp_mlp(blocks, pe_rows, w1, w2, nb, col_block):
    n_blocks, rows, _ = blocks.shape
    out_w = 2 * NSA_KV_DIM
    return pl.pallas_call(
        _cmp_mlp_body,
        grid=(n_blocks // nb,),
        in_specs=[
            pl.BlockSpec((nb, rows, out_w), lambda i: (i, 0, col_block)),
            pl.BlockSpec(pe_rows.shape, lambda i: (0, 0)),
            pl.BlockSpec(w1.shape, lambda i: (0, 0, 0)),
            pl.BlockSpec(w2.shape, lambda i: (0, 0, 0)),
        ],
        out_specs=pl.BlockSpec((nb, out_w), lambda i: (i, 0)),
        out_shape=jax.ShapeDtypeStruct((n_blocks, out_w), F32),
        compiler_params=_cparams(("parallel",)),
        name="cmp_mlp",
    )(blocks, pe_rows, w1, w2)


N_BIAS_TILES = -(-(FAR_DIST + LANES - 1) // LANES) + 1


def _bias_tiles_body(rb_ref, o_ref, cmp_ref):
    h = pl.program_id(0)
    key = lax.broadcasted_iota(jnp.int32, (LANES, Q_BLOCK), 0)
    qry = lax.broadcasted_iota(jnp.int32, (LANES, Q_BLOCK), 1)

    def bias_of(dist):
        return _bias_for_heads(dist, rb_ref, [h])[0] * LOG2_E

    for m in range(N_BIAS_TILES):
        o_ref[0, m] = bias_of(m * LANES + qry - key)
    for i in range(cmp_ref.shape[1]):
        cmp_ref[0, i] = bias_of(i * Q_BLOCK + qry - (key * CMP_BLOCK + CMP_BLOCK - 1))


def _bias_tiles(rel_bias, n_q_blocks):
    tile = (Q_BLOCK, LANES)
    return pl.pallas_call(
        _bias_tiles_body,
        grid=(NSA_HEADS,),
        in_specs=[pl.BlockSpec(memory_space=pltpu.SMEM)],
        out_specs=[pl.BlockSpec((1, N_BIAS_TILES) + tile, lambda h: (h, 0, 0, 0)),
                   pl.BlockSpec((1, n_q_blocks) + tile, lambda h: (h, 0, 0, 0))],
        out_shape=[jax.ShapeDtypeStruct((NSA_HEADS, N_BIAS_TILES) + tile, F32),
                   jax.ShapeDtypeStruct((NSA_HEADS, n_q_blocks) + tile, F32)],
        compiler_params=_cparams(("arbitrary",)),
        name="bias_tiles",
    )(rel_bias)


def _nsa_prompt_body(q_ref, gate_ref, sel_ref, win_ref, kc_ref, bt_ref, cb_ref, y_ref,
                     selk_ref, selvt_ref, wink_ref, winvt_ref, *, n_sel):
    hd = NSA_HEAD_DIM
    qb = Q_BLOCK
    i = pl.program_id(1)
    t_len = selk_ref.shape[0]

    @pl.when(i == 0)
    def _():
        for src_ref, k_ref, vt_ref in ((sel_ref, selk_ref, selvt_ref), (win_ref, wink_ref, winvt_ref)):
            k_ref[...] = src_ref[0, :, :NSA_KV_DIM].astype(BF16)
            for c in range(t_len // LANES):
                rows = slice(c * LANES, (c + 1) * LANES)
                vt_ref[:, rows] = src_ref[0, rows, NSA_KV_DIM:].T.astype(BF16)

    q_bf = (q_ref[0] * (hd ** -0.5 * LOG2_E)).astype(BF16)
    gates_t = jax.nn.sigmoid(gate_ref[0]).T
    iq = lax.broadcasted_iota(jnp.int32, (qb, LANES), 0)
    pos = i * qb + iq
    key = lax.broadcasted_iota(jnp.int32, (LANES, qb), 0)
    qry = lax.broadcasted_iota(jnp.int32, (LANES, qb), 1)
    vis_cmp = _stack_heads_t(i * qb + qry - (key * CMP_BLOCK + CMP_BLOCK - 1) >= 0)
    top_n = min(TOP_N, n_sel)
    sel_tiles = SEL_TILES if t_len % (SEL_TILES * LANES) == 0 else 1
    sel_keys = sel_tiles * LANES
    win_tiles = min(WINDOW // LANES + 1, t_len // LANES)
    win_keys = win_tiles * LANES
    groups = [[g * NSA_HPG + h for h in range(NSA_HPG)] for g in range(NSA_KV_HEADS)]

    def tile_bias(heads, first_tile, n_tiles):
        tiles = []
        for k in range(n_tiles):
            m = jnp.clip(i - (first_tile + k), 0, N_BIAS_TILES - 1)
            tiles.append(jnp.concatenate([bt_ref[hh, m] for hh in heads], axis=1))
        return tiles[0] if n_tiles == 1 else jnp.concatenate(tiles, axis=0)

    def selection_mask(chosen_t, tile, tile_keys):
        n_idx = lax.broadcasted_iota(jnp.int32, (tile_keys, LANES), 1)
        k_idx = lax.broadcasted_iota(jnp.int32, (tile_keys, LANES), 0)
        expand = jnp.where(n_idx == 2 * ((tile * tile_keys + k_idx) >> SEL_SHIFT), 1.0, 0.0).astype(BF16)
        return _dot(expand, chosen_t) > 0.5

    qgs, o_cmps, values = [], [], []
    for g, heads in enumerate(groups):
        qg = jnp.concatenate([q_bf[:, hh * hd:(hh + 1) * hd] for hh in heads], axis=0)
        kc_k = kc_ref[0, :, g * hd:(g + 1) * hd].astype(BF16)
        kc_vt = kc_ref[0, :, NSA_KV_DIM + g * hd:NSA_KV_DIM + (g + 1) * hd].T.astype(BF16)
        s = _dot_nt(kc_k, qg) + jnp.concatenate([cb_ref[hh, 0] for hh in heads], axis=1)
        s = jnp.where(vis_cmp, s, NEG_BIG)
        e = jnp.where(vis_cmp, jnp.exp2(s - jnp.max(s, axis=0, keepdims=True)), 0.0)
        p = e / jnp.maximum(jnp.sum(e, axis=0, keepdims=True), 1e-30)
        o_cmps.append(_dot(kc_vt, p.astype(BF16)))
        imp_t = p[:, 0:qb]
        for h in range(1, NSA_HPG):
            imp_t = imp_t + p[:, h * qb:(h + 1) * qb]
        qgs.append(qg)
        values.append(_selection_values_of(imp_t.T, pos, n_sel))
    chosen = _topk_mask(jnp.concatenate(values, axis=0), top_n)
    chosens = [chosen[g * qb:(g + 1) * qb].T.astype(BF16) for g in range(NSA_KV_HEADS)]

    key_s = lax.broadcasted_iota(jnp.int32, (sel_keys, qb), 0)
    qry_s = lax.broadcasted_iota(jnp.int32, (sel_keys, qb), 1)

    def sel_step(j, carries):
        row0 = pl.multiple_of(j * sel_keys, sel_keys)
        causal = i * qb + qry_s - (j * sel_keys + key_s) >= 0
        out = []
        for g, heads in enumerate(groups):
            mask = _stack_heads_t(selection_mask(chosens[g], j, sel_keys) & causal)
            out.append(_softmax_update_t(carries[g], qgs[g], selk_ref[pl.ds(row0, sel_keys), g * hd:(g + 1) * hd],
                                         selvt_ref[g * hd:(g + 1) * hd, pl.ds(row0, sel_keys)],
                                         tile_bias(heads, j * sel_tiles, sel_tiles), mask))
        return tuple(out)

    n_steps = (i + sel_tiles) // sel_tiles
    sel_carries = lax.fori_loop(0, n_steps, sel_step, tuple(_softmax_init_t(NSA_HPG * qb) for _ in groups))

    first = jnp.maximum(i - (win_tiles - 1), 0)
    row0 = pl.multiple_of(first * LANES, LANES)
    d_win = (i * qb + lax.broadcasted_iota(jnp.int32, (win_keys, qb), 1)
             - (first * LANES + lax.broadcasted_iota(jnp.int32, (win_keys, qb), 0)))
    mask_win = _stack_heads_t((d_win >= 0) & (d_win < WINDOW))
    for g, heads in enumerate(groups):
        o_sel = _softmax_finish(sel_carries[g])
        o_win = _softmax_finish(_softmax_update_t(
            _softmax_init_t(NSA_HPG * qb), qgs[g], wink_ref[pl.ds(row0, win_keys), g * hd:(g + 1) * hd],
            winvt_ref[g * hd:(g + 1) * hd, pl.ds(row0, win_keys)], tile_bias(heads, first, win_tiles),
            mask_win))
        for h, hh in enumerate(heads):
            cols = slice(h * qb, (h + 1) * qb)
            out_t = (gates_t[hh:hh + 1, :] * o_cmps[g][:, cols]
                     + gates_t[NSA_HEADS + hh:NSA_HEADS + hh + 1, :] * o_sel[:, cols]
                     + gates_t[2 * NSA_HEADS + hh:2 * NSA_HEADS + hh + 1, :] * o_win[:, cols])
            y_ref[0, :, hh * hd:(hh + 1) * hd] = out_t.T.astype(BF16)


def _nsa_prompt(pb3, kc, bias_tiles, cmp_bias):
    b, t, _ = pb3.shape
    qb = Q_BLOCK
    half = 2 * NSA_KV_DIM
    return pl.pallas_call(
        functools.partial(_nsa_prompt_body, n_sel=t // SEL_BLOCK),
        grid=(b, t // qb),
        in_specs=[
            pl.BlockSpec((1, qb, NSA_DIM), lambda s, i: (s, i, PB_Q // NSA_DIM)),
            pl.BlockSpec((1, qb, LANES), lambda s, i: (s, i, PB_GATE // LANES)),
            pl.BlockSpec((1, t, half), lambda s, i: (s, 0, PB_SEL // half)),
            pl.BlockSpec((1, t, half), lambda s, i: (s, 0, PB_WIN // half)),
            pl.BlockSpec((1,) + kc.shape[1:], lambda s, i: (s, 0, 0)),
            pl.BlockSpec(bias_tiles.shape, lambda s, i: (0, 0, 0, 0)),
            pl.BlockSpec((NSA_HEADS, 1, qb, LANES), lambda s, i: (0, i, 0, 0)),
        ],
        out_specs=pl.BlockSpec((1, qb, NSA_DIM), lambda s, i: (s, i, 0)),
        out_shape=jax.ShapeDtypeStruct((b, t, NSA_DIM), BF16),
        scratch_shapes=[pltpu.VMEM((t, NSA_KV_DIM), BF16), pltpu.VMEM((NSA_KV_DIM, t), BF16),
                        pltpu.VMEM((t, NSA_KV_DIM), BF16), pltpu.VMEM((NSA_KV_DIM, t), BF16)],
        compiler_params=_cparams(("parallel", "arbitrary")),
        name="nsa_prompt",
    )(pb3, pb3, pb3, pb3, kc, bias_tiles, cmp_bias)


SEL_TILES = 2
SAMPLE_ROWS = 8
SAMPLE_TILE_KEYS = 1024


def _nsa_sample_body(rb_ref, q_ref, gate_ref, kc_ref, past_ref, new_sel_ref, cwin_ref, new_win_ref, y_ref,
                     *, past_len):
    hd = NSA_HEAD_DIM
    qb = SAMPLE_ROWS
    scale = hd ** -0.5
    n_cmp = kc_ref.shape[1]
    win_rows = cwin_ref.shape[1]
    k_off = past_len - win_rows
    tile_keys = min(SAMPLE_TILE_KEYS, past_len)
    assert past_len % tile_keys == 0

    q_bf = q_ref[0].astype(BF16)
    gates = jax.nn.sigmoid(gate_ref[0])
    iq = lax.broadcasted_iota(jnp.int32, (qb, LANES), 0)
    ik = lax.broadcasted_iota(jnp.int32, (qb, LANES), 1)
    pos_t = past_len + lax.broadcasted_iota(jnp.int32, (qb, tile_keys), 0)
    ik_t = lax.broadcasted_iota(jnp.int32, (qb, tile_keys), 1)
    d_win = (past_len + lax.broadcasted_iota(jnp.int32, (qb, win_rows), 0)
             - (k_off + lax.broadcasted_iota(jnp.int32, (qb, win_rows), 1)))
    tq_c = lax.broadcasted_iota(jnp.int32, (qb, n_cmp), 0)
    n_c = lax.broadcasted_iota(jnp.int32, (qb, n_cmp), 1)
    pos_c = past_len + tq_c
    d_cmp = pos_c - (n_c * CMP_BLOCK + CMP_BLOCK - 1)
    bias_cmp = _bias_for_heads(d_cmp, rb_ref, range(NSA_HEADS))
    vis_cmp = _stack_heads(d_cmp >= 0)
    n_sel_past = past_len // SEL_BLOCK
    picks = min(TOP_N, n_sel_past + 1) - 1

    groups = [[g * NSA_HPG + h for h in range(NSA_HPG)] for g in range(NSA_KV_HEADS)]

    def key_cols(g):
        return slice(g * hd, (g + 1) * hd), slice(NSA_KV_DIM + g * hd, NSA_KV_DIM + (g + 1) * hd)

    qgs, o_cmps, values = [], [], []
    for g, heads in enumerate(groups):
        k_cols, v_cols = key_cols(g)
        qg = jnp.concatenate([q_bf[:, hh * hd:(hh + 1) * hd] for hh in heads], axis=0)
        o_cmp, p_cmp = _cmp_branch(qg, kc_ref[0, :, k_cols].astype(BF16), kc_ref[0, :, v_cols].astype(BF16),
                                   jnp.concatenate([bias_cmp[hh] for hh in heads], axis=0), vis_cmp, scale)
        qgs.append(qg)
        o_cmps.append(o_cmp)
        values.append(_selection_values(p_cmp, qb, pos_c, n_sel_past))
    chosen = _topk_mask(jnp.concatenate(values, axis=0), picks).astype(BF16)
    chosens = [chosen[g * qb:(g + 1) * qb] for g in range(NSA_KV_HEADS)]

    def sel_step(j, carries):
        row0 = pl.multiple_of(j * tile_keys, tile_keys)
        dist = pos_t - (j * tile_keys + ik_t)
        out = []
        for g, heads in enumerate(groups):
            k_cols, v_cols = key_cols(g)
            mask = _stack_heads(_expand_selection(chosens[g], j, tile_keys))
            bias = jnp.concatenate(_bias_for_heads(dist, rb_ref, heads), axis=0)
            kt = past_ref[0, g, pl.ds(row0, tile_keys), :].astype(BF16)
            vt = past_ref[0, NSA_KV_HEADS + g, pl.ds(row0, tile_keys), :].astype(BF16)
            out.append(_softmax_update(carries[g], qgs[g], kt, vt, bias, mask, scale))
        return tuple(out)

    sel_carries = lax.fori_loop(0, past_len // tile_keys, sel_step,
                                tuple(_softmax_init(NSA_HPG * qb) for _ in groups))
    d_new = iq - ik
    mask_new = _stack_heads(d_new >= 0)
    mask_win = _stack_heads((d_win >= 0) & (d_win < WINDOW))
    for g, heads in enumerate(groups):
        k_cols, v_cols = key_cols(g)
        qg, o_cmp = qgs[g], o_cmps[g]
        bias_new = jnp.concatenate(_bias_for_heads(d_new, rb_ref, heads), axis=0)
        o_sel = _softmax_finish(_softmax_update(sel_carries[g], qg, new_sel_ref[0, :, k_cols],
                                                new_sel_ref[0, :, v_cols], bias_new, mask_new, scale))
        bias_win = jnp.concatenate(_bias_for_heads(d_win, rb_ref, heads), axis=0)
        carry = _softmax_update(_softmax_init(NSA_HPG * qb), qg, cwin_ref[0, :, k_cols].astype(BF16),
                                cwin_ref[0, :, v_cols].astype(BF16), bias_win, mask_win, scale)
        o_win = _softmax_finish(_softmax_update(carry, qg, new_win_ref[0, :, k_cols], new_win_ref[0, :, v_cols],
                                                bias_new, mask_new, scale))

        for h, hh in enumerate(heads):
            rows = slice(h * qb, (h + 1) * qb)
            out = (gates[:, hh:hh + 1] * o_cmp[rows]
                   + gates[:, NSA_HEADS + hh:NSA_HEADS + hh + 1] * o_sel[rows]
                   + gates[:, 2 * NSA_HEADS + hh:2 * NSA_HEADS + hh + 1] * o_win[rows])
            y_ref[0, :, hh * hd:(hh + 1) * hd] = out.astype(BF16)


def _nsa_sample(rel_bias, q8, gate8, kc, past_sel, new_sel, cwin, new_win, past_len):
    bs = q8.shape[0]
    blk = lambda a: pl.BlockSpec((1,) + a.shape[1:], lambda s: (s,) + (0,) * (a.ndim - 1))
    return pl.pallas_call(
        functools.partial(_nsa_sample_body, past_len=past_len),
        grid=(bs,),
        in_specs=[pl.BlockSpec(memory_space=pltpu.SMEM), blk(q8), blk(gate8), blk(kc), blk(past_sel),
                  blk(new_sel), blk(cwin), blk(new_win)],
        out_specs=pl.BlockSpec((1, SAMPLE_ROWS, NSA_DIM), lambda s: (s, 0, 0)),
        out_shape=jax.ShapeDtypeStruct((bs, SAMPLE_ROWS, NSA_DIM), BF16),
        compiler_params=_cparams(("parallel",)),
        name="nsa_sample",
    )(rel_bias, q8, gate8, kc, past_sel, new_sel, cwin, new_win)


def _merge_body(x_ref, yrw_ref, g_ref, ynsa_ref, gates_ref, wrw_ref, wnsa_ref, wout_ref, post_ref, o_ref):
    tm, d = x_ref.shape
    y_rw = jnp.concatenate([yrw_ref[:, pair, :] for pair in range(RW_DIM // LANES)], axis=1)
    y_rw = (y_rw * g_ref[...]).astype(BF16)
    merged = (jax.nn.sigmoid(gates_ref[:, :d]) * _dot(y_rw, wrw_ref[...])
              + jax.nn.sigmoid(gates_ref[:, d:]) * _dot(ynsa_ref[...], wnsa_ref[...]))
    o_ref[...] = x_ref[...] + _rms(_dot(merged.astype(BF16), wout_ref[...]), post_ref[...])


def _merge(x, y_rw, g_rw, y_nsa, gates, w_rw, w_nsa, w_out, post_g, tm):
    m, d = x.shape
    rows = lambda n: pl.BlockSpec((tm, n), lambda i: (i, 0))
    held = lambda a: pl.BlockSpec(a.shape, lambda i: (0, 0), pipeline_mode=pl.Buffered(1))
    return pl.pallas_call(
        _merge_body,
        grid=(m // tm,),
        in_specs=[rows(d), pl.BlockSpec((tm,) + y_rw.shape[1:], lambda i: (i, 0, 0)),
                  rows(RW_DIM), rows(NSA_DIM), rows(2 * d),
                  held(w_rw), held(w_nsa), held(w_out), pl.BlockSpec((1, d), lambda i: (0, 0))],
        out_specs=rows(d),
        out_shape=jax.ShapeDtypeStruct((m, d), F32),
        compiler_params=_cparams(("parallel",)),
        name="merge",
    )(x, y_rw, g_rw, y_nsa, gates, w_rw, w_nsa, w_out, post_g.reshape(1, d))


def _row_tile(m, want):
    return want if m % want == 0 else m


def _rwkv_mix(p3, shift0, s0, w):
    b, t, c = p3.shape
    n = RW_HEAD_DIM
    tt = 256 if t % 256 == 0 else t
    prev = jnp.concatenate([shift0[:, None, :], p3[:, tt - 1:t - 1:tt]], axis=1).reshape(b, t // tt, 1, c)
    pre, g_rw = _rwkv_pre(p3, prev, w["mu"], w["w0"], w["ww"], w["a0"], w["wa"], w["g2"], w["k_k"], w["k_a"], tt)
    seqs = min(b, LANES // RW_HEADS)
    groups = b // seqs
    pairs = RW_HEADS // 2
    s0_t = s0.reshape(groups, seqs, pairs, 2, n, n).transpose(5, 4, 0, 3, 1, 2).reshape(n, n, b * RW_HEADS)
    per_lane = lambda v: jnp.broadcast_to(v.reshape(pairs, 2, n).transpose(2, 1, 0)[:, :, None, :],
                                          (n, 2, seqs, pairs)).reshape(n, seqs * RW_HEADS)
    tc = 32 if t % 32 == 0 else t
    y_t, s_t = _rwkv_scan(pre, s0_t, per_lane(w["lnx_w"]), per_lane(w["lnx_b"]), per_lane(w["r_k"]), tc, seqs)
    s_fin = s_t.reshape(n, n, groups, 2, seqs, pairs).transpose(2, 4, 5, 3, 1, 0).reshape(b, RW_HEADS, n, n)
    y_tok = y_t.reshape(t, b, pairs, LANES).transpose(1, 0, 2, 3).reshape(b * t, pairs, LANES)
    return y_tok, g_rw.reshape(b * t, RW_DIM), s_fin


def _layer(x, w, rel_bias, shift0, s0, past):
    b, t, d = x.shape
    m = b * t
    tm = _row_tile(m, 512)
    x1 = _ffn(x.reshape(m, d), w["f1_pre"], w["f1_post"], w["f1_w1"], w["f1_w3"], w["f1_w2"], tm, 512)
    p_a = _norm_matmul(x1, w["mix_pre"], w["w_in_a"], tm, "proj_rwkv")
    p_b = _norm_matmul(x1, w["mix_pre"], w["w_in_b"], tm, "proj_nsa")
    p_c = _norm_matmul(x1, w["mix_pre"], w["w_in_c"], tm, "proj_gates")

    p3 = p_a.reshape(b, t, RW_PROJ)
    y_rw, g_rw, s_fin = _rwkv_mix(p3, shift0, s0, w)

    pb3 = p_b.reshape(b, t, PB_COLS)
    kv_new = pb3[:, :, PB_KV:PB_WIN]
    win_new = pb3[:, :, PB_WIN:PB_GATE]
    row_w = 4 * NSA_KV_DIM
    half = 2 * NSA_KV_DIM
    if past is None:
        blocks = p_b.reshape(m // CMP_BLOCK, CMP_BLOCK, PB_COLS)
        kc = _cmp_mlp(blocks, w["cmp_pe"], w["cmp_w1"], w["cmp_w2"], min(64, blocks.shape[0]), PB_KV // half)
        n_cmp = t // CMP_BLOCK
        kc = jnp.pad(kc.reshape(b, n_cmp, 2 * NSA_KV_DIM), ((0, 0), (0, -n_cmp % LANES), (0, 0)))
        y_nsa = _nsa_prompt(pb3, kc, *_bias_tiles(rel_bias, t // Q_BLOCK)).reshape(m, NSA_DIM)
        win_out = win_new[:, t - min(WINDOW, t):]
    else:
        page_table, pool, cache_win, start_after = past
        past_len = page_table.shape[1] * pool.shape[1]
        slabs = row_w // NSA_HEAD_DIM
        pool4 = pool.reshape(pool.shape[0], pool.shape[1], slabs, NSA_HEAD_DIM)
        kc = _paged_kv(page_table, pool4, w["cmp_pe_tiles"], w["cmp_w1cat"], w["cmp_w2"])
        gather_table = lax.optimization_barrier((page_table, start_after))[0]
        past_sel = _gather_selected(gather_table, pool4)
        n_cmp = past_len // CMP_BLOCK
        kc = jnp.pad(kc, ((0, 0), (0, -n_cmp % LANES), (0, 0)))
        pad_q = lambda a, rows: jnp.pad(a, ((0, 0), (0, rows - t), (0, 0)))
        cwin = cache_win.reshape(b, cache_win.shape[1], half)
        y8 = _nsa_sample(rel_bias, pad_q(pb3[:, :, :NSA_DIM], SAMPLE_ROWS), pad_q(pb3[:, :, PB_GATE:], SAMPLE_ROWS),
                         kc, past_sel, pad_q(pb3[:, :, PB_SEL:PB_WIN].astype(BF16), LANES), cwin,
                         pad_q(win_new.astype(BF16), LANES), past_len)
        y_nsa = y8[:, :t].reshape(m, NSA_DIM)
        win_all = jnp.concatenate([cwin, win_new], axis=1)
        win_out = win_all[:, win_all.shape[1] - min(WINDOW, win_all.shape[1]):]

    x2 = _merge(x1, y_rw, g_rw, y_nsa, p_c, w["w_br_rw"], w["w_br_nsa"], w["w_out"], w["mix_post"],
                _row_tile(m, 256))
    y = _ffn(x2, w["f2_pre"], w["f2_post"], w["f2_w1"], w["f2_w3"], w["f2_w2"], tm, 512)
    g, hd = NSA_KV_HEADS, NSA_HEAD_DIM
    return (y.reshape(b, t, d), kv_new.reshape(b, t, 4, g, hd), win_out.reshape(b, -1, 2, g, hd), s_fin, p3[:, -1])


def _prepare_weights(l, ffn1_pre_g, ffn1_post_g, ffn1_w1, ffn1_w3, ffn1_w2, mix_pre_g, mix_post_g, w_in,
                     rw_mu, rw_w0, rw_w2, rw_a0, rw_a2, rw_g2, rw_k_k, rw_k_a, rw_r_k, rw_lnx_w, rw_lnx_b,
                     cmp_pe, cmp_w1, cmp_w2, w_br_rw, w_br_nsa, w_out,
                     ffn2_pre_g, ffn2_post_g, ffn2_w1, ffn2_w3, ffn2_w2):
    d = w_in.shape[1]
    wi = w_in[l]
    c_q = RW_PROJ
    c_kv = c_q + NSA_DIM
    c_gate = c_kv + 6 * NSA_KV_DIM
    c_grw = c_gate + 3 * NSA_HEADS
    w_in_b = jnp.concatenate([wi[:, c_q:c_gate], jnp.pad(wi[:, c_gate:c_grw], ((0, 0), (0, LANES - 3 * NSA_HEADS)))],
                             axis=1)
    zeros_w = jnp.zeros((RW_LORA_A, RW_DIM), F32)
    zeros_a = jnp.zeros((RW_LORA_W, RW_DIM), F32)
    pe = cmp_pe[l]
    pe_rows = jnp.broadcast_to(pe[:, :, None, :], (CMP_BLOCK, 2, NSA_KV_HEADS, NSA_HEAD_DIM))
    return {
        "f1_pre": ffn1_pre_g[l], "f1_post": ffn1_post_g[l],
        "f1_w1": ffn1_w1[l].astype(BF16), "f1_w3": ffn1_w3[l].astype(BF16), "f1_w2": ffn1_w2[l].astype(BF16),
        "mix_pre": mix_pre_g[l], "mix_post": mix_post_g[l],
        "w_in_a": wi[:, :RW_PROJ].astype(BF16), "w_in_b": w_in_b.astype(BF16), "w_in_c": wi[:, c_grw:].astype(BF16),
        "mu": rw_mu[l], "w0": rw_w0[l], "a0": rw_a0[l],
        "ww": jnp.concatenate([rw_w2[l], zeros_w], axis=0).astype(BF16),
        "wa": jnp.concatenate([zeros_a, rw_a2[l]], axis=0).astype(BF16),
        "g2": rw_g2[l].astype(BF16), "k_k": rw_k_k[l], "k_a": rw_k_a[l],
        "r_k": rw_r_k[l].reshape(-1), "lnx_w": rw_lnx_w[l], "lnx_b": rw_lnx_b[l],
        "cmp_pe": pe_rows.reshape(CMP_BLOCK, 2 * NSA_KV_DIM),
        "cmp_pe_tiles": jnp.pad(pe_rows.reshape(CMP_BLOCK, 2 * NSA_KV_HEADS, NSA_HEAD_DIM),
                                ((0, 0), (0, 2 * NSA_KV_HEADS), (0, 0))),
        "cmp_w1": cmp_w1[l].astype(BF16), "cmp_w2": cmp_w2[l].astype(BF16),
        "cmp_w1cat": jnp.concatenate([cmp_w1[l, 0], cmp_w1[l, 1]], axis=1).astype(BF16),
        "w_br_rw": w_br_rw[l].astype(BF16), "w_br_nsa": w_br_nsa[l].astype(BF16), "w_out": w_out[l].astype(BF16),
        "f2_pre": ffn2_pre_g[l], "f2_post": ffn2_post_g[l],
        "f2_w1": ffn2_w1[l].astype(BF16), "f2_w3": ffn2_w3[l].astype(BF16), "f2_w2": ffn2_w2[l].astype(BF16),
    }


def kernel(x_prompt, x_sample, cache_kv, cache_win, state_rwkv, state_shift, page_table,
           ffn1_pre_g, ffn1_post_g, ffn1_w1, ffn1_w3, ffn1_w2, mix_pre_g, mix_post_g, w_in,
           rw_mu, rw_w0, rw_w2, rw_a0, rw_a2, rw_g2, rw_k_k, rw_k_a, rw_r_k, rw_lnx_w, rw_lnx_b,
           cmp_pe, cmp_w1, cmp_w2, w_br_rw, w_br_nsa, w_out,
           ffn2_pre_g, ffn2_post_g, ffn2_w1, ffn2_w3, ffn2_w2, rel_bias):
    depth = w_in.shape[0]
    b_p = x_prompt.shape[0]
    y_p, y_s = x_prompt, x_sample
    outs = [[] for _ in range(8)]
    for l in range(depth):
        w = _prepare_weights(l, ffn1_pre_g, ffn1_post_g, ffn1_w1, ffn1_w3, ffn1_w2, mix_pre_g, mix_post_g, w_in,
                             rw_mu, rw_w0, rw_w2, rw_a0, rw_a2, rw_g2, rw_k_k, rw_k_a, rw_r_k, rw_lnx_w, rw_lnx_b,
                             cmp_pe, cmp_w1, cmp_w2, w_br_rw, w_br_nsa, w_out,
                             ffn2_pre_g, ffn2_post_g, ffn2_w1, ffn2_w3, ffn2_w2)
        y_p, kv_p, win_p, rw_p, sh_p = _layer(
            y_p, w, rel_bias, jnp.zeros((b_p, RW_PROJ), F32),
            jnp.zeros((b_p, RW_HEADS, RW_HEAD_DIM, RW_HEAD_DIM), F32), None)
        y_s, kv_s, win_s, rw_s, sh_s = _layer(
            y_s, w, rel_bias, state_shift[l], state_rwkv[l], (page_table, cache_kv[l], cache_win[l], sh_p))
        for acc, v in zip(outs, (kv_p, kv_s, win_p, win_s, rw_p, rw_s, sh_p, sh_s)):
            acc.append(v)
    return (y_p, y_s) + tuple(jnp.stack(o) for o in outs)
```

```python
import functools
import math

import jax
import jax.numpy as jnp
from jax import lax
from jax.experimental import pallas as pl
from jax.experimental.pallas import tpu as pltpu
from jax.experimental.pallas import tpu_sc as plsc

F32 = jnp.float32
BF16 = jnp.bfloat16

RMS_EPS = 1e-6
RW_HEADS = 16
RW_HEAD_DIM = 64
RW_DIM = RW_HEADS * RW_HEAD_DIM
RW_LORA_W = 64
RW_LORA_A = 64
RW_LORA_G = 128
RW_PROJ = 3 * RW_DIM + RW_LORA_W + RW_LORA_A + RW_LORA_G
RW_LN_EPS = 64e-5
NSA_HEADS = 8
NSA_KV_HEADS = 2
NSA_HPG = NSA_HEADS // NSA_KV_HEADS
NSA_HEAD_DIM = 128
NSA_DIM = NSA_HEADS * NSA_HEAD_DIM
NSA_KV_DIM = NSA_KV_HEADS * NSA_HEAD_DIM
CMP_BLOCK = 32
CMP_HID = 256
SEL_BLOCK = 64
SEL_SHIFT = SEL_BLOCK.bit_length() - 1
TOP_N = 16
WINDOW = 512
N_BUCKETS = 32
REL_MAX_EXACT = 16
REL_MAX_DIST = 1024
Q_BLOCK = 128
NEG_BIG = -1e30
FORCE_BONUS = 1e4
NEVER = -3e38
LOG2_E = math.log2(math.e)
LANES = 128
SUBLANES = 8
SCAN_FIELDS = 6
VMEM_LIMIT = 56 * 1024 * 1024
SCAN_UNROLL = 8
UPDATE_UNROLL = 16

PB_Q = 0
PB_KV = NSA_DIM
PB_SEL = PB_KV + 2 * NSA_KV_DIM
PB_WIN = PB_KV + 4 * NSA_KV_DIM
PB_GATE = PB_WIN + 2 * NSA_KV_DIM
PB_COLS = PB_GATE + LANES


def _bucket_thresholds():
    thr = list(range(1, REL_MAX_EXACT + 1))
    n_log = N_BUCKETS - REL_MAX_EXACT
    ratio = REL_MAX_DIST // REL_MAX_EXACT
    n = REL_MAX_EXACT
    for k in range(1, n_log):
        while n ** n_log < REL_MAX_EXACT ** n_log * ratio ** k:
            n += 1
        thr.append(n)
    return thr


BUCKET_THR = _bucket_thresholds()
FAR_DIST = BUCKET_THR[-1]


def _cparams(sem):
    return pltpu.CompilerParams(dimension_semantics=sem, vmem_limit_bytes=VMEM_LIMIT)


def _rms(x, g):
    ms = jnp.mean(x * x, axis=-1, keepdims=True)
    return x * lax.rsqrt(ms + RMS_EPS) * g


def _dot(a, b):
    return jnp.dot(a, b, preferred_element_type=F32)


def _dot_nt(a, b):
    return lax.dot_general(a, b, (((1,), (1,)), ((), ())), preferred_element_type=F32)


def _bias_for_heads(dist, rb_ref, heads):
    n = jnp.maximum(dist, 0)
    reach = [n >= t for t in BUCKET_THR]
    out = []
    for h in heads:
        val = jnp.full(dist.shape, rb_ref[0, h], F32)
        for b, m in enumerate(reach):
            val = jnp.where(m, rb_ref[b + 1, h], val)
        out.append(val)
    return out


def _softmax_update(carry, q, kt, vt, bias, mask, scale):
    m_run, l_run, acc = carry
    s = _dot_nt(q, kt) * scale + bias
    s = jnp.where(mask, s, NEG_BIG)
    m_new = jnp.maximum(m_run, jnp.max(s, axis=-1, keepdims=True))
    alpha = jnp.exp(m_run - m_new)
    e = jnp.where(mask, jnp.exp(s - m_new), 0.0)
    l_new = alpha * l_run + jnp.sum(e, axis=-1, keepdims=True)
    acc = alpha * acc + _dot(e.astype(BF16), vt)
    return m_new, l_new, acc


def _softmax_init(rows):
    return (jnp.full((rows, 1), NEG_BIG, F32), jnp.zeros((rows, 1), F32),
            jnp.zeros((rows, NSA_HEAD_DIM), F32))


def _softmax_finish(carry):
    _, l_run, acc = carry
    return acc / jnp.maximum(l_run, 1e-30)


def _softmax_update_t(carry, q, kt, v_t, bias_t, mask_t):
    m_run, l_run, acc = carry
    s = _dot_nt(kt, q) + bias_t
    s = jnp.where(mask_t, s, NEG_BIG)
    m_new = jnp.maximum(m_run, jnp.max(s, axis=0, keepdims=True))
    alpha = jnp.exp2(m_run - m_new)
    e = jnp.where(mask_t, jnp.exp2(s - m_new), 0.0)
    l_new = alpha * l_run + jnp.sum(e, axis=0, keepdims=True)
    acc = alpha * acc + _dot(v_t, e.astype(BF16))
    return m_new, l_new, acc


def _softmax_init_t(rows):
    return (jnp.full((1, rows), NEG_BIG, F32), jnp.zeros((1, rows), F32),
            jnp.zeros((NSA_HEAD_DIM, rows), F32))


def _stack_heads(x):
    return jnp.concatenate([x] * NSA_HPG, axis=0)


def _stack_heads_t(x):
    return jnp.concatenate([x] * NSA_HPG, axis=1)


def _topk_mask(val, k):
    lane = lax.broadcasted_iota(jnp.int32, val.shape, 1)
    chosen = jnp.zeros(val.shape, F32)
    for _ in range(k):
        hit = lane == jnp.argmax(val, axis=-1, keepdims=True)
        chosen = jnp.where(hit, 1.0, chosen)
        val = jnp.where(hit, NEVER, val)
    return chosen


def _pair_sum(x):
    parts = []
    for c in range(x.shape[1] // LANES):
        blk = x[:, c * LANES:(c + 1) * LANES]
        parts.append(blk + pltpu.roll(blk, LANES - 1, 1))
    return parts[0] if len(parts) == 1 else jnp.concatenate(parts, axis=1)


def _cmp_branch(qg, kc_k, kc_v, bias, visible, scale):
    s = _dot_nt(qg, kc_k) * scale + bias
    s = jnp.where(visible, s, NEG_BIG)
    e = jnp.where(visible, jnp.exp(s - jnp.max(s, axis=-1, keepdims=True)), 0.0)
    p = e / jnp.maximum(jnp.sum(e, axis=-1, keepdims=True), 1e-30)
    return _dot(p.astype(BF16), kc_v), p


def _selection_values(p, qb, pos, n_sel):
    imp = p[0:qb]
    for h in range(1, NSA_HPG):
        imp = imp + p[h * qb:(h + 1) * qb]
    return _selection_values_of(imp, pos, n_sel)


def _selection_values_of(imp, pos, n_sel):
    imp = _pair_sum(imp)
    lane = lax.broadcasted_iota(jnp.int32, imp.shape, 1)
    sel_id = lane >> 1
    cur = pos >> SEL_SHIFT
    forced = (sel_id == 0) | (sel_id == cur) | (sel_id == cur - 1)
    val = jnp.where(forced, imp + FORCE_BONUS, imp)
    val = jnp.where(sel_id * SEL_BLOCK <= pos, val, NEG_BIG)
    return jnp.where(((lane & 1) == 0) & (sel_id < n_sel), val, NEVER)


def _expand_selection(chosen_bf16, tile, tile_keys=LANES):
    width = chosen_bf16.shape[1]
    n_idx = lax.broadcasted_iota(jnp.int32, (width, tile_keys), 0)
    k_idx = lax.broadcasted_iota(jnp.int32, (width, tile_keys), 1)
    target = 2 * ((tile * tile_keys + k_idx) >> SEL_SHIFT)
    expand = jnp.where(n_idx == target, 1.0, 0.0).astype(BF16)
    return _dot(chosen_bf16, expand) > 0.5


def _ffn_body(x_ref, pre_ref, post_ref, w1_ref, w3_ref, w2_ref, o_ref, h_ref, acc_ref):
    j = pl.program_id(1)

    @pl.when(j == 0)
    def _():
        h_ref[...] = _rms(x_ref[...], pre_ref[...]).astype(BF16)
        acc_ref[...] = jnp.zeros_like(acc_ref)

    h = h_ref[...]
    a = _dot(h, w1_ref[...])
    b = _dot(h, w3_ref[...])
    u = (a * jax.nn.sigmoid(a)) * b
    acc_ref[...] += _dot(u.astype(BF16), w2_ref[...])

    @pl.when(j == pl.num_programs(1) - 1)
    def _():
        o_ref[...] = x_ref[...] + 0.5 * _rms(acc_ref[...], post_ref[...])


def _ffn(x, pre_g, post_g, w1, w3, w2, tm, tf):
    m, d = x.shape
    f = w1.shape[1]
    return pl.pallas_call(
        _ffn_body,
        grid=(m // tm, f // tf),
        in_specs=[
            pl.BlockSpec((tm, d), lambda i, j: (i, 0)),
            pl.BlockSpec((1, d), lambda i, j: (0, 0)),
            pl.BlockSpec((1, d), lambda i, j: (0, 0)),
            pl.BlockSpec((d, tf), lambda i, j: (0, j)),
            pl.BlockSpec((d, tf), lambda i, j: (0, j)),
            pl.BlockSpec((tf, d), lambda i, j: (j, 0)),
        ],
        out_specs=pl.BlockSpec((tm, d), lambda i, j: (i, 0)),
        out_shape=jax.ShapeDtypeStruct((m, d), F32),
        scratch_shapes=[pltpu.VMEM((tm, d), BF16), pltpu.VMEM((tm, d), F32)],
        compiler_params=_cparams(("parallel", "arbitrary")),
        name="ffn",
    )(x, pre_g.reshape(1, d), post_g.reshape(1, d), w1, w3, w2)


def _norm_matmul_body(x_ref, g_ref, w_ref, o_ref, h_ref):
    @pl.when(pl.program_id(1) == 0)
    def _():
        h_ref[...] = _rms(x_ref[...], g_ref[...]).astype(BF16)

    o_ref[...] = _dot(h_ref[...], w_ref[...])


def _norm_matmul(x, g, w, tm, name):
    m, d = x.shape
    n = w.shape[1]
    return pl.pallas_call(
        _norm_matmul_body,
        grid=(m // tm, 1),
        in_specs=[
            pl.BlockSpec((tm, d), lambda i, j: (i, 0)),
            pl.BlockSpec((1, d), lambda i, j: (0, 0)),
            pl.BlockSpec((d, n), lambda i, j: (0, 0), pipeline_mode=pl.Buffered(1)),
        ],
        out_specs=pl.BlockSpec((tm, n), lambda i, j: (i, 0)),
        out_shape=jax.ShapeDtypeStruct((m, n), F32),
        scratch_shapes=[pltpu.VMEM((tm, d), BF16)],
        compiler_params=_cparams(("parallel", "arbitrary")),
        name=name,
    )(x, g.reshape(1, d), w)


def _rwkv_pre_body(p_ref, prev_ref, mu_ref, w0_ref, ww_ref, a0_ref, wa_ref, g2_ref, kk_ref, ka_ref,
                   pre_ref, g_ref, buf_ref):
    tt = p_ref.shape[1]
    p = p_ref[0]
    buf_ref[8:8 + tt, :] = p
    buf_ref[7:8, :] = prev_ref[0, 0]
    p_prev = buf_ref[7:7 + tt, :]
    xs = p + (p_prev - p) * mu_ref[...]
    d = RW_DIM
    r = xs[:, 0:d]
    k = xs[:, d:2 * d]
    v = xs[:, 2 * d:3 * d]
    lora_in = xs[:, 3 * d:3 * d + RW_LORA_W + RW_LORA_A]
    gd = xs[:, 3 * d + RW_LORA_W + RW_LORA_A:]
    z = -(w0_ref[...] + _dot(jnp.tanh(lora_in).astype(BF16), ww_ref[...]))
    softplus = jnp.maximum(z, 0.0) + jnp.log(1.0 + jnp.exp(-jnp.abs(z)))
    decay = jnp.exp(-jnp.exp(-softplus - 0.5))
    a = jax.nn.sigmoid(a0_ref[...] + _dot(lora_in.astype(BF16), wa_ref[...]))
    g_ref[0] = _dot(jax.nn.sigmoid(gd).astype(BF16), g2_ref[...])

    n = RW_HEAD_DIM
    pairs = d // LANES
    low_half = lax.broadcasted_iota(jnp.int32, (tt, LANES), 1) < n

    def put(fp, xa, xb):
        for pair in range(pairs):
            col_a = xa[:, pair * LANES:(pair + 1) * LANES]
            col_b = xb[:, pair * LANES:(pair + 1) * LANES]
            pieces = (jnp.where(low_half, col_a, pltpu.roll(col_b, n, 1)),
                      jnp.where(low_half, pltpu.roll(col_a, n, 1), col_b))
            for parity, piece in enumerate(pieces):
                if tt % SUBLANES:
                    pre_ref[:, fp, parity, pair, :] = piece
                else:
                    q = (fp * 2 + parity) * pairs + pair
                    pre_ref[:, q, :, :] = piece.reshape(tt // SUBLANES, SUBLANES, LANES)

    put(0, r, k * (1.0 + (a - 1.0) * ka_ref[...]))
    put(1, v, decay)
    put(2, k * kk_ref[...], a)


def _rwkv_pre(p3, prev, mu, w0, ww, a0, wa, g2, k_k, k_a, tt):
    b, t, c = p3.shape
    d = RW_DIM
    pairs = RW_HEADS // 2
    row = lambda n: pl.BlockSpec((1, n), lambda i, j: (0, 0))
    full = lambda s: pl.BlockSpec(s, lambda i, j: (0, 0))
    n_pieces = SCAN_FIELDS * pairs
    if tt % SUBLANES:
        pre_spec = pl.BlockSpec((tt, SCAN_FIELDS // 2, 2, pairs, LANES), lambda i, j: (j, 0, 0, i, 0))
        pre_shape = (t, SCAN_FIELDS // 2, 2, b * pairs, LANES)
    else:
        tiles, steps = tt // SUBLANES, t // tt
        pre_spec = pl.BlockSpec((tiles, n_pieces, SUBLANES, LANES), lambda i, j: (i * steps + j, 0, 0, 0))
        pre_shape = (b * t // SUBLANES, n_pieces, SUBLANES, LANES)
    return pl.pallas_call(
        _rwkv_pre_body,
        grid=(b, t // tt),
        in_specs=[
            pl.BlockSpec((1, tt, c), lambda i, j: (i, j, 0)),
            pl.BlockSpec((1, 1, 1, c), lambda i, j: (i, j, 0, 0)),
            row(c), row(d), full(ww.shape), row(d), full(wa.shape), full(g2.shape), row(d), row(d),
        ],
        out_specs=[pre_spec, pl.BlockSpec((1, tt, d), lambda i, j: (i, j, 0))],
        out_shape=[jax.ShapeDtypeStruct(pre_shape, F32), jax.ShapeDtypeStruct((b, t, d), F32)],
        scratch_shapes=[pltpu.VMEM((tt + 8, c), F32)],
        compiler_params=_cparams(("parallel", "arbitrary")),
        name="rwkv_pre",
    )(p3, prev, mu.reshape(1, c), w0.reshape(1, d), ww, a0.reshape(1, d), wa, g2,
      k_k.reshape(1, d), k_a.reshape(1, d))


def _rwkv_scan_body(in_ref, s0_ref, lnw_ref, lnb_ref, rk_ref, y_ref, s_ref, xa_ref, xb_ref, kk_ref, b_ref,
                    out_ref):
    n = RW_HEAD_DIM
    steps = in_ref.shape[0]
    n_fields = in_ref.shape[1]
    half = in_ref.shape[3]
    lanes = 2 * half
    fields_per_trip = -(-n_fields // (n // UPDATE_UNROLL))

    def load_field(t, dst_ref, fp):
        tile = in_ref[t, fp].reshape(lanes, 2 * n)
        dst_ref[pl.ds(pl.multiple_of(fp * 2 * n, 2 * n), 2 * n), :] = tile.T

    def normalise_key(x_ref):
        kk_raw = x_ref[4 * n:5 * n, :]
        norm = jnp.sqrt(jnp.sum(kk_raw * kk_raw, axis=0, keepdims=True))
        kk = kk_raw / jnp.maximum(norm, 1e-12)
        kk_ref[...] = kk
        b_ref[...] = kk * x_ref[5 * n:6 * n, :]

    def one_step(t, x_ref, next_ref):
        t_next = jnp.minimum(t + 1, steps - 1)

        def sk_rows(jb, acc):
            for u in range(SCAN_UNROLL):
                j = jb * SCAN_UNROLL + u
                acc = acc + s_ref[j] * kk_ref[pl.ds(j, 1), :]
            return acc

        sk = lax.fori_loop(0, n // SCAN_UNROLL, sk_rows, jnp.zeros((n, lanes), F32))
        v = x_ref[2 * n:3 * n, :]

        def update_rows(jb, acc):
            for f in range(fields_per_trip):
                load_field(t_next, next_ref, jnp.minimum(jb * fields_per_trip + f, n_fields - 1))
            emit_output(jnp.maximum(t - 1, 0))
            for u in range(UPDATE_UNROLL):
                j = jb * UPDATE_UNROLL + u
                r_j = x_ref[pl.ds(j, 1), :]
                k_j = x_ref[pl.ds(n + j, 1), :]
                w_j = x_ref[pl.ds(3 * n + j, 1), :]
                s_j = s_ref[j] * w_j - sk * b_ref[pl.ds(j, 1), :] + v * k_j
                s_ref[j] = s_j
                acc = acc + s_j * r_j
            return acc

        y = lax.fori_loop(0, n // UPDATE_UNROLL, update_rows, jnp.zeros((n, lanes), F32))
        normalise_key(next_ref)
        mean = jnp.sum(y, axis=0, keepdims=True) * (1.0 / n)
        yc = y - mean
        var = jnp.sum(yc * yc, axis=0, keepdims=True) * (1.0 / n)
        y_norm = yc * lax.rsqrt(var + RW_LN_EPS) * lnw_ref[...] + lnb_ref[...]
        rkr = x_ref[0:n, :] * x_ref[n:2 * n, :] * rk_ref[...]
        out_ref[...] = y_norm + jnp.sum(rkr, axis=0, keepdims=True) * v

    def emit_output(t):
        out = out_ref[...]
        y_ref[t] = jnp.concatenate([out[:, :half], out[:, half:]], axis=0).T

    @pl.when(pl.program_id(1) == 0)
    def _():
        s_ref[...] = s0_ref[...]

    out_ref[...] = jnp.zeros_like(out_ref)
    for c in range(n_fields):
        load_field(0, xa_ref, c)
    normalise_key(xa_ref)

    def step_pair(tp, carry):
        one_step(2 * tp, xa_ref, xb_ref)
        one_step(2 * tp + 1, xb_ref, xa_ref)
        return carry

    lax.fori_loop(0, steps // 2, step_pair, 0)
    emit_output(steps - 1)


def _rwkv_scan(pre5, s0, lnw, lnb, rk, tc, seqs):
    t, field_pairs, _, rows, _ = pre5.shape
    n = RW_HEAD_DIM
    pairs = RW_HEADS // 2
    x_rows = field_pairs * 2 * n
    lb = seqs * RW_HEADS
    assert tc % 2 == 0 and t % tc == 0
    tab = pl.BlockSpec((n, lb), lambda l, i: (0, 0))
    return pl.pallas_call(
        _rwkv_scan_body,
        grid=(rows // (seqs * pairs), t // tc),
        in_specs=[
            pl.BlockSpec((tc, field_pairs, 2, seqs * pairs, LANES), lambda l, i: (i, 0, 0, l, 0)),
            pl.BlockSpec((n, n, lb), lambda l, i: (0, 0, l)),
            tab, tab, tab,
        ],
        out_specs=[
            pl.BlockSpec((tc, seqs * pairs, LANES), lambda l, i: (i, l, 0)),
            pl.BlockSpec((n, n, lb), lambda l, i: (0, 0, l)),
        ],
        out_shape=[jax.ShapeDtypeStruct((t, rows, LANES), F32),
                   jax.ShapeDtypeStruct((n, n, rows * 2), F32)],
        scratch_shapes=[pltpu.VMEM((x_rows, lb), F32), pltpu.VMEM((x_rows, lb), F32),
                        pltpu.VMEM((n, lb), F32), pltpu.VMEM((n, lb), F32), pltpu.VMEM((n, lb), F32)],
        compiler_params=_cparams(("parallel", "arbitrary")),
        name="rwkv_scan",
    )(pre5, s0, lnw, lnb, rk)


GATHER_PAGES = 8


GATHER_WINDOW = 128


def _gather_rows(rows, idx):
    n = idx.shape[1]
    width = rows.shape[1]
    mesh = plsc.VectorSubcoreMesh(core_axis_name="core", subcore_axis_name="subcore")

    @pl.kernel(out_type=jax.ShapeDtypeStruct((n, width), rows.dtype), mesh=mesh)
    def gather_kernel(x_hbm, i_hbm, o_hbm):
        def body(i_vmem, o_vmem):
            pltpu.sync_copy(x_hbm.at[i_vmem.at[0]], o_vmem)

        pltpu.emit_pipeline(
            body,
            grid=(n // GATHER_WINDOW,),
            in_specs=[pl.BlockSpec((1, GATHER_WINDOW), index_map=lambda i: (0, i))],
            out_specs=[pl.BlockSpec((GATHER_WINDOW, width), index_map=lambda i: (i, 0))],
            core_axis_name=("core", "subcore"),
            dimension_semantics=(pltpu.PARALLEL,),
        )(i_hbm, o_hbm)

    return gather_kernel(rows, idx)


def _available_after_body(x_ref, anchor_ref, o_ref):
    o_ref[...] = x_ref[...]


def _available_after(x, anchor):
    return pl.pallas_call(_available_after_body, out_shape=jax.ShapeDtypeStruct(x.shape, x.dtype),
                          name="available_after")(x, anchor)


def _gather_selected(page_table, pool):
    bs, n_pages = page_table.shape
    n_pool, page, slabs, hd = pool.shape
    past_len = n_pages * page
    tok = jnp.arange(past_len, dtype=jnp.int32)
    row0 = (page_table[:, tok // page] * page + tok % page) * slabs
    idx = row0[:, None, :] + (slabs // 2 + jnp.arange(slabs // 2, dtype=jnp.int32))[None, :, None]
    out = _gather_rows(pool.reshape(n_pool * page * slabs, hd), idx.reshape(1, -1))
    return out.reshape(bs, slabs // 2, past_len, hd)


def _paged_kv_body(pt_ref, *refs):
    n_pages = len(refs) - 5
    page_refs = refs[:n_pages]
    pe_ref, w1_ref, w2_ref, kc_ref, out_ref = refs[n_pages:]
    _, rows, slabs, hd = page_refs[0].shape
    per_page = rows // CMP_BLOCK
    cols = []
    for r in range(CMP_BLOCK):
        tiles = [x_ref[0, nl * CMP_BLOCK + r] + pe_ref[r] for x_ref in page_refs for nl in range(per_page)]
        cols.append(jnp.concatenate(tiles, axis=0).astype(BF16))
    hid = jax.nn.gelu(_dot(jnp.concatenate(cols, axis=1), w1_ref[...]), approximate=True)
    out_ref[0] = _dot(hid[:, :CMP_HID].astype(BF16), w2_ref[0])
    out_ref[1] = _dot(hid[:, CMP_HID:].astype(BF16), w2_ref[1])
    n_blocks = n_pages * per_page
    for kg in range(slabs // 2):
        kc_ref[0, :, kg * hd:(kg + 1) * hd] = out_ref[kg // NSA_KV_HEADS, pl.ds(kg, n_blocks, stride=slabs), :]


def _paged_kv(page_table, pool, pe_tiles, w1cat, w2):
    bs, n_pages = page_table.shape
    _, page, slabs, hd = pool.shape
    half = slabs * hd // 2
    per_step = math.gcd(GATHER_PAGES, n_pages)
    blocks_step = per_step * page // CMP_BLOCK

    def page_spec(k):
        return pl.BlockSpec((1, page, slabs, hd), lambda b, p, pt: (pt[b, p * per_step + k], 0, 0, 0))

    held = lambda a: pl.BlockSpec(a.shape, lambda b, p, pt: (0,) * a.ndim)
    return pl.pallas_call(
        _paged_kv_body,
        grid_spec=pltpu.PrefetchScalarGridSpec(
            num_scalar_prefetch=1,
            grid=(bs, n_pages // per_step),
            in_specs=[page_spec(k) for k in range(per_step)] + [held(pe_tiles), held(w1cat), held(w2)],
            out_specs=pl.BlockSpec((1, blocks_step, half), lambda b, p, pt: (b, p, 0)),
            scratch_shapes=[pltpu.VMEM((2, blocks_step * slabs, hd), F32)],
        ),
        out_shape=jax.ShapeDtypeStruct((bs, n_pages * page // CMP_BLOCK, half), F32),
        compiler_params=_cparams(("parallel", "arbitrary")),
        name="paged_kv",
    )(page_table, *([pool] * per_step), pe_tiles, w1cat, w2)


def _cmp_mlp_body(x_ref, pe_ref, w1_ref, w2_ref, o_ref):
    hd = NSA_HEAD_DIM
    for kg in range(2 * NSA_KV_HEADS):
        ch = kg // NSA_KV_HEADS
        acc = None
        for r in range(CMP_BLOCK):
            xr = x_ref[:, r, kg * hd:(kg + 1) * hd]
            xr = (xr + pe_ref[r:r + 1, kg * hd:(kg + 1) * hd]).astype(BF16)
            part = _dot(xr, w1_ref[ch, r * hd:(r + 1) * hd, :])
            acc = part if acc is None else acc + part
        hid = jax.nn.gelu(acc, approximate=True)
        o_ref[:, kg * hd:(kg + 1) * hd] = _dot(hid.astype(BF16), w2_ref[ch])


def _cmp_mlp(blocks, pe_rows, w1, w2, nb, col_block):
    n_blocks, rows, _ = blocks.shape
    out_w = 2 * NSA_KV_DIM
    return pl.pallas_call(
        _cmp_mlp_body,
        grid=(n_blocks // nb,),
        in_specs=[
            pl.BlockSpec((nb, rows, out_w), lambda i: (i, 0, col_block)),
            pl.BlockSpec(pe_rows.shape, lambda i: (0, 0)),
            pl.BlockSpec(w1.shape, lambda i: (0, 0, 0)),
            pl.BlockSpec(w2.shape, lambda i: (0, 0, 0)),
        ],
        out_specs=pl.BlockSpec((nb, out_w), lambda i: (i, 0)),
        out_shape=jax.ShapeDtypeStruct((n_blocks, out_w), F32),
        compiler_params=_cparams(("parallel",)),
        name="cmp_mlp",
    )(blocks, pe_rows, w1, w2)


N_BIAS_TILES = -(-(FAR_DIST + LANES - 1) // LANES) + 1


def _bias_tiles_body(rb_ref, o_ref, cmp_ref):
    h = pl.program_id(0)
    key = lax.broadcasted_iota(jnp.int32, (LANES, Q_BLOCK), 0)
    qry = lax.broadcasted_iota(jnp.int32, (LANES, Q_BLOCK), 1)

    def bias_of(dist):
        return _bias_for_heads(dist, rb_ref, [h])[0] * LOG2_E

    for m in range(N_BIAS_TILES):
        o_ref[0, m] = bias_of(m * LANES + qry - key)
    for i in range(cmp_ref.shape[1]):
        cmp_ref[0, i] = bias_of(i * Q_BLOCK + qry - (key * CMP_BLOCK + CMP_BLOCK - 1))


def _bias_tiles(rel_bias, n_q_blocks):
    tile = (Q_BLOCK, LANES)
    return pl.pallas_call(
        _bias_tiles_body,
        grid=(NSA_HEADS,),
        in_specs=[pl.BlockSpec(memory_space=pltpu.SMEM)],
        out_specs=[pl.BlockSpec((1, N_BIAS_TILES) + tile, lambda h: (h, 0, 0, 0)),
                   pl.BlockSpec((1, n_q_blocks) + tile, lambda h: (h, 0, 0, 0))],
        out_shape=[jax.ShapeDtypeStruct((NSA_HEADS, N_BIAS_TILES) + tile, F32),
                   jax.ShapeDtypeStruct((NSA_HEADS, n_q_blocks) + tile, F32)],
        compiler_params=_cparams(("arbitrary",)),
        name="bias_tiles",
    )(rel_bias)


def _nsa_prompt_body(q_ref, gate_ref, sel_ref, win_ref, kc_ref, bt_ref, cb_ref, y_ref,
                     selk_ref, selvt_ref, wink_ref, winvt_ref, *, n_sel):
    hd = NSA_HEAD_DIM
    qb = Q_BLOCK
    i = pl.program_id(1)
    t_len = selk_ref.shape[0]

    @pl.when(i == 0)
    def _():
        for src_ref, k_ref, vt_ref in ((sel_ref, selk_ref, selvt_ref), (win_ref, wink_ref, winvt_ref)):
            k_ref[...] = src_ref[0, :, :NSA_KV_DIM].astype(BF16)
            for c in range(t_len // LANES):
                rows = slice(c * LANES, (c + 1) * LANES)
                vt_ref[:, rows] = src_ref[0, rows, NSA_KV_DIM:].T.astype(BF16)

    q_bf = (q_ref[0] * (hd ** -0.5 * LOG2_E)).astype(BF16)
    gates_t = jax.nn.sigmoid(gate_ref[0]).T
    iq = lax.broadcasted_iota(jnp.int32, (qb, LANES), 0)
    pos = i * qb + iq
    key = lax.broadcasted_iota(jnp.int32, (LANES, qb), 0)
    qry = lax.broadcasted_iota(jnp.int32, (LANES, qb), 1)
    vis_cmp = _stack_heads_t(i * qb + qry - (key * CMP_BLOCK + CMP_BLOCK - 1) >= 0)
    top_n = min(TOP_N, n_sel)
    sel_tiles = SEL_TILES if t_len % (SEL_TILES * LANES) == 0 else 1
    sel_keys = sel_tiles * LANES
    win_tiles = min(WINDOW // LANES + 1, t_len // LANES)
    win_keys = win_tiles * LANES
    groups = [[g * NSA_HPG + h for h in range(NSA_HPG)] for g in range(NSA_KV_HEADS)]

    def tile_bias(heads, first_tile, n_tiles):
        tiles = []
        for k in range(n_tiles):
            m = jnp.clip(i - (first_tile + k), 0, N_BIAS_TILES - 1)
            tiles.append(jnp.concatenate([bt_ref[hh, m] for hh in heads], axis=1))
        return tiles[0] if n_tiles == 1 else jnp.concatenate(tiles, axis=0)

    def selection_mask(chosen_t, tile, tile_keys):
        n_idx = lax.broadcasted_iota(jnp.int32, (tile_keys, LANES), 1)
        k_idx = lax.broadcasted_iota(jnp.int32, (tile_keys, LANES), 0)
        expand = jnp.where(n_idx == 2 * ((tile * tile_keys + k_idx) >> SEL_SHIFT), 1.0, 0.0).astype(BF16)
        return _dot(expand, chosen_t) > 0.5

    qgs, o_cmps, values = [], [], []
    for g, heads in enumerate(groups):
        qg = jnp.concatenate([q_bf[:, hh * hd:(hh + 1) * hd] for hh in heads], axis=0)
        kc_k = kc_ref[0, :, g * hd:(g + 1) * hd].astype(BF16)
        kc_vt = kc_ref[0, :, NSA_KV_DIM + g * hd:NSA_KV_DIM + (g + 1) * hd].T.astype(BF16)
        s = _dot_nt(kc_k, qg) + jnp.concatenate([cb_ref[hh, 0] for hh in heads], axis=1)
        s = jnp.where(vis_cmp, s, NEG_BIG)
        e = jnp.where(vis_cmp, jnp.exp2(s - jnp.max(s, axis=0, keepdims=True)), 0.0)
        p = e / jnp.maximum(jnp.sum(e, axis=0, keepdims=True), 1e-30)
        o_cmps.append(_dot(kc_vt, p.astype(BF16)))
        imp_t = p[:, 0:qb]
        for h in range(1, NSA_HPG):
            imp_t = imp_t + p[:, h * qb:(h + 1) * qb]
        qgs.append(qg)
        values.append(_selection_values_of(imp_t.T, pos, n_sel))
    chosen = _topk_mask(jnp.concatenate(values, axis=0), top_n)
    chosens = [chosen[g * qb:(g + 1) * qb].T.astype(BF16) for g in range(NSA_KV_HEADS)]

    key_s = lax.broadcasted_iota(jnp.int32, (sel_keys, qb), 0)
    qry_s = lax.broadcasted_iota(jnp.int32, (sel_keys, qb), 1)

    def sel_step(j, carries):
        row0 = pl.multiple_of(j * sel_keys, sel_keys)
        causal = i * qb + qry_s - (j * sel_keys + key_s) >= 0
        out = []
        for g, heads in enumerate(groups):
            mask = _stack_heads_t(selection_mask(chosens[g], j, sel_keys) & causal)
            out.append(_softmax_update_t(carries[g], qgs[g], selk_ref[pl.ds(row0, sel_keys), g * hd:(g + 1) * hd],
                                         selvt_ref[g * hd:(g + 1) * hd, pl.ds(row0, sel_keys)],
                                         tile_bias(heads, j * sel_tiles, sel_tiles), mask))
        return tuple(out)

    n_steps = (i + sel_tiles) // sel_tiles
    sel_carries = lax.fori_loop(0, n_steps, sel_step, tuple(_softmax_init_t(NSA_HPG * qb) for _ in groups))

    first = jnp.maximum(i - (win_tiles - 1), 0)
    row0 = pl.multiple_of(first * LANES, LANES)
    d_win = (i * qb + lax.broadcasted_iota(jnp.int32, (win_keys, qb), 1)
             - (first * LANES + lax.broadcasted_iota(jnp.int32, (win_keys, qb), 0)))
    mask_win = _stack_heads_t((d_win >= 0) & (d_win < WINDOW))
    for g, heads in enumerate(groups):
        o_sel = _softmax_finish(sel_carries[g])
        o_win = _softmax_finish(_softmax_update_t(
            _softmax_init_t(NSA_HPG * qb), qgs[g], wink_ref[pl.ds(row0, win_keys), g * hd:(g + 1) * hd],
            winvt_ref[g * hd:(g + 1) * hd, pl.ds(row0, win_keys)], tile_bias(heads, first, win_tiles),
            mask_win))
        for h, hh in enumerate(heads):
            cols = slice(h * qb, (h + 1) * qb)
            out_t = (gates_t[hh:hh + 1, :] * o_cmps[g][:, cols]
                     + gates_t[NSA_HEADS + hh:NSA_HEADS + hh + 1, :] * o_sel[:, cols]
                     + gates_t[2 * NSA_HEADS + hh:2 * NSA_HEADS + hh + 1, :] * o_win[:, cols])
            y_ref[0, :, hh * hd:(hh + 1) * hd] = out_t.T.astype(BF16)


def _nsa_prompt(pb3, kc, bias_tiles, cmp_bias):
    b, t, _ = pb3.shape
    qb = Q_BLOCK
    half = 2 * NSA_KV_DIM
    return pl.pallas_call(
        functools.partial(_nsa_prompt_body, n_sel=t // SEL_BLOCK),
        grid=(b, t // qb),
        in_specs=[
            pl.BlockSpec((1, qb, NSA_DIM), lambda s, i: (s, i, PB_Q // NSA_DIM)),
            pl.BlockSpec((1, qb, LANES), lambda s, i: (s, i, PB_GATE // LANES)),
            pl.BlockSpec((1, t, half), lambda s, i: (s, 0, PB_SEL // half)),
            pl.BlockSpec((1, t, half), lambda s, i: (s, 0, PB_WIN // half)),
            pl.BlockSpec((1,) + kc.shape[1:], lambda s, i: (s, 0, 0)),
            pl.BlockSpec(bias_tiles.shape, lambda s, i: (0, 0, 0, 0)),
            pl.BlockSpec((NSA_HEADS, 1, qb, LANES), lambda s, i: (0, i, 0, 0)),
        ],
        out_specs=pl.BlockSpec((1, qb, NSA_DIM), lambda s, i: (s, i, 0)),
        out_shape=jax.ShapeDtypeStruct((b, t, NSA_DIM), BF16),
        scratch_shapes=[pltpu.VMEM((t, NSA_KV_DIM), BF16), pltpu.VMEM((NSA_KV_DIM, t), BF16),
                        pltpu.VMEM((t, NSA_KV_DIM), BF16), pltpu.VMEM((NSA_KV_DIM, t), BF16)],
        compiler_params=_cparams(("parallel", "arbitrary")),
        name="nsa_prompt",
    )(pb3, pb3, pb3, pb3, kc, bias_tiles, cmp_bias)


SEL_TILES = 2
SAMPLE_ROWS = 8
SAMPLE_TILE_KEYS = 1024


def _nsa_sample_body(rb_ref, q_ref, gate_ref, kc_ref, past_ref, new_sel_ref, cwin_ref, new_win_ref, y_ref,
                     *, past_len):
    hd = NSA_HEAD_DIM
    qb = SAMPLE_ROWS
    scale = hd ** -0.5
    n_cmp = kc_ref.shape[1]
    win_rows = cwin_ref.shape[1]
    k_off = past_len - win_rows
    tile_keys = min(SAMPLE_TILE_KEYS, past_len)
    assert past_len % tile_keys == 0

    q_bf = q_ref[0].astype(BF16)
    gates = jax.nn.sigmoid(gate_ref[0])
    iq = lax.broadcasted_iota(jnp.int32, (qb, LANES), 0)
    ik = lax.broadcasted_iota(jnp.int32, (qb, LANES), 1)
    pos_t = past_len + lax.broadcasted_iota(jnp.int32, (qb, tile_keys), 0)
    ik_t = lax.broadcasted_iota(jnp.int32, (qb, tile_keys), 1)
    d_win = (past_len + lax.broadcasted_iota(jnp.int32, (qb, win_rows), 0)
             - (k_off + lax.broadcasted_iota(jnp.int32, (qb, win_rows), 1)))
    tq_c = lax.broadcasted_iota(jnp.int32, (qb, n_cmp), 0)
    n_c = lax.broadcasted_iota(jnp.int32, (qb, n_cmp), 1)
    pos_c = past_len + tq_c
    d_cmp = pos_c - (n_c * CMP_BLOCK + CMP_BLOCK - 1)
    bias_cmp = _bias_for_heads(d_cmp, rb_ref, range(NSA_HEADS))
    vis_cmp = _stack_heads(d_cmp >= 0)
    n_sel_past = past_len // SEL_BLOCK
    picks = min(TOP_N, n_sel_past + 1) - 1

    groups = [[g * NSA_HPG + h for h in range(NSA_HPG)] for g in range(NSA_KV_HEADS)]

    def key_cols(g):
        return slice(g * hd, (g + 1) * hd), slice(NSA_KV_DIM + g * hd, NSA_KV_DIM + (g + 1) * hd)

    qgs, o_cmps, values = [], [], []
    for g, heads in enumerate(groups):
        k_cols, v_cols = key_cols(g)
        qg = jnp.concatenate([q_bf[:, hh * hd:(hh + 1) * hd] for hh in heads], axis=0)
        o_cmp, p_cmp = _cmp_branch(qg, kc_ref[0, :, k_cols].astype(BF16), kc_ref[0, :, v_cols].astype(BF16),
                                   jnp.concatenate([bias_cmp[hh] for hh in heads], axis=0), vis_cmp, scale)
        qgs.append(qg)
        o_cmps.append(o_cmp)
        values.append(_selection_values(p_cmp, qb, pos_c, n_sel_past))
    chosen = _topk_mask(jnp.concatenate(values, axis=0), picks).astype(BF16)
    chosens = [chosen[g * qb:(g + 1) * qb] for g in range(NSA_KV_HEADS)]

    def sel_step(j, carries):
        row0 = pl.multiple_of(j * tile_keys, tile_keys)
        dist = pos_t - (j * tile_keys + ik_t)
        out = []
        for g, heads in enumerate(groups):
            k_cols, v_cols = key_cols(g)
            mask = _stack_heads(_expand_selection(chosens[g], j, tile_keys))
            bias = jnp.concatenate(_bias_for_heads(dist, rb_ref, heads), axis=0)
            kt = past_ref[0, g, pl.ds(row0, tile_keys), :].astype(BF16)
            vt = past_ref[0, NSA_KV_HEADS + g, pl.ds(row0, tile_keys), :].astype(BF16)
            out.append(_softmax_update(carries[g], qgs[g], kt, vt, bias, mask, scale))
        return tuple(out)

    sel_carries = lax.fori_loop(0, past_len // tile_keys, sel_step,
                                tuple(_softmax_init(NSA_HPG * qb) for _ in groups))
    d_new = iq - ik
    mask_new = _stack_heads(d_new >= 0)
    mask_win = _stack_heads((d_win >= 0) & (d_win < WINDOW))
    for g, heads in enumerate(groups):
        k_cols, v_cols = key_cols(g)
        qg, o_cmp = qgs[g], o_cmps[g]
        bias_new = jnp.concatenate(_bias_for_heads(d_new, rb_ref, heads), axis=0)
        o_sel = _softmax_finish(_softmax_update(sel_carries[g], qg, new_sel_ref[0, :, k_cols],
                                                new_sel_ref[0, :, v_cols], bias_new, mask_new, scale))
        bias_win = jnp.concatenate(_bias_for_heads(d_win, rb_ref, heads), axis=0)
        carry = _softmax_update(_softmax_init(NSA_HPG * qb), qg, cwin_ref[0, :, k_cols].astype(BF16),
                                cwin_ref[0, :, v_cols].astype(BF16), bias_win, mask_win, scale)
        o_win = _softmax_finish(_softmax_update(carry, qg, new_win_ref[0, :, k_cols], new_win_ref[0, :, v_cols],
                                                bias_new, mask_new, scale))

        for h, hh in enumerate(heads):
            rows = slice(h * qb, (h + 1) * qb)
            out = (gates[:, hh:hh + 1] * o_cmp[rows]
                   + gates[:, NSA_HEADS + hh:NSA_HEADS + hh + 1] * o_sel[rows]
                   + gates[:, 2 * NSA_HEADS + hh:2 * NSA_HEADS + hh + 1] * o_win[rows])
            y_ref[0, :, hh * hd:(hh + 1) * hd] = out.astype(BF16)


def _nsa_sample(rel_bias, q8, gate8, kc, past_sel, new_sel, cwin, new_win, past_len):
    bs = q8.shape[0]
    blk = lambda a: pl.BlockSpec((1,) + a.shape[1:], lambda s: (s,) + (0,) * (a.ndim - 1))
    return pl.pallas_call(
        functools.partial(_nsa_sample_body, past_len=past_len),
        grid=(bs,),
        in_specs=[pl.BlockSpec(memory_space=pltpu.SMEM), blk(q8), blk(gate8), blk(kc), blk(past_sel),
                  blk(new_sel), blk(cwin), blk(new_win)],
        out_specs=pl.BlockSpec((1, SAMPLE_ROWS, NSA_DIM), lambda s: (s, 0, 0)),
        out_shape=jax.ShapeDtypeStruct((bs, SAMPLE_ROWS, NSA_DIM), BF16),
        compiler_params=_cparams(("parallel",)),
        name="nsa_sample",
    )(rel_bias, q8, gate8, kc, past_sel, new_sel, cwin, new_win)


def _merge_body(x_ref, yrw_ref, g_ref, ynsa_ref, gates_ref, wrw_ref, wnsa_ref, wout_ref, post_ref, o_ref):
    tm, d = x_ref.shape
    y_rw = jnp.concatenate([yrw_ref[:, pair, :] for pair in range(RW_DIM // LANES)], axis=1)
    y_rw = (y_rw * g_ref[...]).astype(BF16)
    merged = (jax.nn.sigmoid(gates_ref[:, :d]) * _dot(y_rw, wrw_ref[...])
              + jax.nn.sigmoid(gates_ref[:, d:]) * _dot(ynsa_ref[...], wnsa_ref[...]))
    o_ref[...] = x_ref[...] + _rms(_dot(merged.astype(BF16), wout_ref[...]), post_ref[...])


def _merge(x, y_rw, g_rw, y_nsa, gates, w_rw, w_nsa, w_out, post_g, tm):
    m, d = x.shape
    rows = lambda n: pl.BlockSpec((tm, n), lambda i: (i, 0))
    held = lambda a: pl.BlockSpec(a.shape, lambda i: (0, 0), pipeline_mode=pl.Buffered(1))
    return pl.pallas_call(
        _merge_body,
        grid=(m // tm,),
        in_specs=[rows(d), pl.BlockSpec((tm,) + y_rw.shape[1:], lambda i: (i, 0, 0)),
                  rows(RW_DIM), rows(NSA_DIM), rows(2 * d),
                  held(w_rw), held(w_nsa), held(w_out), pl.BlockSpec((1, d), lambda i: (0, 0))],
        out_specs=rows(d),
        out_shape=jax.ShapeDtypeStruct((m, d), F32),
        compiler_params=_cparams(("parallel",)),
        name="merge",
    )(x, y_rw, g_rw, y_nsa, gates, w_rw, w_nsa, w_out, post_g.reshape(1, d))


def _row_tile(m, want):
    return want if m % want == 0 else m


def _rwkv_inputs(p3, shift0, w):
    b, t, c = p3.shape
    tt = 256 if t % 256 == 0 else t
    prev = jnp.concatenate([shift0[:, None, :], p3[:, tt - 1:t - 1:tt]], axis=1).reshape(b, t // tt, 1, c)
    pre, g_rw = _rwkv_pre(p3, prev, w["mu"], w["w0"], w["ww"], w["a0"], w["wa"], w["g2"], w["k_k"], w["k_a"], tt)
    pairs = RW_HEADS // 2
    if tt % SUBLANES == 0:
        n_pieces = SCAN_FIELDS * pairs
        tok = (jnp.arange(b, dtype=jnp.int32)[None, :] * t + jnp.arange(t, dtype=jnp.int32)[:, None])
        row0 = (tok // SUBLANES) * (n_pieces * SUBLANES) + tok % SUBLANES
        piece = jnp.arange(n_pieces, dtype=jnp.int32).reshape(SCAN_FIELDS // 2, 2, 1, pairs)
        idx = row0[:, None, None, :, None] + piece[None] * SUBLANES
        pre = _gather_rows(pre.reshape(-1, LANES), idx.reshape(1, -1)).reshape(
            t, SCAN_FIELDS // 2, 2, b * pairs, LANES)
    return pre, g_rw.reshape(b * t, RW_DIM)


def _rwkv_recurrence(pre, s0, w, run_after=None):
    t = pre.shape[0]
    b = s0.shape[0]
    n = RW_HEAD_DIM
    seqs = min(b, LANES // RW_HEADS)
    groups = b // seqs
    pairs = RW_HEADS // 2
    s0_t = s0.reshape(groups, seqs, pairs, 2, n, n).transpose(5, 4, 0, 3, 1, 2).reshape(n, n, b * RW_HEADS)
    if run_after is not None:
        s0_t = _available_after(s0_t, run_after)
    per_lane = lambda v: jnp.broadcast_to(v.reshape(pairs, 2, n).transpose(2, 1, 0)[:, :, None, :],
                                          (n, 2, seqs, pairs)).reshape(n, seqs * RW_HEADS)
    tc = 32 if t % 32 == 0 else t
    y_t, s_t = _rwkv_scan(pre, s0_t, per_lane(w["lnx_w"]), per_lane(w["lnx_b"]), per_lane(w["r_k"]), tc, seqs)
    s_fin = s_t.reshape(n, n, groups, 2, seqs, pairs).transpose(2, 4, 5, 3, 1, 0).reshape(b, RW_HEADS, n, n)
    y_tok = y_t.reshape(t, b, pairs, LANES).transpose(1, 0, 2, 3).reshape(b * t, pairs, LANES)
    return y_tok, s_fin


def _layer(x, w, rel_bias, shift0, s0, past):
    b, t, d = x.shape
    m = b * t
    tm = _row_tile(m, 512)
    x1 = _ffn(x.reshape(m, d), w["f1_pre"], w["f1_post"], w["f1_w1"], w["f1_w3"], w["f1_w2"], tm, 512)
    p_a = _norm_matmul(x1, w["mix_pre"], w["w_in_a"], tm, "proj_rwkv")
    p_b = _norm_matmul(x1, w["mix_pre"], w["w_in_b"], tm, "proj_nsa")
    p_c = _norm_matmul(x1, w["mix_pre"], w["w_in_c"], tm, "proj_gates")

    p3 = p_a.reshape(b, t, RW_PROJ)
    pre, g_rw = _rwkv_inputs(p3, shift0, w)

    pb3 = p_b.reshape(b, t, PB_COLS)
    kv_new = pb3[:, :, PB_KV:PB_WIN]
    win_new = pb3[:, :, PB_WIN:PB_GATE]
    row_w = 4 * NSA_KV_DIM
    half = 2 * NSA_KV_DIM
    if past is None:
        blocks = p_b.reshape(m // CMP_BLOCK, CMP_BLOCK, PB_COLS)
        kc = _cmp_mlp(blocks, w["cmp_pe"], w["cmp_w1"], w["cmp_w2"], min(64, blocks.shape[0]), PB_KV // half)
        n_cmp = t // CMP_BLOCK
        kc = jnp.pad(kc.reshape(b, n_cmp, 2 * NSA_KV_DIM), ((0, 0), (0, -n_cmp % LANES), (0, 0)))
        y_nsa = _nsa_prompt(pb3, kc, *_bias_tiles(rel_bias, t // Q_BLOCK)).reshape(m, NSA_DIM)
        win_out = win_new[:, t - min(WINDOW, t):]
        y_rw, s_fin = _rwkv_recurrence(pre, s0, w, run_after=y_nsa[:SUBLANES])
    else:
        page_table, pool, cache_win, start_after = past
        past_len = page_table.shape[1] * pool.shape[1]
        slabs = row_w // NSA_HEAD_DIM
        pool4 = pool.reshape(pool.shape[0], pool.shape[1], slabs, NSA_HEAD_DIM)
        kc = _paged_kv(page_table, pool4, w["cmp_pe_tiles"], w["cmp_w1cat"], w["cmp_w2"])
        y_rw, s_fin = _rwkv_recurrence(pre, s0, w)
        past_sel = _gather_selected(_available_after(page_table, start_after), pool4)
        n_cmp = past_len // CMP_BLOCK
        kc = jnp.pad(kc, ((0, 0), (0, -n_cmp % LANES), (0, 0)))
        pad_q = lambda a, rows: jnp.pad(a, ((0, 0), (0, rows - t), (0, 0)))
        cwin = cache_win.reshape(b, cache_win.shape[1], half)
        y8 = _nsa_sample(rel_bias, pad_q(pb3[:, :, :NSA_DIM], SAMPLE_ROWS), pad_q(pb3[:, :, PB_GATE:], SAMPLE_ROWS),
                         kc, past_sel, pad_q(pb3[:, :, PB_SEL:PB_WIN].astype(BF16), LANES), cwin,
                         pad_q(win_new.astype(BF16), LANES), past_len)
        y_nsa = y8[:, :t].reshape(m, NSA_DIM)
        win_all = jnp.concatenate([cwin, win_new], axis=1)
        win_out = win_all[:, win_all.shape[1] - min(WINDOW, win_all.shape[1]):]

    x2 = _merge(x1, y_rw, g_rw, y_nsa, p_c, w["w_br_rw"], w["w_br_nsa"], w["w_out"], w["mix_post"],
                _row_tile(m, 256))
    y = _ffn(x2, w["f2_pre"], w["f2_post"], w["f2_w1"], w["f2_w3"], w["f2_w2"], tm, 512)
    g, hd = NSA_KV_HEADS, NSA_HEAD_DIM
    return (y.reshape(b, t, d), kv_new.reshape(b, t, 4, g, hd), win_out.reshape(b, -1, 2, g, hd), s_fin, p3[:, -1],
            y_nsa[:SUBLANES])


def _prepare_weights(l, ffn1_pre_g, ffn1_post_g, ffn1_w1, ffn1_w3, ffn1_w2, mix_pre_g, mix_post_g, w_in,
                     rw_mu, rw_w0, rw_w2, rw_a0, rw_a2, rw_g2, rw_k_k, rw_k_a, rw_r_k, rw_lnx_w, rw_lnx_b,
                     cmp_pe, cmp_w1, cmp_w2, w_br_rw, w_br_nsa, w_out,
                     ffn2_pre_g, ffn2_post_g, ffn2_w1, ffn2_w3, ffn2_w2):
    d = w_in.shape[1]
    wi = w_in[l]
    c_q = RW_PROJ
    c_kv = c_q + NSA_DIM
    c_gate = c_kv + 6 * NSA_KV_DIM
    c_grw = c_gate + 3 * NSA_HEADS
    w_in_b = jnp.concatenate([wi[:, c_q:c_gate], jnp.pad(wi[:, c_gate:c_grw], ((0, 0), (0, LANES - 3 * NSA_HEADS)))],
                             axis=1)
    zeros_w = jnp.zeros((RW_LORA_A, RW_DIM), F32)
    zeros_a = jnp.zeros((RW_LORA_W, RW_DIM), F32)
    pe = cmp_pe[l]
    pe_rows = jnp.broadcast_to(pe[:, :, None, :], (CMP_BLOCK, 2, NSA_KV_HEADS, NSA_HEAD_DIM))
    return {
        "f1_pre": ffn1_pre_g[l], "f1_post": ffn1_post_g[l],
        "f1_w1": ffn1_w1[l].astype(BF16), "f1_w3": ffn1_w3[l].astype(BF16), "f1_w2": ffn1_w2[l].astype(BF16),
        "mix_pre": mix_pre_g[l], "mix_post": mix_post_g[l],
        "w_in_a": wi[:, :RW_PROJ].astype(BF16), "w_in_b": w_in_b.astype(BF16), "w_in_c": wi[:, c_grw:].astype(BF16),
        "mu": rw_mu[l], "w0": rw_w0[l], "a0": rw_a0[l],
        "ww": jnp.concatenate([rw_w2[l], zeros_w], axis=0).astype(BF16),
        "wa": jnp.concatenate([zeros_a, rw_a2[l]], axis=0).astype(BF16),
        "g2": rw_g2[l].astype(BF16), "k_k": rw_k_k[l], "k_a": rw_k_a[l],
        "r_k": rw_r_k[l].reshape(-1), "lnx_w": rw_lnx_w[l], "lnx_b": rw_lnx_b[l],
        "cmp_pe": pe_rows.reshape(CMP_BLOCK, 2 * NSA_KV_DIM),
        "cmp_pe_tiles": jnp.pad(pe_rows.reshape(CMP_BLOCK, 2 * NSA_KV_HEADS, NSA_HEAD_DIM),
                                ((0, 0), (0, 2 * NSA_KV_HEADS), (0, 0))),
        "cmp_w1": cmp_w1[l].astype(BF16), "cmp_w2": cmp_w2[l].astype(BF16),
        "cmp_w1cat": jnp.concatenate([cmp_w1[l, 0], cmp_w1[l, 1]], axis=1).astype(BF16),
        "w_br_rw": w_br_rw[l].astype(BF16), "w_br_nsa": w_br_nsa[l].astype(BF16), "w_out": w_out[l].astype(BF16),
        "f2_pre": ffn2_pre_g[l], "f2_post": ffn2_post_g[l],
        "f2_w1": ffn2_w1[l].astype(BF16), "f2_w3": ffn2_w3[l].astype(BF16), "f2_w2": ffn2_w2[l].astype(BF16),
    }


def kernel(x_prompt, x_sample, cache_kv, cache_win, state_rwkv, state_shift, page_table,
           ffn1_pre_g, ffn1_post_g, ffn1_w1, ffn1_w3, ffn1_w2, mix_pre_g, mix_post_g, w_in,
           rw_mu, rw_w0, rw_w2, rw_a0, rw_a2, rw_g2, rw_k_k, rw_k_a, rw_r_k, rw_lnx_w, rw_lnx_b,
           cmp_pe, cmp_w1, cmp_w2, w_br_rw, w_br_nsa, w_out,
           ffn2_pre_g, ffn2_post_g, ffn2_w1, ffn2_w3, ffn2_w2, rel_bias):
    depth = w_in.shape[0]
    b_p = x_prompt.shape[0]
    y_p, y_s = x_prompt, x_sample
    outs = [[] for _ in range(8)]
    for l in range(depth):
        w = _prepare_weights(l, ffn1_pre_g, ffn1_post_g, ffn1_w1, ffn1_w3, ffn1_w2, mix_pre_g, mix_post_g, w_in,
                             rw_mu, rw_w0, rw_w2, rw_a0, rw_a2, rw_g2, rw_k_k, rw_k_a, rw_r_k, rw_lnx_w, rw_lnx_b,
                             cmp_pe, cmp_w1, cmp_w2, w_br_rw, w_br_nsa, w_out,
                             ffn2_pre_g, ffn2_post_g, ffn2_w1, ffn2_w3, ffn2_w2)
        y_p, kv_p, win_p, rw_p, sh_p, regrouped = _layer(
            y_p, w, rel_bias, jnp.zeros((b_p, RW_PROJ), F32),
            jnp.zeros((b_p, RW_HEADS, RW_HEAD_DIM, RW_HEAD_DIM), F32), None)
        y_s, kv_s, win_s, rw_s, sh_s, _ = _layer(
            y_s, w, rel_bias, state_shift[l], state_rwkv[l], (page_table, cache_kv[l], cache_win[l], regrouped))
        for acc, v in zip(outs, (kv_p, kv_s, win_p, win_s, rw_p, rw_s, sh_p, sh_s)):
            acc.append(v)
    return (y_p, y_s) + tuple(jnp.stack(o) for o in outs)
```

```python
import functools
import math

import jax
import jax.numpy as jnp
import numpy as np
from jax import lax
from jax.experimental import pallas as pl
from jax.experimental.pallas import tpu as pltpu
from jax.experimental.pallas import tpu_sc as plsc

F32 = jnp.float32
BF16 = jnp.bfloat16

RMS_EPS = 1e-6
RW_HEADS = 16
RW_HEAD_DIM = 64
RW_DIM = RW_HEADS * RW_HEAD_DIM
RW_LORA_W = 64
RW_LORA_A = 64
RW_LORA_G = 128
RW_PROJ = 3 * RW_DIM + RW_LORA_W + RW_LORA_A + RW_LORA_G
RW_LN_EPS = 64e-5
NSA_HEADS = 8
NSA_KV_HEADS = 2
NSA_HPG = NSA_HEADS // NSA_KV_HEADS
NSA_HEAD_DIM = 128
NSA_DIM = NSA_HEADS * NSA_HEAD_DIM
NSA_KV_DIM = NSA_KV_HEADS * NSA_HEAD_DIM
CMP_BLOCK = 32
CMP_HID = 256
SEL_BLOCK = 64
SEL_SHIFT = SEL_BLOCK.bit_length() - 1
TOP_N = 16
WINDOW = 512
N_BUCKETS = 32
REL_MAX_EXACT = 16
REL_MAX_DIST = 1024
Q_BLOCK = 128
NEG_BIG = -1e30
FORCE_BONUS = 1e4
NEVER = -3e38
LOG2_E = math.log2(math.e)
LANES = 128
SUBLANES = 8
SCAN_FIELDS = 6
VMEM_LIMIT = 56 * 1024 * 1024
SCAN_UNROLL = 16
UPDATE_UNROLL = 16

PB_Q = 0
PB_KV = NSA_DIM
PB_SEL = PB_KV + 2 * NSA_KV_DIM
PB_WIN = PB_KV + 4 * NSA_KV_DIM
PB_GATE = PB_WIN + 2 * NSA_KV_DIM
PB_COLS = PB_GATE + LANES


def _bucket_thresholds():
    thr = list(range(1, REL_MAX_EXACT + 1))
    n_log = N_BUCKETS - REL_MAX_EXACT
    ratio = REL_MAX_DIST // REL_MAX_EXACT
    n = REL_MAX_EXACT
    for k in range(1, n_log):
        while n ** n_log < REL_MAX_EXACT ** n_log * ratio ** k:
            n += 1
        thr.append(n)
    return thr


BUCKET_THR = _bucket_thresholds()
FAR_DIST = BUCKET_THR[-1]


def _cparams(sem):
    return pltpu.CompilerParams(dimension_semantics=sem, vmem_limit_bytes=VMEM_LIMIT)


def _rms(x, g):
    ms = jnp.mean(x * x, axis=-1, keepdims=True)
    return x * lax.rsqrt(ms + RMS_EPS) * g


def _dot(a, b):
    return jnp.dot(a, b, preferred_element_type=F32)


def _dot_nt(a, b):
    return lax.dot_general(a, b, (((1,), (1,)), ((), ())), preferred_element_type=F32)


def _bias_for_heads(dist, rb_ref, heads):
    n = jnp.maximum(dist, 0)
    reach = [n >= t for t in BUCKET_THR]
    out = []
    for h in heads:
        val = jnp.full(dist.shape, rb_ref[0, h], F32)
        for b, m in enumerate(reach):
            val = jnp.where(m, rb_ref[b + 1, h], val)
        out.append(val)
    return out


def _softmax_update(carry, q, kt, vt, bias, mask, scale):
    m_run, l_run, acc = carry
    s = _dot_nt(q, kt) * scale + bias
    s = jnp.where(mask, s, NEG_BIG)
    m_new = jnp.maximum(m_run, jnp.max(s, axis=-1, keepdims=True))
    alpha = jnp.exp(m_run - m_new)
    e = jnp.where(mask, jnp.exp(s - m_new), 0.0)
    l_new = alpha * l_run + jnp.sum(e, axis=-1, keepdims=True)
    acc = alpha * acc + _dot(e.astype(BF16), vt)
    return m_new, l_new, acc


def _softmax_init(rows):
    return (jnp.full((rows, 1), NEG_BIG, F32), jnp.zeros((rows, 1), F32),
            jnp.zeros((rows, NSA_HEAD_DIM), F32))


def _softmax_finish(carry):
    _, l_run, acc = carry
    return acc / jnp.maximum(l_run, 1e-30)


def _softmax_update_t(carry, q, kt, v_t, bias_t, mask_t):
    m_run, l_run, acc = carry
    s = _dot_nt(kt, q) + bias_t
    s = jnp.where(mask_t, s, NEG_BIG)
    m_new = jnp.maximum(m_run, jnp.max(s, axis=0, keepdims=True))
    alpha = jnp.exp2(m_run - m_new)
    e = jnp.where(mask_t, jnp.exp2(s - m_new), 0.0)
    l_new = alpha * l_run + jnp.sum(e, axis=0, keepdims=True)
    acc = alpha * acc + _dot(v_t, e.astype(BF16))
    return m_new, l_new, acc


def _softmax_init_t(rows):
    return (jnp.full((1, rows), NEG_BIG, F32), jnp.zeros((1, rows), F32),
            jnp.zeros((NSA_HEAD_DIM, rows), F32))


def _stack_heads(x):
    return jnp.concatenate([x] * NSA_HPG, axis=0)


def _stack_heads_t(x):
    return jnp.concatenate([x] * NSA_HPG, axis=1)


def _topk_mask(val, k):
    lane = lax.broadcasted_iota(jnp.int32, val.shape, 1)
    chosen = jnp.zeros(val.shape, F32)
    for _ in range(k):
        hit = lane == jnp.argmax(val, axis=-1, keepdims=True)
        chosen = jnp.where(hit, 1.0, chosen)
        val = jnp.where(hit, NEVER, val)
    return chosen


def _pair_sum(x):
    parts = []
    for c in range(x.shape[1] // LANES):
        blk = x[:, c * LANES:(c + 1) * LANES]
        parts.append(blk + pltpu.roll(blk, LANES - 1, 1))
    return parts[0] if len(parts) == 1 else jnp.concatenate(parts, axis=1)


def _cmp_branch(qg, kc_k, kc_v, bias, visible, scale):
    s = _dot_nt(qg, kc_k) * scale + bias
    s = jnp.where(visible, s, NEG_BIG)
    e = jnp.where(visible, jnp.exp(s - jnp.max(s, axis=-1, keepdims=True)), 0.0)
    p = e / jnp.maximum(jnp.sum(e, axis=-1, keepdims=True), 1e-30)
    return _dot(p.astype(BF16), kc_v), p


def _selection_values(p, qb, pos, n_sel):
    imp = p[0:qb]
    for h in range(1, NSA_HPG):
        imp = imp + p[h * qb:(h + 1) * qb]
    return _selection_values_of(imp, pos, n_sel)


def _selection_values_of(imp, pos, n_sel):
    imp = _pair_sum(imp)
    lane = lax.broadcasted_iota(jnp.int32, imp.shape, 1)
    sel_id = lane >> 1
    cur = pos >> SEL_SHIFT
    forced = (sel_id == 0) | (sel_id == cur) | (sel_id == cur - 1)
    val = jnp.where(forced, imp + FORCE_BONUS, imp)
    val = jnp.where(sel_id * SEL_BLOCK <= pos, val, NEG_BIG)
    return jnp.where(((lane & 1) == 0) & (sel_id < n_sel), val, NEVER)


def _expand_selection(chosen_bf16, tile, tile_keys=LANES):
    width = chosen_bf16.shape[1]
    n_idx = lax.broadcasted_iota(jnp.int32, (width, tile_keys), 0)
    k_idx = lax.broadcasted_iota(jnp.int32, (width, tile_keys), 1)
    target = 2 * ((tile * tile_keys + k_idx) >> SEL_SHIFT)
    expand = jnp.where(n_idx == target, 1.0, 0.0).astype(BF16)
    return _dot(chosen_bf16, expand) > 0.5


def _ffn_body(x_ref, pre_ref, post_ref, w1_ref, w3_ref, w2_ref, o_ref, h_ref, acc_ref):
    j = pl.program_id(1)

    @pl.when(j == 0)
    def _():
        h_ref[...] = _rms(x_ref[...], pre_ref[...]).astype(BF16)
        acc_ref[...] = jnp.zeros_like(acc_ref)

    h = h_ref[...]
    a = _dot(h, w1_ref[...])
    b = _dot(h, w3_ref[...])
    u = (a * jax.nn.sigmoid(a)) * b
    acc_ref[...] += _dot(u.astype(BF16), w2_ref[...])

    @pl.when(j == pl.num_programs(1) - 1)
    def _():
        o_ref[...] = x_ref[...] + 0.5 * _rms(acc_ref[...], post_ref[...])


def _ffn(x, pre_g, post_g, w1, w3, w2, tm, tf):
    m, d = x.shape
    f = w1.shape[1]
    return pl.pallas_call(
        _ffn_body,
        grid=(m // tm, f // tf),
        in_specs=[
            pl.BlockSpec((tm, d), lambda i, j: (i, 0)),
            pl.BlockSpec((1, d), lambda i, j: (0, 0)),
            pl.BlockSpec((1, d), lambda i, j: (0, 0)),
            pl.BlockSpec((d, tf), lambda i, j: (0, j)),
            pl.BlockSpec((d, tf), lambda i, j: (0, j)),
            pl.BlockSpec((tf, d), lambda i, j: (j, 0)),
        ],
        out_specs=pl.BlockSpec((tm, d), lambda i, j: (i, 0)),
        out_shape=jax.ShapeDtypeStruct((m, d), F32),
        scratch_shapes=[pltpu.VMEM((tm, d), BF16), pltpu.VMEM((tm, d), F32)],
        compiler_params=_cparams(("parallel", "arbitrary")),
        name="ffn",
    )(x, pre_g.reshape(1, d), post_g.reshape(1, d), w1, w3, w2)


def _norm_matmul_body(x_ref, g_ref, w_ref, o_ref, h_ref):
    @pl.when(pl.program_id(1) == 0)
    def _():
        h_ref[...] = _rms(x_ref[...], g_ref[...]).astype(BF16)

    o_ref[...] = _dot(h_ref[...], w_ref[...])


def _norm_matmul(x, g, w, tm, name):
    m, d = x.shape
    n = w.shape[1]
    return pl.pallas_call(
        _norm_matmul_body,
        grid=(m // tm, 1),
        in_specs=[
            pl.BlockSpec((tm, d), lambda i, j: (i, 0)),
            pl.BlockSpec((1, d), lambda i, j: (0, 0)),
            pl.BlockSpec((d, n), lambda i, j: (0, 0), pipeline_mode=pl.Buffered(1)),
        ],
        out_specs=pl.BlockSpec((tm, n), lambda i, j: (i, 0)),
        out_shape=jax.ShapeDtypeStruct((m, n), F32),
        scratch_shapes=[pltpu.VMEM((tm, d), BF16)],
        compiler_params=_cparams(("parallel", "arbitrary")),
        name=name,
    )(x, g.reshape(1, d), w)


def _rwkv_pre_body(p_ref, prev_ref, mu_ref, w0_ref, ww_ref, a0_ref, wa_ref, g2_ref, kk_ref, ka_ref,
                   pre_ref, g_ref, buf_ref):
    tt = p_ref.shape[1]
    p = p_ref[0]
    buf_ref[8:8 + tt, :] = p
    buf_ref[7:8, :] = prev_ref[0, 0]
    p_prev = buf_ref[7:7 + tt, :]
    xs = p + (p_prev - p) * mu_ref[...]
    d = RW_DIM
    r = xs[:, 0:d]
    k = xs[:, d:2 * d]
    v = xs[:, 2 * d:3 * d]
    lora_in = xs[:, 3 * d:3 * d + RW_LORA_W + RW_LORA_A]
    gd = xs[:, 3 * d + RW_LORA_W + RW_LORA_A:]
    z = -(w0_ref[...] + _dot(jnp.tanh(lora_in).astype(BF16), ww_ref[...]))
    softplus = jnp.maximum(z, 0.0) + jnp.log(1.0 + jnp.exp(-jnp.abs(z)))
    decay = jnp.exp(-jnp.exp(-softplus - 0.5))
    a = jax.nn.sigmoid(a0_ref[...] + _dot(lora_in.astype(BF16), wa_ref[...]))
    g_ref[0] = _dot(jax.nn.sigmoid(gd).astype(BF16), g2_ref[...])

    n = RW_HEAD_DIM
    pairs = d // LANES
    low_half = lax.broadcasted_iota(jnp.int32, (tt, LANES), 1) < n

    def put(fp, xa, xb):
        for pair in range(pairs):
            col_a = xa[:, pair * LANES:(pair + 1) * LANES]
            col_b = xb[:, pair * LANES:(pair + 1) * LANES]
            pieces = (jnp.where(low_half, col_a, pltpu.roll(col_b, n, 1)),
                      jnp.where(low_half, pltpu.roll(col_a, n, 1), col_b))
            for parity, piece in enumerate(pieces):
                if tt % SUBLANES:
                    pre_ref[:, fp, parity, pair, :] = piece
                else:
                    q = (fp * 2 + parity) * pairs + pair
                    pre_ref[:, q, :, :] = piece.reshape(tt // SUBLANES, SUBLANES, LANES)

    put(0, r, k * (1.0 + (a - 1.0) * ka_ref[...]))
    put(1, v, decay)
    put(2, k * kk_ref[...], a)


def _rwkv_pre(p3, prev, mu, w0, ww, a0, wa, g2, k_k, k_a, tt):
    b, t, c = p3.shape
    d = RW_DIM
    pairs = RW_HEADS // 2
    row = lambda n: pl.BlockSpec((1, n), lambda i, j: (0, 0))
    full = lambda s: pl.BlockSpec(s, lambda i, j: (0, 0))
    n_pieces = SCAN_FIELDS * pairs
    if tt % SUBLANES:
        pre_spec = pl.BlockSpec((tt, SCAN_FIELDS // 2, 2, pairs, LANES), lambda i, j: (j, 0, 0, i, 0))
        pre_shape = (t, SCAN_FIELDS // 2, 2, b * pairs, LANES)
    else:
        tiles, steps = tt // SUBLANES, t // tt
        pre_spec = pl.BlockSpec((tiles, n_pieces, SUBLANES, LANES), lambda i, j: (i * steps + j, 0, 0, 0))
        pre_shape = (b * t // SUBLANES, n_pieces, SUBLANES, LANES)
    return pl.pallas_call(
        _rwkv_pre_body,
        grid=(b, t // tt),
        in_specs=[
            pl.BlockSpec((1, tt, c), lambda i, j: (i, j, 0)),
            pl.BlockSpec((1, 1, 1, c), lambda i, j: (i, j, 0, 0)),
            row(c), row(d), full(ww.shape), row(d), full(wa.shape), full(g2.shape), row(d), row(d),
        ],
        out_specs=[pre_spec, pl.BlockSpec((1, tt, d), lambda i, j: (i, j, 0))],
        out_shape=[jax.ShapeDtypeStruct(pre_shape, F32), jax.ShapeDtypeStruct((b, t, d), F32)],
        scratch_shapes=[pltpu.VMEM((tt + 8, c), F32)],
        compiler_params=_cparams(("parallel", "arbitrary")),
        name="rwkv_pre",
    )(p3, prev, mu.reshape(1, c), w0.reshape(1, d), ww, a0.reshape(1, d), wa, g2,
      k_k.reshape(1, d), k_a.reshape(1, d))


def _rwkv_scan_body(in_ref, s0_ref, lnw_ref, lnb_ref, rk_ref, y_ref, s_ref, xa_ref, xb_ref, kk_ref, b_ref,
                    out_ref):
    n = RW_HEAD_DIM
    steps = in_ref.shape[0]
    n_fields = in_ref.shape[1]
    half = in_ref.shape[3]
    lanes = 2 * half
    fields_per_trip = -(-n_fields // (n // UPDATE_UNROLL))

    def load_field(t, dst_ref, fp):
        tile = in_ref[t, fp].reshape(lanes, 2 * n)
        dst_ref[pl.ds(pl.multiple_of(fp * 2 * n, 2 * n), 2 * n), :] = tile.T

    def normalise_key(x_ref):
        kk_raw = x_ref[4 * n:5 * n, :]
        norm = jnp.sqrt(jnp.sum(kk_raw * kk_raw, axis=0, keepdims=True))
        kk = kk_raw / jnp.maximum(norm, 1e-12)
        kk_ref[...] = kk
        b_ref[...] = kk * x_ref[5 * n:6 * n, :]

    def one_step(t, x_ref, next_ref):
        t_next = jnp.minimum(t + 1, steps - 1)

        def sk_rows(jb, acc):
            for u in range(SCAN_UNROLL):
                j = jb * SCAN_UNROLL + u
                acc = acc + s_ref[j] * kk_ref[pl.ds(j, 1), :]
            return acc

        sk = lax.fori_loop(0, n // SCAN_UNROLL, sk_rows, jnp.zeros((n, lanes), F32))
        v = x_ref[2 * n:3 * n, :]

        def update_rows(jb, acc):
            for f in range(fields_per_trip):
                load_field(t_next, next_ref, jnp.minimum(jb * fields_per_trip + f, n_fields - 1))
            emit_output(jnp.maximum(t - 1, 0))
            for u in range(UPDATE_UNROLL):
                j = jb * UPDATE_UNROLL + u
                r_j = x_ref[pl.ds(j, 1), :]
                k_j = x_ref[pl.ds(n + j, 1), :]
                w_j = x_ref[pl.ds(3 * n + j, 1), :]
                s_j = s_ref[j] * w_j - sk * b_ref[pl.ds(j, 1), :] + v * k_j
                s_ref[j] = s_j
                acc = acc + s_j * r_j
            return acc

        y = lax.fori_loop(0, n // UPDATE_UNROLL, update_rows, jnp.zeros((n, lanes), F32))
        normalise_key(next_ref)
        mean = jnp.sum(y, axis=0, keepdims=True) * (1.0 / n)
        yc = y - mean
        var = jnp.sum(yc * yc, axis=0, keepdims=True) * (1.0 / n)
        y_norm = yc * lax.rsqrt(var + RW_LN_EPS) * lnw_ref[...] + lnb_ref[...]
        rkr = x_ref[0:n, :] * x_ref[n:2 * n, :] * rk_ref[...]
        out_ref[...] = y_norm + jnp.sum(rkr, axis=0, keepdims=True) * v

    def emit_output(t):
        out = out_ref[...]
        y_ref[t] = jnp.concatenate([out[:, :half], out[:, half:]], axis=0).T

    @pl.when(pl.program_id(1) == 0)
    def _():
        s_ref[...] = s0_ref[...]

    out_ref[...] = jnp.zeros_like(out_ref)
    for c in range(n_fields):
        load_field(0, xa_ref, c)
    normalise_key(xa_ref)

    def step_pair(tp, carry):
        one_step(2 * tp, xa_ref, xb_ref)
        one_step(2 * tp + 1, xb_ref, xa_ref)
        return carry

    lax.fori_loop(0, steps // 2, step_pair, 0)
    emit_output(steps - 1)


def _rwkv_scan(pre5, s0, lnw, lnb, rk, tc, seqs):
    t, field_pairs, _, rows, _ = pre5.shape
    n = RW_HEAD_DIM
    pairs = RW_HEADS // 2
    x_rows = field_pairs * 2 * n
    lb = seqs * RW_HEADS
    assert tc % 2 == 0 and t % tc == 0
    tab = pl.BlockSpec((n, lb), lambda l, i: (0, 0))
    return pl.pallas_call(
        _rwkv_scan_body,
        grid=(rows // (seqs * pairs), t // tc),
        in_specs=[
            pl.BlockSpec((tc, field_pairs, 2, seqs * pairs, LANES), lambda l, i: (i, 0, 0, l, 0)),
            pl.BlockSpec((n, n, lb), lambda l, i: (0, 0, l)),
            tab, tab, tab,
        ],
        out_specs=[
            pl.BlockSpec((tc, seqs * pairs, LANES), lambda l, i: (i, l, 0)),
            pl.BlockSpec((n, n, lb), lambda l, i: (0, 0, l)),
        ],
        out_shape=[jax.ShapeDtypeStruct((t, rows, LANES), F32),
                   jax.ShapeDtypeStruct((n, n, rows * 2), F32)],
        scratch_shapes=[pltpu.VMEM((x_rows, lb), F32), pltpu.VMEM((x_rows, lb), F32),
                        pltpu.VMEM((n, lb), F32), pltpu.VMEM((n, lb), F32), pltpu.VMEM((n, lb), F32)],
        compiler_params=_cparams(("parallel", "arbitrary")),
        name="rwkv_scan",
    )(pre5, s0, lnw, lnb, rk)


GATHER_PAGES = 8


GATHER_WINDOW = 128


def _gather_rows(rows, idx):
    n = idx.shape[1]
    width = rows.shape[1]
    mesh = plsc.VectorSubcoreMesh(core_axis_name="core", subcore_axis_name="subcore")

    @pl.kernel(out_type=jax.ShapeDtypeStruct((n, width), rows.dtype), mesh=mesh)
    def gather_kernel(x_hbm, i_hbm, o_hbm):
        def body(i_vmem, o_vmem):
            pltpu.sync_copy(x_hbm.at[i_vmem.at[0]], o_vmem)

        pltpu.emit_pipeline(
            body,
            grid=(n // GATHER_WINDOW,),
            in_specs=[pl.BlockSpec((1, GATHER_WINDOW), index_map=lambda i: (0, i))],
            out_specs=[pl.BlockSpec((GATHER_WINDOW, width), index_map=lambda i: (i, 0))],
            core_axis_name=("core", "subcore"),
            dimension_semantics=(pltpu.PARALLEL,),
        )(i_hbm, o_hbm)

    return gather_kernel(rows, idx)


def _available_after_body(x_ref, anchor_ref, o_ref):
    o_ref[...] = x_ref[...]


def _available_after(x, anchor):
    return pl.pallas_call(_available_after_body, out_shape=jax.ShapeDtypeStruct(x.shape, x.dtype),
                          name="available_after")(x, anchor)


def _gather_selected(page_table, pool):
    bs, n_pages = page_table.shape
    n_pool, page, slabs, hd = pool.shape
    past_len = n_pages * page
    tok = jnp.arange(past_len, dtype=jnp.int32)
    row0 = (page_table[:, tok // page] * page + tok % page) * slabs
    idx = row0[:, None, :] + (slabs // 2 + jnp.arange(slabs // 2, dtype=jnp.int32))[None, :, None]
    out = _gather_rows(pool.reshape(n_pool * page * slabs, hd), idx.reshape(1, -1))
    return out.reshape(bs, slabs // 2, past_len, hd)


def _paged_kv_body(pt_ref, *refs):
    n_pages = len(refs) - 5
    page_refs = refs[:n_pages]
    pe_ref, w1_ref, w2_ref, kc_ref, out_ref = refs[n_pages:]
    _, rows, slabs, hd = page_refs[0].shape
    per_page = rows // CMP_BLOCK
    cols = []
    for r in range(CMP_BLOCK):
        tiles = [x_ref[0, nl * CMP_BLOCK + r] + pe_ref[r] for x_ref in page_refs for nl in range(per_page)]
        cols.append(jnp.concatenate(tiles, axis=0).astype(BF16))
    hid = jax.nn.gelu(_dot(jnp.concatenate(cols, axis=1), w1_ref[...]), approximate=True)
    out_ref[0] = _dot(hid[:, :CMP_HID].astype(BF16), w2_ref[0])
    out_ref[1] = _dot(hid[:, CMP_HID:].astype(BF16), w2_ref[1])
    n_blocks = n_pages * per_page
    for kg in range(slabs // 2):
        kc_ref[0, :, kg * hd:(kg + 1) * hd] = out_ref[kg // NSA_KV_HEADS, pl.ds(kg, n_blocks, stride=slabs), :]


def _paged_kv(page_table, pool, pe_tiles, w1cat, w2):
    bs, n_pages = page_table.shape
    _, page, slabs, hd = pool.shape
    half = slabs * hd // 2
    per_step = math.gcd(GATHER_PAGES, n_pages)
    blocks_step = per_step * page // CMP_BLOCK

    def page_spec(k):
        return pl.BlockSpec((1, page, slabs, hd), lambda b, p, pt: (pt[b, p * per_step + k], 0, 0, 0))

    held = lambda a: pl.BlockSpec(a.shape, lambda b, p, pt: (0,) * a.ndim)
    return pl.pallas_call(
        _paged_kv_body,
        grid_spec=pltpu.PrefetchScalarGridSpec(
            num_scalar_prefetch=1,
            grid=(bs, n_pages // per_step),
            in_specs=[page_spec(k) for k in range(per_step)] + [held(pe_tiles), held(w1cat), held(w2)],
            out_specs=pl.BlockSpec((1, blocks_step, half), lambda b, p, pt: (b, p, 0)),
            scratch_shapes=[pltpu.VMEM((2, blocks_step * slabs, hd), F32)],
        ),
        out_shape=jax.ShapeDtypeStruct((bs, n_pages * page // CMP_BLOCK, half), F32),
        compiler_params=_cparams(("parallel", "arbitrary")),
        name="paged_kv",
    )(page_table, *([pool] * per_step), pe_tiles, w1cat, w2)


def _cmp_mlp_body(x_ref, pe_ref, w1_ref, w2_ref, o_ref):
    hd = NSA_HEAD_DIM
    for kg in range(2 * NSA_KV_HEADS):
        ch = kg // NSA_KV_HEADS
        acc = None
        for r in range(CMP_BLOCK):
            xr = x_ref[:, r, kg * hd:(kg + 1) * hd]
            xr = (xr + pe_ref[r:r + 1, kg * hd:(kg + 1) * hd]).astype(BF16)
            part = _dot(xr, w1_ref[ch, r * hd:(r + 1) * hd, :])
            acc = part if acc is None else acc + part
        hid = jax.nn.gelu(acc, approximate=True)
        o_ref[:, kg * hd:(kg + 1) * hd] = _dot(hid.astype(BF16), w2_ref[ch])


def _cmp_mlp(blocks, pe_rows, w1, w2, nb, col_block):
    n_blocks, rows, _ = blocks.shape
    out_w = 2 * NSA_KV_DIM
    return pl.pallas_call(
        _cmp_mlp_body,
        grid=(n_blocks // nb,),
        in_specs=[
            pl.BlockSpec((nb, rows, out_w), lambda i: (i, 0, col_block)),
            pl.BlockSpec(pe_rows.shape, lambda i: (0, 0)),
            pl.BlockSpec(w1.shape, lambda i: (0, 0, 0)),
            pl.BlockSpec(w2.shape, lambda i: (0, 0, 0)),
        ],
        out_specs=pl.BlockSpec((nb, out_w), lambda i: (i, 0)),
        out_shape=jax.ShapeDtypeStruct((n_blocks, out_w), F32),
        compiler_params=_cparams(("parallel",)),
        name="cmp_mlp",
    )(blocks, pe_rows, w1, w2)


N_BIAS_TILES = -(-(FAR_DIST + LANES - 1) // LANES) + 1


def _bias_tiles_body(rb_ref, o_ref, cmp_ref):
    h = pl.program_id(0)
    key = lax.broadcasted_iota(jnp.int32, (LANES, Q_BLOCK), 0)
    qry = lax.broadcasted_iota(jnp.int32, (LANES, Q_BLOCK), 1)

    def bias_of(dist):
        return _bias_for_heads(dist, rb_ref, [h])[0] * LOG2_E

    for m in range(N_BIAS_TILES):
        o_ref[0, m] = bias_of(m * LANES + qry - key)
    for i in range(cmp_ref.shape[1]):
        cmp_ref[0, i] = bias_of(i * Q_BLOCK + qry - (key * CMP_BLOCK + CMP_BLOCK - 1))


def _bias_tiles(rel_bias, n_q_blocks):
    tile = (Q_BLOCK, LANES)
    return pl.pallas_call(
        _bias_tiles_body,
        grid=(NSA_HEADS,),
        in_specs=[pl.BlockSpec(memory_space=pltpu.SMEM)],
        out_specs=[pl.BlockSpec((1, N_BIAS_TILES) + tile, lambda h: (h, 0, 0, 0)),
                   pl.BlockSpec((1, n_q_blocks) + tile, lambda h: (h, 0, 0, 0))],
        out_shape=[jax.ShapeDtypeStruct((NSA_HEADS, N_BIAS_TILES) + tile, F32),
                   jax.ShapeDtypeStruct((NSA_HEADS, n_q_blocks) + tile, F32)],
        compiler_params=_cparams(("arbitrary",)),
        name="bias_tiles",
    )(rel_bias)


def _nsa_prompt_body(q_ref, gate_ref, sel_ref, win_ref, kc_ref, bt_ref, cb_ref, y_ref,
                     selk_ref, selvt_ref, wink_ref, winvt_ref, *, n_sel):
    hd = NSA_HEAD_DIM
    qb = Q_BLOCK
    i = pl.program_id(1)
    t_len = selk_ref.shape[0]

    @pl.when(i == 0)
    def _():
        for src_ref, k_ref, vt_ref in ((sel_ref, selk_ref, selvt_ref), (win_ref, wink_ref, winvt_ref)):
            k_ref[...] = src_ref[0, :, :NSA_KV_DIM].astype(BF16)
            for c in range(t_len // LANES):
                rows = slice(c * LANES, (c + 1) * LANES)
                vt_ref[:, rows] = src_ref[0, rows, NSA_KV_DIM:].T.astype(BF16)

    q_bf = (q_ref[0] * (hd ** -0.5 * LOG2_E)).astype(BF16)
    gates_t = jax.nn.sigmoid(gate_ref[0]).T
    iq = lax.broadcasted_iota(jnp.int32, (qb, LANES), 0)
    pos = i * qb + iq
    key = lax.broadcasted_iota(jnp.int32, (LANES, qb), 0)
    qry = lax.broadcasted_iota(jnp.int32, (LANES, qb), 1)
    vis_cmp = _stack_heads_t(i * qb + qry - (key * CMP_BLOCK + CMP_BLOCK - 1) >= 0)
    top_n = min(TOP_N, n_sel)
    sel_tiles = SEL_TILES if t_len % (SEL_TILES * LANES) == 0 else 1
    sel_keys = sel_tiles * LANES
    win_tiles = min(WINDOW // LANES + 1, t_len // LANES)
    win_keys = win_tiles * LANES
    groups = [[g * NSA_HPG + h for h in range(NSA_HPG)] for g in range(NSA_KV_HEADS)]

    def tile_bias(heads, first_tile, n_tiles):
        tiles = []
        for k in range(n_tiles):
            m = jnp.clip(i - (first_tile + k), 0, N_BIAS_TILES - 1)
            tiles.append(jnp.concatenate([bt_ref[hh, m] for hh in heads], axis=1))
        return tiles[0] if n_tiles == 1 else jnp.concatenate(tiles, axis=0)

    def selection_mask(chosen_t, tile, tile_keys):
        n_idx = lax.broadcasted_iota(jnp.int32, (tile_keys, LANES), 1)
        k_idx = lax.broadcasted_iota(jnp.int32, (tile_keys, LANES), 0)
        expand = jnp.where(n_idx == 2 * ((tile * tile_keys + k_idx) >> SEL_SHIFT), 1.0, 0.0).astype(BF16)
        return _dot(expand, chosen_t) > 0.5

    qgs, o_cmps, values = [], [], []
    for g, heads in enumerate(groups):
        qg = jnp.concatenate([q_bf[:, hh * hd:(hh + 1) * hd] for hh in heads], axis=0)
        kc_k = kc_ref[0, :, g * hd:(g + 1) * hd].astype(BF16)
        kc_vt = kc_ref[0, :, NSA_KV_DIM + g * hd:NSA_KV_DIM + (g + 1) * hd].T.astype(BF16)
        s = _dot_nt(kc_k, qg) + jnp.concatenate([cb_ref[hh, 0] for hh in heads], axis=1)
        s = jnp.where(vis_cmp, s, NEG_BIG)
        e = jnp.where(vis_cmp, jnp.exp2(s - jnp.max(s, axis=0, keepdims=True)), 0.0)
        p = e / jnp.maximum(jnp.sum(e, axis=0, keepdims=True), 1e-30)
        o_cmps.append(_dot(kc_vt, p.astype(BF16)))
        imp_t = p[:, 0:qb]
        for h in range(1, NSA_HPG):
            imp_t = imp_t + p[:, h * qb:(h + 1) * qb]
        qgs.append(qg)
        values.append(_selection_values_of(imp_t.T, pos, n_sel))
    chosen = _topk_mask(jnp.concatenate(values, axis=0), top_n)
    chosens = [chosen[g * qb:(g + 1) * qb].T.astype(BF16) for g in range(NSA_KV_HEADS)]

    key_s = lax.broadcasted_iota(jnp.int32, (sel_keys, qb), 0)
    qry_s = lax.broadcasted_iota(jnp.int32, (sel_keys, qb), 1)

    def sel_step(j, carries):
        row0 = pl.multiple_of(j * sel_keys, sel_keys)
        causal = i * qb + qry_s - (j * sel_keys + key_s) >= 0
        out = []
        for g, heads in enumerate(groups):
            mask = _stack_heads_t(selection_mask(chosens[g], j, sel_keys) & causal)
            out.append(_softmax_update_t(carries[g], qgs[g], selk_ref[pl.ds(row0, sel_keys), g * hd:(g + 1) * hd],
                                         selvt_ref[g * hd:(g + 1) * hd, pl.ds(row0, sel_keys)],
                                         tile_bias(heads, j * sel_tiles, sel_tiles), mask))
        return tuple(out)

    n_steps = (i + sel_tiles) // sel_tiles
    sel_carries = lax.fori_loop(0, n_steps, sel_step, tuple(_softmax_init_t(NSA_HPG * qb) for _ in groups))

    first = jnp.maximum(i - (win_tiles - 1), 0)
    row0 = pl.multiple_of(first * LANES, LANES)
    d_win = (i * qb + lax.broadcasted_iota(jnp.int32, (win_keys, qb), 1)
             - (first * LANES + lax.broadcasted_iota(jnp.int32, (win_keys, qb), 0)))
    mask_win = _stack_heads_t((d_win >= 0) & (d_win < WINDOW))
    for g, heads in enumerate(groups):
        o_sel = _softmax_finish(sel_carries[g])
        o_win = _softmax_finish(_softmax_update_t(
            _softmax_init_t(NSA_HPG * qb), qgs[g], wink_ref[pl.ds(row0, win_keys), g * hd:(g + 1) * hd],
            winvt_ref[g * hd:(g + 1) * hd, pl.ds(row0, win_keys)], tile_bias(heads, first, win_tiles),
            mask_win))
        for h, hh in enumerate(heads):
            cols = slice(h * qb, (h + 1) * qb)
            out_t = (gates_t[hh:hh + 1, :] * o_cmps[g][:, cols]
                     + gates_t[NSA_HEADS + hh:NSA_HEADS + hh + 1, :] * o_sel[:, cols]
                     + gates_t[2 * NSA_HEADS + hh:2 * NSA_HEADS + hh + 1, :] * o_win[:, cols])
            y_ref[0, :, hh * hd:(hh + 1) * hd] = out_t.T.astype(BF16)


def _nsa_prompt(pb3, kc, bias_tiles, cmp_bias):
    b, t, _ = pb3.shape
    qb = Q_BLOCK
    half = 2 * NSA_KV_DIM
    return pl.pallas_call(
        functools.partial(_nsa_prompt_body, n_sel=t // SEL_BLOCK),
        grid=(b, t // qb),
        in_specs=[
            pl.BlockSpec((1, qb, NSA_DIM), lambda s, i: (s, i, PB_Q // NSA_DIM)),
            pl.BlockSpec((1, qb, LANES), lambda s, i: (s, i, PB_GATE // LANES)),
            pl.BlockSpec((1, t, half), lambda s, i: (s, 0, PB_SEL // half)),
            pl.BlockSpec((1, t, half), lambda s, i: (s, 0, PB_WIN // half)),
            pl.BlockSpec((1,) + kc.shape[1:], lambda s, i: (s, 0, 0)),
            pl.BlockSpec(bias_tiles.shape, lambda s, i: (0, 0, 0, 0)),
            pl.BlockSpec((NSA_HEADS, 1, qb, LANES), lambda s, i: (0, i, 0, 0)),
        ],
        out_specs=pl.BlockSpec((1, qb, NSA_DIM), lambda s, i: (s, i, 0)),
        out_shape=jax.ShapeDtypeStruct((b, t, NSA_DIM), BF16),
        scratch_shapes=[pltpu.VMEM((t, NSA_KV_DIM), BF16), pltpu.VMEM((NSA_KV_DIM, t), BF16),
                        pltpu.VMEM((t, NSA_KV_DIM), BF16), pltpu.VMEM((NSA_KV_DIM, t), BF16)],
        compiler_params=_cparams(("parallel", "arbitrary")),
        name="nsa_prompt",
    )(pb3, pb3, pb3, pb3, kc, bias_tiles, cmp_bias)


SEL_TILES = 2
SAMPLE_ROWS = 8
SAMPLE_TILE_KEYS = 1024


def _nsa_sample_body(rb_ref, q_ref, gate_ref, kc_ref, past_ref, new_sel_ref, cwin_ref, new_win_ref, y_ref,
                     *, past_len):
    hd = NSA_HEAD_DIM
    qb = SAMPLE_ROWS
    scale = hd ** -0.5
    n_cmp = kc_ref.shape[1]
    win_rows = cwin_ref.shape[1]
    k_off = past_len - win_rows
    tile_keys = min(SAMPLE_TILE_KEYS, past_len)
    assert past_len % tile_keys == 0

    q_bf = q_ref[0].astype(BF16)
    gates = jax.nn.sigmoid(gate_ref[0])
    iq = lax.broadcasted_iota(jnp.int32, (qb, LANES), 0)
    ik = lax.broadcasted_iota(jnp.int32, (qb, LANES), 1)
    pos_t = past_len + lax.broadcasted_iota(jnp.int32, (qb, tile_keys), 0)
    ik_t = lax.broadcasted_iota(jnp.int32, (qb, tile_keys), 1)
    d_win = (past_len + lax.broadcasted_iota(jnp.int32, (qb, win_rows), 0)
             - (k_off + lax.broadcasted_iota(jnp.int32, (qb, win_rows), 1)))
    tq_c = lax.broadcasted_iota(jnp.int32, (qb, n_cmp), 0)
    n_c = lax.broadcasted_iota(jnp.int32, (qb, n_cmp), 1)
    pos_c = past_len + tq_c
    d_cmp = pos_c - (n_c * CMP_BLOCK + CMP_BLOCK - 1)
    bias_cmp = _bias_for_heads(d_cmp, rb_ref, range(NSA_HEADS))
    vis_cmp = _stack_heads(d_cmp >= 0)
    n_sel_past = past_len // SEL_BLOCK
    picks = min(TOP_N, n_sel_past + 1) - 1

    groups = [[g * NSA_HPG + h for h in range(NSA_HPG)] for g in range(NSA_KV_HEADS)]

    def key_cols(g):
        return slice(g * hd, (g + 1) * hd), slice(NSA_KV_DIM + g * hd, NSA_KV_DIM + (g + 1) * hd)

    qgs, o_cmps, values = [], [], []
    for g, heads in enumerate(groups):
        k_cols, v_cols = key_cols(g)
        qg = jnp.concatenate([q_bf[:, hh * hd:(hh + 1) * hd] for hh in heads], axis=0)
        o_cmp, p_cmp = _cmp_branch(qg, kc_ref[0, :, k_cols].astype(BF16), kc_ref[0, :, v_cols].astype(BF16),
                                   jnp.concatenate([bias_cmp[hh] for hh in heads], axis=0), vis_cmp, scale)
        qgs.append(qg)
        o_cmps.append(o_cmp)
        values.append(_selection_values(p_cmp, qb, pos_c, n_sel_past))
    chosen = _topk_mask(jnp.concatenate(values, axis=0), picks).astype(BF16)
    chosens = [chosen[g * qb:(g + 1) * qb] for g in range(NSA_KV_HEADS)]

    def sel_step(j, carries):
        row0 = pl.multiple_of(j * tile_keys, tile_keys)
        dist = pos_t - (j * tile_keys + ik_t)
        out = []
        for g, heads in enumerate(groups):
            k_cols, v_cols = key_cols(g)
            mask = _stack_heads(_expand_selection(chosens[g], j, tile_keys))
            bias = jnp.concatenate(_bias_for_heads(dist, rb_ref, heads), axis=0)
            kt = past_ref[0, g, pl.ds(row0, tile_keys), :].astype(BF16)
            vt = past_ref[0, NSA_KV_HEADS + g, pl.ds(row0, tile_keys), :].astype(BF16)
            out.append(_softmax_update(carries[g], qgs[g], kt, vt, bias, mask, scale))
        return tuple(out)

    sel_carries = lax.fori_loop(0, past_len // tile_keys, sel_step,
                                tuple(_softmax_init(NSA_HPG * qb) for _ in groups))
    d_new = iq - ik
    mask_new = _stack_heads(d_new >= 0)
    mask_win = _stack_heads((d_win >= 0) & (d_win < WINDOW))
    for g, heads in enumerate(groups):
        k_cols, v_cols = key_cols(g)
        qg, o_cmp = qgs[g], o_cmps[g]
        bias_new = jnp.concatenate(_bias_for_heads(d_new, rb_ref, heads), axis=0)
        o_sel = _softmax_finish(_softmax_update(sel_carries[g], qg, new_sel_ref[0, :, k_cols],
                                                new_sel_ref[0, :, v_cols], bias_new, mask_new, scale))
        bias_win = jnp.concatenate(_bias_for_heads(d_win, rb_ref, heads), axis=0)
        carry = _softmax_update(_softmax_init(NSA_HPG * qb), qg, cwin_ref[0, :, k_cols].astype(BF16),
                                cwin_ref[0, :, v_cols].astype(BF16), bias_win, mask_win, scale)
        o_win = _softmax_finish(_softmax_update(carry, qg, new_win_ref[0, :, k_cols], new_win_ref[0, :, v_cols],
                                                bias_new, mask_new, scale))

        for h, hh in enumerate(heads):
            rows = slice(h * qb, (h + 1) * qb)
            out = (gates[:, hh:hh + 1] * o_cmp[rows]
                   + gates[:, NSA_HEADS + hh:NSA_HEADS + hh + 1] * o_sel[rows]
                   + gates[:, 2 * NSA_HEADS + hh:2 * NSA_HEADS + hh + 1] * o_win[rows])
            y_ref[0, :, hh * hd:(hh + 1) * hd] = out.astype(BF16)


def _nsa_sample(rel_bias, q8, gate8, kc, past_sel, new_sel, cwin, new_win, past_len):
    bs = q8.shape[0]
    blk = lambda a: pl.BlockSpec((1,) + a.shape[1:], lambda s: (s,) + (0,) * (a.ndim - 1))
    return pl.pallas_call(
        functools.partial(_nsa_sample_body, past_len=past_len),
        grid=(bs,),
        in_specs=[pl.BlockSpec(memory_space=pltpu.SMEM), blk(q8), blk(gate8), blk(kc), blk(past_sel),
                  blk(new_sel), blk(cwin), blk(new_win)],
        out_specs=pl.BlockSpec((1, SAMPLE_ROWS, NSA_DIM), lambda s: (s, 0, 0)),
        out_shape=jax.ShapeDtypeStruct((bs, SAMPLE_ROWS, NSA_DIM), BF16),
        compiler_params=_cparams(("parallel",)),
        name="nsa_sample",
    )(rel_bias, q8, gate8, kc, past_sel, new_sel, cwin, new_win)


def _merge_body(x_ref, yrw_ref, g_ref, ynsa_ref, gates_ref, wrw_ref, wnsa_ref, wout_ref, post_ref, o_ref):
    tm, d = x_ref.shape
    y_rw = jnp.concatenate([yrw_ref[:, pair, :] for pair in range(RW_DIM // LANES)], axis=1)
    y_rw = (y_rw * g_ref[...]).astype(BF16)
    merged = (jax.nn.sigmoid(gates_ref[:, :d]) * _dot(y_rw, wrw_ref[...])
              + jax.nn.sigmoid(gates_ref[:, d:]) * _dot(ynsa_ref[...], wnsa_ref[...]))
    o_ref[...] = x_ref[...] + _rms(_dot(merged.astype(BF16), wout_ref[...]), post_ref[...])


def _merge(x, y_rw, g_rw, y_nsa, gates, w_rw, w_nsa, w_out, post_g, tm):
    m, d = x.shape
    rows = lambda n: pl.BlockSpec((tm, n), lambda i: (i, 0))
    held = lambda a: pl.BlockSpec(a.shape, lambda i: (0, 0), pipeline_mode=pl.Buffered(1))
    return pl.pallas_call(
        _merge_body,
        grid=(m // tm,),
        in_specs=[rows(d), pl.BlockSpec((tm,) + y_rw.shape[1:], lambda i: (i, 0, 0)),
                  rows(RW_DIM), rows(NSA_DIM), rows(2 * d),
                  held(w_rw), held(w_nsa), held(w_out), pl.BlockSpec((1, d), lambda i: (0, 0))],
        out_specs=rows(d),
        out_shape=jax.ShapeDtypeStruct((m, d), F32),
        compiler_params=_cparams(("parallel",)),
        name="merge",
    )(x, y_rw, g_rw, y_nsa, gates, w_rw, w_nsa, w_out, post_g.reshape(1, d))


def _row_tile(m, want):
    return want if m % want == 0 else m


def _rwkv_inputs(p3, shift0, w):
    b, t, c = p3.shape
    tt = 256 if t % 256 == 0 else t
    prev = jnp.concatenate([shift0[:, None, :], p3[:, tt - 1:t - 1:tt]], axis=1).reshape(b, t // tt, 1, c)
    pre, g_rw = _rwkv_pre(p3, prev, w["mu"], w["w0"], w["ww"], w["a0"], w["wa"], w["g2"], w["k_k"], w["k_a"], tt)
    pairs = RW_HEADS // 2
    if tt % SUBLANES == 0:
        n_pieces = SCAN_FIELDS * pairs
        tok = np.arange(b, dtype=np.int32)[None, :] * t + np.arange(t, dtype=np.int32)[:, None]
        row0 = (tok // SUBLANES) * (n_pieces * SUBLANES) + tok % SUBLANES
        piece = np.arange(n_pieces, dtype=np.int32).reshape(SCAN_FIELDS // 2, 2, 1, pairs)
        idx = row0[:, None, None, :, None] + piece[None] * SUBLANES
        pre = _gather_rows(pre.reshape(-1, LANES), jnp.asarray(idx.reshape(1, -1))).reshape(
            t, SCAN_FIELDS // 2, 2, b * pairs, LANES)
    return pre, g_rw.reshape(b * t, RW_DIM)


def _rwkv_recurrence(pre, s0, w, run_after=None):
    t = pre.shape[0]
    b = s0.shape[0]
    n = RW_HEAD_DIM
    seqs = min(b, LANES // RW_HEADS)
    groups = b // seqs
    pairs = RW_HEADS // 2
    s0_t = s0.reshape(groups, seqs, pairs, 2, n, n).transpose(5, 4, 0, 3, 1, 2).reshape(n, n, b * RW_HEADS)
    if run_after is not None:
        s0_t = _available_after(s0_t, run_after)
    per_lane = lambda v: jnp.broadcast_to(v.reshape(pairs, 2, n).transpose(2, 1, 0)[:, :, None, :],
                                          (n, 2, seqs, pairs)).reshape(n, seqs * RW_HEADS)
    tc = 32 if t % 32 == 0 else t
    y_t, s_t = _rwkv_scan(pre, s0_t, per_lane(w["lnx_w"]), per_lane(w["lnx_b"]), per_lane(w["r_k"]), tc, seqs)
    s_fin = s_t.reshape(n, n, groups, 2, seqs, pairs).transpose(2, 4, 5, 3, 1, 0).reshape(b, RW_HEADS, n, n)
    y_tok = y_t.reshape(t, b, pairs, LANES).transpose(1, 0, 2, 3).reshape(b * t, pairs, LANES)
    return y_tok, s_fin


def _layer(x, w, rel_bias, shift0, s0, past):
    b, t, d = x.shape
    m = b * t
    tm = _row_tile(m, 512)
    x1 = _ffn(x.reshape(m, d), w["f1_pre"], w["f1_post"], w["f1_w1"], w["f1_w3"], w["f1_w2"], tm, 512)
    p_a = _norm_matmul(x1, w["mix_pre"], w["w_in_a"], tm, "proj_rwkv")
    p_b = _norm_matmul(x1, w["mix_pre"], w["w_in_b"], tm, "proj_nsa")
    p_c = _norm_matmul(x1, w["mix_pre"], w["w_in_c"], tm, "proj_gates")

    p3 = p_a.reshape(b, t, RW_PROJ)
    pre, g_rw = _rwkv_inputs(p3, shift0, w)

    pb3 = p_b.reshape(b, t, PB_COLS)
    kv_new = pb3[:, :, PB_KV:PB_WIN]
    win_new = pb3[:, :, PB_WIN:PB_GATE]
    row_w = 4 * NSA_KV_DIM
    half = 2 * NSA_KV_DIM
    if past is None:
        blocks = p_b.reshape(m // CMP_BLOCK, CMP_BLOCK, PB_COLS)
        kc = _cmp_mlp(blocks, w["cmp_pe"], w["cmp_w1"], w["cmp_w2"], min(64, blocks.shape[0]), PB_KV // half)
        n_cmp = t // CMP_BLOCK
        kc = jnp.pad(kc.reshape(b, n_cmp, 2 * NSA_KV_DIM), ((0, 0), (0, -n_cmp % LANES), (0, 0)))
        y_nsa = _nsa_prompt(pb3, kc, *_bias_tiles(rel_bias, t // Q_BLOCK)).reshape(m, NSA_DIM)
        win_out = win_new[:, t - min(WINDOW, t):]
        y_rw, s_fin = _rwkv_recurrence(pre, s0, w, run_after=y_nsa[:SUBLANES])
    else:
        page_table, pool, cache_win, start_after = past
        past_len = page_table.shape[1] * pool.shape[1]
        slabs = row_w // NSA_HEAD_DIM
        pool4 = pool.reshape(pool.shape[0], pool.shape[1], slabs, NSA_HEAD_DIM)
        kc = _paged_kv(page_table, pool4, w["cmp_pe_tiles"], w["cmp_w1cat"], w["cmp_w2"])
        y_rw, s_fin = _rwkv_recurrence(pre, s0, w)
        past_sel = _gather_selected(_available_after(page_table, start_after), pool4)
        n_cmp = past_len // CMP_BLOCK
        kc = jnp.pad(kc, ((0, 0), (0, -n_cmp % LANES), (0, 0)))
        pad_q = lambda a, rows: jnp.pad(a, ((0, 0), (0, rows - t), (0, 0)))
        cwin = cache_win.reshape(b, cache_win.shape[1], half)
        y8 = _nsa_sample(rel_bias, pad_q(pb3[:, :, :NSA_DIM], SAMPLE_ROWS), pad_q(pb3[:, :, PB_GATE:], SAMPLE_ROWS),
                         kc, past_sel, pad_q(pb3[:, :, PB_SEL:PB_WIN].astype(BF16), LANES), cwin,
                         pad_q(win_new.astype(BF16), LANES), past_len)
        y_nsa = y8[:, :t].reshape(m, NSA_DIM)
        win_all = jnp.concatenate([cwin, win_new], axis=1)
        win_out = win_all[:, win_all.shape[1] - min(WINDOW, win_all.shape[1]):]

    x2 = _merge(x1, y_rw, g_rw, y_nsa, p_c, w["w_br_rw"], w["w_br_nsa"], w["w_out"], w["mix_post"],
                _row_tile(m, 256))
    y = _ffn(x2, w["f2_pre"], w["f2_post"], w["f2_w1"], w["f2_w3"], w["f2_w2"], tm, 512)
    g, hd = NSA_KV_HEADS, NSA_HEAD_DIM
    return (y.reshape(b, t, d), kv_new.reshape(b, t, 4, g, hd), win_out.reshape(b, -1, 2, g, hd), s_fin, p3[:, -1],
            y_nsa[:SUBLANES])


def _prepare_weights(l, ffn1_pre_g, ffn1_post_g, ffn1_w1, ffn1_w3, ffn1_w2, mix_pre_g, mix_post_g, w_in,
                     rw_mu, rw_w0, rw_w2, rw_a0, rw_a2, rw_g2, rw_k_k, rw_k_a, rw_r_k, rw_lnx_w, rw_lnx_b,
                     cmp_pe, cmp_w1, cmp_w2, w_br_rw, w_br_nsa, w_out,
                     ffn2_pre_g, ffn2_post_g, ffn2_w1, ffn2_w3, ffn2_w2):
    d = w_in.shape[1]
    wi = w_in[l]
    c_q = RW_PROJ
    c_kv = c_q + NSA_DIM
    c_gate = c_kv + 6 * NSA_KV_DIM
    c_grw = c_gate + 3 * NSA_HEADS
    w_in_b = jnp.concatenate([wi[:, c_q:c_gate], jnp.pad(wi[:, c_gate:c_grw], ((0, 0), (0, LANES - 3 * NSA_HEADS)))],
                             axis=1)
    zeros_w = jnp.zeros((RW_LORA_A, RW_DIM), F32)
    zeros_a = jnp.zeros((RW_LORA_W, RW_DIM), F32)
    pe = cmp_pe[l]
    pe_rows = jnp.broadcast_to(pe[:, :, None, :], (CMP_BLOCK, 2, NSA_KV_HEADS, NSA_HEAD_DIM))
    return {
        "f1_pre": ffn1_pre_g[l], "f1_post": ffn1_post_g[l],
        "f1_w1": ffn1_w1[l].astype(BF16), "f1_w3": ffn1_w3[l].astype(BF16), "f1_w2": ffn1_w2[l].astype(BF16),
        "mix_pre": mix_pre_g[l], "mix_post": mix_post_g[l],
        "w_in_a": wi[:, :RW_PROJ].astype(BF16), "w_in_b": w_in_b.astype(BF16), "w_in_c": wi[:, c_grw:].astype(BF16),
        "mu": rw_mu[l], "w0": rw_w0[l], "a0": rw_a0[l],
        "ww": jnp.concatenate([rw_w2[l], zeros_w], axis=0).astype(BF16),
        "wa": jnp.concatenate([zeros_a, rw_a2[l]], axis=0).astype(BF16),
        "g2": rw_g2[l].astype(BF16), "k_k": rw_k_k[l], "k_a": rw_k_a[l],
        "r_k": rw_r_k[l].reshape(-1), "lnx_w": rw_lnx_w[l], "lnx_b": rw_lnx_b[l],
        "cmp_pe": pe_rows.reshape(CMP_BLOCK, 2 * NSA_KV_DIM),
        "cmp_pe_tiles": jnp.pad(pe_rows.reshape(CMP_BLOCK, 2 * NSA_KV_HEADS, NSA_HEAD_DIM),
                                ((0, 0), (0, 2 * NSA_KV_HEADS), (0, 0))),
        "cmp_w1": cmp_w1[l].astype(BF16), "cmp_w2": cmp_w2[l].astype(BF16),
        "cmp_w1cat": jnp.concatenate([cmp_w1[l, 0], cmp_w1[l, 1]], axis=1).astype(BF16),
        "w_br_rw": w_br_rw[l].astype(BF16), "w_br_nsa": w_br_nsa[l].astype(BF16), "w_out": w_out[l].astype(BF16),
        "f2_pre": ffn2_pre_g[l], "f2_post": ffn2_post_g[l],
        "f2_w1": ffn2_w1[l].astype(BF16), "f2_w3": ffn2_w3[l].astype(BF16), "f2_w2": ffn2_w2[l].astype(BF16),
    }


def kernel(x_prompt, x_sample, cache_kv, cache_win, state_rwkv, state_shift, page_table,
           ffn1_pre_g, ffn1_post_g, ffn1_w1, ffn1_w3, ffn1_w2, mix_pre_g, mix_post_g, w_in,
           rw_mu, rw_w0, rw_w2, rw_a0, rw_a2, rw_g2, rw_k_k, rw_k_a, rw_r_k, rw_lnx_w, rw_lnx_b,
           cmp_pe, cmp_w1, cmp_w2, w_br_rw, w_br_nsa, w_out,
           ffn2_pre_g, ffn2_post_g, ffn2_w1, ffn2_w3, ffn2_w2, rel_bias):
    depth = w_in.shape[0]
    b_p = x_prompt.shape[0]
    y_p, y_s = x_prompt, x_sample
    outs = [[] for _ in range(8)]
    for l in range(depth):
        w = _prepare_weights(l, ffn1_pre_g, ffn1_post_g, ffn1_w1, ffn1_w3, ffn1_w2, mix_pre_g, mix_post_g, w_in,
                             rw_mu, rw_w0, rw_w2, rw_a0, rw_a2, rw_g2, rw_k_k, rw_k_a, rw_r_k, rw_lnx_w, rw_lnx_b,
                             cmp_pe, cmp_w1, cmp_w2, w_br_rw, w_br_nsa, w_out,
                             ffn2_pre_g, ffn2_post_g, ffn2_w1, ffn2_w3, ffn2_w2)
        y_p, kv_p, win_p, rw_p, sh_p, regrouped = _layer(
            y_p, w, rel_bias, jnp.zeros((b_p, RW_PROJ), F32),
            jnp.zeros((b_p, RW_HEADS, RW_HEAD_DIM, RW_HEAD_DIM), F32), None)
        y_s, kv_s, win_s, rw_s, sh_s, _ = _layer(
            y_s, w, rel_bias, state_shift[l], state_rwkv[l], (page_table, cache_kv[l], cache_win[l], regrouped))
        for acc, v in zip(outs, (kv_p, kv_s, win_p, win_s, rw_p, rw_s, sh_p, sh_s)):
            acc.append(v)
    return (y_p, y_s) + tuple(jnp.stack(o) for o in outs)
```

```python
import functools
import math

import jax
import jax.numpy as jnp
import numpy as np
from jax import lax
from jax.experimental import pallas as pl
from jax.experimental.pallas import tpu as pltpu
from jax.experimental.pallas import tpu_sc as plsc

F32 = jnp.float32
BF16 = jnp.bfloat16

RMS_EPS = 1e-6
RW_HEADS = 16
RW_HEAD_DIM = 64
RW_DIM = RW_HEADS * RW_HEAD_DIM
RW_LORA_W = 64
RW_LORA_A = 64
RW_LORA_G = 128
RW_PROJ = 3 * RW_DIM + RW_LORA_W + RW_LORA_A + RW_LORA_G
RW_LN_EPS = 64e-5
NSA_HEADS = 8
NSA_KV_HEADS = 2
NSA_HPG = NSA_HEADS // NSA_KV_HEADS
NSA_HEAD_DIM = 128
NSA_DIM = NSA_HEADS * NSA_HEAD_DIM
NSA_KV_DIM = NSA_KV_HEADS * NSA_HEAD_DIM
CMP_BLOCK = 32
CMP_HID = 256
SEL_BLOCK = 64
SEL_SHIFT = SEL_BLOCK.bit_length() - 1
TOP_N = 16
WINDOW = 512
N_BUCKETS = 32
REL_MAX_EXACT = 16
REL_MAX_DIST = 1024
Q_BLOCK = 128
NEG_BIG = -1e30
FORCE_BONUS = 1e4
NEVER = -3e38
LOG2_E = math.log2(math.e)
LANES = 128
SUBLANES = 8
SCAN_FIELDS = 6
VMEM_LIMIT = 56 * 1024 * 1024
SCAN_UNROLL = 16
UPDATE_UNROLL = 16

PB_Q = 0
PB_KV = NSA_DIM
PB_SEL = PB_KV + 2 * NSA_KV_DIM
PB_WIN = PB_KV + 4 * NSA_KV_DIM
PB_GATE = PB_WIN + 2 * NSA_KV_DIM
PB_COLS = PB_GATE + LANES


def _bucket_thresholds():
    thr = list(range(1, REL_MAX_EXACT + 1))
    n_log = N_BUCKETS - REL_MAX_EXACT
    ratio = REL_MAX_DIST // REL_MAX_EXACT
    n = REL_MAX_EXACT
    for k in range(1, n_log):
        while n ** n_log < REL_MAX_EXACT ** n_log * ratio ** k:
            n += 1
        thr.append(n)
    return thr


BUCKET_THR = _bucket_thresholds()
FAR_DIST = BUCKET_THR[-1]


def _cparams(sem):
    return pltpu.CompilerParams(dimension_semantics=sem, vmem_limit_bytes=VMEM_LIMIT)


def _rms(x, g):
    ms = jnp.mean(x * x, axis=-1, keepdims=True)
    return x * lax.rsqrt(ms + RMS_EPS) * g


def _dot(a, b):
    return jnp.dot(a, b, preferred_element_type=F32)


def _dot_nt(a, b):
    return lax.dot_general(a, b, (((1,), (1,)), ((), ())), preferred_element_type=F32)


def _bias_for_heads(dist, rb_ref, heads):
    n = jnp.maximum(dist, 0)
    reach = [n >= t for t in BUCKET_THR]
    out = []
    for h in heads:
        val = jnp.full(dist.shape, rb_ref[0, h], F32)
        for b, m in enumerate(reach):
            val = jnp.where(m, rb_ref[b + 1, h], val)
        out.append(val)
    return out


def _softmax_update(carry, q, kt, vt, bias, mask, scale):
    m_run, l_run, acc = carry
    s = _dot_nt(q, kt) * scale + bias
    s = jnp.where(mask, s, NEG_BIG)
    m_new = jnp.maximum(m_run, jnp.max(s, axis=-1, keepdims=True))
    alpha = jnp.exp(m_run - m_new)
    e = jnp.where(mask, jnp.exp(s - m_new), 0.0)
    l_new = alpha * l_run + jnp.sum(e, axis=-1, keepdims=True)
    acc = alpha * acc + _dot(e.astype(BF16), vt)
    return m_new, l_new, acc


def _softmax_init(rows):
    return (jnp.full((rows, 1), NEG_BIG, F32), jnp.zeros((rows, 1), F32),
            jnp.zeros((rows, NSA_HEAD_DIM), F32))


def _softmax_finish(carry):
    _, l_run, acc = carry
    return acc / jnp.maximum(l_run, 1e-30)


def _softmax_update_t(carry, q, kt, v_t, bias_t, mask_t):
    m_run, l_run, acc = carry
    s = _dot_nt(kt, q) + bias_t
    s = jnp.where(mask_t, s, NEG_BIG)
    m_new = jnp.maximum(m_run, jnp.max(s, axis=0, keepdims=True))
    alpha = jnp.exp2(m_run - m_new)
    e = jnp.where(mask_t, jnp.exp2(s - m_new), 0.0)
    l_new = alpha * l_run + jnp.sum(e, axis=0, keepdims=True)
    acc = alpha * acc + _dot(v_t, e.astype(BF16))
    return m_new, l_new, acc


def _softmax_init_t(rows):
    return (jnp.full((1, rows), NEG_BIG, F32), jnp.zeros((1, rows), F32),
            jnp.zeros((NSA_HEAD_DIM, rows), F32))


def _stack_heads(x):
    return jnp.concatenate([x] * NSA_HPG, axis=0)


def _stack_heads_t(x):
    return jnp.concatenate([x] * NSA_HPG, axis=1)


def _topk_mask(val, k):
    lane = lax.broadcasted_iota(jnp.int32, val.shape, 1)
    chosen = jnp.zeros(val.shape, F32)
    for _ in range(k):
        hit = lane == jnp.argmax(val, axis=-1, keepdims=True)
        chosen = jnp.where(hit, 1.0, chosen)
        val = jnp.where(hit, NEVER, val)
    return chosen


def _pair_sum(x):
    parts = []
    for c in range(x.shape[1] // LANES):
        blk = x[:, c * LANES:(c + 1) * LANES]
        parts.append(blk + pltpu.roll(blk, LANES - 1, 1))
    return parts[0] if len(parts) == 1 else jnp.concatenate(parts, axis=1)


def _cmp_branch(qg, kc_k, kc_v, bias, visible, scale):
    s = _dot_nt(qg, kc_k) * scale + bias
    s = jnp.where(visible, s, NEG_BIG)
    e = jnp.where(visible, jnp.exp(s - jnp.max(s, axis=-1, keepdims=True)), 0.0)
    p = e / jnp.maximum(jnp.sum(e, axis=-1, keepdims=True), 1e-30)
    return _dot(p.astype(BF16), kc_v), p


def _selection_values(p, qb, pos, n_sel):
    imp = p[0:qb]
    for h in range(1, NSA_HPG):
        imp = imp + p[h * qb:(h + 1) * qb]
    return _selection_values_of(imp, pos, n_sel)


def _selection_values_of(imp, pos, n_sel):
    imp = _pair_sum(imp)
    lane = lax.broadcasted_iota(jnp.int32, imp.shape, 1)
    sel_id = lane >> 1
    cur = pos >> SEL_SHIFT
    forced = (sel_id == 0) | (sel_id == cur) | (sel_id == cur - 1)
    val = jnp.where(forced, imp + FORCE_BONUS, imp)
    val = jnp.where(sel_id * SEL_BLOCK <= pos, val, NEG_BIG)
    return jnp.where(((lane & 1) == 0) & (sel_id < n_sel), val, NEVER)


def _expand_selection(chosen_bf16, tile, tile_keys=LANES):
    width = chosen_bf16.shape[1]
    n_idx = lax.broadcasted_iota(jnp.int32, (width, tile_keys), 0)
    k_idx = lax.broadcasted_iota(jnp.int32, (width, tile_keys), 1)
    target = 2 * ((tile * tile_keys + k_idx) >> SEL_SHIFT)
    expand = jnp.where(n_idx == target, 1.0, 0.0).astype(BF16)
    return _dot(chosen_bf16, expand) > 0.5


def _ffn_body(x_ref, pre_ref, post_ref, w1_ref, w3_ref, w2_ref, o_ref, h_ref, acc_ref):
    j = pl.program_id(1)

    @pl.when(j == 0)
    def _():
        h_ref[...] = _rms(x_ref[...], pre_ref[...]).astype(BF16)
        acc_ref[...] = jnp.zeros_like(acc_ref)

    h = h_ref[...]
    a = _dot(h, w1_ref[...])
    b = _dot(h, w3_ref[...])
    u = (a * jax.nn.sigmoid(a)) * b
    acc_ref[...] += _dot(u.astype(BF16), w2_ref[...])

    @pl.when(j == pl.num_programs(1) - 1)
    def _():
        o_ref[...] = x_ref[...] + 0.5 * _rms(acc_ref[...], post_ref[...])


def _ffn(x, pre_g, post_g, w1, w3, w2, tm, tf):
    m, d = x.shape
    f = w1.shape[1]
    return pl.pallas_call(
        _ffn_body,
        grid=(m // tm, f // tf),
        in_specs=[
            pl.BlockSpec((tm, d), lambda i, j: (i, 0)),
            pl.BlockSpec((1, d), lambda i, j: (0, 0)),
            pl.BlockSpec((1, d), lambda i, j: (0, 0)),
            pl.BlockSpec((d, tf), lambda i, j: (0, j)),
            pl.BlockSpec((d, tf), lambda i, j: (0, j)),
            pl.BlockSpec((tf, d), lambda i, j: (j, 0)),
        ],
        out_specs=pl.BlockSpec((tm, d), lambda i, j: (i, 0)),
        out_shape=jax.ShapeDtypeStruct((m, d), F32),
        scratch_shapes=[pltpu.VMEM((tm, d), BF16), pltpu.VMEM((tm, d), F32)],
        compiler_params=_cparams(("parallel", "arbitrary")),
        name="ffn",
    )(x, pre_g.reshape(1, d), post_g.reshape(1, d), w1, w3, w2)


def _norm_matmul_body(x_ref, g_ref, w_ref, o_ref, h_ref):
    @pl.when(pl.program_id(1) == 0)
    def _():
        h_ref[...] = _rms(x_ref[...], g_ref[...]).astype(BF16)

    o_ref[...] = _dot(h_ref[...], w_ref[...])


def _norm_matmul(x, g, w, tm, name):
    m, d = x.shape
    n = w.shape[1]
    return pl.pallas_call(
        _norm_matmul_body,
        grid=(m // tm, 1),
        in_specs=[
            pl.BlockSpec((tm, d), lambda i, j: (i, 0)),
            pl.BlockSpec((1, d), lambda i, j: (0, 0)),
            pl.BlockSpec((d, n), lambda i, j: (0, 0), pipeline_mode=pl.Buffered(1)),
        ],
        out_specs=pl.BlockSpec((tm, n), lambda i, j: (i, 0)),
        out_shape=jax.ShapeDtypeStruct((m, n), F32),
        scratch_shapes=[pltpu.VMEM((tm, d), BF16)],
        compiler_params=_cparams(("parallel", "arbitrary")),
        name=name,
    )(x, g.reshape(1, d), w)


def _rwkv_pre_body(p_ref, prev_ref, mu_ref, w0_ref, ww_ref, a0_ref, wa_ref, g2_ref, kk_ref, ka_ref,
                   pre_ref, g_ref, buf_ref):
    tt = p_ref.shape[1]
    p = p_ref[0]
    buf_ref[8:8 + tt, :] = p
    buf_ref[7:8, :] = prev_ref[0, 0]
    p_prev = buf_ref[7:7 + tt, :]
    xs = p + (p_prev - p) * mu_ref[...]
    d = RW_DIM
    r = xs[:, 0:d]
    k = xs[:, d:2 * d]
    v = xs[:, 2 * d:3 * d]
    lora_in = xs[:, 3 * d:3 * d + RW_LORA_W + RW_LORA_A]
    gd = xs[:, 3 * d + RW_LORA_W + RW_LORA_A:]
    z = -(w0_ref[...] + _dot(jnp.tanh(lora_in).astype(BF16), ww_ref[...]))
    softplus = jnp.maximum(z, 0.0) + jnp.log(1.0 + jnp.exp(-jnp.abs(z)))
    decay = jnp.exp(-jnp.exp(-softplus - 0.5))
    a = jax.nn.sigmoid(a0_ref[...] + _dot(lora_in.astype(BF16), wa_ref[...]))
    g_ref[0] = _dot(jax.nn.sigmoid(gd).astype(BF16), g2_ref[...])

    n = RW_HEAD_DIM
    pairs = d // LANES
    low_half = lax.broadcasted_iota(jnp.int32, (tt, LANES), 1) < n

    def put(fp, xa, xb):
        for pair in range(pairs):
            col_a = xa[:, pair * LANES:(pair + 1) * LANES]
            col_b = xb[:, pair * LANES:(pair + 1) * LANES]
            pieces = (jnp.where(low_half, col_a, pltpu.roll(col_b, n, 1)),
                      jnp.where(low_half, pltpu.roll(col_a, n, 1), col_b))
            for parity, piece in enumerate(pieces):
                if tt % SUBLANES:
                    pre_ref[:, fp, parity, pair, :] = piece
                else:
                    q = (fp * 2 + parity) * pairs + pair
                    pre_ref[:, q, :, :] = piece.reshape(tt // SUBLANES, SUBLANES, LANES)

    put(0, r, k * (1.0 + (a - 1.0) * ka_ref[...]))
    put(1, v, decay)
    put(2, k * kk_ref[...], a)


def _rwkv_pre(p3, prev, mu, w0, ww, a0, wa, g2, k_k, k_a, tt):
    b, t, c = p3.shape
    d = RW_DIM
    pairs = RW_HEADS // 2
    row = lambda n: pl.BlockSpec((1, n), lambda i, j: (0, 0))
    full = lambda s: pl.BlockSpec(s, lambda i, j: (0, 0))
    n_pieces = SCAN_FIELDS * pairs
    if tt % SUBLANES:
        pre_spec = pl.BlockSpec((tt, SCAN_FIELDS // 2, 2, pairs, LANES), lambda i, j: (j, 0, 0, i, 0))
        pre_shape = (t, SCAN_FIELDS // 2, 2, b * pairs, LANES)
    else:
        tiles, steps = tt // SUBLANES, t // tt
        pre_spec = pl.BlockSpec((tiles, n_pieces, SUBLANES, LANES), lambda i, j: (i * steps + j, 0, 0, 0))
        pre_shape = (b * t // SUBLANES, n_pieces, SUBLANES, LANES)
    return pl.pallas_call(
        _rwkv_pre_body,
        grid=(b, t // tt),
        in_specs=[
            pl.BlockSpec((1, tt, c), lambda i, j: (i, j, 0)),
            pl.BlockSpec((1, 1, 1, c), lambda i, j: (i, j, 0, 0)),
            row(c), row(d), full(ww.shape), row(d), full(wa.shape), full(g2.shape), row(d), row(d),
        ],
        out_specs=[pre_spec, pl.BlockSpec((1, tt, d), lambda i, j: (i, j, 0))],
        out_shape=[jax.ShapeDtypeStruct(pre_shape, F32), jax.ShapeDtypeStruct((b, t, d), F32)],
        scratch_shapes=[pltpu.VMEM((tt + 8, c), F32)],
        compiler_params=_cparams(("parallel", "arbitrary")),
        name="rwkv_pre",
    )(p3, prev, mu.reshape(1, c), w0.reshape(1, d), ww, a0.reshape(1, d), wa, g2,
      k_k.reshape(1, d), k_a.reshape(1, d))


def _rwkv_scan_body(in_ref, s0_ref, lnw_ref, lnb_ref, rk_ref, y_ref, s_ref, xa_ref, xb_ref, kk_ref, b_ref,
                    out_ref):
    n = RW_HEAD_DIM
    steps = in_ref.shape[0]
    n_fields = in_ref.shape[1]
    half = in_ref.shape[3]
    lanes = 2 * half
    fields_per_trip = -(-n_fields // (n // UPDATE_UNROLL))

    def load_field(t, dst_ref, fp):
        tile = in_ref[t, fp].reshape(lanes, 2 * n)
        dst_ref[pl.ds(pl.multiple_of(fp * 2 * n, 2 * n), 2 * n), :] = tile.T

    def normalise_key(x_ref):
        kk_raw = x_ref[4 * n:5 * n, :]
        norm = jnp.sqrt(jnp.sum(kk_raw * kk_raw, axis=0, keepdims=True))
        kk = kk_raw / jnp.maximum(norm, 1e-12)
        kk_ref[...] = kk
        b_ref[...] = kk * x_ref[5 * n:6 * n, :]

    def one_step(t, x_ref, next_ref):
        t_next = jnp.minimum(t + 1, steps - 1)

        def sk_rows(jb, acc):
            for u in range(SCAN_UNROLL):
                j = jb * SCAN_UNROLL + u
                acc = acc + s_ref[j] * kk_ref[pl.ds(j, 1), :]
            return acc

        sk = lax.fori_loop(0, n // SCAN_UNROLL, sk_rows, jnp.zeros((n, lanes), F32))
        v = x_ref[2 * n:3 * n, :]

        def update_rows(jb, acc):
            for f in range(fields_per_trip):
                load_field(t_next, next_ref, jnp.minimum(jb * fields_per_trip + f, n_fields - 1))
            emit_output(jnp.maximum(t - 1, 0))
            for u in range(UPDATE_UNROLL):
                j = jb * UPDATE_UNROLL + u
                r_j = x_ref[pl.ds(j, 1), :]
                k_j = x_ref[pl.ds(n + j, 1), :]
                w_j = x_ref[pl.ds(3 * n + j, 1), :]
                s_j = s_ref[j] * w_j - sk * b_ref[pl.ds(j, 1), :] + v * k_j
                s_ref[j] = s_j
                acc = acc + s_j * r_j
            return acc

        y = lax.fori_loop(0, n // UPDATE_UNROLL, update_rows, jnp.zeros((n, lanes), F32))
        normalise_key(next_ref)
        mean = jnp.sum(y, axis=0, keepdims=True) * (1.0 / n)
        yc = y - mean
        var = jnp.sum(yc * yc, axis=0, keepdims=True) * (1.0 / n)
        y_norm = yc * lax.rsqrt(var + RW_LN_EPS) * lnw_ref[...] + lnb_ref[...]
        rkr = x_ref[0:n, :] * x_ref[n:2 * n, :] * rk_ref[...]
        out_ref[...] = y_norm + jnp.sum(rkr, axis=0, keepdims=True) * v

    def emit_output(t):
        out = out_ref[...]
        y_ref[t] = jnp.concatenate([out[:, :half], out[:, half:]], axis=0).T

    @pl.when(pl.program_id(1) == 0)
    def _():
        s_ref[...] = s0_ref[...]

    out_ref[...] = jnp.zeros_like(out_ref)
    for c in range(n_fields):
        load_field(0, xa_ref, c)
    normalise_key(xa_ref)

    def step_pair(tp, carry):
        one_step(2 * tp, xa_ref, xb_ref)
        one_step(2 * tp + 1, xb_ref, xa_ref)
        return carry

    lax.fori_loop(0, steps // 2, step_pair, 0)
    emit_output(steps - 1)


def _rwkv_scan(pre5, s0, lnw, lnb, rk, tc, seqs):
    t, field_pairs, _, rows, _ = pre5.shape
    n = RW_HEAD_DIM
    pairs = RW_HEADS // 2
    x_rows = field_pairs * 2 * n
    lb = seqs * RW_HEADS
    assert tc % 2 == 0 and t % tc == 0
    tab = pl.BlockSpec((n, lb), lambda l, i: (0, 0))
    return pl.pallas_call(
        _rwkv_scan_body,
        grid=(rows // (seqs * pairs), t // tc),
        in_specs=[
            pl.BlockSpec((tc, field_pairs, 2, seqs * pairs, LANES), lambda l, i: (i, 0, 0, l, 0)),
            pl.BlockSpec((n, n, lb), lambda l, i: (0, 0, l)),
            tab, tab, tab,
        ],
        out_specs=[
            pl.BlockSpec((tc, seqs * pairs, LANES), lambda l, i: (i, l, 0)),
            pl.BlockSpec((n, n, lb), lambda l, i: (0, 0, l)),
        ],
        out_shape=[jax.ShapeDtypeStruct((t, rows, LANES), F32),
                   jax.ShapeDtypeStruct((n, n, rows * 2), F32)],
        scratch_shapes=[pltpu.VMEM((x_rows, lb), F32), pltpu.VMEM((x_rows, lb), F32),
                        pltpu.VMEM((n, lb), F32), pltpu.VMEM((n, lb), F32), pltpu.VMEM((n, lb), F32)],
        compiler_params=_cparams(("parallel", "arbitrary")),
        name="rwkv_scan",
    )(pre5, s0, lnw, lnb, rk)


GATHER_PAGES = 16


GATHER_WINDOW = 128


def _gather_rows(rows, idx):
    n = idx.shape[1]
    width = rows.shape[1]
    mesh = plsc.VectorSubcoreMesh(core_axis_name="core", subcore_axis_name="subcore")

    @pl.kernel(out_type=jax.ShapeDtypeStruct((n, width), rows.dtype), mesh=mesh)
    def gather_kernel(x_hbm, i_hbm, o_hbm):
        def body(i_vmem, o_vmem):
            pltpu.sync_copy(x_hbm.at[i_vmem.at[0]], o_vmem)

        pltpu.emit_pipeline(
            body,
            grid=(n // GATHER_WINDOW,),
            in_specs=[pl.BlockSpec((1, GATHER_WINDOW), index_map=lambda i: (0, i))],
            out_specs=[pl.BlockSpec((GATHER_WINDOW, width), index_map=lambda i: (i, 0))],
            core_axis_name=("core", "subcore"),
            dimension_semantics=(pltpu.PARALLEL,),
        )(i_hbm, o_hbm)

    return gather_kernel(rows, idx)


def _available_after_body(x_ref, anchor_ref, o_ref):
    o_ref[...] = x_ref[...]


def _available_after(x, anchor):
    return pl.pallas_call(_available_after_body, out_shape=jax.ShapeDtypeStruct(x.shape, x.dtype),
                          name="available_after")(x, anchor)


def _gather_selected(page_table, pool):
    bs, n_pages = page_table.shape
    n_pool, page, slabs, hd = pool.shape
    past_len = n_pages * page
    tok = jnp.arange(past_len, dtype=jnp.int32)
    row0 = (page_table[:, tok // page] * page + tok % page) * slabs
    idx = row0[:, None, :] + (slabs // 2 + jnp.arange(slabs // 2, dtype=jnp.int32))[None, :, None]
    out = _gather_rows(pool.reshape(n_pool * page * slabs, hd), idx.reshape(1, -1))
    return out.reshape(bs, slabs // 2, past_len, hd)


def _paged_kv_body(pt_ref, *refs):
    n_pages = len(refs) - 5
    page_refs = refs[:n_pages]
    pe_ref, w1_ref, w2_ref, kc_ref, out_ref = refs[n_pages:]
    _, rows, slabs, hd = page_refs[0].shape
    per_page = rows // CMP_BLOCK
    cols = []
    for r in range(CMP_BLOCK):
        tiles = [x_ref[0, nl * CMP_BLOCK + r] + pe_ref[r] for x_ref in page_refs for nl in range(per_page)]
        cols.append(jnp.concatenate(tiles, axis=0).astype(BF16))
    hid = jax.nn.gelu(_dot(jnp.concatenate(cols, axis=1), w1_ref[...]), approximate=True)
    out_ref[0] = _dot(hid[:, :CMP_HID].astype(BF16), w2_ref[0])
    out_ref[1] = _dot(hid[:, CMP_HID:].astype(BF16), w2_ref[1])
    n_blocks = n_pages * per_page
    for kg in range(slabs // 2):
        kc_ref[0, :, kg * hd:(kg + 1) * hd] = out_ref[kg // NSA_KV_HEADS, pl.ds(kg, n_blocks, stride=slabs), :]


def _paged_kv(page_table, pool, pe_tiles, w1cat, w2):
    bs, n_pages = page_table.shape
    _, page, slabs, hd = pool.shape
    half = slabs * hd // 2
    per_step = math.gcd(GATHER_PAGES, n_pages)
    blocks_step = per_step * page // CMP_BLOCK

    def page_spec(k):
        return pl.BlockSpec((1, page, slabs, hd), lambda b, p, pt: (pt[b, p * per_step + k], 0, 0, 0))

    held = lambda a: pl.BlockSpec(a.shape, lambda b, p, pt: (0,) * a.ndim)
    return pl.pallas_call(
        _paged_kv_body,
        grid_spec=pltpu.PrefetchScalarGridSpec(
            num_scalar_prefetch=1,
            grid=(bs, n_pages // per_step),
            in_specs=[page_spec(k) for k in range(per_step)] + [held(pe_tiles), held(w1cat), held(w2)],
            out_specs=pl.BlockSpec((1, blocks_step, half), lambda b, p, pt: (b, p, 0)),
            scratch_shapes=[pltpu.VMEM((2, blocks_step * slabs, hd), F32)],
        ),
        out_shape=jax.ShapeDtypeStruct((bs, n_pages * page // CMP_BLOCK, half), F32),
        compiler_params=_cparams(("parallel", "arbitrary")),
        name="paged_kv",
    )(page_table, *([pool] * per_step), pe_tiles, w1cat, w2)


def _cmp_mlp_body(x_ref, pe_ref, w1_ref, w2_ref, o_ref):
    hd = NSA_HEAD_DIM
    for kg in range(2 * NSA_KV_HEADS):
        ch = kg // NSA_KV_HEADS
        acc = None
        for r in range(CMP_BLOCK):
            xr = x_ref[:, r, kg * hd:(kg + 1) * hd]
            xr = (xr + pe_ref[r:r + 1, kg * hd:(kg + 1) * hd]).astype(BF16)
            part = _dot(xr, w1_ref[ch, r * hd:(r + 1) * hd, :])
            acc = part if acc is None else acc + part
        hid = jax.nn.gelu(acc, approximate=True)
        o_ref[:, kg * hd:(kg + 1) * hd] = _dot(hid.astype(BF16), w2_ref[ch])


def _cmp_mlp(blocks, pe_rows, w1, w2, nb, col_block):
    n_blocks, rows, _ = blocks.shape
    out_w = 2 * NSA_KV_DIM
    return pl.pallas_call(
        _cmp_mlp_body,
        grid=(n_blocks // nb,),
        in_specs=[
            pl.BlockSpec((nb, rows, out_w), lambda i: (i, 0, col_block)),
            pl.BlockSpec(pe_rows.shape, lambda i: (0, 0)),
            pl.BlockSpec(w1.shape, lambda i: (0, 0, 0)),
            pl.BlockSpec(w2.shape, lambda i: (0, 0, 0)),
        ],
        out_specs=pl.BlockSpec((nb, out_w), lambda i: (i, 0)),
        out_shape=jax.ShapeDtypeStruct((n_blocks, out_w), F32),
        compiler_params=_cparams(("parallel",)),
        name="cmp_mlp",
    )(blocks, pe_rows, w1, w2)


N_BIAS_TILES = -(-(FAR_DIST + LANES - 1) // LANES) + 1


def _bias_tiles_body(rb_ref, o_ref, cmp_ref):
    h = pl.program_id(0)
    key = lax.broadcasted_iota(jnp.int32, (LANES, Q_BLOCK), 0)
    qry = lax.broadcasted_iota(jnp.int32, (LANES, Q_BLOCK), 1)

    def bias_of(dist):
        return _bias_for_heads(dist, rb_ref, [h])[0] * LOG2_E

    for m in range(N_BIAS_TILES):
        o_ref[0, m] = bias_of(m * LANES + qry - key)
    for i in range(cmp_ref.shape[1]):
        cmp_ref[0, i] = bias_of(i * Q_BLOCK + qry - (key * CMP_BLOCK + CMP_BLOCK - 1))


def _bias_tiles(rel_bias, n_q_blocks):
    tile = (Q_BLOCK, LANES)
    return pl.pallas_call(
        _bias_tiles_body,
        grid=(NSA_HEADS,),
        in_specs=[pl.BlockSpec(memory_space=pltpu.SMEM)],
        out_specs=[pl.BlockSpec((1, N_BIAS_TILES) + tile, lambda h: (h, 0, 0, 0)),
                   pl.BlockSpec((1, n_q_blocks) + tile, lambda h: (h, 0, 0, 0))],
        out_shape=[jax.ShapeDtypeStruct((NSA_HEADS, N_BIAS_TILES) + tile, F32),
                   jax.ShapeDtypeStruct((NSA_HEADS, n_q_blocks) + tile, F32)],
        compiler_params=_cparams(("arbitrary",)),
        name="bias_tiles",
    )(rel_bias)


def _nsa_prompt_body(q_ref, gate_ref, sel_ref, win_ref, kc_ref, bt_ref, cb_ref, y_ref,
                     selk_ref, selvt_ref, wink_ref, winvt_ref, *, n_sel):
    hd = NSA_HEAD_DIM
    qb = Q_BLOCK
    i = pl.program_id(1)
    t_len = selk_ref.shape[0]

    @pl.when(i == 0)
    def _():
        for src_ref, k_ref, vt_ref in ((sel_ref, selk_ref, selvt_ref), (win_ref, wink_ref, winvt_ref)):
            k_ref[...] = src_ref[0, :, :NSA_KV_DIM].astype(BF16)
            for c in range(t_len // LANES):
                rows = slice(c * LANES, (c + 1) * LANES)
                vt_ref[:, rows] = src_ref[0, rows, NSA_KV_DIM:].T.astype(BF16)

    q_bf = (q_ref[0] * (hd ** -0.5 * LOG2_E)).astype(BF16)
    gates_t = jax.nn.sigmoid(gate_ref[0]).T
    iq = lax.broadcasted_iota(jnp.int32, (qb, LANES), 0)
    pos = i * qb + iq
    key = lax.broadcasted_iota(jnp.int32, (LANES, qb), 0)
    qry = lax.broadcasted_iota(jnp.int32, (LANES, qb), 1)
    vis_cmp = _stack_heads_t(i * qb + qry - (key * CMP_BLOCK + CMP_BLOCK - 1) >= 0)
    top_n = min(TOP_N, n_sel)
    sel_tiles = SEL_TILES if t_len % (SEL_TILES * LANES) == 0 else 1
    sel_keys = sel_tiles * LANES
    win_tiles = min(WINDOW // LANES + 1, t_len // LANES)
    win_keys = win_tiles * LANES
    groups = [[g * NSA_HPG + h for h in range(NSA_HPG)] for g in range(NSA_KV_HEADS)]

    def tile_bias(heads, first_tile, n_tiles):
        tiles = []
        for k in range(n_tiles):
            m = jnp.clip(i - (first_tile + k), 0, N_BIAS_TILES - 1)
            tiles.append(jnp.concatenate([bt_ref[hh, m] for hh in heads], axis=1))
        return tiles[0] if n_tiles == 1 else jnp.concatenate(tiles, axis=0)

    def selection_mask(chosen_t, tile, tile_keys):
        n_idx = lax.broadcasted_iota(jnp.int32, (tile_keys, LANES), 1)
        k_idx = lax.broadcasted_iota(jnp.int32, (tile_keys, LANES), 0)
        expand = jnp.where(n_idx == 2 * ((tile * tile_keys + k_idx) >> SEL_SHIFT), 1.0, 0.0).astype(BF16)
        return _dot(expand, chosen_t) > 0.5

    qgs, o_cmps, values = [], [], []
    for g, heads in enumerate(groups):
        qg = jnp.concatenate([q_bf[:, hh * hd:(hh + 1) * hd] for hh in heads], axis=0)
        kc_k = kc_ref[0, :, g * hd:(g + 1) * hd].astype(BF16)
        kc_vt = kc_ref[0, :, NSA_KV_DIM + g * hd:NSA_KV_DIM + (g + 1) * hd].T.astype(BF16)
        s = _dot_nt(kc_k, qg) + jnp.concatenate([cb_ref[hh, 0] for hh in heads], axis=1)
        s = jnp.where(vis_cmp, s, NEG_BIG)
        e = jnp.where(vis_cmp, jnp.exp2(s - jnp.max(s, axis=0, keepdims=True)), 0.0)
        p = e / jnp.maximum(jnp.sum(e, axis=0, keepdims=True), 1e-30)
        o_cmps.append(_dot(kc_vt, p.astype(BF16)))
        imp_t = p[:, 0:qb]
        for h in range(1, NSA_HPG):
            imp_t = imp_t + p[:, h * qb:(h + 1) * qb]
        qgs.append(qg)
        values.append(_selection_values_of(imp_t.T, pos, n_sel))
    chosen = _topk_mask(jnp.concatenate(values, axis=0), top_n)
    chosens = [chosen[g * qb:(g + 1) * qb].T.astype(BF16) for g in range(NSA_KV_HEADS)]

    key_s = lax.broadcasted_iota(jnp.int32, (sel_keys, qb), 0)
    qry_s = lax.broadcasted_iota(jnp.int32, (sel_keys, qb), 1)

    def sel_step(j, carries):
        row0 = pl.multiple_of(j * sel_keys, sel_keys)
        causal = i * qb + qry_s - (j * sel_keys + key_s) >= 0
        out = []
        for g, heads in enumerate(groups):
            mask = _stack_heads_t(selection_mask(chosens[g], j, sel_keys) & causal)
            out.append(_softmax_update_t(carries[g], qgs[g], selk_ref[pl.ds(row0, sel_keys), g * hd:(g + 1) * hd],
                                         selvt_ref[g * hd:(g + 1) * hd, pl.ds(row0, sel_keys)],
                                         tile_bias(heads, j * sel_tiles, sel_tiles), mask))
        return tuple(out)

    n_steps = (i + sel_tiles) // sel_tiles
    sel_carries = lax.fori_loop(0, n_steps, sel_step, tuple(_softmax_init_t(NSA_HPG * qb) for _ in groups))

    first = jnp.maximum(i - (win_tiles - 1), 0)
    row0 = pl.multiple_of(first * LANES, LANES)
    d_win = (i * qb + lax.broadcasted_iota(jnp.int32, (win_keys, qb), 1)
             - (first * LANES + lax.broadcasted_iota(jnp.int32, (win_keys, qb), 0)))
    mask_win = _stack_heads_t((d_win >= 0) & (d_win < WINDOW))
    for g, heads in enumerate(groups):
        o_sel = _softmax_finish(sel_carries[g])
        o_win = _softmax_finish(_softmax_update_t(
            _softmax_init_t(NSA_HPG * qb), qgs[g], wink_ref[pl.ds(row0, win_keys), g * hd:(g + 1) * hd],
            winvt_ref[g * hd:(g + 1) * hd, pl.ds(row0, win_keys)], tile_bias(heads, first, win_tiles),
            mask_win))
        for h, hh in enumerate(heads):
            cols = slice(h * qb, (h + 1) * qb)
            out_t = (gates_t[hh:hh + 1, :] * o_cmps[g][:, cols]
                     + gates_t[NSA_HEADS + hh:NSA_HEADS + hh + 1, :] * o_sel[:, cols]
                     + gates_t[2 * NSA_HEADS + hh:2 * NSA_HEADS + hh + 1, :] * o_win[:, cols])
            y_ref[0, :, hh * hd:(hh + 1) * hd] = out_t.T.astype(BF16)


def _nsa_prompt(pb3, kc, bias_tiles, cmp_bias):
    b, t, _ = pb3.shape
    qb = Q_BLOCK
    half = 2 * NSA_KV_DIM
    return pl.pallas_call(
        functools.partial(_nsa_prompt_body, n_sel=t // SEL_BLOCK),
        grid=(b, t // qb),
        in_specs=[
            pl.BlockSpec((1, qb, NSA_DIM), lambda s, i: (s, i, PB_Q // NSA_DIM)),
            pl.BlockSpec((1, qb, LANES), lambda s, i: (s, i, PB_GATE // LANES)),
            pl.BlockSpec((1, t, half), lambda s, i: (s, 0, PB_SEL // half)),
            pl.BlockSpec((1, t, half), lambda s, i: (s, 0, PB_WIN // half)),
            pl.BlockSpec((1,) + kc.shape[1:], lambda s, i: (s, 0, 0)),
            pl.BlockSpec(bias_tiles.shape, lambda s, i: (0, 0, 0, 0)),
            pl.BlockSpec((NSA_HEADS, 1, qb, LANES), lambda s, i: (0, i, 0, 0)),
        ],
        out_specs=pl.BlockSpec((1, qb, NSA_DIM), lambda s, i: (s, i, 0)),
        out_shape=jax.ShapeDtypeStruct((b, t, NSA_DIM), BF16),
        scratch_shapes=[pltpu.VMEM((t, NSA_KV_DIM), BF16), pltpu.VMEM((NSA_KV_DIM, t), BF16),
                        pltpu.VMEM((t, NSA_KV_DIM), BF16), pltpu.VMEM((NSA_KV_DIM, t), BF16)],
        compiler_params=_cparams(("parallel", "arbitrary")),
        name="nsa_prompt",
    )(pb3, pb3, pb3, pb3, kc, bias_tiles, cmp_bias)


SEL_TILES = 2
SAMPLE_ROWS = 8
SAMPLE_TILE_KEYS = 1024


def _nsa_sample_body(rb_ref, q_ref, gate_ref, kc_ref, past_ref, new_sel_ref, cwin_ref, new_win_ref, y_ref,
                     *, past_len):
    hd = NSA_HEAD_DIM
    qb = SAMPLE_ROWS
    scale = hd ** -0.5
    n_cmp = kc_ref.shape[1]
    win_rows = cwin_ref.shape[1]
    k_off = past_len - win_rows
    tile_keys = min(SAMPLE_TILE_KEYS, past_len)
    assert past_len % tile_keys == 0

    q_bf = q_ref[0].astype(BF16)
    gates = jax.nn.sigmoid(gate_ref[0])
    iq = lax.broadcasted_iota(jnp.int32, (qb, LANES), 0)
    ik = lax.broadcasted_iota(jnp.int32, (qb, LANES), 1)
    pos_t = past_len + lax.broadcasted_iota(jnp.int32, (qb, tile_keys), 0)
    ik_t = lax.broadcasted_iota(jnp.int32, (qb, tile_keys), 1)
    d_win = (past_len + lax.broadcasted_iota(jnp.int32, (qb, win_rows), 0)
             - (k_off + lax.broadcasted_iota(jnp.int32, (qb, win_rows), 1)))
    tq_c = lax.broadcasted_iota(jnp.int32, (qb, n_cmp), 0)
    n_c = lax.broadcasted_iota(jnp.int32, (qb, n_cmp), 1)
    pos_c = past_len + tq_c
    d_cmp = pos_c - (n_c * CMP_BLOCK + CMP_BLOCK - 1)
    bias_cmp = _bias_for_heads(d_cmp, rb_ref, range(NSA_HEADS))
    vis_cmp = _stack_heads(d_cmp >= 0)
    n_sel_past = past_len // SEL_BLOCK
    picks = min(TOP_N, n_sel_past + 1) - 1

    groups = [[g * NSA_HPG + h for h in range(NSA_HPG)] for g in range(NSA_KV_HEADS)]

    def key_cols(g):
        return slice(g * hd, (g + 1) * hd), slice(NSA_KV_DIM + g * hd, NSA_KV_DIM + (g + 1) * hd)

    qgs, o_cmps, values = [], [], []
    for g, heads in enumerate(groups):
        k_cols, v_cols = key_cols(g)
        qg = jnp.concatenate([q_bf[:, hh * hd:(hh + 1) * hd] for hh in heads], axis=0)
        o_cmp, p_cmp = _cmp_branch(qg, kc_ref[0, :, k_cols].astype(BF16), kc_ref[0, :, v_cols].astype(BF16),
                                   jnp.concatenate([bias_cmp[hh] for hh in heads], axis=0), vis_cmp, scale)
        qgs.append(qg)
        o_cmps.append(o_cmp)
        values.append(_selection_values(p_cmp, qb, pos_c, n_sel_past))
    chosen = _topk_mask(jnp.concatenate(values, axis=0), picks).astype(BF16)
    chosens = [chosen[g * qb:(g + 1) * qb] for g in range(NSA_KV_HEADS)]

    def sel_step(j, carries):
        row0 = pl.multiple_of(j * tile_keys, tile_keys)
        dist = pos_t - (j * tile_keys + ik_t)
        out = []
        for g, heads in enumerate(groups):
            k_cols, v_cols = key_cols(g)
            mask = _stack_heads(_expand_selection(chosens[g], j, tile_keys))
            bias = jnp.concatenate(_bias_for_heads(dist, rb_ref, heads), axis=0)
            kt = past_ref[0, g, pl.ds(row0, tile_keys), :].astype(BF16)
            vt = past_ref[0, NSA_KV_HEADS + g, pl.ds(row0, tile_keys), :].astype(BF16)
            out.append(_softmax_update(carries[g], qgs[g], kt, vt, bias, mask, scale))
        return tuple(out)

    sel_carries = lax.fori_loop(0, past_len // tile_keys, sel_step,
                                tuple(_softmax_init(NSA_HPG * qb) for _ in groups))
    d_new = iq - ik
    mask_new = _stack_heads(d_new >= 0)
    mask_win = _stack_heads((d_win >= 0) & (d_win < WINDOW))
    for g, heads in enumerate(groups):
        k_cols, v_cols = key_cols(g)
        qg, o_cmp = qgs[g], o_cmps[g]
        bias_new = jnp.concatenate(_bias_for_heads(d_new, rb_ref, heads), axis=0)
        o_sel = _softmax_finish(_softmax_update(sel_carries[g], qg, new_sel_ref[0, :, k_cols],
                                                new_sel_ref[0, :, v_cols], bias_new, mask_new, scale))
        bias_win = jnp.concatenate(_bias_for_heads(d_win, rb_ref, heads), axis=0)
        carry = _softmax_update(_softmax_init(NSA_HPG * qb), qg, cwin_ref[0, :, k_cols].astype(BF16),
                                cwin_ref[0, :, v_cols].astype(BF16), bias_win, mask_win, scale)
        o_win = _softmax_finish(_softmax_update(carry, qg, new_win_ref[0, :, k_cols], new_win_ref[0, :, v_cols],
                                                bias_new, mask_new, scale))

        for h, hh in enumerate(heads):
            rows = slice(h * qb, (h + 1) * qb)
            out = (gates[:, hh:hh + 1] * o_cmp[rows]
                   + gates[:, NSA_HEADS + hh:NSA_HEADS + hh + 1] * o_sel[rows]
                   + gates[:, 2 * NSA_HEADS + hh:2 * NSA_HEADS + hh + 1] * o_win[rows])
            y_ref[0, :, hh * hd:(hh + 1) * hd] = out.astype(BF16)


def _nsa_sample(rel_bias, q8, gate8, kc, past_sel, new_sel, cwin, new_win, past_len):
    bs = q8.shape[0]
    blk = lambda a: pl.BlockSpec((1,) + a.shape[1:], lambda s: (s,) + (0,) * (a.ndim - 1))
    return pl.pallas_call(
        functools.partial(_nsa_sample_body, past_len=past_len),
        grid=(bs,),
        in_specs=[pl.BlockSpec(memory_space=pltpu.SMEM), blk(q8), blk(gate8), blk(kc), blk(past_sel),
                  blk(new_sel), blk(cwin), blk(new_win)],
        out_specs=pl.BlockSpec((1, SAMPLE_ROWS, NSA_DIM), lambda s: (s, 0, 0)),
        out_shape=jax.ShapeDtypeStruct((bs, SAMPLE_ROWS, NSA_DIM), BF16),
        compiler_params=_cparams(("parallel",)),
        name="nsa_sample",
    )(rel_bias, q8, gate8, kc, past_sel, new_sel, cwin, new_win)


def _merge_body(x_ref, yrw_ref, g_ref, ynsa_ref, gates_ref, wrw_ref, wnsa_ref, wout_ref, post_ref, o_ref):
    tm, d = x_ref.shape
    y_rw = jnp.concatenate([yrw_ref[:, pair, :] for pair in range(RW_DIM // LANES)], axis=1)
    y_rw = (y_rw * g_ref[...]).astype(BF16)
    merged = (jax.nn.sigmoid(gates_ref[:, :d]) * _dot(y_rw, wrw_ref[...])
              + jax.nn.sigmoid(gates_ref[:, d:]) * _dot(ynsa_ref[...], wnsa_ref[...]))
    o_ref[...] = x_ref[...] + _rms(_dot(merged.astype(BF16), wout_ref[...]), post_ref[...])


def _merge(x, y_rw, g_rw, y_nsa, gates, w_rw, w_nsa, w_out, post_g, tm):
    m, d = x.shape
    rows = lambda n: pl.BlockSpec((tm, n), lambda i: (i, 0))
    held = lambda a: pl.BlockSpec(a.shape, lambda i: (0, 0), pipeline_mode=pl.Buffered(1))
    return pl.pallas_call(
        _merge_body,
        grid=(m // tm,),
        in_specs=[rows(d), pl.BlockSpec((tm,) + y_rw.shape[1:], lambda i: (i, 0, 0)),
                  rows(RW_DIM), rows(NSA_DIM), rows(2 * d),
                  held(w_rw), held(w_nsa), held(w_out), pl.BlockSpec((1, d), lambda i: (0, 0))],
        out_specs=rows(d),
        out_shape=jax.ShapeDtypeStruct((m, d), F32),
        compiler_params=_cparams(("parallel",)),
        name="merge",
    )(x, y_rw, g_rw, y_nsa, gates, w_rw, w_nsa, w_out, post_g.reshape(1, d))


def _row_tile(m, want):
    return want if m % want == 0 else m


def _rwkv_inputs(p3, shift0, w):
    b, t, c = p3.shape
    tt = 256 if t % 256 == 0 else t
    prev = jnp.concatenate([shift0[:, None, :], p3[:, tt - 1:t - 1:tt]], axis=1).reshape(b, t // tt, 1, c)
    pre, g_rw = _rwkv_pre(p3, prev, w["mu"], w["w0"], w["ww"], w["a0"], w["wa"], w["g2"], w["k_k"], w["k_a"], tt)
    pairs = RW_HEADS // 2
    if tt % SUBLANES == 0:
        n_pieces = SCAN_FIELDS * pairs
        tok = np.arange(b, dtype=np.int32)[None, :] * t + np.arange(t, dtype=np.int32)[:, None]
        row0 = (tok // SUBLANES) * (n_pieces * SUBLANES) + tok % SUBLANES
        piece = np.arange(n_pieces, dtype=np.int32).reshape(SCAN_FIELDS // 2, 2, 1, pairs)
        idx = row0[:, None, None, :, None] + piece[None] * SUBLANES
        pre = _gather_rows(pre.reshape(-1, LANES), jnp.asarray(idx.reshape(1, -1))).reshape(
            t, SCAN_FIELDS // 2, 2, b * pairs, LANES)
    return pre, g_rw.reshape(b * t, RW_DIM)


def _rwkv_recurrence(pre, s0, w, run_after=None):
    t = pre.shape[0]
    b = s0.shape[0]
    n = RW_HEAD_DIM
    seqs = min(b, LANES // RW_HEADS)
    groups = b // seqs
    pairs = RW_HEADS // 2
    s0_t = s0.reshape(groups, seqs, pairs, 2, n, n).transpose(5, 4, 0, 3, 1, 2).reshape(n, n, b * RW_HEADS)
    if run_after is not None:
        s0_t = _available_after(s0_t, run_after)
    per_lane = lambda v: jnp.broadcast_to(v.reshape(pairs, 2, n).transpose(2, 1, 0)[:, :, None, :],
                                          (n, 2, seqs, pairs)).reshape(n, seqs * RW_HEADS)
    tc = 32 if t % 32 == 0 else t
    y_t, s_t = _rwkv_scan(pre, s0_t, per_lane(w["lnx_w"]), per_lane(w["lnx_b"]), per_lane(w["r_k"]), tc, seqs)
    s_fin = s_t.reshape(n, n, groups, 2, seqs, pairs).transpose(2, 4, 5, 3, 1, 0).reshape(b, RW_HEADS, n, n)
    y_tok = y_t.reshape(t, b, pairs, LANES).transpose(1, 0, 2, 3).reshape(b * t, pairs, LANES)
    return y_tok, s_fin


def _layer(x, w, rel_bias, shift0, s0, past):
    b, t, d = x.shape
    m = b * t
    tm = _row_tile(m, 512)
    x1 = _ffn(x.reshape(m, d), w["f1_pre"], w["f1_post"], w["f1_w1"], w["f1_w3"], w["f1_w2"], tm, 512)
    p_a = _norm_matmul(x1, w["mix_pre"], w["w_in_a"], tm, "proj_rwkv")
    p_b = _norm_matmul(x1, w["mix_pre"], w["w_in_b"], tm, "proj_nsa")
    p_c = _norm_matmul(x1, w["mix_pre"], w["w_in_c"], tm, "proj_gates")

    p3 = p_a.reshape(b, t, RW_PROJ)
    pre, g_rw = _rwkv_inputs(p3, shift0, w)

    pb3 = p_b.reshape(b, t, PB_COLS)
    kv_new = pb3[:, :, PB_KV:PB_WIN]
    win_new = pb3[:, :, PB_WIN:PB_GATE]
    row_w = 4 * NSA_KV_DIM
    half = 2 * NSA_KV_DIM
    if past is None:
        blocks = p_b.reshape(m // CMP_BLOCK, CMP_BLOCK, PB_COLS)
        kc = _cmp_mlp(blocks, w["cmp_pe"], w["cmp_w1"], w["cmp_w2"], min(64, blocks.shape[0]), PB_KV // half)
        n_cmp = t // CMP_BLOCK
        kc = jnp.pad(kc.reshape(b, n_cmp, 2 * NSA_KV_DIM), ((0, 0), (0, -n_cmp % LANES), (0, 0)))
        y_nsa = _nsa_prompt(pb3, kc, *_bias_tiles(rel_bias, t // Q_BLOCK)).reshape(m, NSA_DIM)
        win_out = win_new[:, t - min(WINDOW, t):]
        y_rw, s_fin = _rwkv_recurrence(pre, s0, w, run_after=y_nsa[:SUBLANES])
    else:
        page_table, pool, cache_win, start_after = past
        past_len = page_table.shape[1] * pool.shape[1]
        slabs = row_w // NSA_HEAD_DIM
        pool4 = pool.reshape(pool.shape[0], pool.shape[1], slabs, NSA_HEAD_DIM)
        kc = _paged_kv(page_table, pool4, w["cmp_pe_tiles"], w["cmp_w1cat"], w["cmp_w2"])
        y_rw, s_fin = _rwkv_recurrence(pre, s0, w)
        past_sel = _gather_selected(_available_after(page_table, start_after), pool4)
        n_cmp = past_len // CMP_BLOCK
        kc = jnp.pad(kc, ((0, 0), (0, -n_cmp % LANES), (0, 0)))
        pad_q = lambda a, rows: jnp.pad(a, ((0, 0), (0, rows - t), (0, 0)))
        cwin = cache_win.reshape(b, cache_win.shape[1], half)
        y8 = _nsa_sample(rel_bias, pad_q(pb3[:, :, :NSA_DIM], SAMPLE_ROWS), pad_q(pb3[:, :, PB_GATE:], SAMPLE_ROWS),
                         kc, past_sel, pad_q(pb3[:, :, PB_SEL:PB_WIN].astype(BF16), LANES), cwin,
                         pad_q(win_new.astype(BF16), LANES), past_len)
        y_nsa = y8[:, :t].reshape(m, NSA_DIM)
        win_all = jnp.concatenate([cwin, win_new], axis=1)
        win_out = win_all[:, win_all.shape[1] - min(WINDOW, win_all.shape[1]):]

    x2 = _merge(x1, y_rw, g_rw, y_nsa, p_c, w["w_br_rw"], w["w_br_nsa"], w["w_out"], w["mix_post"],
                _row_tile(m, 256))
    y = _ffn(x2, w["f2_pre"], w["f2_post"], w["f2_w1"], w["f2_w3"], w["f2_w2"], tm, 512)
    g, hd = NSA_KV_HEADS, NSA_HEAD_DIM
    return (y.reshape(b, t, d), kv_new.reshape(b, t, 4, g, hd), win_out.reshape(b, -1, 2, g, hd), s_fin, p3[:, -1],
            y_nsa[:SUBLANES])


def _prepare_weights(l, ffn1_pre_g, ffn1_post_g, ffn1_w1, ffn1_w3, ffn1_w2, mix_pre_g, mix_post_g, w_in,
                     rw_mu, rw_w0, rw_w2, rw_a0, rw_a2, rw_g2, rw_k_k, rw_k_a, rw_r_k, rw_lnx_w, rw_lnx_b,
                     cmp_pe, cmp_w1, cmp_w2, w_br_rw, w_br_nsa, w_out,
                     ffn2_pre_g, ffn2_post_g, ffn2_w1, ffn2_w3, ffn2_w2):
    d = w_in.shape[1]
    wi = w_in[l]
    c_q = RW_PROJ
    c_kv = c_q + NSA_DIM
    c_gate = c_kv + 6 * NSA_KV_DIM
    c_grw = c_gate + 3 * NSA_HEADS
    w_in_b = jnp.concatenate([wi[:, c_q:c_gate], jnp.pad(wi[:, c_gate:c_grw], ((0, 0), (0, LANES - 3 * NSA_HEADS)))],
                             axis=1)
    zeros_w = jnp.zeros((RW_LORA_A, RW_DIM), F32)
    zeros_a = jnp.zeros((RW_LORA_W, RW_DIM), F32)
    pe = cmp_pe[l]
    pe_rows = jnp.broadcast_to(pe[:, :, None, :], (CMP_BLOCK, 2, NSA_KV_HEADS, NSA_HEAD_DIM))
    return {
        "f1_pre": ffn1_pre_g[l], "f1_post": ffn1_post_g[l],
        "f1_w1": ffn1_w1[l].astype(BF16), "f1_w3": ffn1_w3[l].astype(BF16), "f1_w2": ffn1_w2[l].astype(BF16),
        "mix_pre": mix_pre_g[l], "mix_post": mix_post_g[l],
        "w_in_a": wi[:, :RW_PROJ].astype(BF16), "w_in_b": w_in_b.astype(BF16), "w_in_c": wi[:, c_grw:].astype(BF16),
        "mu": rw_mu[l], "w0": rw_w0[l], "a0": rw_a0[l],
        "ww": jnp.concatenate([rw_w2[l], zeros_w], axis=0).astype(BF16),
        "wa": jnp.concatenate([zeros_a, rw_a2[l]], axis=0).astype(BF16),
        "g2": rw_g2[l].astype(BF16), "k_k": rw_k_k[l], "k_a": rw_k_a[l],
        "r_k": rw_r_k[l].reshape(-1), "lnx_w": rw_lnx_w[l], "lnx_b": rw_lnx_b[l],
        "cmp_pe": pe_rows.reshape(CMP_BLOCK, 2 * NSA_KV_DIM),
        "cmp_pe_tiles": jnp.pad(pe_rows.reshape(CMP_BLOCK, 2 * NSA_KV_HEADS, NSA_HEAD_DIM),
                                ((0, 0), (0, 2 * NSA_KV_HEADS), (0, 0))),
        "cmp_w1": cmp_w1[l].astype(BF16), "cmp_w2": cmp_w2[l].astype(BF16),
        "cmp_w1cat": jnp.concatenate([cmp_w1[l, 0], cmp_w1[l, 1]], axis=1).astype(BF16),
        "w_br_rw": w_br_rw[l].astype(BF16), "w_br_nsa": w_br_nsa[l].astype(BF16), "w_out": w_out[l].astype(BF16),
        "f2_pre": ffn2_pre_g[l], "f2_post": ffn2_post_g[l],
        "f2_w1": ffn2_w1[l].astype(BF16), "f2_w3": ffn2_w3[l].astype(BF16), "f2_w2": ffn2_w2[l].astype(BF16),
    }


def kernel(x_prompt, x_sample, cache_kv, cache_win, state_rwkv, state_shift, page_table,
           ffn1_pre_g, ffn1_post_g, ffn1_w1, ffn1_w3, ffn1_w2, mix_pre_g, mix_post_g, w_in,
           rw_mu, rw_w0, rw_w2, rw_a0, rw_a2, rw_g2, rw_k_k, rw_k_a, rw_r_k, rw_lnx_w, rw_lnx_b,
           cmp_pe, cmp_w1, cmp_w2, w_br_rw, w_br_nsa, w_out,
           ffn2_pre_g, ffn2_post_g, ffn2_w1, ffn2_w3, ffn2_w2, rel_bias):
    depth = w_in.shape[0]
    b_p = x_prompt.shape[0]
    y_p, y_s = x_prompt, x_sample
    outs = [[] for _ in range(8)]
    for l in range(depth):
        w = _prepare_weights(l, ffn1_pre_g, ffn1_post_g, ffn1_w1, ffn1_w3, ffn1_w2, mix_pre_g, mix_post_g, w_in,
                             rw_mu, rw_w0, rw_w2, rw_a0, rw_a2, rw_g2, rw_k_k, rw_k_a, rw_r_k, rw_lnx_w, rw_lnx_b,
                             cmp_pe, cmp_w1, cmp_w2, w_br_rw, w_br_nsa, w_out,
                             ffn2_pre_g, ffn2_post_g, ffn2_w1, ffn2_w3, ffn2_w2)
        y_p, kv_p, win_p, rw_p, sh_p, regrouped = _layer(
            y_p, w, rel_bias, jnp.zeros((b_p, RW_PROJ), F32),
            jnp.zeros((b_p, RW_HEADS, RW_HEAD_DIM, RW_HEAD_DIM), F32), None)
        y_s, kv_s, win_s, rw_s, sh_s, _ = _layer(
            y_s, w, rel_bias, state_shift[l], state_rwkv[l], (page_table, cache_kv[l], cache_win[l], regrouped))
        for acc, v in zip(outs, (kv_p, kv_s, win_p, win_s, rw_p, rw_s, sh_p, sh_s)):
            acc.append(v)
    return (y_p, y_s) + tuple(jnp.stack(o) for o in outs)
```

```python
import functools
import math

import jax
import jax.numpy as jnp
import numpy as np
from jax import lax
from jax.experimental import pallas as pl
from jax.experimental.pallas import tpu as pltpu
from jax.experimental.pallas import tpu_sc as plsc

F32 = jnp.float32
BF16 = jnp.bfloat16

RMS_EPS = 1e-6
RW_HEADS = 16
RW_HEAD_DIM = 64
RW_DIM = RW_HEADS * RW_HEAD_DIM
RW_LORA_W = 64
RW_LORA_A = 64
RW_LORA_G = 128
RW_PROJ = 3 * RW_DIM + RW_LORA_W + RW_LORA_A + RW_LORA_G
RW_LN_EPS = 64e-5
NSA_HEADS = 8
NSA_KV_HEADS = 2
NSA_HPG = NSA_HEADS // NSA_KV_HEADS
NSA_HEAD_DIM = 128
NSA_DIM = NSA_HEADS * NSA_HEAD_DIM
NSA_KV_DIM = NSA_KV_HEADS * NSA_HEAD_DIM
CMP_BLOCK = 32
CMP_HID = 256
SEL_BLOCK = 64
SEL_SHIFT = SEL_BLOCK.bit_length() - 1
TOP_N = 16
WINDOW = 512
N_BUCKETS = 32
REL_MAX_EXACT = 16
REL_MAX_DIST = 1024
Q_BLOCK = 128
NEG_BIG = -1e30
FORCE_BONUS = 1e4
NEVER = -3e38
LOG2_E = math.log2(math.e)
LANES = 128
SUBLANES = 8
SCAN_FIELDS = 6
VMEM_LIMIT = 56 * 1024 * 1024
SCAN_UNROLL = 16
UPDATE_UNROLL = 16

PB_Q = 0
PB_KV = NSA_DIM
PB_SEL = PB_KV + 2 * NSA_KV_DIM
PB_WIN = PB_KV + 4 * NSA_KV_DIM
PB_GATE = PB_WIN + 2 * NSA_KV_DIM
PB_COLS = PB_GATE + LANES


def _bucket_thresholds():
    thr = list(range(1, REL_MAX_EXACT + 1))
    n_log = N_BUCKETS - REL_MAX_EXACT
    ratio = REL_MAX_DIST // REL_MAX_EXACT
    n = REL_MAX_EXACT
    for k in range(1, n_log):
        while n ** n_log < REL_MAX_EXACT ** n_log * ratio ** k:
            n += 1
        thr.append(n)
    return thr


BUCKET_THR = _bucket_thresholds()
FAR_DIST = BUCKET_THR[-1]


def _cparams(sem):
    return pltpu.CompilerParams(dimension_semantics=sem, vmem_limit_bytes=VMEM_LIMIT)


def _rms(x, g):
    ms = jnp.mean(x * x, axis=-1, keepdims=True)
    return x * lax.rsqrt(ms + RMS_EPS) * g


def _dot(a, b):
    return jnp.dot(a, b, preferred_element_type=F32)


def _dot_nt(a, b):
    return lax.dot_general(a, b, (((1,), (1,)), ((), ())), preferred_element_type=F32)


def _bias_for_heads(dist, rb_ref, heads):
    n = jnp.maximum(dist, 0)
    reach = [n >= t for t in BUCKET_THR]
    out = []
    for h in heads:
        val = jnp.full(dist.shape, rb_ref[0, h], F32)
        for b, m in enumerate(reach):
            val = jnp.where(m, rb_ref[b + 1, h], val)
        out.append(val)
    return out


def _softmax_update(carry, q, kt, vt, bias, mask, scale):
    m_run, l_run, acc = carry
    s = _dot_nt(q, kt) * scale + bias
    s = jnp.where(mask, s, NEG_BIG)
    m_new = jnp.maximum(m_run, jnp.max(s, axis=-1, keepdims=True))
    alpha = jnp.exp(m_run - m_new)
    e = jnp.where(mask, jnp.exp(s - m_new), 0.0)
    l_new = alpha * l_run + jnp.sum(e, axis=-1, keepdims=True)
    acc = alpha * acc + _dot(e.astype(BF16), vt)
    return m_new, l_new, acc


def _softmax_init(rows):
    return (jnp.full((rows, 1), NEG_BIG, F32), jnp.zeros((rows, 1), F32),
            jnp.zeros((rows, NSA_HEAD_DIM), F32))


def _softmax_finish(carry):
    _, l_run, acc = carry
    return acc / jnp.maximum(l_run, 1e-30)


def _softmax_update_t(carry, q, kt, v_t, bias_t, mask_t):
    m_run, l_run, acc = carry
    s = _dot_nt(kt, q) + bias_t
    s = jnp.where(mask_t, s, NEG_BIG)
    m_new = jnp.maximum(m_run, jnp.max(s, axis=0, keepdims=True))
    alpha = jnp.exp2(m_run - m_new)
    e = jnp.where(mask_t, jnp.exp2(s - m_new), 0.0)
    l_new = alpha * l_run + jnp.sum(e, axis=0, keepdims=True)
    acc = alpha * acc + _dot(v_t, e.astype(BF16))
    return m_new, l_new, acc


def _softmax_init_t(rows):
    return (jnp.full((1, rows), NEG_BIG, F32), jnp.zeros((1, rows), F32),
            jnp.zeros((NSA_HEAD_DIM, rows), F32))


def _stack_heads(x):
    return jnp.concatenate([x] * NSA_HPG, axis=0)


def _stack_heads_t(x):
    return jnp.concatenate([x] * NSA_HPG, axis=1)


def _topk_mask(val, k):
    lane = lax.broadcasted_iota(jnp.int32, val.shape, 1)
    chosen = jnp.zeros(val.shape, F32)
    for _ in range(k):
        hit = lane == jnp.argmax(val, axis=-1, keepdims=True)
        chosen = jnp.where(hit, 1.0, chosen)
        val = jnp.where(hit, NEVER, val)
    return chosen


def _pair_sum(x):
    parts = []
    for c in range(x.shape[1] // LANES):
        blk = x[:, c * LANES:(c + 1) * LANES]
        parts.append(blk + pltpu.roll(blk, LANES - 1, 1))
    return parts[0] if len(parts) == 1 else jnp.concatenate(parts, axis=1)


def _cmp_branch(qg, kc_k, kc_v, bias, visible, scale):
    s = _dot_nt(qg, kc_k) * scale + bias
    s = jnp.where(visible, s, NEG_BIG)
    e = jnp.where(visible, jnp.exp(s - jnp.max(s, axis=-1, keepdims=True)), 0.0)
    p = e / jnp.maximum(jnp.sum(e, axis=-1, keepdims=True), 1e-30)
    return _dot(p.astype(BF16), kc_v), p


def _selection_values(p, qb, pos, n_sel):
    imp = p[0:qb]
    for h in range(1, NSA_HPG):
        imp = imp + p[h * qb:(h + 1) * qb]
    return _selection_values_of(imp, pos, n_sel)


def _selection_values_of(imp, pos, n_sel):
    imp = _pair_sum(imp)
    lane = lax.broadcasted_iota(jnp.int32, imp.shape, 1)
    sel_id = lane >> 1
    cur = pos >> SEL_SHIFT
    forced = (sel_id == 0) | (sel_id == cur) | (sel_id == cur - 1)
    val = jnp.where(forced, imp + FORCE_BONUS, imp)
    val = jnp.where(sel_id * SEL_BLOCK <= pos, val, NEG_BIG)
    return jnp.where(((lane & 1) == 0) & (sel_id < n_sel), val, NEVER)


def _expand_selection(chosen_bf16, tile, tile_keys=LANES):
    width = chosen_bf16.shape[1]
    n_idx = lax.broadcasted_iota(jnp.int32, (width, tile_keys), 0)
    k_idx = lax.broadcasted_iota(jnp.int32, (width, tile_keys), 1)
    target = 2 * ((tile * tile_keys + k_idx) >> SEL_SHIFT)
    expand = jnp.where(n_idx == target, 1.0, 0.0).astype(BF16)
    return _dot(chosen_bf16, expand) > 0.5


def _ffn_body(x_ref, pre_ref, post_ref, w1_ref, w3_ref, w2_ref, o_ref, h_ref, acc_ref):
    j = pl.program_id(1)

    @pl.when(j == 0)
    def _():
        h_ref[...] = _rms(x_ref[...], pre_ref[...]).astype(BF16)
        acc_ref[...] = jnp.zeros_like(acc_ref)

    h = h_ref[...]
    a = _dot(h, w1_ref[...])
    b = _dot(h, w3_ref[...])
    u = (a * jax.nn.sigmoid(a)) * b
    acc_ref[...] += _dot(u.astype(BF16), w2_ref[...])

    @pl.when(j == pl.num_programs(1) - 1)
    def _():
        o_ref[...] = x_ref[...] + 0.5 * _rms(acc_ref[...], post_ref[...])


def _ffn(x, pre_g, post_g, w1, w3, w2, tm, tf):
    m, d = x.shape
    f = w1.shape[1]
    return pl.pallas_call(
        _ffn_body,
        grid=(m // tm, f // tf),
        in_specs=[
            pl.BlockSpec((tm, d), lambda i, j: (i, 0)),
            pl.BlockSpec((1, d), lambda i, j: (0, 0)),
            pl.BlockSpec((1, d), lambda i, j: (0, 0)),
            pl.BlockSpec((d, tf), lambda i, j: (0, j)),
            pl.BlockSpec((d, tf), lambda i, j: (0, j)),
            pl.BlockSpec((tf, d), lambda i, j: (j, 0)),
        ],
        out_specs=pl.BlockSpec((tm, d), lambda i, j: (i, 0)),
        out_shape=jax.ShapeDtypeStruct((m, d), F32),
        scratch_shapes=[pltpu.VMEM((tm, d), BF16), pltpu.VMEM((tm, d), F32)],
        compiler_params=_cparams(("parallel", "arbitrary")),
        name="ffn",
    )(x, pre_g.reshape(1, d), post_g.reshape(1, d), w1, w3, w2)


def _norm_matmul_body(x_ref, g_ref, w_ref, o_ref, h_ref):
    @pl.when(pl.program_id(1) == 0)
    def _():
        h_ref[...] = _rms(x_ref[...], g_ref[...]).astype(BF16)

    o_ref[...] = _dot(h_ref[...], w_ref[...])


def _norm_matmul(x, g, w, tm, name):
    m, d = x.shape
    n = w.shape[1]
    return pl.pallas_call(
        _norm_matmul_body,
        grid=(m // tm, 1),
        in_specs=[
            pl.BlockSpec((tm, d), lambda i, j: (i, 0)),
            pl.BlockSpec((1, d), lambda i, j: (0, 0)),
            pl.BlockSpec((d, n), lambda i, j: (0, 0), pipeline_mode=pl.Buffered(1)),
        ],
        out_specs=pl.BlockSpec((tm, n), lambda i, j: (i, 0)),
        out_shape=jax.ShapeDtypeStruct((m, n), F32),
        scratch_shapes=[pltpu.VMEM((tm, d), BF16)],
        compiler_params=_cparams(("parallel", "arbitrary")),
        name=name,
    )(x, g.reshape(1, d), w)


def _rwkv_pre_body(p_ref, prev_ref, mu_ref, w0_ref, ww_ref, a0_ref, wa_ref, g2_ref, kk_ref, ka_ref,
                   pre_ref, g_ref, buf_ref):
    tt = p_ref.shape[1]
    p = p_ref[0]
    buf_ref[8:8 + tt, :] = p
    buf_ref[7:8, :] = prev_ref[0, 0]
    p_prev = buf_ref[7:7 + tt, :]
    xs = p + (p_prev - p) * mu_ref[...]
    d = RW_DIM
    r = xs[:, 0:d]
    k = xs[:, d:2 * d]
    v = xs[:, 2 * d:3 * d]
    lora_in = xs[:, 3 * d:3 * d + RW_LORA_W + RW_LORA_A]
    gd = xs[:, 3 * d + RW_LORA_W + RW_LORA_A:]
    z = -(w0_ref[...] + _dot(jnp.tanh(lora_in).astype(BF16), ww_ref[...]))
    softplus = jnp.maximum(z, 0.0) + jnp.log(1.0 + jnp.exp(-jnp.abs(z)))
    decay = jnp.exp(-jnp.exp(-softplus - 0.5))
    a = jax.nn.sigmoid(a0_ref[...] + _dot(lora_in.astype(BF16), wa_ref[...]))
    g_ref[0] = _dot(jax.nn.sigmoid(gd).astype(BF16), g2_ref[...])

    n = RW_HEAD_DIM
    pairs = d // LANES
    low_half = lax.broadcasted_iota(jnp.int32, (tt, LANES), 1) < n

    def put(fp, xa, xb):
        for pair in range(pairs):
            col_a = xa[:, pair * LANES:(pair + 1) * LANES]
            col_b = xb[:, pair * LANES:(pair + 1) * LANES]
            pieces = (jnp.where(low_half, col_a, pltpu.roll(col_b, n, 1)),
                      jnp.where(low_half, pltpu.roll(col_a, n, 1), col_b))
            for parity, piece in enumerate(pieces):
                if tt % SUBLANES:
                    pre_ref[:, fp, parity, pair, :] = piece
                else:
                    q = (fp * 2 + parity) * pairs + pair
                    pre_ref[:, q, :, :] = piece.reshape(tt // SUBLANES, SUBLANES, LANES)

    put(0, r, k * (1.0 + (a - 1.0) * ka_ref[...]))
    put(1, v, decay)
    put(2, k * kk_ref[...], a)


def _rwkv_pre(p3, prev, mu, w0, ww, a0, wa, g2, k_k, k_a, tt):
    b, t, c = p3.shape
    d = RW_DIM
    pairs = RW_HEADS // 2
    row = lambda n: pl.BlockSpec((1, n), lambda i, j: (0, 0))
    full = lambda s: pl.BlockSpec(s, lambda i, j: (0, 0))
    n_pieces = SCAN_FIELDS * pairs
    if tt % SUBLANES:
        pre_spec = pl.BlockSpec((tt, SCAN_FIELDS // 2, 2, pairs, LANES), lambda i, j: (j, 0, 0, i, 0))
        pre_shape = (t, SCAN_FIELDS // 2, 2, b * pairs, LANES)
    else:
        tiles, steps = tt // SUBLANES, t // tt
        pre_spec = pl.BlockSpec((tiles, n_pieces, SUBLANES, LANES), lambda i, j: (i * steps + j, 0, 0, 0))
        pre_shape = (b * t // SUBLANES, n_pieces, SUBLANES, LANES)
    return pl.pallas_call(
        _rwkv_pre_body,
        grid=(b, t // tt),
        in_specs=[
            pl.BlockSpec((1, tt, c), lambda i, j: (i, j, 0)),
            pl.BlockSpec((1, 1, 1, c), lambda i, j: (i, j, 0, 0)),
            row(c), row(d), full(ww.shape), row(d), full(wa.shape), full(g2.shape), row(d), row(d),
        ],
        out_specs=[pre_spec, pl.BlockSpec((1, tt, d), lambda i, j: (i, j, 0))],
        out_shape=[jax.ShapeDtypeStruct(pre_shape, F32), jax.ShapeDtypeStruct((b, t, d), F32)],
        scratch_shapes=[pltpu.VMEM((tt + 8, c), F32)],
        compiler_params=_cparams(("parallel", "arbitrary")),
        name="rwkv_pre",
    )(p3, prev, mu.reshape(1, c), w0.reshape(1, d), ww, a0.reshape(1, d), wa, g2,
      k_k.reshape(1, d), k_a.reshape(1, d))


def _rwkv_scan_body(in_ref, s0_ref, lnw_ref, lnb_ref, rk_ref, y_ref, s_ref, xa_ref, xb_ref, kk_ref, b_ref,
                    out_ref):
    n = RW_HEAD_DIM
    steps = in_ref.shape[0]
    n_fields = in_ref.shape[1]
    half = in_ref.shape[3]
    lanes = 2 * half
    fields_per_trip = -(-n_fields // (n // UPDATE_UNROLL))

    def load_field(t, dst_ref, fp):
        tile = in_ref[t, fp].reshape(lanes, 2 * n)
        dst_ref[pl.ds(pl.multiple_of(fp * 2 * n, 2 * n), 2 * n), :] = tile.T

    def normalise_key(x_ref):
        kk_raw = x_ref[4 * n:5 * n, :]
        norm = jnp.sqrt(jnp.sum(kk_raw * kk_raw, axis=0, keepdims=True))
        kk = kk_raw / jnp.maximum(norm, 1e-12)
        kk_ref[...] = kk
        b_ref[...] = kk * x_ref[5 * n:6 * n, :]

    def one_step(t, x_ref, next_ref):
        t_next = jnp.minimum(t + 1, steps - 1)

        def sk_rows(jb, acc):
            for u in range(SCAN_UNROLL):
                j = jb * SCAN_UNROLL + u
                acc = acc + s_ref[j] * kk_ref[pl.ds(j, 1), :]
            return acc

        sk = lax.fori_loop(0, n // SCAN_UNROLL, sk_rows, jnp.zeros((n, lanes), F32))
        v = x_ref[2 * n:3 * n, :]

        def update_rows(jb, acc):
            for f in range(fields_per_trip):
                load_field(t_next, next_ref, jnp.minimum(jb * fields_per_trip + f, n_fields - 1))
            emit_output(jnp.maximum(t - 1, 0))
            for u in range(UPDATE_UNROLL):
                j = jb * UPDATE_UNROLL + u
                r_j = x_ref[pl.ds(j, 1), :]
                k_j = x_ref[pl.ds(n + j, 1), :]
                w_j = x_ref[pl.ds(3 * n + j, 1), :]
                s_j = s_ref[j] * w_j - sk * b_ref[pl.ds(j, 1), :] + v * k_j
                s_ref[j] = s_j
                acc = acc + s_j * r_j
            return acc

        y = lax.fori_loop(0, n // UPDATE_UNROLL, update_rows, jnp.zeros((n, lanes), F32))
        normalise_key(next_ref)
        mean = jnp.sum(y, axis=0, keepdims=True) * (1.0 / n)
        yc = y - mean
        var = jnp.sum(yc * yc, axis=0, keepdims=True) * (1.0 / n)
        y_norm = yc * lax.rsqrt(var + RW_LN_EPS) * lnw_ref[...] + lnb_ref[...]
        rkr = x_ref[0:n, :] * x_ref[n:2 * n, :] * rk_ref[...]
        out_ref[...] = y_norm + jnp.sum(rkr, axis=0, keepdims=True) * v

    def emit_output(t):
        out = out_ref[...]
        y_ref[t] = jnp.concatenate([out[:, :half], out[:, half:]], axis=0).T

    @pl.when(pl.program_id(1) == 0)
    def _():
        s_ref[...] = s0_ref[...]

    out_ref[...] = jnp.zeros_like(out_ref)
    for c in range(n_fields):
        load_field(0, xa_ref, c)
    normalise_key(xa_ref)

    def step_pair(tp, carry):
        one_step(2 * tp, xa_ref, xb_ref)
        one_step(2 * tp + 1, xb_ref, xa_ref)
        return carry

    lax.fori_loop(0, steps // 2, step_pair, 0)
    emit_output(steps - 1)


def _rwkv_scan(pre5, s0, lnw, lnb, rk, tc, seqs):
    t, field_pairs, _, rows, _ = pre5.shape
    n = RW_HEAD_DIM
    pairs = RW_HEADS // 2
    x_rows = field_pairs * 2 * n
    lb = seqs * RW_HEADS
    assert tc % 2 == 0 and t % tc == 0
    tab = pl.BlockSpec((n, lb), lambda l, i: (0, 0))
    return pl.pallas_call(
        _rwkv_scan_body,
        grid=(rows // (seqs * pairs), t // tc),
        in_specs=[
            pl.BlockSpec((tc, field_pairs, 2, seqs * pairs, LANES), lambda l, i: (i, 0, 0, l, 0)),
            pl.BlockSpec((n, n, lb), lambda l, i: (0, 0, l)),
            tab, tab, tab,
        ],
        out_specs=[
            pl.BlockSpec((tc, seqs * pairs, LANES), lambda l, i: (i, l, 0)),
            pl.BlockSpec((n, n, lb), lambda l, i: (0, 0, l)),
        ],
        out_shape=[jax.ShapeDtypeStruct((t, rows, LANES), F32),
                   jax.ShapeDtypeStruct((n, n, rows * 2), F32)],
        scratch_shapes=[pltpu.VMEM((x_rows, lb), F32), pltpu.VMEM((x_rows, lb), F32),
                        pltpu.VMEM((n, lb), F32), pltpu.VMEM((n, lb), F32), pltpu.VMEM((n, lb), F32)],
        compiler_params=_cparams(("parallel", "arbitrary")),
        name="rwkv_scan",
    )(pre5, s0, lnw, lnb, rk)


GATHER_PAGES = 16


GATHER_WINDOW = 128


def _gather_rows(rows, idx):
    n = idx.shape[1]
    width = rows.shape[1]
    mesh = plsc.VectorSubcoreMesh(core_axis_name="core", subcore_axis_name="subcore")

    @pl.kernel(out_type=jax.ShapeDtypeStruct((n, width), rows.dtype), mesh=mesh)
    def gather_kernel(x_hbm, i_hbm, o_hbm):
        def body(i_vmem, o_vmem):
            pltpu.sync_copy(x_hbm.at[i_vmem.at[0]], o_vmem)

        pltpu.emit_pipeline(
            body,
            grid=(n // GATHER_WINDOW,),
            in_specs=[pl.BlockSpec((1, GATHER_WINDOW), index_map=lambda i: (0, i))],
            out_specs=[pl.BlockSpec((GATHER_WINDOW, width), index_map=lambda i: (i, 0))],
            core_axis_name=("core", "subcore"),
            dimension_semantics=(pltpu.PARALLEL,),
        )(i_hbm, o_hbm)

    return gather_kernel(rows, idx)


def _available_after_body(x_ref, anchor_ref, o_ref):
    o_ref[...] = x_ref[...]


def _available_after(x, anchor):
    return pl.pallas_call(_available_after_body, out_shape=jax.ShapeDtypeStruct(x.shape, x.dtype),
                          name="available_after")(x, anchor)


def _gather_selected(page_table, pool):
    bs, n_pages = page_table.shape
    n_pool, page, slabs, hd = pool.shape
    past_len = n_pages * page
    tok = jnp.arange(past_len, dtype=jnp.int32)
    row0 = (page_table[:, tok // page] * page + tok % page) * slabs
    idx = row0[:, None, :] + (slabs // 2 + jnp.arange(slabs // 2, dtype=jnp.int32))[None, :, None]
    out = _gather_rows(pool.reshape(n_pool * page * slabs, hd), idx.reshape(1, -1))
    return out.reshape(bs, slabs // 2, past_len, hd)


def _paged_kv_body(pt_ref, *refs):
    n_pages = len(refs) - 5
    page_refs = refs[:n_pages]
    pe_ref, w1_ref, w2_ref, kc_ref, out_ref = refs[n_pages:]
    _, rows, slabs, hd = page_refs[0].shape
    per_page = rows // CMP_BLOCK
    cols = []
    for r in range(CMP_BLOCK):
        tiles = [x_ref[0, nl * CMP_BLOCK + r] + pe_ref[r] for x_ref in page_refs for nl in range(per_page)]
        cols.append(jnp.concatenate(tiles, axis=0).astype(BF16))
    hid = jax.nn.gelu(_dot(jnp.concatenate(cols, axis=1), w1_ref[...]), approximate=True)
    out_ref[0] = _dot(hid[:, :CMP_HID].astype(BF16), w2_ref[0])
    out_ref[1] = _dot(hid[:, CMP_HID:].astype(BF16), w2_ref[1])
    n_blocks = n_pages * per_page
    for kg in range(slabs // 2):
        kc_ref[0, :, kg * hd:(kg + 1) * hd] = out_ref[kg // NSA_KV_HEADS, pl.ds(kg, n_blocks, stride=slabs), :]


def _paged_kv(page_table, pool, pe_tiles, w1cat, w2):
    bs, n_pages = page_table.shape
    _, page, slabs, hd = pool.shape
    half = slabs * hd // 2
    per_step = math.gcd(GATHER_PAGES, n_pages)
    blocks_step = per_step * page // CMP_BLOCK

    def page_spec(k):
        return pl.BlockSpec((1, page, slabs, hd), lambda b, p, pt: (pt[b, p * per_step + k], 0, 0, 0))

    held = lambda a: pl.BlockSpec(a.shape, lambda b, p, pt: (0,) * a.ndim)
    return pl.pallas_call(
        _paged_kv_body,
        grid_spec=pltpu.PrefetchScalarGridSpec(
            num_scalar_prefetch=1,
            grid=(bs, n_pages // per_step),
            in_specs=[page_spec(k) for k in range(per_step)] + [held(pe_tiles), held(w1cat), held(w2)],
            out_specs=pl.BlockSpec((1, blocks_step, half), lambda b, p, pt: (b, p, 0)),
            scratch_shapes=[pltpu.VMEM((2, blocks_step * slabs, hd), F32)],
        ),
        out_shape=jax.ShapeDtypeStruct((bs, n_pages * page // CMP_BLOCK, half), F32),
        compiler_params=_cparams(("parallel", "arbitrary")),
        name="paged_kv",
    )(page_table, *([pool] * per_step), pe_tiles, w1cat, w2)


def _cmp_mlp_body(x_ref, pe_ref, w1_ref, w2_ref, o_ref):
    hd = NSA_HEAD_DIM
    for kg in range(2 * NSA_KV_HEADS):
        ch = kg // NSA_KV_HEADS
        acc = None
        for r in range(CMP_BLOCK):
            xr = x_ref[:, r, kg * hd:(kg + 1) * hd]
            xr = (xr + pe_ref[r:r + 1, kg * hd:(kg + 1) * hd]).astype(BF16)
            part = _dot(xr, w1_ref[ch, r * hd:(r + 1) * hd, :])
            acc = part if acc is None else acc + part
        hid = jax.nn.gelu(acc, approximate=True)
        o_ref[:, kg * hd:(kg + 1) * hd] = _dot(hid.astype(BF16), w2_ref[ch])


def _cmp_mlp(blocks, pe_rows, w1, w2, nb, col_block):
    n_blocks, rows, _ = blocks.shape
    out_w = 2 * NSA_KV_DIM
    return pl.pallas_call(
        _cmp_mlp_body,
        grid=(n_blocks // nb,),
        in_specs=[
            pl.BlockSpec((nb, rows, out_w), lambda i: (i, 0, col_block)),
            pl.BlockSpec(pe_rows.shape, lambda i: (0, 0)),
            pl.BlockSpec(w1.shape, lambda i: (0, 0, 0)),
            pl.BlockSpec(w2.shape, lambda i: (0, 0, 0)),
        ],
        out_specs=pl.BlockSpec((nb, out_w), lambda i: (i, 0)),
        out_shape=jax.ShapeDtypeStruct((n_blocks, out_w), F32),
        compiler_params=_cparams(("parallel",)),
        name="cmp_mlp",
    )(blocks, pe_rows, w1, w2)


N_BIAS_TILES = -(-(FAR_DIST + LANES - 1) // LANES) + 1


def _bias_tiles_body(rb_ref, o_ref, cmp_ref):
    h = pl.program_id(0)
    key = lax.broadcasted_iota(jnp.int32, (LANES, Q_BLOCK), 0)
    qry = lax.broadcasted_iota(jnp.int32, (LANES, Q_BLOCK), 1)

    def bias_of(dist):
        return _bias_for_heads(dist, rb_ref, [h])[0] * LOG2_E

    for m in range(N_BIAS_TILES):
        o_ref[0, m] = bias_of(m * LANES + qry - key)
    for i in range(cmp_ref.shape[1]):
        cmp_ref[0, i] = bias_of(i * Q_BLOCK + qry - (key * CMP_BLOCK + CMP_BLOCK - 1))


def _bias_tiles(rel_bias, n_q_blocks):
    tile = (Q_BLOCK, LANES)
    return pl.pallas_call(
        _bias_tiles_body,
        grid=(NSA_HEADS,),
        in_specs=[pl.BlockSpec(memory_space=pltpu.SMEM)],
        out_specs=[pl.BlockSpec((1, N_BIAS_TILES) + tile, lambda h: (h, 0, 0, 0)),
                   pl.BlockSpec((1, n_q_blocks) + tile, lambda h: (h, 0, 0, 0))],
        out_shape=[jax.ShapeDtypeStruct((NSA_HEADS, N_BIAS_TILES) + tile, F32),
                   jax.ShapeDtypeStruct((NSA_HEADS, n_q_blocks) + tile, F32)],
        compiler_params=_cparams(("arbitrary",)),
        name="bias_tiles",
    )(rel_bias)


def _nsa_prompt_body(q_ref, gate_ref, sel_ref, win_ref, kc_ref, bt_ref, cb_ref, y_ref,
                     selk_ref, selvt_ref, wink_ref, winvt_ref, *, n_sel):
    hd = NSA_HEAD_DIM
    qb = Q_BLOCK
    i = pl.program_id(1)
    t_len = selk_ref.shape[0]

    @pl.when(i == 0)
    def _():
        for src_ref, k_ref, vt_ref in ((sel_ref, selk_ref, selvt_ref), (win_ref, wink_ref, winvt_ref)):
            k_ref[...] = src_ref[0, :, :NSA_KV_DIM].astype(BF16)
            for c in range(t_len // LANES):
                rows = slice(c * LANES, (c + 1) * LANES)
                vt_ref[:, rows] = src_ref[0, rows, NSA_KV_DIM:].T.astype(BF16)

    q_bf = (q_ref[0] * (hd ** -0.5 * LOG2_E)).astype(BF16)
    gates_t = jax.nn.sigmoid(gate_ref[0]).T
    iq = lax.broadcasted_iota(jnp.int32, (qb, LANES), 0)
    pos = i * qb + iq
    key = lax.broadcasted_iota(jnp.int32, (LANES, qb), 0)
    qry = lax.broadcasted_iota(jnp.int32, (LANES, qb), 1)
    vis_cmp = _stack_heads_t(i * qb + qry - (key * CMP_BLOCK + CMP_BLOCK - 1) >= 0)
    top_n = min(TOP_N, n_sel)
    sel_tiles = SEL_TILES if t_len % (SEL_TILES * LANES) == 0 else 1
    sel_keys = sel_tiles * LANES
    win_tiles = min(WINDOW // LANES + 1, t_len // LANES)
    win_keys = win_tiles * LANES
    groups = [[g * NSA_HPG + h for h in range(NSA_HPG)] for g in range(NSA_KV_HEADS)]

    def tile_bias(heads, first_tile, n_tiles):
        tiles = []
        for k in range(n_tiles):
            m = jnp.clip(i - (first_tile + k), 0, N_BIAS_TILES - 1)
            tiles.append(jnp.concatenate([bt_ref[hh, m] for hh in heads], axis=1))
        return tiles[0] if n_tiles == 1 else jnp.concatenate(tiles, axis=0)

    def selection_mask(chosen_t, tile, tile_keys):
        n_idx = lax.broadcasted_iota(jnp.int32, (tile_keys, LANES), 1)
        k_idx = lax.broadcasted_iota(jnp.int32, (tile_keys, LANES), 0)
        expand = jnp.where(n_idx == 2 * ((tile * tile_keys + k_idx) >> SEL_SHIFT), 1.0, 0.0).astype(BF16)
        return _dot(expand, chosen_t) > 0.5

    qgs, o_cmps, values = [], [], []
    for g, heads in enumerate(groups):
        qg = jnp.concatenate([q_bf[:, hh * hd:(hh + 1) * hd] for hh in heads], axis=0)
        kc_k = kc_ref[0, :, g * hd:(g + 1) * hd].astype(BF16)
        kc_vt = kc_ref[0, :, NSA_KV_DIM + g * hd:NSA_KV_DIM + (g + 1) * hd].T.astype(BF16)
        s = _dot_nt(kc_k, qg) + jnp.concatenate([cb_ref[hh, 0] for hh in heads], axis=1)
        s = jnp.where(vis_cmp, s, NEG_BIG)
        e = jnp.where(vis_cmp, jnp.exp2(s - jnp.max(s, axis=0, keepdims=True)), 0.0)
        p = e / jnp.maximum(jnp.sum(e, axis=0, keepdims=True), 1e-30)
        o_cmps.append(_dot(kc_vt, p.astype(BF16)))
        imp_t = p[:, 0:qb]
        for h in range(1, NSA_HPG):
            imp_t = imp_t + p[:, h * qb:(h + 1) * qb]
        qgs.append(qg)
        values.append(_selection_values_of(imp_t.T, pos, n_sel))
    chosen = _topk_mask(jnp.concatenate(values, axis=0), top_n)
    chosens = [chosen[g * qb:(g + 1) * qb].T.astype(BF16) for g in range(NSA_KV_HEADS)]

    key_s = lax.broadcasted_iota(jnp.int32, (sel_keys, qb), 0)
    qry_s = lax.broadcasted_iota(jnp.int32, (sel_keys, qb), 1)

    def sel_step(j, carries):
        row0 = pl.multiple_of(j * sel_keys, sel_keys)
        causal = i * qb + qry_s - (j * sel_keys + key_s) >= 0
        out = []
        for g, heads in enumerate(groups):
            mask = _stack_heads_t(selection_mask(chosens[g], j, sel_keys) & causal)
            out.append(_softmax_update_t(carries[g], qgs[g], selk_ref[pl.ds(row0, sel_keys), g * hd:(g + 1) * hd],
                                         selvt_ref[g * hd:(g + 1) * hd, pl.ds(row0, sel_keys)],
                                         tile_bias(heads, j * sel_tiles, sel_tiles), mask))
        return tuple(out)

    n_steps = (i + sel_tiles) // sel_tiles
    sel_carries = lax.fori_loop(0, n_steps, sel_step, tuple(_softmax_init_t(NSA_HPG * qb) for _ in groups))

    first = jnp.maximum(i - (win_tiles - 1), 0)
    row0 = pl.multiple_of(first * LANES, LANES)
    d_win = (i * qb + lax.broadcasted_iota(jnp.int32, (win_keys, qb), 1)
             - (first * LANES + lax.broadcasted_iota(jnp.int32, (win_keys, qb), 0)))
    mask_win = _stack_heads_t((d_win >= 0) & (d_win < WINDOW))
    for g, heads in enumerate(groups):
        o_sel = _softmax_finish(sel_carries[g])
        o_win = _softmax_finish(_softmax_update_t(
            _softmax_init_t(NSA_HPG * qb), qgs[g], wink_ref[pl.ds(row0, win_keys), g * hd:(g + 1) * hd],
            winvt_ref[g * hd:(g + 1) * hd, pl.ds(row0, win_keys)], tile_bias(heads, first, win_tiles),
            mask_win))
        for h, hh in enumerate(heads):
            cols = slice(h * qb, (h + 1) * qb)
            out_t = (gates_t[hh:hh + 1, :] * o_cmps[g][:, cols]
                     + gates_t[NSA_HEADS + hh:NSA_HEADS + hh + 1, :] * o_sel[:, cols]
                     + gates_t[2 * NSA_HEADS + hh:2 * NSA_HEADS + hh + 1, :] * o_win[:, cols])
            y_ref[0, :, hh * hd:(hh + 1) * hd] = out_t.T.astype(BF16)


def _nsa_prompt(pb3, kc, bias_tiles, cmp_bias):
    b, t, _ = pb3.shape
    qb = Q_BLOCK
    half = 2 * NSA_KV_DIM
    return pl.pallas_call(
        functools.partial(_nsa_prompt_body, n_sel=t // SEL_BLOCK),
        grid=(b, t // qb),
        in_specs=[
            pl.BlockSpec((1, qb, NSA_DIM), lambda s, i: (s, i, PB_Q // NSA_DIM)),
            pl.BlockSpec((1, qb, LANES), lambda s, i: (s, i, PB_GATE // LANES)),
            pl.BlockSpec((1, t, half), lambda s, i: (s, 0, PB_SEL // half)),
            pl.BlockSpec((1, t, half), lambda s, i: (s, 0, PB_WIN // half)),
            pl.BlockSpec((1,) + kc.shape[1:], lambda s, i: (s, 0, 0)),
            pl.BlockSpec(bias_tiles.shape, lambda s, i: (0, 0, 0, 0)),
            pl.BlockSpec((NSA_HEADS, 1, qb, LANES), lambda s, i: (0, i, 0, 0)),
        ],
        out_specs=pl.BlockSpec((1, qb, NSA_DIM), lambda s, i: (s, i, 0)),
        out_shape=jax.ShapeDtypeStruct((b, t, NSA_DIM), BF16),
        scratch_shapes=[pltpu.VMEM((t, NSA_KV_DIM), BF16), pltpu.VMEM((NSA_KV_DIM, t), BF16),
                        pltpu.VMEM((t, NSA_KV_DIM), BF16), pltpu.VMEM((NSA_KV_DIM, t), BF16)],
        compiler_params=_cparams(("parallel", "arbitrary")),
        name="nsa_prompt",
    )(pb3, pb3, pb3, pb3, kc, bias_tiles, cmp_bias)


SEL_TILES = 4
SAMPLE_ROWS = 8
SAMPLE_TILE_KEYS = 1024


def _nsa_sample_body(rb_ref, q_ref, gate_ref, kc_ref, past_ref, new_sel_ref, cwin_ref, new_win_ref, y_ref,
                     *, past_len):
    hd = NSA_HEAD_DIM
    qb = SAMPLE_ROWS
    scale = hd ** -0.5
    n_cmp = kc_ref.shape[1]
    win_rows = cwin_ref.shape[1]
    k_off = past_len - win_rows
    tile_keys = min(SAMPLE_TILE_KEYS, past_len)
    assert past_len % tile_keys == 0

    q_bf = q_ref[0].astype(BF16)
    gates = jax.nn.sigmoid(gate_ref[0])
    iq = lax.broadcasted_iota(jnp.int32, (qb, LANES), 0)
    ik = lax.broadcasted_iota(jnp.int32, (qb, LANES), 1)
    pos_t = past_len + lax.broadcasted_iota(jnp.int32, (qb, tile_keys), 0)
    ik_t = lax.broadcasted_iota(jnp.int32, (qb, tile_keys), 1)
    d_win = (past_len + lax.broadcasted_iota(jnp.int32, (qb, win_rows), 0)
             - (k_off + lax.broadcasted_iota(jnp.int32, (qb, win_rows), 1)))
    tq_c = lax.broadcasted_iota(jnp.int32, (qb, n_cmp), 0)
    n_c = lax.broadcasted_iota(jnp.int32, (qb, n_cmp), 1)
    pos_c = past_len + tq_c
    d_cmp = pos_c - (n_c * CMP_BLOCK + CMP_BLOCK - 1)
    bias_cmp = _bias_for_heads(d_cmp, rb_ref, range(NSA_HEADS))
    vis_cmp = _stack_heads(d_cmp >= 0)
    n_sel_past = past_len // SEL_BLOCK
    picks = min(TOP_N, n_sel_past + 1) - 1

    groups = [[g * NSA_HPG + h for h in range(NSA_HPG)] for g in range(NSA_KV_HEADS)]

    def key_cols(g):
        return slice(g * hd, (g + 1) * hd), slice(NSA_KV_DIM + g * hd, NSA_KV_DIM + (g + 1) * hd)

    qgs, o_cmps, values = [], [], []
    for g, heads in enumerate(groups):
        k_cols, v_cols = key_cols(g)
        qg = jnp.concatenate([q_bf[:, hh * hd:(hh + 1) * hd] for hh in heads], axis=0)
        o_cmp, p_cmp = _cmp_branch(qg, kc_ref[0, :, k_cols].astype(BF16), kc_ref[0, :, v_cols].astype(BF16),
                                   jnp.concatenate([bias_cmp[hh] for hh in heads], axis=0), vis_cmp, scale)
        qgs.append(qg)
        o_cmps.append(o_cmp)
        values.append(_selection_values(p_cmp, qb, pos_c, n_sel_past))
    chosen = _topk_mask(jnp.concatenate(values, axis=0), picks).astype(BF16)
    chosens = [chosen[g * qb:(g + 1) * qb] for g in range(NSA_KV_HEADS)]

    def sel_step(j, carries):
        row0 = pl.multiple_of(j * tile_keys, tile_keys)
        dist = pos_t - (j * tile_keys + ik_t)
        out = []
        for g, heads in enumerate(groups):
            k_cols, v_cols = key_cols(g)
            mask = _stack_heads(_expand_selection(chosens[g], j, tile_keys))
            bias = jnp.concatenate(_bias_for_heads(dist, rb_ref, heads), axis=0)
            kt = past_ref[0, g, pl.ds(row0, tile_keys), :].astype(BF16)
            vt = past_ref[0, NSA_KV_HEADS + g, pl.ds(row0, tile_keys), :].astype(BF16)
            out.append(_softmax_update(carries[g], qgs[g], kt, vt, bias, mask, scale))
        return tuple(out)

    sel_carries = lax.fori_loop(0, past_len // tile_keys, sel_step,
                                tuple(_softmax_init(NSA_HPG * qb) for _ in groups))
    d_new = iq - ik
    mask_new = _stack_heads(d_new >= 0)
    mask_win = _stack_heads((d_win >= 0) & (d_win < WINDOW))
    for g, heads in enumerate(groups):
        k_cols, v_cols = key_cols(g)
        qg, o_cmp = qgs[g], o_cmps[g]
        bias_new = jnp.concatenate(_bias_for_heads(d_new, rb_ref, heads), axis=0)
        o_sel = _softmax_finish(_softmax_update(sel_carries[g], qg, new_sel_ref[0, :, k_cols],
                                                new_sel_ref[0, :, v_cols], bias_new, mask_new, scale))
        bias_win = jnp.concatenate(_bias_for_heads(d_win, rb_ref, heads), axis=0)
        carry = _softmax_update(_softmax_init(NSA_HPG * qb), qg, cwin_ref[0, :, k_cols].astype(BF16),
                                cwin_ref[0, :, v_cols].astype(BF16), bias_win, mask_win, scale)
        o_win = _softmax_finish(_softmax_update(carry, qg, new_win_ref[0, :, k_cols], new_win_ref[0, :, v_cols],
                                                bias_new, mask_new, scale))

        for h, hh in enumerate(heads):
            rows = slice(h * qb, (h + 1) * qb)
            out = (gates[:, hh:hh + 1] * o_cmp[rows]
                   + gates[:, NSA_HEADS + hh:NSA_HEADS + hh + 1] * o_sel[rows]
                   + gates[:, 2 * NSA_HEADS + hh:2 * NSA_HEADS + hh + 1] * o_win[rows])
            y_ref[0, :, hh * hd:(hh + 1) * hd] = out.astype(BF16)


def _nsa_sample(rel_bias, q8, gate8, kc, past_sel, new_sel, cwin, new_win, past_len):
    bs = q8.shape[0]
    blk = lambda a: pl.BlockSpec((1,) + a.shape[1:], lambda s: (s,) + (0,) * (a.ndim - 1))
    return pl.pallas_call(
        functools.partial(_nsa_sample_body, past_len=past_len),
        grid=(bs,),
        in_specs=[pl.BlockSpec(memory_space=pltpu.SMEM), blk(q8), blk(gate8), blk(kc), blk(past_sel),
                  blk(new_sel), blk(cwin), blk(new_win)],
        out_specs=pl.BlockSpec((1, SAMPLE_ROWS, NSA_DIM), lambda s: (s, 0, 0)),
        out_shape=jax.ShapeDtypeStruct((bs, SAMPLE_ROWS, NSA_DIM), BF16),
        compiler_params=_cparams(("parallel",)),
        name="nsa_sample",
    )(rel_bias, q8, gate8, kc, past_sel, new_sel, cwin, new_win)


def _merge_body(x_ref, yrw_ref, g_ref, ynsa_ref, gates_ref, wrw_ref, wnsa_ref, wout_ref, post_ref, o_ref):
    tm, d = x_ref.shape
    y_rw = jnp.concatenate([yrw_ref[:, pair, :] for pair in range(RW_DIM // LANES)], axis=1)
    y_rw = (y_rw * g_ref[...]).astype(BF16)
    merged = (jax.nn.sigmoid(gates_ref[:, :d]) * _dot(y_rw, wrw_ref[...])
              + jax.nn.sigmoid(gates_ref[:, d:]) * _dot(ynsa_ref[...], wnsa_ref[...]))
    o_ref[...] = x_ref[...] + _rms(_dot(merged.astype(BF16), wout_ref[...]), post_ref[...])


def _merge(x, y_rw, g_rw, y_nsa, gates, w_rw, w_nsa, w_out, post_g, tm):
    m, d = x.shape
    rows = lambda n: pl.BlockSpec((tm, n), lambda i: (i, 0))
    held = lambda a: pl.BlockSpec(a.shape, lambda i: (0, 0), pipeline_mode=pl.Buffered(1))
    return pl.pallas_call(
        _merge_body,
        grid=(m // tm,),
        in_specs=[rows(d), pl.BlockSpec((tm,) + y_rw.shape[1:], lambda i: (i, 0, 0)),
                  rows(RW_DIM), rows(NSA_DIM), rows(2 * d),
                  held(w_rw), held(w_nsa), held(w_out), pl.BlockSpec((1, d), lambda i: (0, 0))],
        out_specs=rows(d),
        out_shape=jax.ShapeDtypeStruct((m, d), F32),
        compiler_params=_cparams(("parallel",)),
        name="merge",
    )(x, y_rw, g_rw, y_nsa, gates, w_rw, w_nsa, w_out, post_g.reshape(1, d))


def _row_tile(m, want):
    return want if m % want == 0 else m


def _rwkv_inputs(p3, shift0, w):
    b, t, c = p3.shape
    tt = 256 if t % 256 == 0 else t
    prev = jnp.concatenate([shift0[:, None, :], p3[:, tt - 1:t - 1:tt]], axis=1).reshape(b, t // tt, 1, c)
    pre, g_rw = _rwkv_pre(p3, prev, w["mu"], w["w0"], w["ww"], w["a0"], w["wa"], w["g2"], w["k_k"], w["k_a"], tt)
    pairs = RW_HEADS // 2
    if tt % SUBLANES == 0:
        n_pieces = SCAN_FIELDS * pairs
        tok = np.arange(b, dtype=np.int32)[None, :] * t + np.arange(t, dtype=np.int32)[:, None]
        row0 = (tok // SUBLANES) * (n_pieces * SUBLANES) + tok % SUBLANES
        piece = np.arange(n_pieces, dtype=np.int32).reshape(SCAN_FIELDS // 2, 2, 1, pairs)
        idx = row0[:, None, None, :, None] + piece[None] * SUBLANES
        pre = _gather_rows(pre.reshape(-1, LANES), jnp.asarray(idx.reshape(1, -1))).reshape(
            t, SCAN_FIELDS // 2, 2, b * pairs, LANES)
    return pre, g_rw.reshape(b * t, RW_DIM)


def _rwkv_recurrence(pre, s0, w, run_after=None):
    t = pre.shape[0]
    b = s0.shape[0]
    n = RW_HEAD_DIM
    seqs = min(b, LANES // RW_HEADS)
    groups = b // seqs
    pairs = RW_HEADS // 2
    s0_t = s0.reshape(groups, seqs, pairs, 2, n, n).transpose(5, 4, 0, 3, 1, 2).reshape(n, n, b * RW_HEADS)
    if run_after is not None:
        s0_t = _available_after(s0_t, run_after)
    per_lane = lambda v: jnp.broadcast_to(v.reshape(pairs, 2, n).transpose(2, 1, 0)[:, :, None, :],
                                          (n, 2, seqs, pairs)).reshape(n, seqs * RW_HEADS)
    tc = 32 if t % 32 == 0 else t
    y_t, s_t = _rwkv_scan(pre, s0_t, per_lane(w["lnx_w"]), per_lane(w["lnx_b"]), per_lane(w["r_k"]), tc, seqs)
    s_fin = s_t.reshape(n, n, groups, 2, seqs, pairs).transpose(2, 4, 5, 3, 1, 0).reshape(b, RW_HEADS, n, n)
    y_tok = y_t.reshape(t, b, pairs, LANES).transpose(1, 0, 2, 3).reshape(b * t, pairs, LANES)
    return y_tok, s_fin


def _layer(x, w, rel_bias, shift0, s0, past):
    b, t, d = x.shape
    m = b * t
    tm = _row_tile(m, 512)
    x1 = _ffn(x.reshape(m, d), w["f1_pre"], w["f1_post"], w["f1_w1"], w["f1_w3"], w["f1_w2"], tm, 512)
    p_a = _norm_matmul(x1, w["mix_pre"], w["w_in_a"], tm, "proj_rwkv")
    p_b = _norm_matmul(x1, w["mix_pre"], w["w_in_b"], tm, "proj_nsa")
    p_c = _norm_matmul(x1, w["mix_pre"], w["w_in_c"], tm, "proj_gates")

    p3 = p_a.reshape(b, t, RW_PROJ)
    pre, g_rw = _rwkv_inputs(p3, shift0, w)

    pb3 = p_b.reshape(b, t, PB_COLS)
    kv_new = pb3[:, :, PB_KV:PB_WIN]
    win_new = pb3[:, :, PB_WIN:PB_GATE]
    row_w = 4 * NSA_KV_DIM
    half = 2 * NSA_KV_DIM
    if past is None:
        blocks = p_b.reshape(m // CMP_BLOCK, CMP_BLOCK, PB_COLS)
        kc = _cmp_mlp(blocks, w["cmp_pe"], w["cmp_w1"], w["cmp_w2"], min(64, blocks.shape[0]), PB_KV // half)
        n_cmp = t // CMP_BLOCK
        kc = jnp.pad(kc.reshape(b, n_cmp, 2 * NSA_KV_DIM), ((0, 0), (0, -n_cmp % LANES), (0, 0)))
        y_nsa = _nsa_prompt(pb3, kc, *_bias_tiles(rel_bias, t // Q_BLOCK)).reshape(m, NSA_DIM)
        win_out = win_new[:, t - min(WINDOW, t):]
        y_rw, s_fin = _rwkv_recurrence(pre, s0, w, run_after=y_nsa[:SUBLANES])
    else:
        page_table, pool, cache_win, start_after = past
        past_len = page_table.shape[1] * pool.shape[1]
        slabs = row_w // NSA_HEAD_DIM
        pool4 = pool.reshape(pool.shape[0], pool.shape[1], slabs, NSA_HEAD_DIM)
        kc = _paged_kv(page_table, pool4, w["cmp_pe_tiles"], w["cmp_w1cat"], w["cmp_w2"])
        y_rw, s_fin = _rwkv_recurrence(pre, s0, w)
        past_sel = _gather_selected(_available_after(page_table, start_after), pool4)
        n_cmp = past_len // CMP_BLOCK
        kc = jnp.pad(kc, ((0, 0), (0, -n_cmp % LANES), (0, 0)))
        pad_q = lambda a, rows: jnp.pad(a, ((0, 0), (0, rows - t), (0, 0)))
        cwin = cache_win.reshape(b, cache_win.shape[1], half)
        y8 = _nsa_sample(rel_bias, pad_q(pb3[:, :, :NSA_DIM], SAMPLE_ROWS), pad_q(pb3[:, :, PB_GATE:], SAMPLE_ROWS),
                         kc, past_sel, pad_q(pb3[:, :, PB_SEL:PB_WIN].astype(BF16), LANES), cwin,
                         pad_q(win_new.astype(BF16), LANES), past_len)
        y_nsa = y8[:, :t].reshape(m, NSA_DIM)
        win_all = jnp.concatenate([cwin, win_new], axis=1)
        win_out = win_all[:, win_all.shape[1] - min(WINDOW, win_all.shape[1]):]

    x2 = _merge(x1, y_rw, g_rw, y_nsa, p_c, w["w_br_rw"], w["w_br_nsa"], w["w_out"], w["mix_post"],
                _row_tile(m, 256))
    y = _ffn(x2, w["f2_pre"], w["f2_post"], w["f2_w1"], w["f2_w3"], w["f2_w2"], tm, 512)
    g, hd = NSA_KV_HEADS, NSA_HEAD_DIM
    return (y.reshape(b, t, d), kv_new.reshape(b, t, 4, g, hd), win_out.reshape(b, -1, 2, g, hd), s_fin, p3[:, -1],
            y_nsa[:SUBLANES])


def _prepare_weights(l, ffn1_pre_g, ffn1_post_g, ffn1_w1, ffn1_w3, ffn1_w2, mix_pre_g, mix_post_g, w_in,
                     rw_mu, rw_w0, rw_w2, rw_a0, rw_a2, rw_g2, rw_k_k, rw_k_a, rw_r_k, rw_lnx_w, rw_lnx_b,
                     cmp_pe, cmp_w1, cmp_w2, w_br_rw, w_br_nsa, w_out,
                     ffn2_pre_g, ffn2_post_g, ffn2_w1, ffn2_w3, ffn2_w2):
    d = w_in.shape[1]
    wi = w_in[l]
    c_q = RW_PROJ
    c_kv = c_q + NSA_DIM
    c_gate = c_kv + 6 * NSA_KV_DIM
    c_grw = c_gate + 3 * NSA_HEADS
    w_in_b = jnp.concatenate([wi[:, c_q:c_gate], jnp.pad(wi[:, c_gate:c_grw], ((0, 0), (0, LANES - 3 * NSA_HEADS)))],
                             axis=1)
    zeros_w = jnp.zeros((RW_LORA_A, RW_DIM), F32)
    zeros_a = jnp.zeros((RW_LORA_W, RW_DIM), F32)
    pe = cmp_pe[l]
    pe_rows = jnp.broadcast_to(pe[:, :, None, :], (CMP_BLOCK, 2, NSA_KV_HEADS, NSA_HEAD_DIM))
    return {
        "f1_pre": ffn1_pre_g[l], "f1_post": ffn1_post_g[l],
        "f1_w1": ffn1_w1[l].astype(BF16), "f1_w3": ffn1_w3[l].astype(BF16), "f1_w2": ffn1_w2[l].astype(BF16),
        "mix_pre": mix_pre_g[l], "mix_post": mix_post_g[l],
        "w_in_a": wi[:, :RW_PROJ].astype(BF16), "w_in_b": w_in_b.astype(BF16), "w_in_c": wi[:, c_grw:].astype(BF16),
        "mu": rw_mu[l], "w0": rw_w0[l], "a0": rw_a0[l],
        "ww": jnp.concatenate([rw_w2[l], zeros_w], axis=0).astype(BF16),
        "wa": jnp.concatenate([zeros_a, rw_a2[l]], axis=0).astype(BF16),
        "g2": rw_g2[l].astype(BF16), "k_k": rw_k_k[l], "k_a": rw_k_a[l],
        "r_k": rw_r_k[l].reshape(-1), "lnx_w": rw_lnx_w[l], "lnx_b": rw_lnx_b[l],
        "cmp_pe": pe_rows.reshape(CMP_BLOCK, 2 * NSA_KV_DIM),
        "cmp_pe_tiles": jnp.pad(pe_rows.reshape(CMP_BLOCK, 2 * NSA_KV_HEADS, NSA_HEAD_DIM),
                                ((0, 0), (0, 2 * NSA_KV_HEADS), (0, 0))),
        "cmp_w1": cmp_w1[l].astype(BF16), "cmp_w2": cmp_w2[l].astype(BF16),
        "cmp_w1cat": jnp.concatenate([cmp_w1[l, 0], cmp_w1[l, 1]], axis=1).astype(BF16),
        "w_br_rw": w_br_rw[l].astype(BF16), "w_br_nsa": w_br_nsa[l].astype(BF16), "w_out": w_out[l].astype(BF16),
        "f2_pre": ffn2_pre_g[l], "f2_post": ffn2_post_g[l],
        "f2_w1": ffn2_w1[l].astype(BF16), "f2_w3": ffn2_w3[l].astype(BF16), "f2_w2": ffn2_w2[l].astype(BF16),
    }


def kernel(x_prompt, x_sample, cache_kv, cache_win, state_rwkv, state_shift, page_table,
           ffn1_pre_g, ffn1_post_g, ffn1_w1, ffn1_w3, ffn1_w2, mix_pre_g, mix_post_g, w_in,
           rw_mu, rw_w0, rw_w2, rw_a0, rw_a2, rw_g2, rw_k_k, rw_k_a, rw_r_k, rw_lnx_w, rw_lnx_b,
           cmp_pe, cmp_w1, cmp_w2, w_br_rw, w_br_nsa, w_out,
           ffn2_pre_g, ffn2_post_g, ffn2_w1, ffn2_w3, ffn2_w2, rel_bias):
    depth = w_in.shape[0]
    b_p = x_prompt.shape[0]
    y_p, y_s = x_prompt, x_sample
    outs = [[] for _ in range(8)]
    for l in range(depth):
        w = _prepare_weights(l, ffn1_pre_g, ffn1_post_g, ffn1_w1, ffn1_w3, ffn1_w2, mix_pre_g, mix_post_g, w_in,
                             rw_mu, rw_w0, rw_w2, rw_a0, rw_a2, rw_g2, rw_k_k, rw_k_a, rw_r_k, rw_lnx_w, rw_lnx_b,
                             cmp_pe, cmp_w1, cmp_w2, w_br_rw, w_br_nsa, w_out,
                             ffn2_pre_g, ffn2_post_g, ffn2_w1, ffn2_w3, ffn2_w2)
        y_p, kv_p, win_p, rw_p, sh_p, regrouped = _layer(
            y_p, w, rel_bias, jnp.zeros((b_p, RW_PROJ), F32),
            jnp.zeros((b_p, RW_HEADS, RW_HEAD_DIM, RW_HEAD_DIM), F32), None)
        y_s, kv_s, win_s, rw_s, sh_s, _ = _layer(
            y_s, w, rel_bias, state_shift[l], state_rwkv[l], (page_table, cache_kv[l], cache_win[l], regrouped))
        for acc, v in zip(outs, (kv_p, kv_s, win_p, win_s, rw_p, rw_s, sh_p, sh_s)):
            acc.append(v)
    return (y_p, y_s) + tuple(jnp.stack(o) for o in outs)
```
